```python
import math
import jax, jax.numpy as jnp
from jax import lax
import numpy as np

D_MODEL = 1024
BATCH = 16
SEQ = 256
DEPTH = 4
DEC_BATCH = 8
DEC_SEQ = 1024
PAST_LEN = 256

GRID_W = 64
N_MIXERS = 4
EPS = 1e-6
N_MOD = 6
CONV_WIDTH = 3
S5_GROUP = 16
S5_GROUPS = D_MODEL // S5_GROUP
S5_STATE = 64
HEAD_DIM = 64
N_HEADS = D_MODEL // HEAD_DIM
N_KV_HEADS = 4
Q_PER_KV = N_HEADS // N_KV_HEADS
WINDOW = 128
BLOCK = 128
ROPE_THETA = 10000.0
ROPE_FREQS = HEAD_DIM // 4
FNET_GROUPS = 4
D_FF = 2816
N_EXPERTS = 8
TOP_K = 2
D_FF_EXPERT = 3584
NEG_INF = -1e30

N_CONV_LAYERS = (DEPTH + 3) // 4
N_SSM_LAYERS = (DEPTH + 2) // 4
N_ATTN_LAYERS = (DEPTH + 1) // 4
N_FNET_LAYERS = DEPTH // 4
N_DENSE_LAYERS = (DEPTH + 1) // 2
N_MOE_LAYERS = DEPTH // 2

kernel_name = 'hybrid_diffusion_prefix_trunk_step'


def rmsnorm(x, g):
    xf = x.astype(jnp.float32)
    y = xf * lax.rsqrt(jnp.mean(xf * xf, axis=-1, keepdims=True) + EPS)
    return (y * g.astype(jnp.float32)).astype(x.dtype)


def swiglu(t, w_gu, w_down):
    g, u = jnp.split(t @ w_gu, 2, axis=-1)
    return (jax.nn.silu(g) * u) @ w_down


def moe_swiglu(h, w_router, w_gu, w_down):
    B, L, D = h.shape
    t = h.reshape(B * L, D)
    logits = jnp.dot(t, w_router, preferred_element_type=jnp.float32)
    top_v, top_i = lax.top_k(logits, TOP_K)
    gates = jax.nn.softmax(top_v, axis=-1)
    weight = jnp.einsum('tk,tke->te', gates,
                        jax.nn.one_hot(top_i, N_EXPERTS, dtype=jnp.float32)).astype(h.dtype)
    out = jnp.zeros_like(t)
    for e in range(N_EXPERTS):
        out = out + weight[:, e:e + 1] * swiglu(t, w_gu[e], w_down[e])
    return out.reshape(B, L, D)


def short_conv_mixer(h, w_in, conv_w, w_out):
    L = h.shape[1]
    gb, gc, v = jnp.split(h @ w_in, 3, axis=-1)
    half = CONV_WIDTH // 2
    u = jnp.pad(gc * v, ((0, 0), (half, half), (0, 0)))
    conv = u[:, 0:L] * conv_w[0]
    for k in range(1, CONV_WIDTH):
        conv = conv + u[:, k:k + L] * conv_w[k]
    return (gb * conv) @ w_out


def _linear_combine(e1, e2):
    a1, b1 = e1
    a2, b2 = e2
    return a2 * a1, a2 * b1 + b2


def s5_scan(u, lam_re, lam_im, b_re, b_im, c_re, c_im, log_dt, h0, reverse):
    f32 = jnp.float32
    L = u.shape[1]
    lam = lax.complex(lam_re.astype(f32), lam_im.astype(f32))
    lam_bar = jnp.exp(lam * jnp.exp(log_dt.astype(f32))[:, None])
    b_bar = ((lam_bar - 1.0) / lam)[..., None] * lax.complex(b_re.astype(f32), b_im.astype(f32))
    bu = jnp.einsum('gps,blgs->blgp', b_bar, u.astype(jnp.complex64))
    if h0 is not None:
        bu = bu.at[:, L - 1 if reverse else 0].add(lam_bar * h0)
    a = jnp.broadcast_to(lam_bar, (1, L) + lam_bar.shape)
    _, hs = lax.associative_scan(_linear_combine, (a, bu), reverse=reverse, axis=1)
    c_mat = lax.complex(c_re.astype(f32), c_im.astype(f32))
    y = jnp.real(jnp.einsum('gsp,blgp->blgs', c_mat, hs))
    return y, hs


def s5_mixer(h, lam_re, lam_im, b_re, b_im, c_re, c_im, log_dt, d_skip, w_glu, h0_re, h0_im):
    B, L, D = h.shape
    f32 = jnp.float32
    collect = h0_re is None
    u = h.astype(f32).reshape(B, L, S5_GROUPS, S5_GROUP)
    y = u * d_skip.astype(f32).reshape(S5_GROUPS, S5_GROUP)
    finals = []
    for dr in range(2):
        h0 = None if collect else lax.complex(h0_re[:, dr].astype(f32), h0_im[:, dr].astype(f32))
        y_dir, hs = s5_scan(u, lam_re[dr], lam_im[dr], b_re[dr], b_im[dr], c_re[dr], c_im[dr],
                            log_dt[dr], h0, dr == 1)
        y = y + y_dir
        if collect:
            finals.append(hs[:, 0] if dr == 1 else hs[:, -1])
    z = jax.nn.gelu(y.reshape(B, L, D)).astype(h.dtype)
    a, g = jnp.split(z @ w_glu, 2, axis=-1)
    out = a * jax.nn.sigmoid(g)
    if collect:
        fin = jnp.stack(finals, axis=1)
        return out, jnp.real(fin).astype(h.dtype), jnp.imag(fin).astype(h.dtype)
    return out, None, None


def axial_rope(L):
    rows = L // GRID_W
    row = jnp.repeat(jnp.arange(rows), GRID_W).astype(jnp.float32)
    col = jnp.tile(jnp.arange(GRID_W), rows).astype(jnp.float32)
    inv = ROPE_THETA ** (-jnp.arange(ROPE_FREQS, dtype=jnp.float32) / ROPE_FREQS)
    ang = jnp.concatenate([row[:, None] * inv, col[:, None] * inv], axis=-1)
    return jnp.cos(ang), jnp.sin(ang)


def apply_rope(x, cos, sin):
    half = HEAD_DIM // 2
    bshape = (1, cos.shape[0]) + (1,) * (x.ndim - 3) + (half,)
    c, s = cos.reshape(bshape), sin.reshape(bshape)
    xf = x.astype(jnp.float32)
    x1, x2 = xf[..., :half], xf[..., half:]
    return jnp.concatenate([x1 * c - x2 * s, x2 * c + x1 * s], axis=-1).astype(x.dtype)


def attn_context(h, w_q, w_kv, w_o, sink):
    B, L, _ = h.shape
    scale = HEAD_DIM ** -0.5
    q = (h @ w_q).reshape(B, L, N_KV_HEADS, Q_PER_KV, HEAD_DIM)
    kv = (h @ w_kv).reshape(B, L, 2, N_KV_HEADS, HEAD_DIM)
    k, v = kv[:, :, 0], kv[:, :, 1]
    s = jnp.einsum('bqkgd,bskd->bkgqs', q, k, preferred_element_type=jnp.float32) * scale
    s_sink = jnp.broadcast_to(sink.astype(jnp.float32).reshape(1, N_KV_HEADS, Q_PER_KV, 1, 1),
                              s.shape[:-1] + (1,))
    p = jax.nn.softmax(jnp.concatenate([s_sink, s], axis=-1), axis=-1).astype(h.dtype)
    o = jnp.einsum('bkgqs,bskd->bqkgd', p[..., 1:], v)
    return o.reshape(B, L, N_HEADS * HEAD_DIM) @ w_o, k, v


def attn_latent(h, ck, cv, w_q, w_kv, w_o, sink):
    B, L, _ = h.shape
    nb = L // BLOCK
    span = BLOCK + 2 * WINDOW
    scale = HEAD_DIM ** -0.5
    cos, sin = axial_rope(L)
    q = apply_rope((h @ w_q).reshape(B, L, N_KV_HEADS, Q_PER_KV, HEAD_DIM), cos, sin)
    kv = (h @ w_kv).reshape(B, L, 2, N_KV_HEADS, HEAD_DIM)
    k = apply_rope(kv[:, :, 0], cos, sin)
    v = kv[:, :, 1]
    pad = ((0, 0), (WINDOW, WINDOW), (0, 0), (0, 0))
    idx = jnp.arange(nb)[:, None] * BLOCK + jnp.arange(span)[None, :]
    k_win = jnp.pad(k, pad)[:, idx]
    v_win = jnp.pad(v, pad)[:, idx]
    q_pos = jnp.arange(L).reshape(nb, BLOCK)
    k_pos = idx - WINDOW
    mask = ((k_pos[:, None, :] >= 0) & (k_pos[:, None, :] < L)
            & (jnp.abs(q_pos[:, :, None] - k_pos[:, None, :]) <= WINDOW))
    qb = q.reshape(B, nb, BLOCK, N_KV_HEADS, Q_PER_KV, HEAD_DIM)
    s_win = jnp.einsum('bnqkgd,bnskd->bnkgqs', qb, k_win, preferred_element_type=jnp.float32) * scale
    s_win = jnp.where(mask[None, :, None, None], s_win, NEG_INF)
    s_ctx = jnp.einsum('bnqkgd,bskd->bnkgqs', qb, ck, preferred_element_type=jnp.float32) * scale
    s_sink = jnp.broadcast_to(sink.astype(jnp.float32).reshape(1, 1, N_KV_HEADS, Q_PER_KV, 1, 1),
                              s_ctx.shape[:-1] + (1,))
    p = jax.nn.softmax(jnp.concatenate([s_sink, s_ctx, s_win], axis=-1), axis=-1).astype(h.dtype)
    n_ctx = ck.shape[1]
    o = (jnp.einsum('bnkgqs,bskd->bnqkgd', p[..., 1:1 + n_ctx], cv)
         + jnp.einsum('bnkgqs,bnskd->bnqkgd', p[..., 1 + n_ctx:], v_win))
    return o.reshape(B, L, N_HEADS * HEAD_DIM) @ w_o


def fourier_mixer(h, w_out):
    B, L, D = h.shape
    hg = h.astype(jnp.float32).reshape(B, L, FNET_GROUPS, D // FNET_GROUPS)
    f = jnp.real(jnp.fft.fft2(hg, axes=(1, 3), norm='ortho'))
    return f.reshape(B, L, D).astype(h.dtype) @ w_out


def trunk(x, cond, p, cache_k, cache_v, st_re, st_im):
    is_ctx = cache_k is None
    new_k, new_v, new_re, new_im = [], [], [], []
    for i in range(DEPTH):
        mod = (jax.nn.silu(cond) @ p['w_ada'][i] + p['b_ada'][i])[:, None, :]
        sh1, sc1, g1, sh2, sc2, g2 = jnp.split(mod, N_MOD, axis=-1)
        h = rmsnorm(x, p['g_mix'][i]) * (1.0 + sc1) + sh1
        kind, j = i % N_MIXERS, i // N_MIXERS
        if kind == 0:
            y = short_conv_mixer(h, p['conv_w_in'][j], p['conv_w'][j], p['conv_w_out'][j])
        elif kind == 1:
            y, fr, fi = s5_mixer(h, p['s5_lambda_re'][j], p['s5_lambda_im'][j], p['s5_b_re'][j],
                                 p['s5_b_im'][j], p['s5_c_re'][j], p['s5_c_im'][j], p['s5_log_dt'][j],
                                 p['s5_d'][j], p['s5_w_glu'][j],
                                 None if is_ctx else st_re[:, j], None if is_ctx else st_im[:, j])
            if is_ctx:
                new_re.append(fr)
                new_im.append(fi)
        elif kind == 2:
            if is_ctx:
                y, k, v = attn_context(h, p['attn_w_q'][j], p['attn_w_kv'][j], p['attn_w_o'][j],
                                       p['attn_sink'][j])
                new_k.append(k)
                new_v.append(v)
            else:
                y = attn_latent(h, cache_k[:, j], cache_v[:, j], p['attn_w_q'][j], p['attn_w_kv'][j],
                                p['attn_w_o'][j], p['attn_sink'][j])
        else:
            y = fourier_mixer(h, p['fnet_w_out'][j])
        x = x + g1 * y
        h = rmsnorm(x, p['g_ffn'][i]) * (1.0 + sc2) + sh2
        if i % 2 == 0:
            y = swiglu(h, p['ffn_w_gu'][i // 2], p['ffn_w_down'][i // 2])
        else:
            y = moe_swiglu(h, p['moe_w_router'][i // 2], p['moe_w_gu'][i // 2], p['moe_w_down'][i // 2])
        x = x + g2 * y
    return rmsnorm(x, p['g_final']), new_k, new_v, new_re, new_im


def setup_inputs(seed: int = 0) -> dict:
    key = jax.random.key(seed)
    keys = list(jax.random.split(key, 48))
    f32 = jnp.float32
    D = D_MODEL

    def nrm(shape, scale):
        return scale * jax.random.normal(keys.pop(), shape, f32)

    ssm_shape = (N_SSM_LAYERS, 2, S5_GROUPS, S5_STATE)
    return {
        'x_prompt': nrm((BATCH, SEQ, D), 1.0),
        'x_sample': nrm((DEC_BATCH, DEC_SEQ, D), 1.0),
        'cache_k': nrm((DEC_BATCH, N_ATTN_LAYERS, PAST_LEN, N_KV_HEADS, HEAD_DIM), 1.0),
        'cache_v': nrm((DEC_BATCH, N_ATTN_LAYERS, PAST_LEN, N_KV_HEADS, HEAD_DIM), 1.0),
        'state_ssm_re': nrm((DEC_BATCH, N_SSM_LAYERS, 2, S5_GROUPS, S5_STATE), 0.1),
        'state_ssm_im': nrm((DEC_BATCH, N_SSM_LAYERS, 2, S5_GROUPS, S5_STATE), 0.1),
        'c': nrm((DEC_BATCH, D), 1.0),
        'c_ctx': nrm((D,), 1.0),
        'w_ada': nrm((DEPTH, D, N_MOD * D), 0.5 * D ** -0.5),
        'b_ada': nrm((DEPTH, N_MOD * D), 0.01),
        'g_mix': 1.0 + nrm((DEPTH, D), 0.01),
        'g_ffn': 1.0 + nrm((DEPTH, D), 0.01),
        'g_final': 1.0 + nrm((D,), 0.01),
        'conv_w_in': nrm((N_CONV_LAYERS, D, 3 * D), D ** -0.5),
        'conv_w': nrm((N_CONV_LAYERS, CONV_WIDTH, D), CONV_WIDTH ** -0.5),
        'conv_w_out': nrm((N_CONV_LAYERS, D, D), D ** -0.5),
        's5_lambda_re': -0.5 + nrm(ssm_shape, 0.01),
        's5_lambda_im': math.pi * jnp.arange(S5_STATE, dtype=f32) + nrm(ssm_shape, 0.01),
        's5_b_re': nrm((N_SSM_LAYERS, 2, S5_GROUPS, S5_STATE, S5_GROUP), (2 * S5_GROUP) ** -0.5),
        's5_b_im': nrm((N_SSM_LAYERS, 2, S5_GROUPS, S5_STATE, S5_GROUP), (2 * S5_GROUP) ** -0.5),
        's5_c_re': nrm((N_SSM_LAYERS, 2, S5_GROUPS, S5_GROUP, S5_STATE), (2 * S5_STATE) ** -0.5),
        's5_c_im': nrm((N_SSM_LAYERS, 2, S5_GROUPS, S5_GROUP, S5_STATE), (2 * S5_STATE) ** -0.5),
        's5_log_dt': jax.random.uniform(keys.pop(), (N_SSM_LAYERS, 2, S5_GROUPS), f32,
                                        math.log(1e-3), math.log(1e-1)),
        's5_d': nrm((N_SSM_LAYERS, D), 1.0),
        's5_w_glu': nrm((N_SSM_LAYERS, D, 2 * D), D ** -0.5),
        'attn_w_q': nrm((N_ATTN_LAYERS, D, N_HEADS * HEAD_DIM), D ** -0.5),
        'attn_w_kv': nrm((N_ATTN_LAYERS, D, 2 * N_KV_HEADS * HEAD_DIM), D ** -0.5),
        'attn_w_o': nrm((N_ATTN_LAYERS, N_HEADS * HEAD_DIM, D), (N_HEADS * HEAD_DIM) ** -0.5),
        'attn_sink': nrm((N_ATTN_LAYERS, N_HEADS), 0.5),
        'fnet_w_out': nrm((N_FNET_LAYERS, D, D), D ** -0.5),
        'ffn_w_gu': nrm((N_DENSE_LAYERS, D, 2 * D_FF), D ** -0.5),
        'ffn_w_down': nrm((N_DENSE_LAYERS, D_FF, D), D_FF ** -0.5),
        'moe_w_router': nrm((N_MOE_LAYERS, D, N_EXPERTS), D ** -0.5),
        'moe_w_gu': nrm((N_MOE_LAYERS, N_EXPERTS, D, 2 * D_FF_EXPERT), D ** -0.5),
        'moe_w_down': nrm((N_MOE_LAYERS, N_EXPERTS, D_FF_EXPERT, D), D_FF_EXPERT ** -0.5),
    }


def reference(x_prompt, x_sample, cache_k, cache_v, state_ssm_re, state_ssm_im, c, c_ctx,
              w_ada, b_ada, g_mix, g_ffn, g_final, conv_w_in, conv_w, conv_w_out,
              s5_lambda_re, s5_lambda_im, s5_b_re, s5_b_im, s5_c_re, s5_c_im, s5_log_dt, s5_d,
              s5_w_glu, attn_w_q, attn_w_kv, attn_w_o, attn_sink, fnet_w_out, ffn_w_gu, ffn_w_down,
              moe_w_router, moe_w_gu, moe_w_down):
    p = {
        'w_ada': w_ada, 'b_ada': b_ada, 'g_mix': g_mix, 'g_ffn': g_ffn, 'g_final': g_final,
        'conv_w_in': conv_w_in, 'conv_w': conv_w, 'conv_w_out': conv_w_out,
        's5_lambda_re': s5_lambda_re, 's5_lambda_im': s5_lambda_im, 's5_b_re': s5_b_re,
        's5_b_im': s5_b_im, 's5_c_re': s5_c_re, 's5_c_im': s5_c_im, 's5_log_dt': s5_log_dt,
        's5_d': s5_d, 's5_w_glu': s5_w_glu,
        'attn_w_q': attn_w_q, 'attn_w_kv': attn_w_kv, 'attn_w_o': attn_w_o, 'attn_sink': attn_sink,
        'fnet_w_out': fnet_w_out, 'ffn_w_gu': ffn_w_gu, 'ffn_w_down': ffn_w_down,
        'moe_w_router': moe_w_router, 'moe_w_gu': moe_w_gu, 'moe_w_down': moe_w_down,
    }
    y_prompt, ks, vs, res, ims = trunk(x_prompt, c_ctx[None, :], p, None, None, None, None)
    new_cache_k = jnp.stack(ks, axis=1)
    new_cache_v = jnp.stack(vs, axis=1)
    new_state_ssm_re = jnp.stack(res, axis=1)
    new_state_ssm_im = jnp.stack(ims, axis=1)
    y_sample, _, _, _, _ = trunk(x_sample, c, p, cache_k, cache_v, state_ssm_re, state_ssm_im)
    return (y_prompt, y_sample, new_cache_k, new_cache_v, new_state_ssm_re, new_state_ssm_im)
```

```python
import functools
import math

import numpy as np
import jax
import jax.numpy as jnp
from jax import lax
from jax.experimental import pallas as pl
from jax.experimental.pallas import tpu as pltpu

F32 = jnp.float32
BF16 = jnp.bfloat16
I32 = jnp.int32

D = 1024
BATCH = 16
SEQ = 256
DEPTH = 4
DEC_BATCH = 8
DEC_SEQ = 1024
PAST_LEN = 256
GRID_W = 64
EPS = 1e-6
N_MOD = 6
S5_GROUP = 16
S5_GROUPS = D // S5_GROUP
S5_STATE = 64
HEAD_DIM = 64
N_HEADS = D // HEAD_DIM
N_KV_HEADS = 4
Q_PER_KV = N_HEADS // N_KV_HEADS
WINDOW = 128
ROPE_THETA = 10000.0
ROPE_FREQS = HEAD_DIM // 4
FNET_GROUPS = 4
D_FF = 2816
N_EXPERTS = 8
TOP_K = 2
D_FF_EXPERT = 3584
NEG_INF = -1e30

T_CTX = BATCH * SEQ
T_LAT = DEC_BATCH * DEC_SEQ
T_ALL = T_CTX + T_LAT

VMEM_LIMIT_V7X = 56 * 1024 * 1024
LANES = 128

TM = 1024
ROW_CHUNK = 256
TME = 1024
N_EXPERT_TILES = (T_ALL * TOP_K) // TME + N_EXPERTS
P_MAX = N_EXPERT_TILES * TME


def _cparams(sem):
    return pltpu.CompilerParams(dimension_semantics=sem, vmem_limit_bytes=VMEM_LIMIT_V7X)


def _dot(a, b):
    return jnp.dot(a, b, preferred_element_type=F32)


def _dot_nt(a, b):
    return lax.dot_general(a, b, (((1,), (1,)), ((), ())), preferred_element_type=F32)


def _split_bf16(a):
    hi = a.astype(BF16)
    lo = (a - hi.astype(F32)).astype(BF16)
    return hi, lo


def _tile_rows(a, reps, axis=0):
    assert axis == 0
    return jnp.concatenate([a] * reps, axis=0)


def _norm_mod(x, g, sc, sh):
    ms = jnp.mean(x * x, axis=-1, keepdims=True)
    y = x * lax.rsqrt(ms + EPS) * g
    return y * (1.0 + sc) + sh


def _mod_row(i, tm):
    nct = T_CTX // tm
    lpb = DEC_SEQ // tm
    return jnp.where(i < nct, 0, 8 + (i - nct) // lpb)


def _mod_spec(layer, tm):
    return pl.BlockSpec((None, None, 1, N_MOD * D), lambda i, *_: (layer, _mod_row(i, tm), 0, 0))


def _row_spec():
    return pl.BlockSpec((1, D), lambda *_: (0, 0))


def _full_spec(shape):
    nd = len(shape)
    return pl.BlockSpec(shape, lambda *_: (0,) * nd)


def _mod_kernel(c_ref, w_ref, b_ref, o_ref):
    c = c_ref[...]
    s = (c * jax.nn.sigmoid(c)).astype(BF16)
    o_ref[...] = _dot(s, w_ref[...].astype(BF16)) + b_ref[...]


def _mod_all(cond16, w_ada, b_ada):
    tn = 1024
    return pl.pallas_call(
        _mod_kernel,
        grid=(DEPTH, N_MOD * D // tn),
        in_specs=[pl.BlockSpec((16, D), lambda l, n: (0, 0)),
                  pl.BlockSpec((None, D, tn), lambda l, n: (l, 0, n)),
                  pl.BlockSpec((None, 1, tn), lambda l, n: (l, 0, n))],
        out_specs=pl.BlockSpec((None, 16, tn), lambda l, n: (l, 0, n)),
        out_shape=jax.ShapeDtypeStruct((DEPTH, 16, N_MOD * D), F32),
        compiler_params=_cparams(("arbitrary", "arbitrary")),
        name="adaln_mod",
    )(cond16, w_ada, b_ada.reshape(DEPTH, 1, N_MOD * D))


def _conv_kernel(x_ref, mod_ref, g_ref, win_ref, cw_ref, wout_ref, o_ref, gb_s, u_s):
    i = pl.program_id(0)
    is_ctx = i < (T_CTX // TM)
    sh = mod_ref[:, 0:D]
    sc = mod_ref[:, D:2 * D]
    gate = mod_ref[:, 2 * D:3 * D]
    g = g_ref[...]
    nchunk = TM // ROW_CHUNK
    zero8 = jnp.zeros((8, D), F32)
    u_s[0:8, :] = zero8
    u_s[8 + TM:16 + TM, :] = zero8
    for c in range(nchunk):
        r0 = c * ROW_CHUNK
        h = _norm_mod(x_ref[r0:r0 + ROW_CHUNK, :], g, sc, sh).astype(BF16)
        proj = _dot(h, win_ref[...])
        gb_s[r0:r0 + ROW_CHUNK, :] = proj[:, 0:D]
        u_s[8 + r0:8 + r0 + ROW_CHUNK, :] = proj[:, D:2 * D] * proj[:, 2 * D:3 * D]
    row = lax.broadcasted_iota(I32, (ROW_CHUNK, 1), 0)
    first = jnp.logical_and(is_ctx, row == 0)
    last = jnp.logical_and(is_ctx, row == ROW_CHUNK - 1)
    for c in range(nchunk):
        r0 = c * ROW_CHUNK
        up = jnp.where(first, 0.0, u_s[7 + r0:7 + r0 + ROW_CHUNK, :])
        mid = u_s[8 + r0:8 + r0 + ROW_CHUNK, :]
        dn = jnp.where(last, 0.0, u_s[9 + r0:9 + r0 + ROW_CHUNK, :])
        conv = up * cw_ref[0:1, :] + mid * cw_ref[1:2, :] + dn * cw_ref[2:3, :]
        y = _dot((gb_s[r0:r0 + ROW_CHUNK, :] * conv).astype(BF16), wout_ref[...])
        o_ref[r0:r0 + ROW_CHUNK, :] = x_ref[r0:r0 + ROW_CHUNK, :] + gate * y


def _conv_mixer(x, mod4, layer, g_mix, w_in, conv_w, w_out):
    assert SEQ == ROW_CHUNK and DEC_SEQ == TM
    return pl.pallas_call(
        _conv_kernel,
        grid=(T_ALL // TM,),
        in_specs=[pl.BlockSpec((TM, D), lambda i: (i, 0)),
                  _mod_spec(layer, TM), _row_spec(),
                  _full_spec((D, 3 * D)), _full_spec((3, D)), _full_spec((D, D))],
        out_specs=pl.BlockSpec((TM, D), lambda i: (i, 0)),
        out_shape=jax.ShapeDtypeStruct((T_ALL, D), F32),
        scratch_shapes=[pltpu.VMEM((TM, D), F32), pltpu.VMEM((TM + 16, D), F32)],
        compiler_params=_cparams(("arbitrary",)),
        name="conv_mixer",
    )(x, mod4, g_mix.reshape(1, D), w_in, conv_w, w_out)


FFN_TF = D_FF // 2


def _ffn_kernel(x_ref, mod_ref, g_ref, wg_ref, wu_ref, wd_ref, o_ref, h_s):
    f = pl.program_id(1)
    nf = pl.num_programs(1)
    sh = mod_ref[:, 3 * D:4 * D]
    sc = mod_ref[:, 4 * D:5 * D]
    gate = mod_ref[:, 5 * D:6 * D]
    nchunk = x_ref.shape[0] // ROW_CHUNK

    @pl.when(f == 0)
    def _():
        for c in range(nchunk):
            r0 = c * ROW_CHUNK
            h_s[r0:r0 + ROW_CHUNK, :] = _norm_mod(x_ref[r0:r0 + ROW_CHUNK, :], g_ref[...], sc, sh).astype(BF16)

    for c in range(nchunk):
        r0 = c * ROW_CHUNK
        hc = h_s[r0:r0 + ROW_CHUNK, :]
        gg = _dot(hc, wg_ref[...])
        uu = _dot(hc, wu_ref[...])
        a = (gg * jax.nn.sigmoid(gg) * uu).astype(BF16)
        contrib = _dot(a, wd_ref[...])

        @pl.when(f == 0)
        def _():
            o_ref[r0:r0 + ROW_CHUNK, :] = contrib

        @pl.when(f > 0)
        def _():
            o_ref[r0:r0 + ROW_CHUNK, :] = o_ref[r0:r0 + ROW_CHUNK, :] + contrib

    @pl.when(f == nf - 1)
    def _():
        for c in range(nchunk):
            r0 = c * ROW_CHUNK
            o_ref[r0:r0 + ROW_CHUNK, :] = x_ref[r0:r0 + ROW_CHUNK, :] + gate * o_ref[r0:r0 + ROW_CHUNK, :]


def _dense_ffn(x, mod4, layer, g_ffn, w_gu, w_down):
    nf = D_FF // FFN_TF
    return pl.pallas_call(
        _ffn_kernel,
        grid=(T_ALL // TM, nf),
        in_specs=[pl.BlockSpec((TM, D), lambda i, f: (i, 0)),
                  _mod_spec(layer, TM), _row_spec(),
                  pl.BlockSpec((D, FFN_TF), lambda i, f: (0, f)),
                  pl.BlockSpec((D, FFN_TF), lambda i, f: (0, nf + f)),
                  pl.BlockSpec((FFN_TF, D), lambda i, f: (f, 0))],
        out_specs=pl.BlockSpec((TM, D), lambda i, f: (i, 0)),
        out_shape=jax.ShapeDtypeStruct((T_ALL, D), F32),
        scratch_shapes=[pltpu.VMEM((TM, D), BF16)],
        compiler_params=_cparams(("arbitrary", "arbitrary")),
        name="dense_swiglu",
    )(x, mod4, g_ffn.reshape(1, D), w_gu, w_gu, w_down)


def _s5_prep_kernel(lr_ref, li_ref, ldt_ref, br_ref, bi_ref, lbr_ref, lbi_ref, bbr_ref, bbi_ref):
    lr = lr_ref[...]
    li = li_ref[...]
    dt = jnp.exp(ldt_ref[...])
    mag = jnp.exp(lr * dt)
    ar = mag * jnp.cos(li * dt)
    ai = mag * jnp.sin(li * dt)
    nr = ar - 1.0
    den = lr * lr + li * li
    fr = (nr * lr + ai * li) / den
    fi = (ai * lr - nr * li) / den
    br = br_ref[...]
    bi = bi_ref[...]
    lbr_ref[...] = ar
    lbi_ref[...] = ai
    bbr_ref[...] = fr * br - fi * bi
    bbi_ref[...] = fr * bi + fi * br


def _s5_prep(lam_re, lam_im, log_dt, b_re, b_im):
    rows = 2 * S5_GROUPS
    cols = S5_STATE * S5_GROUP
    exp = lambda a: jnp.repeat(a.reshape(rows, S5_STATE), S5_GROUP, axis=1)
    ldt = jnp.broadcast_to(log_dt.reshape(rows, 1), (rows, cols))
    outs = pl.pallas_call(
        _s5_prep_kernel,
        out_shape=[jax.ShapeDtypeStruct((rows, cols), F32)] * 4,
        name="s5_discretize",
    )(exp(lam_re), exp(lam_im), ldt, b_re.reshape(rows, cols), b_im.reshape(rows, cols))
    lbr, lbi, bbr, bbi = outs
    shp = (2, S5_GROUPS, S5_STATE, S5_GROUP)
    return lbr.reshape(shp)[..., 0], lbi.reshape(shp)[..., 0], bbr.reshape(shp), bbi.reshape(shp)


S5_JT = 8
S5_GPT = LANES // S5_GROUP
S5_HALF = S5_GPT * S5_STATE
S5_ROWS = 256


def _s5_scan_kernel(x_ref, mod_ref, g_ref, whi_ref, wlo_ref, wc_ref, lam_ref, h0_ref,
                    y_ref, fin_ref, bu_s, st_s, *, nb, jgroup):
    d = pl.program_id(0)
    c = pl.program_id(1)
    lc = S5_ROWS // nb

    @pl.when(c == 0)
    def _():
        st_s[...] = h0_ref[...]

    rep = S5_ROWS // 8
    sh = _tile_rows(mod_ref[:, 0:D], rep, axis=0)
    sc = _tile_rows(mod_ref[:, D:2 * D], rep, axis=0)
    u = _norm_mod(x_ref[...], g_ref[...], sc, sh)
    u_hi, u_lo = _split_bf16(u)
    for j in range(S5_JT):
        uh = u_hi[:, j * LANES:(j + 1) * LANES]
        ul = u_lo[:, j * LANES:(j + 1) * LANES]
        bu_s[j] = _dot(uh, whi_ref[j]) + _dot(ul, whi_ref[j]) + _dot(uh, wlo_ref[j])

    for j0 in range(0, S5_JT, jgroup):
        js = list(range(j0, j0 + jgroup))
        lam = [(jnp.broadcast_to(lam_ref[j][:, 0:S5_HALF], (nb, S5_HALF)),
                jnp.broadcast_to(lam_ref[j][:, S5_HALF:], (nb, S5_HALF))) for j in js]

        def body(t, carry):
            l = jnp.where(d == 0, t, lc - 1 - t)
            r0 = pl.multiple_of(l * nb, nb)
            out = []
            for k, j in enumerate(js):
                sr, si = carry[k]
                ar, ai = lam[k]
                bu = bu_s[j, pl.ds(r0, nb), :]
                hr = ar * sr - ai * si + bu[:, 0:S5_HALF]
                hi = ar * si + ai * sr + bu[:, S5_HALF:]
                bu_s[j, pl.ds(r0, nb), 0:S5_HALF] = hr
                bu_s[j, pl.ds(r0, nb), S5_HALF:] = hi
                out.append((hr, hi))
            return tuple(out)

        init = tuple((st_s[j][:, 0:S5_HALF], st_s[j][:, S5_HALF:]) for j in js)
        fin = lax.fori_loop(0, lc, body, init)
        for k, j in enumerate(js):
            st_s[j, :, 0:S5_HALF] = fin[k][0]
            st_s[j, :, S5_HALF:] = fin[k][1]

    for j in range(S5_JT):
        y_ref[:, j * LANES:(j + 1) * LANES] = _dot(bu_s[j].astype(BF16), wc_ref[j])
    fin_ref[...] = st_s[...]


def _s5_scan(xt, nb, mod3, layer, path, g_mix, w_hi, w_lo, w_c, lam_s, h0):
    rows = xt.shape[0]
    nc = rows // S5_ROWS
    chunk = lambda d, c: c + d * (nc - 1 - 2 * c)
    kern = functools.partial(_s5_scan_kernel, nb=nb, jgroup=2 if nb == 8 else 1)
    return pl.pallas_call(
        kern,
        grid=(2, nc),
        in_specs=[pl.BlockSpec((S5_ROWS, D), lambda d, c: (chunk(d, c), 0)),
                  pl.BlockSpec((None, 8, N_MOD * D), lambda d, c: (layer, path, 0)),
                  _row_spec(),
                  pl.BlockSpec((None, S5_JT, LANES, 2 * S5_HALF), lambda d, c: (d, 0, 0, 0)),
                  pl.BlockSpec((None, S5_JT, LANES, 2 * S5_HALF), lambda d, c: (d, 0, 0, 0)),
                  pl.BlockSpec((None, S5_JT, 2 * S5_HALF, LANES), lambda d, c: (d, 0, 0, 0)),
                  pl.BlockSpec((None, S5_JT, 1, 2 * S5_HALF), lambda d, c: (d, 0, 0, 0)),
                  pl.BlockSpec((None, S5_JT, nb, 2 * S5_HALF), lambda d, c: (d, 0, 0, 0))],
        out_specs=[pl.BlockSpec((None, S5_ROWS, D), lambda d, c: (d, chunk(d, c), 0)),
                   pl.BlockSpec((None, S5_JT, nb, 2 * S5_HALF), lambda d, c: (d, 0, 0, 0))],
        out_shape=[jax.ShapeDtypeStruct((2, rows, D), F32),
                   jax.ShapeDtypeStruct((2, S5_JT, nb, 2 * S5_HALF), F32)],
        scratch_shapes=[pltpu.VMEM((S5_JT, S5_ROWS, 2 * S5_HALF), F32),
                        pltpu.VMEM((S5_JT, nb, 2 * S5_HALF), F32)],
        compiler_params=_cparams(("arbitrary", "arbitrary")),
        name="s5_scan_b%d" % nb,
    )(xt, mod3, g_mix.reshape(1, D), w_hi, w_lo, w_c, lam_s, h0)


S5_GLU_ROWS = 512


def _s5_glu_kernel(x_ref, yf_ref, yb_ref, mod_ref, g_ref, dsk_ref, w_ref, o_ref):
    rep = ROW_CHUNK // 8
    sh = _tile_rows(mod_ref[:, 0:D], rep, axis=0)
    sc = _tile_rows(mod_ref[:, D:2 * D], rep, axis=0)
    gate = _tile_rows(mod_ref[:, 2 * D:3 * D], rep, axis=0)
    for c in range(S5_GLU_ROWS // ROW_CHUNK):
        r0 = c * ROW_CHUNK
        x = x_ref[r0:r0 + ROW_CHUNK, :]
        u = _norm_mod(x, g_ref[...], sc, sh)
        y = u * dsk_ref[...] + yf_ref[r0:r0 + ROW_CHUNK, :] + yb_ref[r0:r0 + ROW_CHUNK, :]
        z = jax.nn.gelu(y).astype(BF16)
        ag = _dot(z, w_ref[...])
        out = ag[:, 0:D] * jax.nn.sigmoid(ag[:, D:2 * D])
        o_ref[r0:r0 + ROW_CHUNK, :] = x + gate * out


def _s5_glu(xt, y2, mod3, layer, path, g_mix, d_skip, w_glu):
    rows = xt.shape[0]
    return pl.pallas_call(
        _s5_glu_kernel,
        grid=(rows // S5_GLU_ROWS,),
        in_specs=[pl.BlockSpec((S5_GLU_ROWS, D), lambda i: (i, 0)),
                  pl.BlockSpec((None, S5_GLU_ROWS, D), lambda i: (0, i, 0)),
                  pl.BlockSpec((None, S5_GLU_ROWS, D), lambda i: (1, i, 0)),
                  pl.BlockSpec((None, 8, N_MOD * D), lambda i: (layer, path, 0)),
                  _row_spec(), _row_spec(), _full_spec((D, 2 * D))],
        out_specs=pl.BlockSpec((S5_GLU_ROWS, D), lambda i: (i, 0)),
        out_shape=jax.ShapeDtypeStruct((rows, D), F32),
        compiler_params=_cparams(("arbitrary",)),
        name="s5_glu",
    )(xt, y2, y2, mod3, g_mix.reshape(1, D), d_skip.reshape(1, D), w_glu)


def _s5_weights(lbr, lbi, bbr, bbi, c_re, c_im):
    eye = jnp.eye(S5_GPT, dtype=F32)
    bb = jnp.stack([bbr, bbi]).reshape(2, 2, S5_JT, S5_GPT, S5_STATE, S5_GROUP)
    w_bu = jnp.einsum('rdjgps,gh->djgsrhp', bb, eye).reshape(2, S5_JT, LANES, 2 * S5_HALF)
    w_hi = w_bu.astype(BF16)
    w_lo = (w_bu - w_hi.astype(F32)).astype(BF16)
    cc = jnp.stack([c_re, -c_im]).reshape(2, 2, S5_JT, S5_GPT, S5_GROUP, S5_STATE)
    w_c = jnp.einsum('rdjgsp,gh->djrgphs', cc, eye).reshape(2, S5_JT, 2 * S5_HALF, LANES).astype(BF16)
    lam_s = jnp.concatenate([lbr.reshape(2, S5_JT, S5_HALF), lbi.reshape(2, S5_JT, S5_HALF)], axis=-1)
    return w_hi, w_lo, w_c, lam_s.reshape(2, S5_JT, 1, 2 * S5_HALF)


def _s5_state_in(st_re, st_im):
    def lay(a):
        b = a.shape[0]
        return a.transpose(1, 0, 2, 3).reshape(2, b, S5_JT, S5_HALF).transpose(0, 2, 1, 3)
    return jnp.concatenate([lay(st_re), lay(st_im)], axis=-1)


def _s5_state_out(fin):
    def lay(a):
        b = a.shape[2]
        return a.transpose(2, 0, 1, 3).reshape(b, 2, S5_GROUPS, S5_STATE)
    return lay(fin[..., 0:S5_HALF]), lay(fin[..., S5_HALF:])


QKV_TM = 512


def _qkv_ctx_kernel(x_ref, mod_ref, g_ref, w_ref, q_ref, k_ref, v_ref, kv_ref):
    sh = mod_ref[:, 0:D]
    sc = mod_ref[:, D:2 * D]
    for c in range(QKV_TM // ROW_CHUNK):
        r0 = c * ROW_CHUNK
        h = _norm_mod(x_ref[r0:r0 + ROW_CHUNK, :], g_ref[...], sc, sh).astype(BF16)
        p = _dot(h, w_ref[...])
        q_ref[r0:r0 + ROW_CHUNK, :] = p[:, 0:D].astype(BF16)
        k_ref[r0:r0 + ROW_CHUNK, :] = p[:, D:2 * D].astype(BF16)
        v_ref[r0:r0 + ROW_CHUNK, :] = p[:, 2 * D:3 * D].astype(BF16)
        kv_ref[r0:r0 + ROW_CHUNK, :] = p[:, 3 * D:3 * D + 2 * N_KV_HEADS * HEAD_DIM]


def _qkv_ctx(x, mod4, layer, g_mix, w_all):
    n = w_all.shape[1]
    bf = jax.ShapeDtypeStruct((T_CTX, D), BF16)
    tok = pl.BlockSpec((QKV_TM, D), lambda i: (i, 0))
    return pl.pallas_call(
        _qkv_ctx_kernel,
        grid=(T_CTX // QKV_TM,),
        in_specs=[tok, _mod_spec(layer, QKV_TM), _row_spec(), _full_spec((D, n))],
        out_specs=[tok, tok, tok, pl.BlockSpec((QKV_TM, 512), lambda i: (i, 0))],
        out_shape=[bf, bf, bf, jax.ShapeDtypeStruct((T_CTX, 512), F32)],
        compiler_params=_cparams(("arbitrary",)),
        name="qkv_ctx",
    )(x, mod4, g_mix.reshape(1, D), w_all)


def _qkv_lat_kernel(x_ref, mod_ref, g_ref, w_ref, cos_ref, sin_ref, q_ref, k_ref, v_ref):
    sh = mod_ref[:, 0:D]
    sc = mod_ref[:, D:2 * D]
    for c in range(QKV_TM // ROW_CHUNK):
        r0 = c * ROW_CHUNK
        h = _norm_mod(x_ref[r0:r0 + ROW_CHUNK, :], g_ref[...], sc, sh).astype(BF16)
        p = _dot(h, w_ref[...])
        cos = cos_ref[r0:r0 + ROW_CHUNK, :]
        sin = sin_ref[r0:r0 + ROW_CHUNK, :]
        q_ref[r0:r0 + ROW_CHUNK, :] = (p[:, 0:D] * cos + p[:, D:2 * D] * sin).astype(BF16)
        k_ref[r0:r0 + ROW_CHUNK, :] = (p[:, 2 * D:3 * D] * cos + p[:, 3 * D:4 * D] * sin).astype(BF16)
        v_ref[r0:r0 + ROW_CHUNK, :] = p[:, 4 * D:5 * D].astype(BF16)


def _qkv_lat(x, mod4, layer, g_mix, w_all, cos_t, sin_t):
    n = w_all.shape[1]
    nct = T_CTX // QKV_TM
    lpb = DEC_SEQ // QKV_TM
    bf = jax.ShapeDtypeStruct((T_LAT, D), BF16)
    tok_out = pl.BlockSpec((QKV_TM, D), lambda i: (i, 0))
    rope = pl.BlockSpec((QKV_TM, D), lambda i: (i % lpb, 0))
    return pl.pallas_call(
        _qkv_lat_kernel,
        grid=(T_LAT // QKV_TM,),
        in_specs=[pl.BlockSpec((QKV_TM, D), lambda i: (i + nct, 0)),
                  pl.BlockSpec((None, None, 1, N_MOD * D), lambda i: (layer, 8 + i // lpb, 0, 0)),
                  _row_spec(), _full_spec((D, n)), rope, rope],
        out_specs=[tok_out, tok_out, tok_out],
        out_shape=[bf, bf, bf],
        compiler_params=_cparams(("arbitrary",)),
        name="qkv_lat",
    )(x, mod4, g_mix.reshape(1, D), w_all, cos_t, sin_t)


KVW = Q_PER_KV * HEAD_DIM


def _head_masks(rows):
    lane = lax.broadcasted_iota(I32, (rows, KVW), 1)
    return [jnp.logical_and(lane >= g * HEAD_DIM, lane < (g + 1) * HEAD_DIM) for g in range(Q_PER_KV)]


def _attn_ctx_kernel(sink_ref, q_ref, k_ref, v_ref, o_ref):
    scale = HEAD_DIM ** -0.5
    masks = _head_masks(SEQ)
    for kv in range(N_KV_HEADS):
        c0 = kv * KVW
        q = q_ref[:, c0:c0 + KVW]
        k = k_ref[:, c0:c0 + KVW]
        v = v_ref[:, c0:c0 + KVW]
        acc = jnp.zeros((SEQ, KVW), F32)
        for g in range(Q_PER_KV):
            sink = sink_ref[kv * Q_PER_KV + g]
            qg = jnp.where(masks[g], q, jnp.zeros_like(q))
            s = _dot_nt(qg, k) * scale
            m = jnp.maximum(jnp.max(s, axis=-1, keepdims=True), sink)
            e = jnp.exp(s - m)
            den = jnp.sum(e, axis=-1, keepdims=True) + jnp.exp(sink - m)
            og = _dot(e.astype(BF16), v) / den
            acc = jnp.where(masks[g], og, acc)
        o_ref[:, c0:c0 + KVW] = acc.astype(BF16)


def _attn_ctx(sink, q, k, v):
    tok = pl.BlockSpec((SEQ, D), lambda b, *_: (b, 0))
    return pl.pallas_call(
        _attn_ctx_kernel,
        grid_spec=pltpu.PrefetchScalarGridSpec(
            num_scalar_prefetch=1, grid=(BATCH,),
            in_specs=[tok, tok, tok], out_specs=tok),
        out_shape=jax.ShapeDtypeStruct((T_CTX, D), BF16),
        compiler_params=_cparams(("arbitrary",)),
        name="attn_ctx",
    )(sink, q, k, v)


ATT_TQ = 128
ATT_SPAN = ATT_TQ + 2 * WINDOW


def _attn_lat_kernel(sink_ref, q_ref, k_ref, v_ref, ck_ref, cv_ref, o_ref):
    qb = pl.program_id(1)
    scale = HEAD_DIM ** -0.5
    w0 = pl.multiple_of(jnp.clip(qb * ATT_TQ - WINDOW, 0, DEC_SEQ - ATT_SPAN), ATT_TQ)
    rows = Q_PER_KV * ATT_TQ
    ridx = lax.broadcasted_iota(I32, (rows, ATT_SPAN), 0)
    qpos = qb * ATT_TQ + (ridx & (ATT_TQ - 1))
    kpos = w0 + lax.broadcasted_iota(I32, (rows, ATT_SPAN), 1)
    valid = jnp.abs(qpos - kpos) <= WINDOW
    rcol = lax.broadcasted_iota(I32, (rows, 1), 0)
    masks = _head_masks(ATT_TQ)
    for kv in range(N_KV_HEADS):
        c0 = kv * KVW
        q = q_ref[:, c0:c0 + KVW]
        qs = jnp.concatenate([jnp.where(masks[g], q, jnp.zeros_like(q)) for g in range(Q_PER_KV)], axis=0)
        sink = jnp.zeros((rows, 1), F32)
        for g in range(Q_PER_KV):
            sink = jnp.where(rcol >= g * ATT_TQ, sink_ref[kv * Q_PER_KV + g], sink)
        s_ctx = _dot_nt(qs, ck_ref[:, c0:c0 + KVW]) * scale
        s_win = _dot_nt(qs, k_ref[pl.ds(w0, ATT_SPAN), c0:c0 + KVW]) * scale
        s_win = jnp.where(valid, s_win, NEG_INF)
        m = jnp.maximum(jnp.maximum(jnp.max(s_ctx, axis=-1, keepdims=True),
                                    jnp.max(s_win, axis=-1, keepdims=True)), sink)
        e_ctx = jnp.exp(s_ctx - m)
        e_win = jnp.exp(s_win - m)
        den = (jnp.exp(sink - m) + jnp.sum(e_ctx, axis=-1, keepdims=True)
               + jnp.sum(e_win, axis=-1, keepdims=True))
        o = (_dot(e_ctx.astype(BF16), cv_ref[:, c0:c0 + KVW])
             + _dot(e_win.astype(BF16), v_ref[pl.ds(w0, ATT_SPAN), c0:c0 + KVW])) / den
        acc = jnp.zeros((ATT_TQ, KVW), F32)
        for g in range(Q_PER_KV):
            acc = jnp.where(masks[g], o[g * ATT_TQ:(g + 1) * ATT_TQ, :], acc)
        o_ref[:, c0:c0 + KVW] = acc.astype(BF16)


def _attn_lat(sink, q, k, v, ck, cv):
    nqb = DEC_SEQ // ATT_TQ
    qspec = pl.BlockSpec((ATT_TQ, D), lambda b, i, *_: (b * nqb + i, 0))
    seq = pl.BlockSpec((DEC_SEQ, D), lambda b, i, *_: (b, 0))
    ctx = pl.BlockSpec((PAST_LEN, D), lambda b, i, *_: (b, 0))
    return pl.pallas_call(
        _attn_lat_kernel,
        grid_spec=pltpu.PrefetchScalarGridSpec(
            num_scalar_prefetch=1, grid=(DEC_BATCH, nqb),
            in_specs=[qspec, seq, seq, ctx, ctx], out_specs=qspec),
        out_shape=jax.ShapeDtypeStruct((T_LAT, D), BF16),
        compiler_params=_cparams(("arbitrary", "arbitrary")),
        name="attn_lat",
    )(sink, q, k, v, ck, cv)


def _resproj_kernel(x_ref, a_ref, mod_ref, w_ref, o_ref):
    gate = mod_ref[:, 2 * D:3 * D]
    for c in range(x_ref.shape[0] // ROW_CHUNK):
        r0 = c * ROW_CHUNK
        y = _dot(a_ref[r0:r0 + ROW_CHUNK, :], w_ref[...])
        o_ref[r0:r0 + ROW_CHUNK, :] = x_ref[r0:r0 + ROW_CHUNK, :] + gate * y


def _resproj(x, a, mod4, layer, w):
    tok = pl.BlockSpec((TM, D), lambda i: (i, 0))
    return pl.pallas_call(
        _resproj_kernel,
        grid=(T_ALL // TM,),
        in_specs=[tok, tok, _mod_spec(layer, TM), _full_spec((D, D))],
        out_specs=tok,
        out_shape=jax.ShapeDtypeStruct((T_ALL, D), F32),
        compiler_params=_cparams(("arbitrary",)),
        name="attn_out_proj",
    )(x, a, mod4, w)


def _rope_tables():
    rows = DEC_SEQ // GRID_W
    row = jnp.repeat(jnp.arange(rows), GRID_W).astype(F32)
    col = jnp.tile(jnp.arange(GRID_W), rows).astype(F32)
    inv = ROPE_THETA ** (-jnp.arange(ROPE_FREQS, dtype=F32) / ROPE_FREQS)
    ang = jnp.concatenate([row[:, None] * inv, col[:, None] * inv], axis=-1)
    cos = jnp.cos(ang)
    sin = jnp.sin(ang)
    cos_h = jnp.concatenate([cos, cos], axis=-1)
    sin_h = jnp.concatenate([sin, sin], axis=-1)
    return jnp.tile(cos_h, (1, N_HEADS)), jnp.tile(sin_h, (1, N_HEADS))


def _rot_half_cols(w):
    k = w.shape[0]
    w4 = w.reshape(k, -1, 2, HEAD_DIM // 2)
    return jnp.stack([-w4[:, :, 1], w4[:, :, 0]], axis=2).reshape(k, -1)


def _expand_kv_cols(w):
    k = w.shape[0]
    w3 = w.reshape(k, N_KV_HEADS, 1, HEAD_DIM)
    return jnp.broadcast_to(w3, (k, N_KV_HEADS, Q_PER_KV, HEAD_DIM)).reshape(k, N_HEADS * HEAD_DIM)


FG = D // FNET_GROUPS


def _fnet_kernel(x_ref, mod_ref, g_ref, cs_ref, fl_ref, w_ref, o_ref, ab_s):
    sh = mod_ref[:, 0:D]
    sc = mod_ref[:, D:2 * D]
    gate = mod_ref[:, 2 * D:3 * D]
    nchunk = TM // ROW_CHUNK
    for c in range(nchunk):
        r0 = c * ROW_CHUNK
        h = _norm_mod(x_ref[r0:r0 + ROW_CHUNK, :], g_ref[...], sc, sh).astype(BF16)
        for g in range(FNET_GROUPS):
            ab = _dot(h[:, g * FG:(g + 1) * FG], cs_ref[...])
            ab_s[r0:r0 + ROW_CHUNK, g * FG:(g + 1) * FG] = ab[:, 0:FG].astype(BF16)
            ab_s[TM + r0:TM + r0 + ROW_CHUNK, g * FG:(g + 1) * FG] = ab[:, FG:2 * FG].astype(BF16)
    for c in range(nchunk):
        r0 = c * ROW_CHUNK
        f = _dot(fl_ref[r0:r0 + ROW_CHUNK, :], ab_s[...])
        y = _dot(f.astype(BF16), w_ref[...])
        o_ref[r0:r0 + ROW_CHUNK, :] = x_ref[r0:r0 + ROW_CHUNK, :] + gate * y


def _fnet_mixer(x, mod4, layer, g_mix, cs, fl, w_out):
    nct = T_CTX // TM
    tok = pl.BlockSpec((TM, D), lambda i: (i, 0))
    return pl.pallas_call(
        _fnet_kernel,
        grid=(T_ALL // TM,),
        in_specs=[tok, _mod_spec(layer, TM), _row_spec(), _full_spec((FG, 2 * FG)),
                  pl.BlockSpec((None, TM, 2 * TM), lambda i: (jnp.where(i < nct, 0, 1), 0, 0)),
                  _full_spec((D, D))],
        out_specs=tok,
        out_shape=jax.ShapeDtypeStruct((T_ALL, D), F32),
        scratch_shapes=[pltpu.VMEM((2 * TM, D), BF16)],
        compiler_params=_cparams(("arbitrary",)),
        name="fnet_mixer",
    )(x, mod4, g_mix.reshape(1, D), cs, fl, w_out)


def _dft_cos_sin(n):
    k = np.arange(n)
    ang = 2.0 * np.pi * ((k[:, None] * k[None, :]) % n) / n
    return np.cos(ang), np.sin(ang)


def _fnet_tables():
    cc, sc = _dft_cos_sin(FG)
    cs = np.concatenate([cc, sc], axis=1) / math.sqrt(FG)
    mats = []
    for seq in (SEQ, DEC_SEQ):
        cl, sl = _dft_cos_sin(seq)
        reps = TM // seq
        eye = np.eye(reps)
        mats.append(np.concatenate([np.kron(eye, cl), -np.kron(eye, sl)], axis=1) / math.sqrt(seq))
    return jnp.asarray(cs, F32).astype(BF16), jnp.asarray(np.stack(mats), F32).astype(BF16)


ROUTER_TM = 512


def _router_kernel(x_ref, mod_ref, g_ref, wr_ref, h_ref, sel_ref, gates_ref):
    sh = mod_ref[:, 3 * D:4 * D]
    sc = mod_ref[:, 4 * D:5 * D]
    w_hi = wr_ref[0]
    w_lo = wr_ref[1]
    for c in range(ROUTER_TM // ROW_CHUNK):
        r0 = c * ROW_CHUNK
        h = _norm_mod(x_ref[r0:r0 + ROW_CHUNK, :], g_ref[...], sc, sh)
        h_ref[r0:r0 + ROW_CHUNK, :] = h
        h_hi, h_lo = _split_bf16(h)
        logits = _dot(h_hi, w_hi) + _dot(h_lo, w_hi) + _dot(h_hi, w_lo)
        lane = lax.broadcasted_iota(I32, logits.shape, 1)
        lg = jnp.where(lane < N_EXPERTS, logits, -jnp.inf)
        m1 = jnp.max(lg, axis=-1, keepdims=True)
        i1 = jnp.min(jnp.where(lg == m1, lane, LANES), axis=-1, keepdims=True)
        lg2 = jnp.where(lane == i1, -jnp.inf, lg)
        m2 = jnp.max(lg2, axis=-1, keepdims=True)
        i2 = jnp.min(jnp.where(lg2 == m2, lane, LANES), axis=-1, keepdims=True)
        e2 = jnp.exp(m2 - m1)
        den = 1.0 + e2
        sel_ref[r0:r0 + ROW_CHUNK, :] = jnp.where(lane == i1, 1, jnp.where(lane == i2, 2, 0)).astype(I32)
        gates_ref[r0:r0 + ROW_CHUNK, :] = jnp.where(lane == 0, 1.0 / den, jnp.where(lane == 1, e2 / den, 0.0))


def _router(x, mod4, layer, g_ffn, w_router):
    wr = jnp.zeros((D, LANES), F32).at[:, 0:N_EXPERTS].set(w_router)
    wr_hi = wr.astype(BF16)
    wr_lo = (wr - wr_hi.astype(F32)).astype(BF16)
    tok = pl.BlockSpec((ROUTER_TM, D), lambda i: (i, 0))
    nar = pl.BlockSpec((ROUTER_TM, LANES), lambda i: (i, 0))
    return pl.pallas_call(
        _router_kernel,
        grid=(T_ALL // ROUTER_TM,),
        in_specs=[tok, _mod_spec(layer, ROUTER_TM), _row_spec(), _full_spec((2, D, LANES))],
        out_specs=[tok, nar, nar],
        out_shape=[jax.ShapeDtypeStruct((T_ALL, D), F32),
                   jax.ShapeDtypeStruct((T_ALL, LANES), I32),
                   jax.ShapeDtypeStruct((T_ALL, LANES), F32)],
        compiler_params=_cparams(("arbitrary",)),
        name="moe_router",
    )(x, mod4, g_ffn.reshape(1, D), jnp.stack([wr_hi, wr_lo]))


def _route_plan(sel):
    s8 = sel[:, 0:N_EXPERTS]
    onehot = (s8 > 0).astype(I32)
    csum = jnp.cumsum(onehot, axis=0)
    rank = csum - onehot
    counts = csum[-1]
    padded = ((counts + TME - 1) // TME) * TME
    gend = jnp.cumsum(padded)
    gstart = gend - padded
    pos_te = gstart[None, :] + rank
    pos1 = jnp.sum(jnp.where(s8 == 1, pos_te, 0), axis=1).astype(I32)
    pos2 = jnp.sum(jnp.where(s8 == 2, pos_te, 0), axis=1).astype(I32)
    tok = jnp.arange(T_ALL, dtype=I32)
    src = jnp.zeros((P_MAX,), I32).at[pos1].set(tok).at[pos2].set(tok)
    tile_start = jnp.arange(N_EXPERT_TILES, dtype=I32) * TME
    tile_valid = (tile_start < gend[-1]).astype(I32)
    n_valid = gend[-1] // TME
    last_start = jnp.maximum(n_valid - 1, 0) * TME
    eff_start = jnp.where(tile_valid > 0, tile_start, last_start)
    tile_expert = jnp.minimum(jnp.sum((eff_start[:, None] >= gend[None, :]).astype(I32), axis=1),
                              N_EXPERTS - 1).astype(I32)
    return pos1, pos2, src, tile_expert, tile_valid


def _gather_kernel(src_ref, valid_ref, h_hbm, o_ref, sem):
    m = pl.program_id(0)

    @pl.when(valid_ref[m] > 0)
    def _():
        def body(r, carry):
            t = src_ref[m * TME + r]
            pltpu.make_async_copy(h_hbm.at[pl.ds(t, 1), :], o_ref.at[pl.ds(r, 1), :], sem).start()
            return carry
        lax.fori_loop(0, TME, body, 0, unroll=8)
        pltpu.make_async_copy(h_hbm.at[pl.ds(0, TME), :], o_ref, sem).wait()

    @pl.when(valid_ref[m] == 0)
    def _():
        o_ref[...] = jnp.zeros_like(o_ref)


def _gather_rows(src, tile_valid, h):
    return pl.pallas_call(
        _gather_kernel,
        grid_spec=pltpu.PrefetchScalarGridSpec(
            num_scalar_prefetch=2, grid=(N_EXPERT_TILES,),
            in_specs=[pl.BlockSpec(memory_space=pl.ANY)],
            out_specs=pl.BlockSpec((TME, D), lambda m, *_: (m, 0)),
            scratch_shapes=[pltpu.SemaphoreType.DMA(())]),
        out_shape=jax.ShapeDtypeStruct((P_MAX, D), F32),
        compiler_params=_cparams(("arbitrary",)),
        name="moe_dispatch_gather",
    )(src, tile_valid, h)


EXP_TF = 512
EXP_NF = D_FF_EXPERT // EXP_TF


def _expert_kernel(te_ref, valid_ref, xs_ref, wg_ref, wu_ref, wd_ref, o_ref, xb_s):
    m = pl.program_id(0)
    f = pl.program_id(1)
    nchunk = TME // ROW_CHUNK

    @pl.when(valid_ref[m] > 0)
    def _():
        @pl.when(f == 0)
        def _():
            for c in range(nchunk):
                r0 = c * ROW_CHUNK
                xb_s[r0:r0 + ROW_CHUNK, :] = xs_ref[r0:r0 + ROW_CHUNK, :].astype(BF16)

        wg = wg_ref[...].astype(BF16)
        wu = wu_ref[...].astype(BF16)
        wd = wd_ref[...].astype(BF16)
        for c in range(nchunk):
            r0 = c * ROW_CHUNK
            xb = xb_s[r0:r0 + ROW_CHUNK, :]
            gg = _dot(xb, wg)
            uu = _dot(xb, wu)
            a = (gg * jax.nn.sigmoid(gg) * uu).astype(BF16)
            contrib = _dot(a, wd)

            @pl.when(f == 0)
            def _():
                o_ref[r0:r0 + ROW_CHUNK, :] = contrib

            @pl.when(f > 0)
            def _():
                o_ref[r0:r0 + ROW_CHUNK, :] = o_ref[r0:r0 + ROW_CHUNK, :] + contrib

    @pl.when(jnp.logical_and(valid_ref[m] == 0, f == 0))
    def _():
        o_ref[...] = jnp.zeros_like(o_ref)


def _expert_ffn(tile_expert, tile_valid, xs, w_gu, w_down):
    def feff(m, f, valid):
        return jnp.where(valid[m] > 0, f, EXP_NF - 1)
    return pl.pallas_call(
        _expert_kernel,
        grid_spec=pltpu.PrefetchScalarGridSpec(
            num_scalar_prefetch=2, grid=(N_EXPERT_TILES, EXP_NF),
            in_specs=[pl.BlockSpec((TME, D), lambda m, f, te, va: (m, 0)),
                      pl.BlockSpec((None, D, EXP_TF), lambda m, f, te, va: (te[m], 0, feff(m, f, va))),
                      pl.BlockSpec((None, D, EXP_TF), lambda m, f, te, va: (te[m], 0, EXP_NF + feff(m, f, va))),
                      pl.BlockSpec((None, EXP_TF, D), lambda m, f, te, va: (te[m], feff(m, f, va), 0))],
            out_specs=pl.BlockSpec((TME, D), lambda m, f, te, va: (m, 0)),
            scratch_shapes=[pltpu.VMEM((TME, D), BF16)]),
        out_shape=jax.ShapeDtypeStruct((P_MAX, D), F32),
        compiler_params=_cparams(("arbitrary", "arbitrary")),
        name="moe_expert_swiglu",
    )(tile_expert, tile_valid, xs, w_gu, w_gu, w_down)


COMB_TM = 512


def _combine_kernel(p1_ref, p2_ref, x_ref, gates_ref, mod_ref, gf_ref, y_hbm, o_ref, a_s, b_s, sem, *, final):
    i = pl.program_id(0)

    def body(r, carry):
        t = i * COMB_TM + r
        pltpu.make_async_copy(y_hbm.at[pl.ds(p1_ref[t], 1), :], a_s.at[pl.ds(r, 1), :], sem.at[0]).start()
        pltpu.make_async_copy(y_hbm.at[pl.ds(p2_ref[t], 1), :], b_s.at[pl.ds(r, 1), :], sem.at[1]).start()
        return carry
    lax.fori_loop(0, COMB_TM, body, 0, unroll=8)
    pltpu.make_async_copy(y_hbm.at[pl.ds(0, COMB_TM), :], a_s, sem.at[0]).wait()
    pltpu.make_async_copy(y_hbm.at[pl.ds(0, COMB_TM), :], b_s, sem.at[1]).wait()

    gate = mod_ref[:, 5 * D:6 * D]
    for c in range(COMB_TM // ROW_CHUNK):
        r0 = c * ROW_CHUNK
        w1 = gates_ref[r0:r0 + ROW_CHUNK, 0:1]
        w2 = gates_ref[r0:r0 + ROW_CHUNK, 1:2]
        y = w1 * a_s[r0:r0 + ROW_CHUNK, :] + w2 * b_s[r0:r0 + ROW_CHUNK, :]
        xn = x_ref[r0:r0 + ROW_CHUNK, :] + gate * y
        if final:
            ms = jnp.mean(xn * xn, axis=-1, keepdims=True)
            xn = xn * lax.rsqrt(ms + EPS) * gf_ref[...]
        o_ref[r0:r0 + ROW_CHUNK, :] = xn


def _combine(pos1, pos2, x, gates, mod4, layer, g_final, y, final):
    tok = pl.BlockSpec((COMB_TM, D), lambda i, *_: (i, 0))
    return pl.pallas_call(
        functools.partial(_combine_kernel, final=final),
        grid_spec=pltpu.PrefetchScalarGridSpec(
            num_scalar_prefetch=2, grid=(T_ALL // COMB_TM,),
            in_specs=[tok, pl.BlockSpec((COMB_TM, LANES), lambda i, *_: (i, 0)),
                      _mod_spec(layer, COMB_TM), _row_spec(),
                      pl.BlockSpec(memory_space=pl.ANY)],
            out_specs=tok,
            scratch_shapes=[pltpu.VMEM((COMB_TM, D), F32), pltpu.VMEM((COMB_TM, D), F32),
                            pltpu.SemaphoreType.DMA((2,))]),
        out_shape=jax.ShapeDtypeStruct((T_ALL, D), F32),
        compiler_params=_cparams(("arbitrary",)),
        name="moe_combine_final" if final else "moe_combine",
    )(pos1, pos2, x, gates, mod4, g_final.reshape(1, D), y)


def _moe_layer(x, mod4, layer, g_ffn, w_router, w_gu, w_down, g_final, final):
    h, sel, gates = _router(x, mod4, layer, g_ffn, w_router)
    pos1, pos2, src, tile_expert, tile_valid = _route_plan(sel)
    xs = _gather_rows(src, tile_valid, h)
    ys = _expert_ffn(tile_expert, tile_valid, xs, w_gu, w_down)
    return _combine(pos1, pos2, x, gates, mod4, layer, g_final, ys, final)


def kernel(x_prompt, x_sample, cache_k, cache_v, state_ssm_re, state_ssm_im, c, c_ctx, w_ada, b_ada, g_mix, g_ffn, g_final, conv_w_in, conv_w, conv_w_out, s5_lambda_re, s5_lambda_im, s5_b_re, s5_b_im, s5_c_re, s5_c_im, s5_log_dt, s5_d, s5_w_glu, attn_w_q, attn_w_kv, attn_w_o, attn_sink, fnet_w_out, ffn_w_gu, ffn_w_down, moe_w_router, moe_w_gu, moe_w_down):
    x = jnp.concatenate([x_prompt.reshape(T_CTX, D), x_sample.reshape(T_LAT, D)], axis=0)
    cond16 = jnp.concatenate([jnp.broadcast_to(c_ctx[None, :], (8, D)), c], axis=0)
    mod3 = _mod_all(cond16, w_ada, b_ada)
    mod4 = mod3.reshape(DEPTH, 16, 1, N_MOD * D)

    x = _conv_mixer(x, mod4, 0, g_mix[0], conv_w_in[0].astype(BF16), conv_w[0], conv_w_out[0].astype(BF16))
    x = _dense_ffn(x, mod4, 0, g_ffn[0], ffn_w_gu[0].astype(BF16), ffn_w_down[0].astype(BF16))

    lbr, lbi, bbr, bbi = _s5_prep(s5_lambda_re[0], s5_lambda_im[0], s5_log_dt[0], s5_b_re[0], s5_b_im[0])
    w_hi, w_lo, w_c, lam_s = _s5_weights(lbr, lbi, bbr, bbi, s5_c_re[0], s5_c_im[0])
    w_glu = s5_w_glu[0].astype(BF16)
    xc = x[0:T_CTX].reshape(BATCH, SEQ, D).transpose(1, 0, 2).reshape(T_CTX, D)
    xl = x[T_CTX:].reshape(DEC_BATCH, DEC_SEQ, D).transpose(1, 0, 2).reshape(T_LAT, D)
    h0_ctx = jnp.zeros((2, S5_JT, BATCH, 2 * S5_HALF), F32)
    h0_lat = _s5_state_in(state_ssm_re[:, 0], state_ssm_im[:, 0])
    yc, fin_c = _s5_scan(xc, BATCH, mod3, 1, 0, g_mix[1], w_hi, w_lo, w_c, lam_s, h0_ctx)
    yl, _ = _s5_scan(xl, DEC_BATCH, mod3, 1, 1, g_mix[1], w_hi, w_lo, w_c, lam_s, h0_lat)
    xc = _s5_glu(xc, yc, mod3, 1, 0, g_mix[1], s5_d[0], w_glu)
    xl = _s5_glu(xl, yl, mod3, 1, 1, g_mix[1], s5_d[0], w_glu)
    x = jnp.concatenate([xc.reshape(SEQ, BATCH, D).transpose(1, 0, 2).reshape(T_CTX, D),
                         xl.reshape(DEC_SEQ, DEC_BATCH, D).transpose(1, 0, 2).reshape(T_LAT, D)], axis=0)
    new_re, new_im = _s5_state_out(fin_c)
    x = _moe_layer(x, mod4, 1, g_ffn[1], moe_w_router[0], moe_w_gu[0], moe_w_down[0], g_final, False)

    wq = attn_w_q[0]
    wk = _expand_kv_cols(attn_w_kv[0][:, 0:N_KV_HEADS * HEAD_DIM])
    wv = _expand_kv_cols(attn_w_kv[0][:, N_KV_HEADS * HEAD_DIM:])
    w_ctx = jnp.concatenate([wq, wk, wv, attn_w_kv[0]], axis=1).astype(BF16)
    w_lat = jnp.concatenate([wq, _rot_half_cols(wq), wk, _rot_half_cols(wk), wv], axis=1).astype(BF16)
    cos_t, sin_t = _rope_tables()
    q_c, k_c, v_c, kv_c = _qkv_ctx(x, mod4, 2, g_mix[2], w_ctx)
    q_l, k_l, v_l = _qkv_lat(x, mod4, 2, g_mix[2], w_lat, cos_t, sin_t)
    sink = attn_sink[0]
    o_c = _attn_ctx(sink, q_c, k_c, v_c)
    expand = lambda a: jnp.broadcast_to(
        a.reshape(DEC_BATCH * PAST_LEN, N_KV_HEADS, 1, HEAD_DIM),
        (DEC_BATCH * PAST_LEN, N_KV_HEADS, Q_PER_KV, HEAD_DIM)).reshape(DEC_BATCH * PAST_LEN, D).astype(BF16)
    o_l = _attn_lat(sink, q_l, k_l, v_l, expand(cache_k[:, 0]), expand(cache_v[:, 0]))
    x = _resproj(x, jnp.concatenate([o_c, o_l], axis=0), mod4, 2, attn_w_o[0].astype(BF16))
    kvw = N_KV_HEADS * HEAD_DIM
    new_k = kv_c[:, 0:kvw].reshape(BATCH, 1, SEQ, N_KV_HEADS, HEAD_DIM)
    new_v = kv_c[:, kvw:].reshape(BATCH, 1, SEQ, N_KV_HEADS, HEAD_DIM)
    x = _dense_ffn(x, mod4, 2, g_ffn[2], ffn_w_gu[1].astype(BF16), ffn_w_down[1].astype(BF16))

    cs, fl = _fnet_tables()
    x = _fnet_mixer(x, mod4, 3, g_mix[3], cs, fl, fnet_w_out[0].astype(BF16))
    x = _moe_layer(x, mod4, 3, g_ffn[3], moe_w_router[1], moe_w_gu[1], moe_w_down[1], g_final, True)

    y_prompt = x[0:T_CTX].reshape(BATCH, SEQ, D)
    y_sample = x[T_CTX:].reshape(DEC_BATCH, DEC_SEQ, D)
    return (y_prompt, y_sample, new_k, new_v, new_re[:, None], new_im[:, None])
```

```python
import functools
import math

import numpy as np
import jax
import jax.numpy as jnp
from jax import lax
from jax.experimental import pallas as pl
from jax.experimental.pallas import tpu as pltpu

F32 = jnp.float32
BF16 = jnp.bfloat16
I32 = jnp.int32

D = 1024
BATCH = 16
SEQ = 256
DEPTH = 4
DEC_BATCH = 8
DEC_SEQ = 1024
PAST_LEN = 256
GRID_W = 64
EPS = 1e-6
N_MOD = 6
S5_GROUP = 16
S5_GROUPS = D // S5_GROUP
S5_STATE = 64
HEAD_DIM = 64
N_HEADS = D // HEAD_DIM
N_KV_HEADS = 4
Q_PER_KV = N_HEADS // N_KV_HEADS
WINDOW = 128
ROPE_THETA = 10000.0
ROPE_FREQS = HEAD_DIM // 4
FNET_GROUPS = 4
D_FF = 2816
N_EXPERTS = 8
TOP_K = 2
D_FF_EXPERT = 3584
NEG_INF = -1e30

T_CTX = BATCH * SEQ
T_LAT = DEC_BATCH * DEC_SEQ
T_ALL = T_CTX + T_LAT

VMEM_LIMIT_V7X = 56 * 1024 * 1024
LANES = 128

TM = 1024
ROW_CHUNK = 256
MM_CHUNK = 512
TME = 1024
N_EXPERT_TILES = (T_ALL * TOP_K) // TME + N_EXPERTS
P_MAX = N_EXPERT_TILES * TME


def _cparams(sem):
    return pltpu.CompilerParams(dimension_semantics=sem, vmem_limit_bytes=VMEM_LIMIT_V7X)


def _dot(a, b):
    return jnp.dot(a, b, preferred_element_type=F32)


def _dot_nt(a, b):
    return lax.dot_general(a, b, (((1,), (1,)), ((), ())), preferred_element_type=F32)


def _split_bf16(a):
    hi = a.astype(BF16)
    lo = (a - hi.astype(F32)).astype(BF16)
    return hi, lo


def _tile_rows(a, reps, axis=0):
    assert axis == 0
    return jnp.concatenate([a] * reps, axis=0)


def _norm_mod(x, g, sc, sh):
    ms = jnp.mean(x * x, axis=-1, keepdims=True)
    y = x * lax.rsqrt(ms + EPS) * g
    return y * (1.0 + sc) + sh


def _mod_row(i, tm):
    nct = T_CTX // tm
    lpb = DEC_SEQ // tm
    return jnp.where(i < nct, 0, 8 + (i - nct) // lpb)


def _mod_spec(layer, tm):
    return pl.BlockSpec((None, None, 1, N_MOD * D), lambda i, *_: (layer, _mod_row(i, tm), 0, 0))


def _row_spec():
    return pl.BlockSpec((1, D), lambda *_: (0, 0))


def _full_spec(shape):
    nd = len(shape)
    return pl.BlockSpec(shape, lambda *_: (0,) * nd, pipeline_mode=pl.Buffered(1))


def _mod_kernel(c_ref, w_ref, b_ref, o_ref):
    c = c_ref[...]
    s = (c * jax.nn.sigmoid(c)).astype(BF16)
    o_ref[...] = _dot(s, w_ref[...].astype(BF16)) + b_ref[...]


def _mod_all(cond16, w_ada, b_ada):
    tn = 1024
    return pl.pallas_call(
        _mod_kernel,
        grid=(DEPTH, N_MOD * D // tn),
        in_specs=[pl.BlockSpec((16, D), lambda l, n: (0, 0)),
                  pl.BlockSpec((None, D, tn), lambda l, n: (l, 0, n)),
                  pl.BlockSpec((None, 1, tn), lambda l, n: (l, 0, n))],
        out_specs=pl.BlockSpec((None, 16, tn), lambda l, n: (l, 0, n)),
        out_shape=jax.ShapeDtypeStruct((DEPTH, 16, N_MOD * D), F32),
        compiler_params=_cparams(("arbitrary", "arbitrary")),
        name="adaln_mod",
    )(cond16, w_ada, b_ada.reshape(DEPTH, 1, N_MOD * D))


def _conv_kernel(x_ref, mod_ref, g_ref, win_ref, cw_ref, wout_ref, o_ref, gb_s, u_s, z_s):
    i = pl.program_id(0)
    is_ctx = i < (T_CTX // TM)
    sh = mod_ref[:, 0:D]
    sc = mod_ref[:, D:2 * D]
    gate = mod_ref[:, 2 * D:3 * D]
    g = g_ref[...]
    zero8 = jnp.zeros((8, D), F32)
    u_s[0:8, :] = zero8
    u_s[8 + TM:16 + TM, :] = zero8
    for c in range(TM // MM_CHUNK):
        r0 = c * MM_CHUNK
        h = _norm_mod(x_ref[r0:r0 + MM_CHUNK, :], g, sc, sh).astype(BF16)
        proj = _dot(h, win_ref[...])
        gb_s[r0:r0 + MM_CHUNK, :] = proj[:, 0:D]
        u_s[8 + r0:8 + r0 + MM_CHUNK, :] = proj[:, D:2 * D] * proj[:, 2 * D:3 * D]
    row = lax.broadcasted_iota(I32, (ROW_CHUNK, 1), 0)
    first = jnp.logical_and(is_ctx, row == 0)
    last = jnp.logical_and(is_ctx, row == ROW_CHUNK - 1)
    for c in range(TM // ROW_CHUNK):
        r0 = c * ROW_CHUNK
        up = jnp.where(first, 0.0, u_s[7 + r0:7 + r0 + ROW_CHUNK, :])
        mid = u_s[8 + r0:8 + r0 + ROW_CHUNK, :]
        dn = jnp.where(last, 0.0, u_s[9 + r0:9 + r0 + ROW_CHUNK, :])
        conv = up * cw_ref[0:1, :] + mid * cw_ref[1:2, :] + dn * cw_ref[2:3, :]
        z_s[r0:r0 + ROW_CHUNK, :] = (gb_s[r0:r0 + ROW_CHUNK, :] * conv).astype(BF16)
    for c in range(TM // MM_CHUNK):
        r0 = c * MM_CHUNK
        y = _dot(z_s[r0:r0 + MM_CHUNK, :], wout_ref[...])
        o_ref[r0:r0 + MM_CHUNK, :] = x_ref[r0:r0 + MM_CHUNK, :] + gate * y


def _conv_mixer(x, mod4, layer, g_mix, w_in, conv_w, w_out):
    assert SEQ == ROW_CHUNK and DEC_SEQ == TM
    return pl.pallas_call(
        _conv_kernel,
        grid=(T_ALL // TM,),
        in_specs=[pl.BlockSpec((TM, D), lambda i: (i, 0)),
                  _mod_spec(layer, TM), _row_spec(),
                  _full_spec((D, 3 * D)), _full_spec((3, D)), _full_spec((D, D))],
        out_specs=pl.BlockSpec((TM, D), lambda i: (i, 0)),
        out_shape=jax.ShapeDtypeStruct((T_ALL, D), F32),
        scratch_shapes=[pltpu.VMEM((TM, D), F32), pltpu.VMEM((TM + 16, D), F32), pltpu.VMEM((TM, D), BF16)],
        compiler_params=_cparams(("arbitrary",)),
        name="conv_mixer",
    )(x, mod4, g_mix.reshape(1, D), w_in, conv_w, w_out)


FFN_TF = D_FF // 2


def _ffn_kernel(x_ref, mod_ref, g_ref, wg_ref, wu_ref, wd_ref, o_ref, h_s):
    f = pl.program_id(1)
    nf = pl.num_programs(1)
    sh = mod_ref[:, 3 * D:4 * D]
    sc = mod_ref[:, 4 * D:5 * D]
    gate = mod_ref[:, 5 * D:6 * D]
    nchunk = x_ref.shape[0] // ROW_CHUNK

    @pl.when(f == 0)
    def _():
        for c in range(nchunk):
            r0 = c * ROW_CHUNK
            h_s[r0:r0 + ROW_CHUNK, :] = _norm_mod(x_ref[r0:r0 + ROW_CHUNK, :], g_ref[...], sc, sh).astype(BF16)

    hc = h_s[...]
    gg = _dot(hc, wg_ref[...])
    uu = _dot(hc, wu_ref[...])
    a = (gg * jax.nn.sigmoid(gg) * uu).astype(BF16)
    contrib = _dot(a, wd_ref[...])

    @pl.when(f == 0)
    def _():
        o_ref[...] = contrib

    @pl.when(jnp.logical_and(f > 0, f < nf - 1))
    def _():
        o_ref[...] = o_ref[...] + contrib

    @pl.when(f == nf - 1)
    def _():
        o_ref[...] = x_ref[...] + gate * (o_ref[...] + contrib)


def _dense_ffn(x, mod4, layer, g_ffn, w_gu, w_down):
    nf = D_FF // FFN_TF
    return pl.pallas_call(
        _ffn_kernel,
        grid=(T_ALL // TM, nf),
        in_specs=[pl.BlockSpec((TM, D), lambda i, f: (i, 0)),
                  _mod_spec(layer, TM), _row_spec(),
                  pl.BlockSpec((D, FFN_TF), lambda i, f: (0, f)),
                  pl.BlockSpec((D, FFN_TF), lambda i, f: (0, nf + f)),
                  pl.BlockSpec((FFN_TF, D), lambda i, f: (f, 0))],
        out_specs=pl.BlockSpec((TM, D), lambda i, f: (i, 0)),
        out_shape=jax.ShapeDtypeStruct((T_ALL, D), F32),
        scratch_shapes=[pltpu.VMEM((TM, D), BF16)],
        compiler_params=_cparams(("arbitrary", "arbitrary")),
        name="dense_swiglu",
    )(x, mod4, g_ffn.reshape(1, D), w_gu, w_gu, w_down)


def _s5_prep_kernel(lr_ref, li_ref, ldt_ref, br_ref, bi_ref, lbr_ref, lbi_ref, bbr_ref, bbi_ref):
    lr = lr_ref[...]
    li = li_ref[...]
    dt = jnp.exp(ldt_ref[...])
    mag = jnp.exp(lr * dt)
    ar = mag * jnp.cos(li * dt)
    ai = mag * jnp.sin(li * dt)
    nr = ar - 1.0
    den = lr * lr + li * li
    fr = (nr * lr + ai * li) / den
    fi = (ai * lr - nr * li) / den
    br = br_ref[...]
    bi = bi_ref[...]
    lbr_ref[...] = ar
    lbi_ref[...] = ai
    bbr_ref[...] = fr * br - fi * bi
    bbi_ref[...] = fr * bi + fi * br


def _s5_prep(lam_re, lam_im, log_dt, b_re, b_im):
    rows = 2 * S5_GROUPS
    cols = S5_STATE * S5_GROUP
    exp = lambda a: jnp.repeat(a.reshape(rows, S5_STATE), S5_GROUP, axis=1)
    ldt = jnp.broadcast_to(log_dt.reshape(rows, 1), (rows, cols))
    outs = pl.pallas_call(
        _s5_prep_kernel,
        out_shape=[jax.ShapeDtypeStruct((rows, cols), F32)] * 4,
        name="s5_discretize",
    )(exp(lam_re), exp(lam_im), ldt, b_re.reshape(rows, cols), b_im.reshape(rows, cols))
    lbr, lbi, bbr, bbi = outs
    shp = (2, S5_GROUPS, S5_STATE, S5_GROUP)
    return lbr.reshape(shp)[..., 0], lbi.reshape(shp)[..., 0], bbr.reshape(shp), bbi.reshape(shp)


S5_JT = 8
S5_GPT = LANES // S5_GROUP
S5_HALF = S5_GPT * S5_STATE
S5_ROWS = 512


def _s5_scan_kernel(x_ref, mod_ref, g_ref, wb_ref, wc_ref, lam_ref, h0_ref,
                    y_ref, fin_ref, bu_s, st_s, *, nb, jgroup):
    d = pl.program_id(0)
    c = pl.program_id(1)
    lc = S5_ROWS // nb

    @pl.when(c == 0)
    def _():
        st_s[...] = h0_ref[...]

    rep = S5_ROWS // 8
    sh = _tile_rows(mod_ref[:, 0:D], rep, axis=0)
    sc = _tile_rows(mod_ref[:, D:2 * D], rep, axis=0)
    u = _norm_mod(x_ref[...], g_ref[...], sc, sh).astype(BF16)
    for j in range(S5_JT):
        bu_s[j] = _dot(u[:, j * LANES:(j + 1) * LANES], wb_ref[j])

    for j0 in range(0, S5_JT, jgroup):
        js = list(range(j0, j0 + jgroup))
        lam = [(jnp.broadcast_to(lam_ref[j][:, 0:S5_HALF], (nb, S5_HALF)),
                jnp.broadcast_to(lam_ref[j][:, S5_HALF:], (nb, S5_HALF))) for j in js]

        def body(t, carry):
            l = jnp.where(d == 0, t, lc - 1 - t)
            r0 = pl.multiple_of(l * nb, nb)
            out = []
            for k, j in enumerate(js):
                sr, si = carry[k]
                ar, ai = lam[k]
                bu = bu_s[j, pl.ds(r0, nb), :]
                hr = ar * sr - ai * si + bu[:, 0:S5_HALF]
                hi = ar * si + ai * sr + bu[:, S5_HALF:]
                bu_s[j, pl.ds(r0, nb), 0:S5_HALF] = hr
                bu_s[j, pl.ds(r0, nb), S5_HALF:] = hi
                out.append((hr, hi))
            return tuple(out)

        init = tuple((st_s[j][:, 0:S5_HALF], st_s[j][:, S5_HALF:]) for j in js)
        fin = lax.fori_loop(0, lc, body, init)
        for k, j in enumerate(js):
            st_s[j, :, 0:S5_HALF] = fin[k][0]
            st_s[j, :, S5_HALF:] = fin[k][1]

    for j in range(S5_JT):
        y_ref[:, j * LANES:(j + 1) * LANES] = _dot(bu_s[j].astype(BF16), wc_ref[j])
    fin_ref[...] = st_s[...]


def _s5_scan(xt, nb, mod3, layer, path, g_mix, w_b, w_c, lam_s, h0):
    rows = xt.shape[0]
    nc = rows // S5_ROWS
    chunk = lambda d, c: c + d * (nc - 1 - 2 * c)
    kern = functools.partial(_s5_scan_kernel, nb=nb, jgroup=2 if nb == 8 else 1)
    return pl.pallas_call(
        kern,
        grid=(2, nc),
        in_specs=[pl.BlockSpec((S5_ROWS, D), lambda d, c: (chunk(d, c), 0)),
                  pl.BlockSpec((None, 8, N_MOD * D), lambda d, c: (layer, path, 0)),
                  _row_spec(),
                  pl.BlockSpec((None, S5_JT, LANES, 2 * S5_HALF), lambda d, c: (d, 0, 0, 0)),
                  pl.BlockSpec((None, S5_JT, 2 * S5_HALF, LANES), lambda d, c: (d, 0, 0, 0)),
                  pl.BlockSpec((None, S5_JT, 1, 2 * S5_HALF), lambda d, c: (d, 0, 0, 0)),
                  pl.BlockSpec((None, S5_JT, nb, 2 * S5_HALF), lambda d, c: (d, 0, 0, 0))],
        out_specs=[pl.BlockSpec((None, S5_ROWS, D), lambda d, c: (d, chunk(d, c), 0)),
                   pl.BlockSpec((None, S5_JT, nb, 2 * S5_HALF), lambda d, c: (d, 0, 0, 0))],
        out_shape=[jax.ShapeDtypeStruct((2, rows, D), F32),
                   jax.ShapeDtypeStruct((2, S5_JT, nb, 2 * S5_HALF), F32)],
        scratch_shapes=[pltpu.VMEM((S5_JT, S5_ROWS, 2 * S5_HALF), F32),
                        pltpu.VMEM((S5_JT, nb, 2 * S5_HALF), F32)],
        compiler_params=_cparams(("arbitrary", "arbitrary")),
        name="s5_scan_b%d" % nb,
    )(xt, mod3, g_mix.reshape(1, D), w_b, w_c, lam_s, h0)


S5_GLU_ROWS = 512


def _s5_glu_kernel(x_ref, yf_ref, yb_ref, mod_ref, g_ref, dsk_ref, w_ref, o_ref):
    rep = MM_CHUNK // 8
    sh = _tile_rows(mod_ref[:, 0:D], rep, axis=0)
    sc = _tile_rows(mod_ref[:, D:2 * D], rep, axis=0)
    gate = _tile_rows(mod_ref[:, 2 * D:3 * D], rep, axis=0)
    for c in range(S5_GLU_ROWS // MM_CHUNK):
        r0 = c * MM_CHUNK
        x = x_ref[r0:r0 + MM_CHUNK, :]
        u = _norm_mod(x, g_ref[...], sc, sh)
        y = u * dsk_ref[...] + yf_ref[r0:r0 + MM_CHUNK, :] + yb_ref[r0:r0 + MM_CHUNK, :]
        z = jax.nn.gelu(y).astype(BF16)
        ag = _dot(z, w_ref[...])
        out = ag[:, 0:D] * jax.nn.sigmoid(ag[:, D:2 * D])
        o_ref[r0:r0 + MM_CHUNK, :] = x + gate * out


def _s5_glu(xt, y2, mod3, layer, path, g_mix, d_skip, w_glu):
    rows = xt.shape[0]
    return pl.pallas_call(
        _s5_glu_kernel,
        grid=(rows // S5_GLU_ROWS,),
        in_specs=[pl.BlockSpec((S5_GLU_ROWS, D), lambda i: (i, 0)),
                  pl.BlockSpec((None, S5_GLU_ROWS, D), lambda i: (0, i, 0)),
                  pl.BlockSpec((None, S5_GLU_ROWS, D), lambda i: (1, i, 0)),
                  pl.BlockSpec((None, 8, N_MOD * D), lambda i: (layer, path, 0)),
                  _row_spec(), _row_spec(), _full_spec((D, 2 * D))],
        out_specs=pl.BlockSpec((S5_GLU_ROWS, D), lambda i: (i, 0)),
        out_shape=jax.ShapeDtypeStruct((rows, D), F32),
        compiler_params=_cparams(("arbitrary",)),
        name="s5_glu",
    )(xt, y2, y2, mod3, g_mix.reshape(1, D), d_skip.reshape(1, D), w_glu)


def _s5_weights(lbr, lbi, bbr, bbi, c_re, c_im):
    eye = jnp.eye(S5_GPT, dtype=F32)
    bb = jnp.stack([bbr, bbi]).reshape(2, 2, S5_JT, S5_GPT, S5_STATE, S5_GROUP)
    w_bu = jnp.einsum('rdjgps,gh->djgsrhp', bb, eye).reshape(2, S5_JT, LANES, 2 * S5_HALF)
    w_b = w_bu.astype(BF16)
    cc =jnp.stack([c_re, -c_im]).reshape(2, 2, S5_JT, S5_GPT, S5_GROUP, S5_STATE)
    w_c = jnp.einsum('rdjgsp,gh->djrgphs', cc, eye).reshape(2, S5_JT, 2 * S5_HALF, LANES).astype(BF16)
    lam_s = jnp.concatenate([lbr.reshape(2, S5_JT, S5_HALF), lbi.reshape(2, S5_JT, S5_HALF)], axis=-1)
    return w_b, w_c, lam_s.reshape(2, S5_JT, 1, 2 * S5_HALF)


def _s5_state_in(st_re, st_im):
    def lay(a):
        b = a.shape[0]
        return a.transpose(1, 0, 2, 3).reshape(2, b, S5_JT, S5_HALF).transpose(0, 2, 1, 3)
    return jnp.concatenate([lay(st_re), lay(st_im)], axis=-1)


def _s5_state_out(fin):
    def lay(a):
        b = a.shape[2]
        return a.transpose(2, 0, 1, 3).reshape(b, 2, S5_GROUPS, S5_STATE)
    return lay(fin[..., 0:S5_HALF]), lay(fin[..., S5_HALF:])


QKV_TM = 512


def _qkv_ctx_kernel(x_ref, mod_ref, g_ref, w_ref, q_ref, k_ref, v_ref, kv_ref):
    sh = mod_ref[:, 0:D]
    sc = mod_ref[:, D:2 * D]
    for c in range(QKV_TM // MM_CHUNK):
        r0 = c * MM_CHUNK
        h = _norm_mod(x_ref[r0:r0 + MM_CHUNK, :], g_ref[...], sc, sh).astype(BF16)
        q_ref[r0:r0 + MM_CHUNK, :] = _dot(h, w_ref[:, 0:D]).astype(BF16)
        k_ref[r0:r0 + MM_CHUNK, :] = _dot(h, w_ref[:, D:2 * D]).astype(BF16)
        v_ref[r0:r0 + MM_CHUNK, :] = _dot(h, w_ref[:, 2 * D:3 * D]).astype(BF16)
        kv_ref[r0:r0 + MM_CHUNK, :] = _dot(h, w_ref[:, 3 * D:3 * D + 2 * N_KV_HEADS * HEAD_DIM])


def _qkv_ctx(x, mod4, layer, g_mix, w_all):
    n = w_all.shape[1]
    bf = jax.ShapeDtypeStruct((T_CTX, D), BF16)
    tok = pl.BlockSpec((QKV_TM, D), lambda i: (i, 0))
    return pl.pallas_call(
        _qkv_ctx_kernel,
        grid=(T_CTX // QKV_TM,),
        in_specs=[tok, _mod_spec(layer, QKV_TM), _row_spec(), _full_spec((D, n))],
        out_specs=[tok, tok, tok, pl.BlockSpec((QKV_TM, 512), lambda i: (i, 0))],
        out_shape=[bf, bf, bf, jax.ShapeDtypeStruct((T_CTX, 512), F32)],
        compiler_params=_cparams(("arbitrary",)),
        name="qkv_ctx",
    )(x, mod4, g_mix.reshape(1, D), w_all)


def _qkv_lat_kernel(x_ref, mod_ref, g_ref, w_ref, cos_ref, sin_ref, q_ref, k_ref, v_ref):
    sh = mod_ref[:, 0:D]
    sc = mod_ref[:, D:2 * D]
    for c in range(QKV_TM // MM_CHUNK):
        r0 = c * MM_CHUNK
        h = _norm_mod(x_ref[r0:r0 + MM_CHUNK, :], g_ref[...], sc, sh).astype(BF16)
        cos = cos_ref[r0:r0 + MM_CHUNK, :]
        sin = sin_ref[r0:r0 + MM_CHUNK, :]
        q_ref[r0:r0 + MM_CHUNK, :] = (_dot(h, w_ref[:, 0:D]) * cos + _dot(h, w_ref[:, D:2 * D]) * sin).astype(BF16)
        k_ref[r0:r0 + MM_CHUNK, :] = (_dot(h, w_ref[:, 2 * D:3 * D]) * cos
                                      + _dot(h, w_ref[:, 3 * D:4 * D]) * sin).astype(BF16)
        v_ref[r0:r0 + MM_CHUNK, :] = _dot(h, w_ref[:, 4 * D:5 * D]).astype(BF16)


def _qkv_lat(x, mod4, layer, g_mix, w_all, cos_t, sin_t):
    n = w_all.shape[1]
    nct = T_CTX // QKV_TM
    lpb = DEC_SEQ // QKV_TM
    bf = jax.ShapeDtypeStruct((T_LAT, D), BF16)
    tok_out = pl.BlockSpec((QKV_TM, D), lambda i: (i, 0))
    rope = pl.BlockSpec((QKV_TM, D), lambda i: (i % lpb, 0))
    return pl.pallas_call(
        _qkv_lat_kernel,
        grid=(T_LAT // QKV_TM,),
        in_specs=[pl.BlockSpec((QKV_TM, D), lambda i: (i + nct, 0)),
                  pl.BlockSpec((None, None, 1, N_MOD * D), lambda i: (layer, 8 + i // lpb, 0, 0)),
                  _row_spec(), _full_spec((D, n)), rope, rope],
        out_specs=[tok_out, tok_out, tok_out],
        out_shape=[bf, bf, bf],
        compiler_params=_cparams(("arbitrary",)),
        name="qkv_lat",
    )(x, mod4, g_mix.reshape(1, D), w_all, cos_t, sin_t)


KVW = Q_PER_KV * HEAD_DIM


def _head_masks(rows):
    lane = lax.broadcasted_iota(I32, (rows, KVW), 1)
    return [jnp.logical_and(lane >= g * HEAD_DIM, lane < (g + 1) * HEAD_DIM) for g in range(Q_PER_KV)]


def _attn_ctx_kernel(sink_ref, q_ref, k_ref, v_ref, o_ref):
    scale = HEAD_DIM ** -0.5
    masks = _head_masks(SEQ)
    for kv in range(N_KV_HEADS):
        c0 = kv * KVW
        q = q_ref[:, c0:c0 + KVW]
        k = k_ref[:, c0:c0 + KVW]
        v = v_ref[:, c0:c0 + KVW]
        acc = jnp.zeros((SEQ, KVW), F32)
        for g in range(Q_PER_KV):
            sink = sink_ref[kv * Q_PER_KV + g]
            qg = jnp.where(masks[g], q, jnp.zeros_like(q))
            s = _dot_nt(qg, k) * scale
            m = jnp.maximum(jnp.max(s, axis=-1, keepdims=True), sink)
            e = jnp.exp(s - m)
            den = jnp.sum(e, axis=-1, keepdims=True) + jnp.exp(sink - m)
            og = _dot(e.astype(BF16), v) / den
            acc = jnp.where(masks[g], og, acc)
        o_ref[:, c0:c0 + KVW] = acc.astype(BF16)


def _attn_ctx(sink, q, k, v):
    tok = pl.BlockSpec((SEQ, D), lambda b, *_: (b, 0))
    return pl.pallas_call(
        _attn_ctx_kernel,
        grid_spec=pltpu.PrefetchScalarGridSpec(
            num_scalar_prefetch=1, grid=(BATCH,),
            in_specs=[tok, tok, tok], out_specs=tok),
        out_shape=jax.ShapeDtypeStruct((T_CTX, D), BF16),
        compiler_params=_cparams(("arbitrary",)),
        name="attn_ctx",
    )(sink, q, k, v)


ATT_TQ = 128
ATT_SPAN = ATT_TQ + 2 * WINDOW


def _attn_lat_kernel(sink_ref, q_ref, k_ref, v_ref, ck_ref, cv_ref, o_ref):
    qb = pl.program_id(1)
    scale = HEAD_DIM ** -0.5
    w0 = pl.multiple_of(jnp.clip(qb * ATT_TQ - WINDOW, 0, DEC_SEQ - ATT_SPAN), ATT_TQ)
    rows = Q_PER_KV * ATT_TQ
    ridx = lax.broadcasted_iota(I32, (rows, ATT_SPAN), 0)
    qpos = qb * ATT_TQ + (ridx & (ATT_TQ - 1))
    kpos = w0 + lax.broadcasted_iota(I32, (rows, ATT_SPAN), 1)
    valid = jnp.abs(qpos - kpos) <= WINDOW
    rcol = lax.broadcasted_iota(I32, (rows, 1), 0)
    masks = _head_masks(ATT_TQ)
    for kv in range(N_KV_HEADS):
        c0 = kv * KVW
        q = q_ref[:, c0:c0 + KVW]
        qs = jnp.concatenate([jnp.where(masks[g], q, jnp.zeros_like(q)) for g in range(Q_PER_KV)], axis=0)
        sink = jnp.zeros((rows, 1), F32)
        for g in range(Q_PER_KV):
            sink = jnp.where(rcol >= g * ATT_TQ, sink_ref[kv * Q_PER_KV + g], sink)
        s_ctx = _dot_nt(qs, ck_ref[:, c0:c0 + KVW]) * scale
        s_win = _dot_nt(qs, k_ref[pl.ds(w0, ATT_SPAN), c0:c0 + KVW]) * scale
        s_win = jnp.where(valid, s_win, NEG_INF)
        m = jnp.maximum(jnp.maximum(jnp.max(s_ctx, axis=-1, keepdims=True),
                                    jnp.max(s_win, axis=-1, keepdims=True)), sink)
        e_ctx = jnp.exp(s_ctx - m)
        e_win = jnp.exp(s_win - m)
        den = (jnp.exp(sink - m) + jnp.sum(e_ctx, axis=-1, keepdims=True)
               + jnp.sum(e_win, axis=-1, keepdims=True))
        o = (_dot(e_ctx.astype(BF16), cv_ref[:, c0:c0 + KVW])
             + _dot(e_win.astype(BF16), v_ref[pl.ds(w0, ATT_SPAN), c0:c0 + KVW])) / den
        acc = jnp.zeros((ATT_TQ, KVW), F32)
        for g in range(Q_PER_KV):
            acc = jnp.where(masks[g], o[g * ATT_TQ:(g + 1) * ATT_TQ, :], acc)
        o_ref[:, c0:c0 + KVW] = acc.astype(BF16)


def _attn_lat(sink, q, k, v, ck, cv):
    nqb = DEC_SEQ // ATT_TQ
    qspec = pl.BlockSpec((ATT_TQ, D), lambda b, i, *_: (b * nqb + i, 0))
    seq = pl.BlockSpec((DEC_SEQ, D), lambda b, i, *_: (b, 0))
    ctx = pl.BlockSpec((PAST_LEN, D), lambda b, i, *_: (b, 0))
    return pl.pallas_call(
        _attn_lat_kernel,
        grid_spec=pltpu.PrefetchScalarGridSpec(
            num_scalar_prefetch=1, grid=(DEC_BATCH, nqb),
            in_specs=[qspec, seq, seq, ctx, ctx], out_specs=qspec),
        out_shape=jax.ShapeDtypeStruct((T_LAT, D), BF16),
        compiler_params=_cparams(("arbitrary", "arbitrary")),
        name="attn_lat",
    )(sink, q, k, v, ck, cv)


def _resproj_kernel(x_ref, a_ref, mod_ref, w_ref, o_ref):
    gate = mod_ref[:, 2 * D:3 * D]
    o_ref[...] = x_ref[...] + gate * _dot(a_ref[...], w_ref[...])


def _resproj(x, a, mod4, layer, w):
    tok = pl.BlockSpec((TM, D), lambda i: (i, 0))
    return pl.pallas_call(
        _resproj_kernel,
        grid=(T_ALL // TM,),
        in_specs=[tok, tok, _mod_spec(layer, TM), _full_spec((D, D))],
        out_specs=tok,
        out_shape=jax.ShapeDtypeStruct((T_ALL, D), F32),
        compiler_params=_cparams(("arbitrary",)),
        name="attn_out_proj",
    )(x, a, mod4, w)


def _rope_tables():
    rows = DEC_SEQ // GRID_W
    row = jnp.repeat(jnp.arange(rows), GRID_W).astype(F32)
    col = jnp.tile(jnp.arange(GRID_W), rows).astype(F32)
    inv = ROPE_THETA ** (-jnp.arange(ROPE_FREQS, dtype=F32) / ROPE_FREQS)
    ang = jnp.concatenate([row[:, None] * inv, col[:, None] * inv], axis=-1)
    cos = jnp.cos(ang)
    sin = jnp.sin(ang)
    cos_h = jnp.concatenate([cos, cos], axis=-1)
    sin_h = jnp.concatenate([sin, sin], axis=-1)
    return jnp.tile(cos_h, (1, N_HEADS)), jnp.tile(sin_h, (1, N_HEADS))


def _rot_half_cols(w):
    k = w.shape[0]
    w4 = w.reshape(k, -1, 2, HEAD_DIM // 2)
    return jnp.stack([-w4[:, :, 1], w4[:, :, 0]], axis=2).reshape(k, -1)


def _expand_kv_cols(w):
    k = w.shape[0]
    w3 = w.reshape(k, N_KV_HEADS, 1, HEAD_DIM)
    return jnp.broadcast_to(w3, (k, N_KV_HEADS, Q_PER_KV, HEAD_DIM)).reshape(k, N_HEADS * HEAD_DIM)


FG = D // FNET_GROUPS


def _fnet_kernel(x_ref, mod_ref, g_ref, cs_ref, fl_ref, w_ref, o_ref, h_s, ab_s):
    sh = mod_ref[:, 0:D]
    sc = mod_ref[:, D:2 * D]
    gate = mod_ref[:, 2 * D:3 * D]
    for c in range(TM // ROW_CHUNK):
        r0 = c * ROW_CHUNK
        h_s[r0:r0 + ROW_CHUNK, :] = _norm_mod(x_ref[r0:r0 + ROW_CHUNK, :], g_ref[...], sc, sh).astype(BF16)
    for g in range(FNET_GROUPS):
        ab = _dot(h_s[:, g * FG:(g + 1) * FG], cs_ref[...])
        ab_s[0:TM, g * FG:(g + 1) * FG] = ab[:, 0:FG].astype(BF16)
        ab_s[TM:2 * TM, g * FG:(g + 1) * FG] = ab[:, FG:2 * FG].astype(BF16)
    for c in range(TM // MM_CHUNK):
        r0 = c * MM_CHUNK
        f = _dot(fl_ref[r0:r0 + MM_CHUNK, :], ab_s[...])
        y = _dot(f.astype(BF16), w_ref[...])
        o_ref[r0:r0 + MM_CHUNK, :] = x_ref[r0:r0 + MM_CHUNK, :] + gate * y


def _fnet_mixer(x, mod4, layer, g_mix, cs, fl, w_out):
    nct = T_CTX // TM
    tok = pl.BlockSpec((TM, D), lambda i: (i, 0))
    return pl.pallas_call(
        _fnet_kernel,
        grid=(T_ALL // TM,),
        in_specs=[tok, _mod_spec(layer, TM), _row_spec(), _full_spec((FG, 2 * FG)),
                  pl.BlockSpec((None, TM, 2 * TM), lambda i: (jnp.where(i < nct, 0, 1), 0, 0)),
                  _full_spec((D, D))],
        out_specs=tok,
        out_shape=jax.ShapeDtypeStruct((T_ALL, D), F32),
        scratch_shapes=[pltpu.VMEM((TM, D), BF16), pltpu.VMEM((2 * TM, D), BF16)],
        compiler_params=_cparams(("arbitrary",)),
        name="fnet_mixer",
    )(x, mod4, g_mix.reshape(1, D), cs, fl, w_out)


def _dft_cos_sin(n):
    k = np.arange(n)
    ang = 2.0 * np.pi * ((k[:, None] * k[None, :]) % n) / n
    return np.cos(ang), np.sin(ang)


def _fnet_tables():
    cc, sc = _dft_cos_sin(FG)
    cs = np.concatenate([cc, sc], axis=1) / math.sqrt(FG)
    mats = []
    for seq in (SEQ, DEC_SEQ):
        cl, sl = _dft_cos_sin(seq)
        reps = TM // seq
        eye = np.eye(reps)
        mats.append(np.concatenate([np.kron(eye, cl), -np.kron(eye, sl)], axis=1) / math.sqrt(seq))
    return jnp.asarray(cs, F32).astype(BF16), jnp.asarray(np.stack(mats), F32).astype(BF16)


ROUTER_TM = 512


def _router_kernel(x_ref, mod_ref, g_ref, wr_ref, h_ref, sel_ref, gates_ref):
    sh = mod_ref[:, 3 * D:4 * D]
    sc = mod_ref[:, 4 * D:5 * D]
    w_hi = wr_ref[0]
    w_lo = wr_ref[1]
    for c in range(ROUTER_TM // ROW_CHUNK):
        r0 = c * ROW_CHUNK
        h = _norm_mod(x_ref[r0:r0 + ROW_CHUNK, :], g_ref[...], sc, sh)
        h_ref[r0:r0 + ROW_CHUNK, :] = h
        h_hi, h_lo = _split_bf16(h)
        logits = _dot(h_hi, w_hi) + _dot(h_lo, w_hi) + _dot(h_hi, w_lo)
        lane = lax.broadcasted_iota(I32, logits.shape, 1)
        lg = jnp.where(lane < N_EXPERTS, logits, -jnp.inf)
        m1 = jnp.max(lg, axis=-1, keepdims=True)
        i1 = jnp.min(jnp.where(lg == m1, lane, LANES), axis=-1, keepdims=True)
        lg2 = jnp.where(lane == i1, -jnp.inf, lg)
        m2 = jnp.max(lg2, axis=-1, keepdims=True)
        i2 = jnp.min(jnp.where(lg2 == m2, lane, LANES), axis=-1, keepdims=True)
        e2 = jnp.exp(m2 - m1)
        den = 1.0 + e2
        sel_ref[r0:r0 + ROW_CHUNK, :] = jnp.where(lane == i1, 1, jnp.where(lane == i2, 2, 0)).astype(I32)
        gates_ref[r0:r0 + ROW_CHUNK, :] = jnp.where(lane == 0, 1.0 / den, jnp.where(lane == 1, e2 / den, 0.0))


def _router(x, mod4, layer, g_ffn, w_router):
    wr = jnp.zeros((D, LANES), F32).at[:, 0:N_EXPERTS].set(w_router)
    wr_hi = wr.astype(BF16)
    wr_lo = (wr - wr_hi.astype(F32)).astype(BF16)
    tok = pl.BlockSpec((ROUTER_TM, D), lambda i: (i, 0))
    nar = pl.BlockSpec((ROUTER_TM, LANES), lambda i: (i, 0))
    return pl.pallas_call(
        _router_kernel,
        grid=(T_ALL // ROUTER_TM,),
        in_specs=[tok, _mod_spec(layer, ROUTER_TM), _row_spec(), _full_spec((2, D, LANES))],
        out_specs=[tok, nar, nar],
        out_shape=[jax.ShapeDtypeStruct((T_ALL, D), F32),
                   jax.ShapeDtypeStruct((T_ALL, LANES), I32),
                   jax.ShapeDtypeStruct((T_ALL, LANES), F32)],
        compiler_params=_cparams(("arbitrary",)),
        name="moe_router",
    )(x, mod4, g_ffn.reshape(1, D), jnp.stack([wr_hi, wr_lo]))


def _route_plan(sel):
    s8 = sel[:, 0:N_EXPERTS]
    onehot = (s8 > 0).astype(I32)
    csum = jnp.cumsum(onehot, axis=0)
    rank = csum - onehot
    counts = csum[-1]
    padded = ((counts + TME - 1) // TME) * TME
    gend = jnp.cumsum(padded)
    gstart = gend - padded
    pos_te = gstart[None, :] + rank
    pos1 = jnp.sum(jnp.where(s8 == 1, pos_te, 0), axis=1).astype(I32)
    pos2 = jnp.sum(jnp.where(s8 == 2, pos_te, 0), axis=1).astype(I32)
    tok = jnp.arange(T_ALL, dtype=I32)
    src = jnp.zeros((P_MAX,), I32).at[pos1].set(tok).at[pos2].set(tok)
    tile_start = jnp.arange(N_EXPERT_TILES, dtype=I32) * TME
    tile_valid = (tile_start < gend[-1]).astype(I32)
    n_valid = gend[-1] // TME
    last_start = jnp.maximum(n_valid - 1, 0) * TME
    eff_start = jnp.where(tile_valid > 0, tile_start, last_start)
    tile_expert = jnp.minimum(jnp.sum((eff_start[:, None] >= gend[None, :]).astype(I32), axis=1),
                              N_EXPERTS - 1).astype(I32)
    return pos1, pos2, src, tile_expert, tile_valid


def _gather_kernel(src_ref, valid_ref, h_hbm, o_ref, sem):
    m = pl.program_id(0)

    @pl.when(valid_ref[m] > 0)
    def _():
        def body(r, carry):
            t = src_ref[m * TME + r]
            pltpu.make_async_copy(h_hbm.at[pl.ds(t, 1), :], o_ref.at[pl.ds(r, 1), :], sem).start()
            return carry
        lax.fori_loop(0, TME, body, 0, unroll=8)
        pltpu.make_async_copy(h_hbm.at[pl.ds(0, TME), :], o_ref, sem).wait()

    @pl.when(valid_ref[m] == 0)
    def _():
        o_ref[...] = jnp.zeros_like(o_ref)


def _gather_rows(src, tile_valid, h):
    return pl.pallas_call(
        _gather_kernel,
        grid_spec=pltpu.PrefetchScalarGridSpec(
            num_scalar_prefetch=2, grid=(N_EXPERT_TILES,),
            in_specs=[pl.BlockSpec(memory_space=pl.ANY)],
            out_specs=pl.BlockSpec((TME, D), lambda m, *_: (m, 0)),
            scratch_shapes=[pltpu.SemaphoreType.DMA(())]),
        out_shape=jax.ShapeDtypeStruct((P_MAX, D), F32),
        compiler_params=_cparams(("arbitrary",)),
        name="moe_dispatch_gather",
    )(src, tile_valid, h)


EXP_TF = 512
EXP_NF = D_FF_EXPERT // EXP_TF


def _expert_kernel(te_ref, valid_ref, xs_ref, wg_ref, wu_ref, wd_ref, o_ref, xb_s):
    m = pl.program_id(0)
    f = pl.program_id(1)
    nchunk = TME // ROW_CHUNK

    @pl.when(valid_ref[m] > 0)
    def _():
        @pl.when(f == 0)
        def _():
            for c in range(nchunk):
                r0 = c * ROW_CHUNK
                xb_s[r0:r0 + ROW_CHUNK, :] = xs_ref[r0:r0 + ROW_CHUNK, :].astype(BF16)

        xb = xb_s[...]
        gg = _dot(xb, wg_ref[...].astype(BF16))
        uu = _dot(xb, wu_ref[...].astype(BF16))
        a = (gg * jax.nn.sigmoid(gg) * uu).astype(BF16)
        contrib = _dot(a, wd_ref[...].astype(BF16))

        @pl.when(f == 0)
        def _():
            o_ref[...] = contrib

        @pl.when(f > 0)
        def _():
            o_ref[...] = o_ref[...] + contrib

    @pl.when(jnp.logical_and(valid_ref[m] == 0, f == 0))
    def _():
        o_ref[...] = jnp.zeros_like(o_ref)


def _expert_ffn(tile_expert, tile_valid, xs, w_gu, w_down, li):
    def feff(m, f, valid):
        return jnp.where(valid[m] > 0, f, EXP_NF - 1)
    return pl.pallas_call(
        _expert_kernel,
        grid_spec=pltpu.PrefetchScalarGridSpec(
            num_scalar_prefetch=2, grid=(N_EXPERT_TILES, EXP_NF),
            in_specs=[pl.BlockSpec((TME, D), lambda m, f, te, va: (m, 0)),
                      pl.BlockSpec((None, None, D, EXP_TF),
                                   lambda m, f, te, va: (li, te[m], 0, feff(m, f, va))),
                      pl.BlockSpec((None, None, D, EXP_TF),
                                   lambda m, f, te, va: (li, te[m], 0, EXP_NF + feff(m, f, va))),
                      pl.BlockSpec((None, None, EXP_TF, D),
                                   lambda m, f, te, va: (li, te[m], feff(m, f, va), 0))],
            out_specs=pl.BlockSpec((TME, D), lambda m, f, te, va: (m, 0)),
            scratch_shapes=[pltpu.VMEM((TME, D), BF16)]),
        out_shape=jax.ShapeDtypeStruct((P_MAX, D), F32),
        compiler_params=_cparams(("arbitrary", "arbitrary")),
        name="moe_expert_swiglu",
    )(tile_expert, tile_valid, xs, w_gu, w_gu, w_down)


COMB_TM = 512


def _combine_kernel(p1_ref, p2_ref, x_ref, gates_ref, mod_ref, gf_ref, y_hbm, o_ref, a_s, b_s, sem, *, final):
    i = pl.program_id(0)

    def body(r, carry):
        t = i * COMB_TM + r
        pltpu.make_async_copy(y_hbm.at[pl.ds(p1_ref[t], 1), :], a_s.at[pl.ds(r, 1), :], sem.at[0]).start()
        pltpu.make_async_copy(y_hbm.at[pl.ds(p2_ref[t], 1), :], b_s.at[pl.ds(r, 1), :], sem.at[1]).start()
        return carry
    lax.fori_loop(0, COMB_TM, body, 0, unroll=8)
    pltpu.make_async_copy(y_hbm.at[pl.ds(0, COMB_TM), :], a_s, sem.at[0]).wait()
    pltpu.make_async_copy(y_hbm.at[pl.ds(0, COMB_TM), :], b_s, sem.at[1]).wait()

    gate = mod_ref[:, 5 * D:6 * D]
    for c in range(COMB_TM // ROW_CHUNK):
        r0 = c * ROW_CHUNK
        w1 = gates_ref[r0:r0 + ROW_CHUNK, 0:1]
        w2 = gates_ref[r0:r0 + ROW_CHUNK, 1:2]
        y = w1 * a_s[r0:r0 + ROW_CHUNK, :] + w2 * b_s[r0:r0 + ROW_CHUNK, :]
        xn = x_ref[r0:r0 + ROW_CHUNK, :] + gate * y
        if final:
            ms = jnp.mean(xn * xn, axis=-1, keepdims=True)
            xn = xn * lax.rsqrt(ms + EPS) * gf_ref[...]
        o_ref[r0:r0 + ROW_CHUNK, :] = xn


def _combine(pos1, pos2, x, gates, mod4, layer, g_final, y, final):
    tok = pl.BlockSpec((COMB_TM, D), lambda i, *_: (i, 0))
    return pl.pallas_call(
        functools.partial(_combine_kernel, final=final),
        grid_spec=pltpu.PrefetchScalarGridSpec(
            num_scalar_prefetch=2, grid=(T_ALL // COMB_TM,),
            in_specs=[tok, pl.BlockSpec((COMB_TM, LANES), lambda i, *_: (i, 0)),
                      _mod_spec(layer, COMB_TM), _row_spec(),
                      pl.BlockSpec(memory_space=pl.ANY)],
            out_specs=tok,
            scratch_shapes=[pltpu.VMEM((COMB_TM, D), F32), pltpu.VMEM((COMB_TM, D), F32),
                            pltpu.SemaphoreType.DMA((2,))]),
        out_shape=jax.ShapeDtypeStruct((T_ALL, D), F32),
        compiler_params=_cparams(("arbitrary",)),
        name="moe_combine_final" if final else "moe_combine",
    )(pos1, pos2, x, gates, mod4, g_final.reshape(1, D), y)


def _moe_layer(x, mod4, layer, g_ffn, w_router, w_gu, w_down, li, g_final, final):
    h, sel, gates = _router(x, mod4, layer, g_ffn, w_router)
    pos1, pos2, src, tile_expert, tile_valid = _route_plan(sel)
    xs = _gather_rows(src, tile_valid, h)
    ys = _expert_ffn(tile_expert, tile_valid, xs, w_gu, w_down, li)
    return _combine(pos1, pos2, x, gates, mod4, layer, g_final, ys, final)


def kernel(x_prompt, x_sample, cache_k, cache_v, state_ssm_re, state_ssm_im, c, c_ctx, w_ada, b_ada, g_mix, g_ffn, g_final, conv_w_in, conv_w, conv_w_out, s5_lambda_re, s5_lambda_im, s5_b_re, s5_b_im, s5_c_re, s5_c_im, s5_log_dt, s5_d, s5_w_glu, attn_w_q, attn_w_kv, attn_w_o, attn_sink, fnet_w_out, ffn_w_gu, ffn_w_down, moe_w_router, moe_w_gu, moe_w_down):
    x = jnp.concatenate([x_prompt.reshape(T_CTX, D), x_sample.reshape(T_LAT, D)], axis=0)
    cond16 = jnp.concatenate([jnp.broadcast_to(c_ctx[None, :], (8, D)), c], axis=0)
    mod3 = _mod_all(cond16, w_ada, b_ada)
    mod4 = mod3.reshape(DEPTH, 16, 1, N_MOD * D)

    x = _conv_mixer(x, mod4, 0, g_mix[0], conv_w_in[0].astype(BF16), conv_w[0], conv_w_out[0].astype(BF16))
    x = _dense_ffn(x, mod4, 0, g_ffn[0], ffn_w_gu[0].astype(BF16), ffn_w_down[0].astype(BF16))

    lbr, lbi, bbr, bbi = _s5_prep(s5_lambda_re[0], s5_lambda_im[0], s5_log_dt[0], s5_b_re[0], s5_b_im[0])
    w_b, w_c, lam_s = _s5_weights(lbr, lbi, bbr, bbi, s5_c_re[0], s5_c_im[0])
    w_glu = s5_w_glu[0].astype(BF16)
    xc = x[0:T_CTX].reshape(BATCH, SEQ, D).transpose(1, 0, 2).reshape(T_CTX, D)
    xl = x[T_CTX:].reshape(DEC_BATCH, DEC_SEQ, D).transpose(1, 0, 2).reshape(T_LAT, D)
    h0_ctx = jnp.zeros((2, S5_JT, BATCH, 2 * S5_HALF), F32)
    h0_lat = _s5_state_in(state_ssm_re[:, 0], state_ssm_im[:, 0])
    yc, fin_c = _s5_scan(xc, BATCH, mod3, 1, 0, g_mix[1], w_b, w_c, lam_s, h0_ctx)
    yl, _ = _s5_scan(xl, DEC_BATCH, mod3, 1, 1, g_mix[1], w_b, w_c, lam_s, h0_lat)
    xc = _s5_glu(xc, yc, mod3, 1, 0, g_mix[1], s5_d[0], w_glu)
    xl = _s5_glu(xl, yl, mod3, 1, 1, g_mix[1], s5_d[0], w_glu)
    x = jnp.concatenate([xc.reshape(SEQ, BATCH, D).transpose(1, 0, 2).reshape(T_CTX, D),
                         xl.reshape(DEC_SEQ, DEC_BATCH, D).transpose(1, 0, 2).reshape(T_LAT, D)], axis=0)
    new_re, new_im = _s5_state_out(fin_c)
    x = _moe_layer(x, mod4, 1, g_ffn[1], moe_w_router[0], moe_w_gu, moe_w_down, 0, g_final, False)

    wq = attn_w_q[0]
    wk = _expand_kv_cols(attn_w_kv[0][:, 0:N_KV_HEADS * HEAD_DIM])
    wv = _expand_kv_cols(attn_w_kv[0][:, N_KV_HEADS * HEAD_DIM:])
    w_ctx = jnp.concatenate([wq, wk, wv, attn_w_kv[0]], axis=1).astype(BF16)
    w_lat = jnp.concatenate([wq, _rot_half_cols(wq), wk, _rot_half_cols(wk), wv], axis=1).astype(BF16)
    cos_t, sin_t = _rope_tables()
    q_c, k_c, v_c, kv_c = _qkv_ctx(x, mod4, 2, g_mix[2], w_ctx)
    q_l, k_l, v_l = _qkv_lat(x, mod4, 2, g_mix[2], w_lat, cos_t, sin_t)
    sink = attn_sink[0]
    o_c = _attn_ctx(sink, q_c, k_c, v_c)
    expand = lambda a: jnp.broadcast_to(
        a.reshape(DEC_BATCH * PAST_LEN, N_KV_HEADS, 1, HEAD_DIM),
        (DEC_BATCH * PAST_LEN, N_KV_HEADS, Q_PER_KV, HEAD_DIM)).reshape(DEC_BATCH * PAST_LEN, D).astype(BF16)
    o_l = _attn_lat(sink, q_l, k_l, v_l, expand(cache_k[:, 0]), expand(cache_v[:, 0]))
    x = _resproj(x, jnp.concatenate([o_c, o_l], axis=0), mod4, 2, attn_w_o[0].astype(BF16))
    kvw = N_KV_HEADS * HEAD_DIM
    new_k = kv_c[:, 0:kvw].reshape(BATCH, 1, SEQ, N_KV_HEADS, HEAD_DIM)
    new_v = kv_c[:, kvw:].reshape(BATCH, 1, SEQ, N_KV_HEADS, HEAD_DIM)
    x = _dense_ffn(x, mod4, 2, g_ffn[2], ffn_w_gu[1].astype(BF16), ffn_w_down[1].astype(BF16))

    cs, fl = _fnet_tables()
    x = _fnet_mixer(x, mod4, 3, g_mix[3], cs, fl, fnet_w_out[0].astype(BF16))
    x = _moe_layer(x, mod4, 3, g_ffn[3], moe_w_router[1], moe_w_gu, moe_w_down, 1, g_final, True)

    y_prompt = x[0:T_CTX].reshape(BATCH, SEQ, D)
    y_sample = x[T_CTX:].reshape(DEC_BATCH, DEC_SEQ, D)
    return (y_prompt, y_sample, new_k, new_v, new_re[:, None], new_im[:, None])
```

```python
import functools
import math

import numpy as np
import jax
import jax.numpy as jnp
from jax import lax
from jax.experimental import pallas as pl
from jax.experimental.pallas import tpu as pltpu

F32 = jnp.float32
BF16 = jnp.bfloat16
I32 = jnp.int32

D = 1024
BATCH = 16
SEQ = 256
DEPTH = 4
DEC_BATCH = 8
DEC_SEQ = 1024
PAST_LEN = 256
GRID_W = 64
EPS = 1e-6
N_MOD = 6
S5_GROUP = 16
S5_GROUPS = D // S5_GROUP
S5_STATE = 64
HEAD_DIM = 64
N_HEADS = D // HEAD_DIM
N_KV_HEADS = 4
Q_PER_KV = N_HEADS // N_KV_HEADS
WINDOW = 128
ROPE_THETA = 10000.0
ROPE_FREQS = HEAD_DIM // 4
FNET_GROUPS = 4
D_FF = 2816
N_EXPERTS = 8
TOP_K = 2
D_FF_EXPERT = 3584
NEG_INF = -1e30

T_CTX = BATCH * SEQ
T_LAT = DEC_BATCH * DEC_SEQ
T_ALL = T_CTX + T_LAT

VMEM_LIMIT_V7X = 56 * 1024 * 1024
LANES = 128

TM = 1024
ROW_CHUNK = 256
MM_CHUNK = 512
TME = 1024
N_EXPERT_TILES = (T_ALL * TOP_K) // TME + N_EXPERTS
P_MAX = N_EXPERT_TILES * TME


def _cparams(sem):
    return pltpu.CompilerParams(dimension_semantics=sem, vmem_limit_bytes=VMEM_LIMIT_V7X)


def _dot(a, b):
    return jnp.dot(a, b, preferred_element_type=F32)


def _dot_nt(a, b):
    return lax.dot_general(a, b, (((1,), (1,)), ((), ())), preferred_element_type=F32)


def _split_bf16(a):
    hi = a.astype(BF16)
    lo = (a - hi.astype(F32)).astype(BF16)
    return hi, lo


def _tile_rows(a, reps, axis=0):
    assert axis == 0
    return jnp.concatenate([a] * reps, axis=0)


def _norm_mod(x, g, sc, sh):
    ms = jnp.mean(x * x, axis=-1, keepdims=True)
    y = x * lax.rsqrt(ms + EPS) * g
    return y * (1.0 + sc) + sh


def _mod_row(i, tm):
    nct = T_CTX // tm
    lpb = DEC_SEQ // tm
    return jnp.where(i < nct, 0, 8 + (i - nct) // lpb)


def _mod_spec(layer, tm):
    return pl.BlockSpec((None, None, 1, N_MOD * D), lambda i, *_: (layer, _mod_row(i, tm), 0, 0))


def _row_spec():
    return pl.BlockSpec((1, D), lambda *_: (0, 0))


def _full_spec(shape):
    nd = len(shape)
    return pl.BlockSpec(shape, lambda *_: (0,) * nd, pipeline_mode=pl.Buffered(1))


def _mod_kernel(c_ref, w_ref, b_ref, o_ref):
    c = c_ref[...]
    s = (c * jax.nn.sigmoid(c)).astype(BF16)
    o_ref[...] = _dot(s, w_ref[...].astype(BF16)) + b_ref[...]


def _mod_all(cond16, w_ada, b_ada):
    tn = 1024
    return pl.pallas_call(
        _mod_kernel,
        grid=(DEPTH, N_MOD * D // tn),
        in_specs=[pl.BlockSpec((16, D), lambda l, n: (0, 0)),
                  pl.BlockSpec((None, D, tn), lambda l, n: (l, 0, n)),
                  pl.BlockSpec((None, 1, tn), lambda l, n: (l, 0, n))],
        out_specs=pl.BlockSpec((None, 16, tn), lambda l, n: (l, 0, n)),
        out_shape=jax.ShapeDtypeStruct((DEPTH, 16, N_MOD * D), F32),
        compiler_params=_cparams(("arbitrary", "arbitrary")),
        name="adaln_mod",
    )(cond16, w_ada, b_ada.reshape(DEPTH, 1, N_MOD * D))


def _conv_kernel(x_ref, mod_ref, g_ref, win_ref, cw_ref, wout_ref, o_ref, gb_s, u_s, z_s):
    i = pl.program_id(0)
    is_ctx = i < (T_CTX // TM)
    sh = mod_ref[:, 0:D]
    sc = mod_ref[:, D:2 * D]
    gate = mod_ref[:, 2 * D:3 * D]
    g = g_ref[...]
    zero8 = jnp.zeros((8, D), F32)
    u_s[0:8, :] = zero8
    u_s[8 + TM:16 + TM, :] = zero8
    for c in range(TM // MM_CHUNK):
        r0 = c * MM_CHUNK
        h = _norm_mod(x_ref[r0:r0 + MM_CHUNK, :], g, sc, sh).astype(BF16)
        proj = _dot(h, win_ref[...])
        gb_s[r0:r0 + MM_CHUNK, :] = proj[:, 0:D]
        u_s[8 + r0:8 + r0 + MM_CHUNK, :] = proj[:, D:2 * D] * proj[:, 2 * D:3 * D]
    row = lax.broadcasted_iota(I32, (ROW_CHUNK, 1), 0)
    first = jnp.logical_and(is_ctx, row == 0)
    last = jnp.logical_and(is_ctx, row == ROW_CHUNK - 1)
    for c in range(TM // ROW_CHUNK):
        r0 = c * ROW_CHUNK
        up = jnp.where(first, 0.0, u_s[7 + r0:7 + r0 + ROW_CHUNK, :])
        mid = u_s[8 + r0:8 + r0 + ROW_CHUNK, :]
        dn = jnp.where(last, 0.0, u_s[9 + r0:9 + r0 + ROW_CHUNK, :])
        conv = up * cw_ref[0:1, :] + mid * cw_ref[1:2, :] + dn * cw_ref[2:3, :]
        z_s[r0:r0 + ROW_CHUNK, :] = (gb_s[r0:r0 + ROW_CHUNK, :] * conv).astype(BF16)
    for c in range(TM // MM_CHUNK):
        r0 = c * MM_CHUNK
        y = _dot(z_s[r0:r0 + MM_CHUNK, :], wout_ref[...])
        o_ref[r0:r0 + MM_CHUNK, :] = x_ref[r0:r0 + MM_CHUNK, :] + gate * y


def _conv_mixer(x, mod4, layer, g_mix, w_in, conv_w, w_out):
    assert SEQ == ROW_CHUNK and DEC_SEQ == TM
    return pl.pallas_call(
        _conv_kernel,
        grid=(T_ALL // TM,),
        in_specs=[pl.BlockSpec((TM, D), lambda i: (i, 0)),
                  _mod_spec(layer, TM), _row_spec(),
                  _full_spec((D, 3 * D)), _full_spec((3, D)), _full_spec((D, D))],
        out_specs=pl.BlockSpec((TM, D), lambda i: (i, 0)),
        out_shape=jax.ShapeDtypeStruct((T_ALL, D), F32),
        scratch_shapes=[pltpu.VMEM((TM, D), F32), pltpu.VMEM((TM + 16, D), F32), pltpu.VMEM((TM, D), BF16)],
        compiler_params=_cparams(("arbitrary",)),
        name="conv_mixer",
    )(x, mod4, g_mix.reshape(1, D), w_in, conv_w, w_out)


FFN_TF = D_FF // 2


def _ffn_kernel(x_ref, mod_ref, g_ref, wg_ref, wu_ref, wd_ref, o_ref, h_s):
    f = pl.program_id(1)
    nf = pl.num_programs(1)
    sh = mod_ref[:, 3 * D:4 * D]
    sc = mod_ref[:, 4 * D:5 * D]
    gate = mod_ref[:, 5 * D:6 * D]
    nchunk = x_ref.shape[0] // ROW_CHUNK

    @pl.when(f == 0)
    def _():
        for c in range(nchunk):
            r0 = c * ROW_CHUNK
            h_s[r0:r0 + ROW_CHUNK, :] = _norm_mod(x_ref[r0:r0 + ROW_CHUNK, :], g_ref[...], sc, sh).astype(BF16)

    hc = h_s[...]
    gg = _dot(hc, wg_ref[...])
    uu = _dot(hc, wu_ref[...])
    a = (gg * jax.nn.sigmoid(gg) * uu).astype(BF16)
    contrib = _dot(a, wd_ref[...])

    @pl.when(f == 0)
    def _():
        o_ref[...] = contrib

    @pl.when(jnp.logical_and(f > 0, f < nf - 1))
    def _():
        o_ref[...] = o_ref[...] + contrib

    @pl.when(f == nf - 1)
    def _():
        o_ref[...] = x_ref[...] + gate * (o_ref[...] + contrib)


def _dense_ffn(x, mod4, layer, g_ffn, w_gu, w_down):
    nf = D_FF // FFN_TF
    return pl.pallas_call(
        _ffn_kernel,
        grid=(T_ALL // TM, nf),
        in_specs=[pl.BlockSpec((TM, D), lambda i, f: (i, 0)),
                  _mod_spec(layer, TM), _row_spec(),
                  pl.BlockSpec((D, FFN_TF), lambda i, f: (0, f)),
                  pl.BlockSpec((D, FFN_TF), lambda i, f: (0, nf + f)),
                  pl.BlockSpec((FFN_TF, D), lambda i, f: (f, 0))],
        out_specs=pl.BlockSpec((TM, D), lambda i, f: (i, 0)),
        out_shape=jax.ShapeDtypeStruct((T_ALL, D), F32),
        scratch_shapes=[pltpu.VMEM((TM, D), BF16)],
        compiler_params=_cparams(("arbitrary", "arbitrary")),
        name="dense_swiglu",
    )(x, mod4, g_ffn.reshape(1, D), w_gu, w_gu, w_down)


def _s5_prep_kernel(lr_ref, li_ref, ldt_ref, br_ref, bi_ref, lbr_ref, lbi_ref, bbr_ref, bbi_ref):
    lr = lr_ref[...]
    li = li_ref[...]
    dt = jnp.exp(ldt_ref[...])
    mag = jnp.exp(lr * dt)
    ar = mag * jnp.cos(li * dt)
    ai = mag * jnp.sin(li * dt)
    nr = ar - 1.0
    den = lr * lr + li * li
    fr = (nr * lr + ai * li) / den
    fi = (ai * lr - nr * li) / den
    br = br_ref[...]
    bi = bi_ref[...]
    lbr_ref[...] = ar
    lbi_ref[...] = ai
    bbr_ref[...] = fr * br - fi * bi
    bbi_ref[...] = fr * bi + fi * br


def _s5_prep(lam_re, lam_im, log_dt, b_re, b_im):
    rows = 2 * S5_GROUPS
    cols = S5_STATE * S5_GROUP
    exp = lambda a: jnp.repeat(a.reshape(rows, S5_STATE), S5_GROUP, axis=1)
    ldt = jnp.broadcast_to(log_dt.reshape(rows, 1), (rows, cols))
    outs = pl.pallas_call(
        _s5_prep_kernel,
        out_shape=[jax.ShapeDtypeStruct((rows, cols), F32)] * 4,
        name="s5_discretize",
    )(exp(lam_re), exp(lam_im), ldt, b_re.reshape(rows, cols), b_im.reshape(rows, cols))
    lbr, lbi, bbr, bbi = outs
    shp = (2, S5_GROUPS, S5_STATE, S5_GROUP)
    return lbr.reshape(shp)[..., 0], lbi.reshape(shp)[..., 0], bbr.reshape(shp), bbi.reshape(shp)


S5_JT = 8
S5_GPT = LANES // S5_GROUP
S5_HALF = S5_GPT * S5_STATE
S5_ROWS = 512


def _s5_scan_kernel(x_ref, mod_ref, g_ref, wb_ref, wc_ref, lam_ref, h0_ref,
                    y_ref, fin_ref, bu_s, st_s, *, nb, jgroup):
    d = pl.program_id(0)
    c = pl.program_id(1)
    lc = S5_ROWS // nb

    @pl.when(c == 0)
    def _():
        st_s[...] = h0_ref[...]

    rep = S5_ROWS // 8
    sh = _tile_rows(mod_ref[:, 0:D], rep, axis=0)
    sc = _tile_rows(mod_ref[:, D:2 * D], rep, axis=0)
    u = _norm_mod(x_ref[...], g_ref[...], sc, sh).astype(BF16)
    for j in range(S5_JT):
        bu_s[j] = _dot(u[:, j * LANES:(j + 1) * LANES], wb_ref[j])

    for j0 in range(0, S5_JT, jgroup):
        js = list(range(j0, j0 + jgroup))
        lam = [(jnp.broadcast_to(lam_ref[j][:, 0:S5_HALF], (nb, S5_HALF)),
                jnp.broadcast_to(lam_ref[j][:, S5_HALF:], (nb, S5_HALF))) for j in js]

        def body(t, carry):
            l = jnp.where(d == 0, t, lc - 1 - t)
            r0 = pl.multiple_of(l * nb, nb)
            out = []
            for k, j in enumerate(js):
                sr, si = carry[k]
                ar, ai = lam[k]
                bu = bu_s[j, pl.ds(r0, nb), :]
                hr = ar * sr - ai * si + bu[:, 0:S5_HALF]
                hi = ar * si + ai * sr + bu[:, S5_HALF:]
                bu_s[j, pl.ds(r0, nb), 0:S5_HALF] = hr
                bu_s[j, pl.ds(r0, nb), S5_HALF:] = hi
                out.append((hr, hi))
            return tuple(out)

        init = tuple((st_s[j][:, 0:S5_HALF], st_s[j][:, S5_HALF:]) for j in js)
        fin = lax.fori_loop(0, lc, body, init)
        for k, j in enumerate(js):
            st_s[j, :, 0:S5_HALF] = fin[k][0]
            st_s[j, :, S5_HALF:] = fin[k][1]

    for j in range(S5_JT):
        y_ref[:, j * LANES:(j + 1) * LANES] = _dot(bu_s[j].astype(BF16), wc_ref[j])
    fin_ref[...] = st_s[...]


def _s5_scan(xt, nb, mod3, layer, path, g_mix, w_b, w_c, lam_s, h0):
    rows = xt.shape[0]
    nc = rows // S5_ROWS
    chunk = lambda d, c: c + d * (nc - 1 - 2 * c)
    kern = functools.partial(_s5_scan_kernel, nb=nb, jgroup=2 if nb == 8 else 1)
    return pl.pallas_call(
        kern,
        grid=(2, nc),
        in_specs=[pl.BlockSpec((S5_ROWS, D), lambda d, c: (chunk(d, c), 0)),
                  pl.BlockSpec((None, 8, N_MOD * D), lambda d, c: (layer, path, 0)),
                  _row_spec(),
                  pl.BlockSpec((None, S5_JT, LANES, 2 * S5_HALF), lambda d, c: (d, 0, 0, 0)),
                  pl.BlockSpec((None, S5_JT, 2 * S5_HALF, LANES), lambda d, c: (d, 0, 0, 0)),
                  pl.BlockSpec((None, S5_JT, 1, 2 * S5_HALF), lambda d, c: (d, 0, 0, 0)),
                  pl.BlockSpec((None, S5_JT, nb, 2 * S5_HALF), lambda d, c: (d, 0, 0, 0))],
        out_specs=[pl.BlockSpec((None, S5_ROWS, D), lambda d, c: (d, chunk(d, c), 0)),
                   pl.BlockSpec((None, S5_JT, nb, 2 * S5_HALF), lambda d, c: (d, 0, 0, 0))],
        out_shape=[jax.ShapeDtypeStruct((2, rows, D), F32),
                   jax.ShapeDtypeStruct((2, S5_JT, nb, 2 * S5_HALF), F32)],
        scratch_shapes=[pltpu.VMEM((S5_JT, S5_ROWS, 2 * S5_HALF), F32),
                        pltpu.VMEM((S5_JT, nb, 2 * S5_HALF), F32)],
        compiler_params=_cparams(("arbitrary", "arbitrary")),
        name="s5_scan_b%d" % nb,
    )(xt, mod3, g_mix.reshape(1, D), w_b, w_c, lam_s, h0)


S5_GLU_ROWS = 512


def _s5_glu_kernel(x_ref, yf_ref, yb_ref, mod_ref, g_ref, dsk_ref, w_ref, o_ref):
    rep = MM_CHUNK // 8
    sh = _tile_rows(mod_ref[:, 0:D], rep, axis=0)
    sc = _tile_rows(mod_ref[:, D:2 * D], rep, axis=0)
    gate = _tile_rows(mod_ref[:, 2 * D:3 * D], rep, axis=0)
    for c in range(S5_GLU_ROWS // MM_CHUNK):
        r0 = c * MM_CHUNK
        x = x_ref[r0:r0 + MM_CHUNK, :]
        u = _norm_mod(x, g_ref[...], sc, sh)
        y = u * dsk_ref[...] + yf_ref[r0:r0 + MM_CHUNK, :] + yb_ref[r0:r0 + MM_CHUNK, :]
        z = jax.nn.gelu(y).astype(BF16)
        ag = _dot(z, w_ref[...])
        out = ag[:, 0:D] * jax.nn.sigmoid(ag[:, D:2 * D])
        o_ref[r0:r0 + MM_CHUNK, :] = x + gate * out


def _s5_glu(xt, y2, mod3, layer, path, g_mix, d_skip, w_glu):
    rows = xt.shape[0]
    return pl.pallas_call(
        _s5_glu_kernel,
        grid=(rows // S5_GLU_ROWS,),
        in_specs=[pl.BlockSpec((S5_GLU_ROWS, D), lambda i: (i, 0)),
                  pl.BlockSpec((None, S5_GLU_ROWS, D), lambda i: (0, i, 0)),
                  pl.BlockSpec((None, S5_GLU_ROWS, D), lambda i: (1, i, 0)),
                  pl.BlockSpec((None, 8, N_MOD * D), lambda i: (layer, path, 0)),
                  _row_spec(), _row_spec(), _full_spec((D, 2 * D))],
        out_specs=pl.BlockSpec((S5_GLU_ROWS, D), lambda i: (i, 0)),
        out_shape=jax.ShapeDtypeStruct((rows, D), F32),
        compiler_params=_cparams(("arbitrary",)),
        name="s5_glu",
    )(xt, y2, y2, mod3, g_mix.reshape(1, D), d_skip.reshape(1, D), w_glu)


def _s5_weights(lbr, lbi, bbr, bbi, c_re, c_im):
    eye = jnp.eye(S5_GPT, dtype=F32)
    bb = jnp.stack([bbr, bbi]).reshape(2, 2, S5_JT, S5_GPT, S5_STATE, S5_GROUP)
    w_bu = jnp.einsum('rdjgps,gh->djgsrhp', bb, eye).reshape(2, S5_JT, LANES, 2 * S5_HALF)
    w_b = w_bu.astype(BF16)
    cc =jnp.stack([c_re, -c_im]).reshape(2, 2, S5_JT, S5_GPT, S5_GROUP, S5_STATE)
    w_c = jnp.einsum('rdjgsp,gh->djrgphs', cc, eye).reshape(2, S5_JT, 2 * S5_HALF, LANES).astype(BF16)
    lam_s = jnp.concatenate([lbr.reshape(2, S5_JT, S5_HALF), lbi.reshape(2, S5_JT, S5_HALF)], axis=-1)
    return w_b, w_c, lam_s.reshape(2, S5_JT, 1, 2 * S5_HALF)


def _s5_state_in(st_re, st_im):
    def lay(a):
        b = a.shape[0]
        return a.transpose(1, 0, 2, 3).reshape(2, b, S5_JT, S5_HALF).transpose(0, 2, 1, 3)
    return jnp.concatenate([lay(st_re), lay(st_im)], axis=-1)


def _s5_state_out(fin):
    def lay(a):
        b = a.shape[2]
        return a.transpose(2, 0, 1, 3).reshape(b, 2, S5_GROUPS, S5_STATE)
    return lay(fin[..., 0:S5_HALF]), lay(fin[..., S5_HALF:])


QKV_TM = 512


def _qkv_ctx_kernel(x_ref, mod_ref, g_ref, w_ref, q_ref, k_ref, v_ref, kv_ref):
    sh = mod_ref[:, 0:D]
    sc = mod_ref[:, D:2 * D]
    for c in range(QKV_TM // MM_CHUNK):
        r0 = c * MM_CHUNK
        h = _norm_mod(x_ref[r0:r0 + MM_CHUNK, :], g_ref[...], sc, sh).astype(BF16)
        q_ref[r0:r0 + MM_CHUNK, :] = _dot(h, w_ref[:, 0:D]).astype(BF16)
        k_ref[r0:r0 + MM_CHUNK, :] = _dot(h, w_ref[:, D:2 * D]).astype(BF16)
        v_ref[r0:r0 + MM_CHUNK, :] = _dot(h, w_ref[:, 2 * D:3 * D]).astype(BF16)
        kv_ref[r0:r0 + MM_CHUNK, :] = _dot(h, w_ref[:, 3 * D:3 * D + 2 * N_KV_HEADS * HEAD_DIM])


def _qkv_ctx(x, mod4, layer, g_mix, w_all):
    n = w_all.shape[1]
    bf = jax.ShapeDtypeStruct((T_CTX, D), BF16)
    tok = pl.BlockSpec((QKV_TM, D), lambda i: (i, 0))
    return pl.pallas_call(
        _qkv_ctx_kernel,
        grid=(T_CTX // QKV_TM,),
        in_specs=[tok, _mod_spec(layer, QKV_TM), _row_spec(), _full_spec((D, n))],
        out_specs=[tok, tok, tok, pl.BlockSpec((QKV_TM, 512), lambda i: (i, 0))],
        out_shape=[bf, bf, bf, jax.ShapeDtypeStruct((T_CTX, 512), F32)],
        compiler_params=_cparams(("arbitrary",)),
        name="qkv_ctx",
    )(x, mod4, g_mix.reshape(1, D), w_all)


def _qkv_lat_kernel(x_ref, mod_ref, g_ref, w_ref, cos_ref, sin_ref, q_ref, k_ref, v_ref):
    sh = mod_ref[:, 0:D]
    sc = mod_ref[:, D:2 * D]
    for c in range(QKV_TM // MM_CHUNK):
        r0 = c * MM_CHUNK
        h = _norm_mod(x_ref[r0:r0 + MM_CHUNK, :], g_ref[...], sc, sh).astype(BF16)
        cos = cos_ref[r0:r0 + MM_CHUNK, :]
        sin = sin_ref[r0:r0 + MM_CHUNK, :]
        q_ref[r0:r0 + MM_CHUNK, :] = (_dot(h, w_ref[:, 0:D]) * cos + _dot(h, w_ref[:, D:2 * D]) * sin).astype(BF16)
        k_ref[r0:r0 + MM_CHUNK, :] = (_dot(h, w_ref[:, 2 * D:3 * D]) * cos
                                      + _dot(h, w_ref[:, 3 * D:4 * D]) * sin).astype(BF16)
        v_ref[r0:r0 + MM_CHUNK, :] = _dot(h, w_ref[:, 4 * D:5 * D]).astype(BF16)


def _qkv_lat(x, mod4, layer, g_mix, w_all, cos_t, sin_t):
    n = w_all.shape[1]
    nct = T_CTX // QKV_TM
    lpb = DEC_SEQ // QKV_TM
    bf = jax.ShapeDtypeStruct((T_LAT, D), BF16)
    tok_out = pl.BlockSpec((QKV_TM, D), lambda i: (i, 0))
    rope = pl.BlockSpec((QKV_TM, D), lambda i: (i % lpb, 0))
    return pl.pallas_call(
        _qkv_lat_kernel,
        grid=(T_LAT // QKV_TM,),
        in_specs=[pl.BlockSpec((QKV_TM, D), lambda i: (i + nct, 0)),
                  pl.BlockSpec((None, None, 1, N_MOD * D), lambda i: (layer, 8 + i // lpb, 0, 0)),
                  _row_spec(), _full_spec((D, n)), rope, rope],
        out_specs=[tok_out, tok_out, tok_out],
        out_shape=[bf, bf, bf],
        compiler_params=_cparams(("arbitrary",)),
        name="qkv_lat",
    )(x, mod4, g_mix.reshape(1, D), w_all, cos_t, sin_t)


KVW = Q_PER_KV * HEAD_DIM


def _head_masks(rows):
    lane = lax.broadcasted_iota(I32, (rows, KVW), 1)
    return [jnp.logical_and(lane >= g * HEAD_DIM, lane < (g + 1) * HEAD_DIM) for g in range(Q_PER_KV)]


def _attn_ctx_kernel(sink_ref, q_ref, k_ref, v_ref, o_ref):
    scale = HEAD_DIM ** -0.5
    masks = _head_masks(SEQ)
    for kv in range(N_KV_HEADS):
        c0 = kv * KVW
        q = q_ref[:, c0:c0 + KVW]
        k = k_ref[:, c0:c0 + KVW]
        v = v_ref[:, c0:c0 + KVW]
        acc = jnp.zeros((SEQ, KVW), F32)
        for g in range(Q_PER_KV):
            sink = sink_ref[kv * Q_PER_KV + g]
            qg = jnp.where(masks[g], q, jnp.zeros_like(q))
            s = _dot_nt(qg, k) * scale
            m = jnp.maximum(jnp.max(s, axis=-1, keepdims=True), sink)
            e = jnp.exp(s - m)
            den = jnp.sum(e, axis=-1, keepdims=True) + jnp.exp(sink - m)
            og = _dot(e.astype(BF16), v) / den
            acc = jnp.where(masks[g], og, acc)
        o_ref[:, c0:c0 + KVW] = acc.astype(BF16)


def _attn_ctx(sink, q, k, v):
    tok = pl.BlockSpec((SEQ, D), lambda b, *_: (b, 0))
    return pl.pallas_call(
        _attn_ctx_kernel,
        grid_spec=pltpu.PrefetchScalarGridSpec(
            num_scalar_prefetch=1, grid=(BATCH,),
            in_specs=[tok, tok, tok], out_specs=tok),
        out_shape=jax.ShapeDtypeStruct((T_CTX, D), BF16),
        compiler_params=_cparams(("arbitrary",)),
        name="attn_ctx",
    )(sink, q, k, v)


ATT_TQ = 128
ATT_SPAN = ATT_TQ + 2 * WINDOW


def _attn_lat_kernel(sink_ref, q_ref, k_ref, v_ref, ck_ref, cv_ref, o_ref):
    qb = pl.program_id(1)
    scale = HEAD_DIM ** -0.5
    w0 = pl.multiple_of(jnp.clip(qb * ATT_TQ - WINDOW, 0, DEC_SEQ - ATT_SPAN), ATT_TQ)
    rows = Q_PER_KV * ATT_TQ
    ridx = lax.broadcasted_iota(I32, (rows, ATT_SPAN), 0)
    qpos = qb * ATT_TQ + (ridx & (ATT_TQ - 1))
    kpos = w0 + lax.broadcasted_iota(I32, (rows, ATT_SPAN), 1)
    valid = jnp.abs(qpos - kpos) <= WINDOW
    rcol = lax.broadcasted_iota(I32, (rows, 1), 0)
    masks = _head_masks(ATT_TQ)
    for kv in range(N_KV_HEADS):
        c0 = kv * KVW
        q = q_ref[:, c0:c0 + KVW]
        qs = jnp.concatenate([jnp.where(masks[g], q, jnp.zeros_like(q)) for g in range(Q_PER_KV)], axis=0)
        sink = jnp.zeros((rows, 1), F32)
        for g in range(Q_PER_KV):
            sink = jnp.where(rcol >= g * ATT_TQ, sink_ref[kv * Q_PER_KV + g], sink)
        s_ctx = _dot_nt(qs, ck_ref[:, c0:c0 + KVW]) * scale
        s_win = _dot_nt(qs, k_ref[pl.ds(w0, ATT_SPAN), c0:c0 + KVW]) * scale
        s_win = jnp.where(valid, s_win, NEG_INF)
        m = jnp.maximum(jnp.maximum(jnp.max(s_ctx, axis=-1, keepdims=True),
                                    jnp.max(s_win, axis=-1, keepdims=True)), sink)
        e_ctx = jnp.exp(s_ctx - m)
        e_win = jnp.exp(s_win - m)
        den = (jnp.exp(sink - m) + jnp.sum(e_ctx, axis=-1, keepdims=True)
               + jnp.sum(e_win, axis=-1, keepdims=True))
        o = (_dot(e_ctx.astype(BF16), cv_ref[:, c0:c0 + KVW])
             + _dot(e_win.astype(BF16), v_ref[pl.ds(w0, ATT_SPAN), c0:c0 + KVW])) / den
        acc = jnp.zeros((ATT_TQ, KVW), F32)
        for g in range(Q_PER_KV):
            acc = jnp.where(masks[g], o[g * ATT_TQ:(g + 1) * ATT_TQ, :], acc)
        o_ref[:, c0:c0 + KVW] = acc.astype(BF16)


def _attn_lat(sink, q, k, v, ck, cv):
    nqb = DEC_SEQ // ATT_TQ
    qspec = pl.BlockSpec((ATT_TQ, D), lambda b, i, *_: (b * nqb + i, 0))
    seq = pl.BlockSpec((DEC_SEQ, D), lambda b, i, *_: (b, 0))
    ctx = pl.BlockSpec((PAST_LEN, D), lambda b, i, *_: (b, 0))
    return pl.pallas_call(
        _attn_lat_kernel,
        grid_spec=pltpu.PrefetchScalarGridSpec(
            num_scalar_prefetch=1, grid=(DEC_BATCH, nqb),
            in_specs=[qspec, seq, seq, ctx, ctx], out_specs=qspec),
        out_shape=jax.ShapeDtypeStruct((T_LAT, D), BF16),
        compiler_params=_cparams(("arbitrary", "arbitrary")),
        name="attn_lat",
    )(sink, q, k, v, ck, cv)


def _resproj_kernel(x_ref, a_ref, mod_ref, w_ref, o_ref):
    gate = mod_ref[:, 2 * D:3 * D]
    o_ref[...] = x_ref[...] + gate * _dot(a_ref[...], w_ref[...])


def _resproj(x, a, mod4, layer, w):
    tok = pl.BlockSpec((TM, D), lambda i: (i, 0))
    return pl.pallas_call(
        _resproj_kernel,
        grid=(T_ALL // TM,),
        in_specs=[tok, tok, _mod_spec(layer, TM), _full_spec((D, D))],
        out_specs=tok,
        out_shape=jax.ShapeDtypeStruct((T_ALL, D), F32),
        compiler_params=_cparams(("arbitrary",)),
        name="attn_out_proj",
    )(x, a, mod4, w)


def _rope_tables():
    rows = DEC_SEQ // GRID_W
    row = jnp.repeat(jnp.arange(rows), GRID_W).astype(F32)
    col = jnp.tile(jnp.arange(GRID_W), rows).astype(F32)
    inv = ROPE_THETA ** (-jnp.arange(ROPE_FREQS, dtype=F32) / ROPE_FREQS)
    ang = jnp.concatenate([row[:, None] * inv, col[:, None] * inv], axis=-1)
    cos = jnp.cos(ang)
    sin = jnp.sin(ang)
    cos_h = jnp.concatenate([cos, cos], axis=-1)
    sin_h = jnp.concatenate([sin, sin], axis=-1)
    return jnp.tile(cos_h, (1, N_HEADS)), jnp.tile(sin_h, (1, N_HEADS))


def _rot_half_cols(w):
    k = w.shape[0]
    w4 = w.reshape(k, -1, 2, HEAD_DIM // 2)
    return jnp.stack([-w4[:, :, 1], w4[:, :, 0]], axis=2).reshape(k, -1)


def _expand_kv_cols(w):
    k = w.shape[0]
    w3 = w.reshape(k, N_KV_HEADS, 1, HEAD_DIM)
    return jnp.broadcast_to(w3, (k, N_KV_HEADS, Q_PER_KV, HEAD_DIM)).reshape(k, N_HEADS * HEAD_DIM)


FG = D // FNET_GROUPS


def _fnet_kernel(x_ref, mod_ref, g_ref, cs_ref, fl_ref, w_ref, o_ref, h_s, ab_s):
    sh = mod_ref[:, 0:D]
    sc = mod_ref[:, D:2 * D]
    gate = mod_ref[:, 2 * D:3 * D]
    for c in range(TM // ROW_CHUNK):
        r0 = c * ROW_CHUNK
        h_s[r0:r0 + ROW_CHUNK, :] = _norm_mod(x_ref[r0:r0 + ROW_CHUNK, :], g_ref[...], sc, sh).astype(BF16)
    for g in range(FNET_GROUPS):
        ab = _dot(h_s[:, g * FG:(g + 1) * FG], cs_ref[...])
        ab_s[0:TM, g * FG:(g + 1) * FG] = ab[:, 0:FG].astype(BF16)
        ab_s[TM:2 * TM, g * FG:(g + 1) * FG] = ab[:, FG:2 * FG].astype(BF16)
    for c in range(TM // MM_CHUNK):
        r0 = c * MM_CHUNK
        f = _dot(fl_ref[r0:r0 + MM_CHUNK, :], ab_s[...])
        y = _dot(f.astype(BF16), w_ref[...])
        o_ref[r0:r0 + MM_CHUNK, :] = x_ref[r0:r0 + MM_CHUNK, :] + gate * y


def _fnet_mixer(x, mod4, layer, g_mix, cs, fl, w_out):
    nct = T_CTX // TM
    tok = pl.BlockSpec((TM, D), lambda i: (i, 0))
    return pl.pallas_call(
        _fnet_kernel,
        grid=(T_ALL // TM,),
        in_specs=[tok, _mod_spec(layer, TM), _row_spec(), _full_spec((FG, 2 * FG)),
                  pl.BlockSpec((None, TM, 2 * TM), lambda i: (jnp.where(i < nct, 0, 1), 0, 0)),
                  _full_spec((D, D))],
        out_specs=tok,
        out_shape=jax.ShapeDtypeStruct((T_ALL, D), F32),
        scratch_shapes=[pltpu.VMEM((TM, D), BF16), pltpu.VMEM((2 * TM, D), BF16)],
        compiler_params=_cparams(("arbitrary",)),
        name="fnet_mixer",
    )(x, mod4, g_mix.reshape(1, D), cs, fl, w_out)


def _dft_cos_sin(n):
    k = np.arange(n)
    ang = 2.0 * np.pi * ((k[:, None] * k[None, :]) % n) / n
    return np.cos(ang), np.sin(ang)


def _fnet_tables():
    cc, sc = _dft_cos_sin(FG)
    cs = np.concatenate([cc, sc], axis=1) / math.sqrt(FG)
    mats = []
    for seq in (SEQ, DEC_SEQ):
        cl, sl = _dft_cos_sin(seq)
        reps = TM // seq
        eye = np.eye(reps)
        mats.append(np.concatenate([np.kron(eye, cl), -np.kron(eye, sl)], axis=1) / math.sqrt(seq))
    return jnp.asarray(cs, F32).astype(BF16), jnp.asarray(np.stack(mats), F32).astype(BF16)


ROUTER_TM = 512


def _router_kernel(x_ref, mod_ref, g_ref, wr_ref, h_ref, sel_ref, gates_ref):
    sh = mod_ref[:, 3 * D:4 * D]
    sc = mod_ref[:, 4 * D:5 * D]
    w_hi = wr_ref[0]
    w_lo = wr_ref[1]
    for c in range(ROUTER_TM // ROW_CHUNK):
        r0 = c * ROW_CHUNK
        h = _norm_mod(x_ref[r0:r0 + ROW_CHUNK, :], g_ref[...], sc, sh)
        h_ref[r0:r0 + ROW_CHUNK, :] = h
        h_hi, h_lo = _split_bf16(h)
        logits = _dot(h_hi, w_hi) + _dot(h_lo, w_hi) + _dot(h_hi, w_lo)
        lane = lax.broadcasted_iota(I32, logits.shape, 1)
        lg = jnp.where(lane < N_EXPERTS, logits, -jnp.inf)
        m1 = jnp.max(lg, axis=-1, keepdims=True)
        i1 = jnp.min(jnp.where(lg == m1, lane, LANES), axis=-1, keepdims=True)
        lg2 = jnp.where(lane == i1, -jnp.inf, lg)
        m2 = jnp.max(lg2, axis=-1, keepdims=True)
        i2 = jnp.min(jnp.where(lg2 == m2, lane, LANES), axis=-1, keepdims=True)
        e2 = jnp.exp(m2 - m1)
        den = 1.0 + e2
        sel_ref[r0:r0 + ROW_CHUNK, :] = jnp.where(lane == i1, 1, jnp.where(lane == i2, 2, 0)).astype(I32)
        gates_ref[r0:r0 + ROW_CHUNK, :] = jnp.where(lane == 0, 1.0 / den, jnp.where(lane == 1, e2 / den, 0.0))


def _router(x, mod4, layer, g_ffn, w_router):
    wr = jnp.zeros((D, LANES), F32).at[:, 0:N_EXPERTS].set(w_router)
    wr_hi = wr.astype(BF16)
    wr_lo = (wr - wr_hi.astype(F32)).astype(BF16)
    tok = pl.BlockSpec((ROUTER_TM, D), lambda i: (i, 0))
    nar = pl.BlockSpec((ROUTER_TM, LANES), lambda i: (i, 0))
    return pl.pallas_call(
        _router_kernel,
        grid=(T_ALL // ROUTER_TM,),
        in_specs=[tok, _mod_spec(layer, ROUTER_TM), _row_spec(), _full_spec((2, D, LANES))],
        out_specs=[tok, nar, nar],
        out_shape=[jax.ShapeDtypeStruct((T_ALL, D), F32),
                   jax.ShapeDtypeStruct((T_ALL, LANES), I32),
                   jax.ShapeDtypeStruct((T_ALL, LANES), F32)],
        compiler_params=_cparams(("arbitrary",)),
        name="moe_router",
    )(x, mod4, g_ffn.reshape(1, D), jnp.stack([wr_hi, wr_lo]))


def _route_plan(sel):
    s8 = sel[:, 0:N_EXPERTS]
    onehot = (s8 > 0).astype(I32)
    csum = jnp.cumsum(onehot, axis=0)
    rank = csum - onehot
    counts = csum[-1]
    padded = ((counts + TME - 1) // TME) * TME
    gend = jnp.cumsum(padded)
    gstart = gend - padded
    pos_te = gstart[None, :] + rank
    pos1 = jnp.sum(jnp.where(s8 == 1, pos_te, 0), axis=1).astype(I32)
    pos2 = jnp.sum(jnp.where(s8 == 2, pos_te, 0), axis=1).astype(I32)
    src = _inverse_map(pos1, pos2)
    n_valid = gend[-1] // TME
    tile_start = jnp.arange(N_EXPERT_TILES, dtype=I32) * TME
    eff_start = jnp.minimum(tile_start, jnp.maximum(n_valid - 1, 0) * TME)
    tile_expert = jnp.minimum(jnp.sum((eff_start[:, None] >= gend[None, :]).astype(I32), axis=1),
                              N_EXPERTS - 1).astype(I32)
    return pos1, pos2, src, tile_expert, n_valid.astype(I32).reshape(1)


def _inverse_map_kernel(p1_ref, p2_ref, src_ref):
    def clear(p, carry):
        src_ref[p] = 0
        return carry
    lax.fori_loop(0, P_MAX, clear, 0, unroll=8)

    def body(t, carry):
        src_ref[p1_ref[t]] = t
        src_ref[p2_ref[t]] = t
        return carry
    lax.fori_loop(0, T_ALL, body, 0, unroll=8)


def _inverse_map(pos1, pos2):
    smem = pl.BlockSpec(memory_space=pltpu.SMEM)
    return pl.pallas_call(
        _inverse_map_kernel,
        in_specs=[smem, smem], out_specs=smem,
        out_shape=jax.ShapeDtypeStruct((P_MAX,), I32),
        name="moe_inverse_map",
    )(pos1, pos2)


EXP_TF = 512
EXP_NF = D_FF_EXPERT // EXP_TF
GATHER_ROWS = 152
XS_ROWS = EXP_NF * GATHER_ROWS
assert XS_ROWS >= TME


def _expert_kernel(te_ref, nv_ref, src_ref, h_hbm, wg_ref, wu_ref, wd_ref, o_ref, xs_buf, xb_s, sem):
    m = pl.program_id(0)
    f = pl.program_id(1)
    nt = pl.num_programs(0)
    nf = pl.num_programs(1)
    n_valid = nv_ref[0]
    slot = m % 2
    nslot = 1 - slot

    def row_copy(tile, r, dst_slot):
        t = src_ref[tile * TME + jnp.minimum(r, TME - 1)]
        return pltpu.make_async_copy(h_hbm.at[pl.ds(t, 1), :], xs_buf.at[dst_slot, pl.ds(r, 1), :],
                                     sem.at[dst_slot])

    def wait_slot(s):
        pltpu.make_async_copy(h_hbm.at[pl.ds(0, XS_ROWS), :], xs_buf.at[s], sem.at[s]).wait()

    @pl.when(m < n_valid)
    def _():
        @pl.when(f == 0)
        def _():
            @pl.when(m == 0)
            def _():
                def body(r, carry):
                    row_copy(0, r, 0).start()
                    return carry
                lax.fori_loop(0, XS_ROWS, body, 0, unroll=8)

            wait_slot(slot)
            for c in range(TME // ROW_CHUNK):
                r0 = c * ROW_CHUNK
                xb_s[r0:r0 + ROW_CHUNK, :] = xs_buf[slot, r0:r0 + ROW_CHUNK, :].astype(BF16)

        nxt = jnp.minimum(m + 1, nt - 1)
        base = f * GATHER_ROWS
        for k in range(GATHER_ROWS):
            row_copy(nxt, base + k, nslot).start()

        xb = xb_s[...]
        gg = _dot(xb, wg_ref[...].astype(BF16))
        uu = _dot(xb, wu_ref[...].astype(BF16))
        a = (gg * jax.nn.sigmoid(gg) * uu).astype(BF16)
        contrib = _dot(a, wd_ref[...].astype(BF16))

        @pl.when(f == 0)
        def _():
            o_ref[...] = contrib

        @pl.when(f > 0)
        def _():
            o_ref[...] = o_ref[...] + contrib

        @pl.when(jnp.logical_and(m == nt - 1, f == nf - 1))
        def _():
            wait_slot(nslot)

    @pl.when(jnp.logical_and(m >= n_valid, f == 0))
    def _():
        @pl.when(m == n_valid)
        def _():
            wait_slot(slot)
        o_ref[...] = jnp.zeros_like(o_ref)


def _expert_ffn(tile_expert, n_valid, src, h, w_gu, w_down, li):
    def feff(m, f, nv):
        return jnp.where(m < nv[0], f, EXP_NF - 1)
    return pl.pallas_call(
        _expert_kernel,
        grid_spec=pltpu.PrefetchScalarGridSpec(
            num_scalar_prefetch=3, grid=(N_EXPERT_TILES, EXP_NF),
            in_specs=[pl.BlockSpec(memory_space=pl.ANY),
                      pl.BlockSpec((None, None, D, EXP_TF),
                                   lambda m, f, te, nv, sr: (li, te[m], 0, feff(m, f, nv))),
                      pl.BlockSpec((None, None, D, EXP_TF),
                                   lambda m, f, te, nv, sr: (li, te[m], 0, EXP_NF + feff(m, f, nv))),
                      pl.BlockSpec((None, None, EXP_TF, D),
                                   lambda m, f, te, nv, sr: (li, te[m], feff(m, f, nv), 0))],
            out_specs=pl.BlockSpec((TME, D), lambda m, f, te, nv, sr: (m, 0)),
            scratch_shapes=[pltpu.VMEM((2, XS_ROWS, D), F32), pltpu.VMEM((TME, D), BF16),
                            pltpu.SemaphoreType.DMA((2,))]),
        out_shape=jax.ShapeDtypeStruct((P_MAX, D), F32),
        compiler_params=_cparams(("arbitrary", "arbitrary")),
        name="moe_expert_swiglu",
    )(tile_expert, n_valid, src, h, w_gu, w_gu, w_down)


COMB_TM = 512


def _combine_kernel(p1_ref, p2_ref, x_ref, gates_ref, mod_ref, gf_ref, y_hbm, o_ref, a_s, b_s, sem, *, final):
    i = pl.program_id(0)

    def body(r, carry):
        t = i * COMB_TM + r
        pltpu.make_async_copy(y_hbm.at[pl.ds(p1_ref[t], 1), :], a_s.at[pl.ds(r, 1), :], sem.at[0]).start()
        pltpu.make_async_copy(y_hbm.at[pl.ds(p2_ref[t], 1), :], b_s.at[pl.ds(r, 1), :], sem.at[1]).start()
        return carry
    lax.fori_loop(0, COMB_TM, body, 0, unroll=8)
    pltpu.make_async_copy(y_hbm.at[pl.ds(0, COMB_TM), :], a_s, sem.at[0]).wait()
    pltpu.make_async_copy(y_hbm.at[pl.ds(0, COMB_TM), :], b_s, sem.at[1]).wait()

    gate = mod_ref[:, 5 * D:6 * D]
    for c in range(COMB_TM // ROW_CHUNK):
        r0 = c * ROW_CHUNK
        w1 = gates_ref[r0:r0 + ROW_CHUNK, 0:1]
        w2 = gates_ref[r0:r0 + ROW_CHUNK, 1:2]
        y = w1 * a_s[r0:r0 + ROW_CHUNK, :] + w2 * b_s[r0:r0 + ROW_CHUNK, :]
        xn = x_ref[r0:r0 + ROW_CHUNK, :] + gate * y
        if final:
            ms = jnp.mean(xn * xn, axis=-1, keepdims=True)
            xn = xn * lax.rsqrt(ms + EPS) * gf_ref[...]
        o_ref[r0:r0 + ROW_CHUNK, :] = xn


def _combine(pos1, pos2, x, gates, mod4, layer, g_final, y, final):
    tok = pl.BlockSpec((COMB_TM, D), lambda i, *_: (i, 0))
    return pl.pallas_call(
        functools.partial(_combine_kernel, final=final),
        grid_spec=pltpu.PrefetchScalarGridSpec(
            num_scalar_prefetch=2, grid=(T_ALL // COMB_TM,),
            in_specs=[tok, pl.BlockSpec((COMB_TM, LANES), lambda i, *_: (i, 0)),
                      _mod_spec(layer, COMB_TM), _row_spec(),
                      pl.BlockSpec(memory_space=pl.ANY)],
            out_specs=tok,
            scratch_shapes=[pltpu.VMEM((COMB_TM, D), F32), pltpu.VMEM((COMB_TM, D), F32),
                            pltpu.SemaphoreType.DMA((2,))]),
        out_shape=jax.ShapeDtypeStruct((T_ALL, D), F32),
        compiler_params=_cparams(("arbitrary",)),
        name="moe_combine_final" if final else "moe_combine",
    )(pos1, pos2, x, gates, mod4, g_final.reshape(1, D), y)


def _moe_layer(x, mod4, layer, g_ffn, w_router, w_gu, w_down, li, g_final, final):
    h, sel, gates = _router(x, mod4, layer, g_ffn, w_router)
    pos1, pos2, src, tile_expert, n_valid = _route_plan(sel)
    ys = _expert_ffn(tile_expert, n_valid, src, h, w_gu, w_down, li)
    return _combine(pos1, pos2, x, gates, mod4, layer, g_final, ys, final)


def kernel(x_prompt, x_sample, cache_k, cache_v, state_ssm_re, state_ssm_im, c, c_ctx, w_ada, b_ada, g_mix, g_ffn, g_final, conv_w_in, conv_w, conv_w_out, s5_lambda_re, s5_lambda_im, s5_b_re, s5_b_im, s5_c_re, s5_c_im, s5_log_dt, s5_d, s5_w_glu, attn_w_q, attn_w_kv, attn_w_o, attn_sink, fnet_w_out, ffn_w_gu, ffn_w_down, moe_w_router, moe_w_gu, moe_w_down):
    x = jnp.concatenate([x_prompt.reshape(T_CTX, D), x_sample.reshape(T_LAT, D)], axis=0)
    cond16 = jnp.concatenate([jnp.broadcast_to(c_ctx[None, :], (8, D)), c], axis=0)
    mod3 = _mod_all(cond16, w_ada, b_ada)
    mod4 = mod3.reshape(DEPTH, 16, 1, N_MOD * D)

    x = _conv_mixer(x, mod4, 0, g_mix[0], conv_w_in[0].astype(BF16), conv_w[0], conv_w_out[0].astype(BF16))
    x = _dense_ffn(x, mod4, 0, g_ffn[0], ffn_w_gu[0].astype(BF16), ffn_w_down[0].astype(BF16))

    lbr, lbi, bbr, bbi = _s5_prep(s5_lambda_re[0], s5_lambda_im[0], s5_log_dt[0], s5_b_re[0], s5_b_im[0])
    w_b, w_c, lam_s = _s5_weights(lbr, lbi, bbr, bbi, s5_c_re[0], s5_c_im[0])
    w_glu = s5_w_glu[0].astype(BF16)
    xc = x[0:T_CTX].reshape(BATCH, SEQ, D).transpose(1, 0, 2).reshape(T_CTX, D)
    xl = x[T_CTX:].reshape(DEC_BATCH, DEC_SEQ, D).transpose(1, 0, 2).reshape(T_LAT, D)
    h0_ctx = jnp.zeros((2, S5_JT, BATCH, 2 * S5_HALF), F32)
    h0_lat = _s5_state_in(state_ssm_re[:, 0], state_ssm_im[:, 0])
    yc, fin_c = _s5_scan(xc, BATCH, mod3, 1, 0, g_mix[1], w_b, w_c, lam_s, h0_ctx)
    yl, _ = _s5_scan(xl, DEC_BATCH, mod3, 1, 1, g_mix[1], w_b, w_c, lam_s, h0_lat)
    xc = _s5_glu(xc, yc, mod3, 1, 0, g_mix[1], s5_d[0], w_glu)
    xl = _s5_glu(xl, yl, mod3, 1, 1, g_mix[1], s5_d[0], w_glu)
    x = jnp.concatenate([xc.reshape(SEQ, BATCH, D).transpose(1, 0, 2).reshape(T_CTX, D),
                         xl.reshape(DEC_SEQ, DEC_BATCH, D).transpose(1, 0, 2).reshape(T_LAT, D)], axis=0)
    new_re, new_im = _s5_state_out(fin_c)
    x = _moe_layer(x, mod4, 1, g_ffn[1], moe_w_router[0], moe_w_gu, moe_w_down, 0, g_final, False)

    wq = attn_w_q[0]
    wk = _expand_kv_cols(attn_w_kv[0][:, 0:N_KV_HEADS * HEAD_DIM])
    wv = _expand_kv_cols(attn_w_kv[0][:, N_KV_HEADS * HEAD_DIM:])
    w_ctx = jnp.concatenate([wq, wk, wv, attn_w_kv[0]], axis=1).astype(BF16)
    w_lat = jnp.concatenate([wq, _rot_half_cols(wq), wk, _rot_half_cols(wk), wv], axis=1).astype(BF16)
    cos_t, sin_t = _rope_tables()
    q_c, k_c, v_c, kv_c = _qkv_ctx(x, mod4, 2, g_mix[2], w_ctx)
    q_l, k_l, v_l = _qkv_lat(x, mod4, 2, g_mix[2], w_lat, cos_t, sin_t)
    sink = attn_sink[0]
    o_c = _attn_ctx(sink, q_c, k_c, v_c)
    expand = lambda a: jnp.broadcast_to(
        a.reshape(DEC_BATCH * PAST_LEN, N_KV_HEADS, 1, HEAD_DIM),
        (DEC_BATCH * PAST_LEN, N_KV_HEADS, Q_PER_KV, HEAD_DIM)).reshape(DEC_BATCH * PAST_LEN, D).astype(BF16)
    o_l = _attn_lat(sink, q_l, k_l, v_l, expand(cache_k[:, 0]), expand(cache_v[:, 0]))
    x = _resproj(x, jnp.concatenate([o_c, o_l], axis=0), mod4, 2, attn_w_o[0].astype(BF16))
    kvw = N_KV_HEADS * HEAD_DIM
    new_k = kv_c[:, 0:kvw].reshape(BATCH, 1, SEQ, N_KV_HEADS, HEAD_DIM)
    new_v = kv_c[:, kvw:].reshape(BATCH, 1, SEQ, N_KV_HEADS, HEAD_DIM)
    x = _dense_ffn(x, mod4, 2, g_ffn[2], ffn_w_gu[1].astype(BF16), ffn_w_down[1].astype(BF16))

    cs, fl = _fnet_tables()
    x = _fnet_mixer(x, mod4, 3, g_mix[3], cs, fl, fnet_w_out[0].astype(BF16))
    x = _moe_layer(x, mod4, 3, g_ffn[3], moe_w_router[1], moe_w_gu, moe_w_down, 1, g_final, True)

    y_prompt = x[0:T_CTX].reshape(BATCH, SEQ, D)
    y_sample = x[T_CTX:].reshape(DEC_BATCH, DEC_SEQ, D)
    return (y_prompt, y_sample, new_k, new_v, new_re[:, None], new_im[:, None])
```

```python
import functools
import math

import numpy as np
import jax
import jax.numpy as jnp
from jax import lax
from jax.experimental import pallas as pl
from jax.experimental.pallas import tpu as pltpu

F32 = jnp.float32
BF16 = jnp.bfloat16
I32 = jnp.int32

D = 1024
BATCH = 16
SEQ = 256
DEPTH = 4
DEC_BATCH = 8
DEC_SEQ = 1024
PAST_LEN = 256
GRID_W = 64
EPS = 1e-6
N_MOD = 6
S5_GROUP = 16
S5_GROUPS = D // S5_GROUP
S5_STATE = 64
HEAD_DIM = 64
N_HEADS = D // HEAD_DIM
N_KV_HEADS = 4
Q_PER_KV = N_HEADS // N_KV_HEADS
WINDOW = 128
ROPE_THETA = 10000.0
ROPE_FREQS = HEAD_DIM // 4
FNET_GROUPS = 4
D_FF = 2816
N_EXPERTS = 8
TOP_K = 2
D_FF_EXPERT = 3584
NEG_INF = -1e30

T_CTX = BATCH * SEQ
T_LAT = DEC_BATCH * DEC_SEQ
T_ALL = T_CTX + T_LAT

VMEM_LIMIT_V7X = 56 * 1024 * 1024
LANES = 128

TM = 1024
ROW_CHUNK = 256
MM_CHUNK = 512
TME = 1024
N_EXPERT_TILES = (T_ALL * TOP_K) // TME + N_EXPERTS
P_MAX = N_EXPERT_TILES * TME


def _cparams(sem):
    return pltpu.CompilerParams(dimension_semantics=sem, vmem_limit_bytes=VMEM_LIMIT_V7X)


def _dot(a, b):
    return jnp.dot(a, b, preferred_element_type=F32)


def _dot_nt(a, b):
    return lax.dot_general(a, b, (((1,), (1,)), ((), ())), preferred_element_type=F32)


def _split_bf16(a):
    hi = a.astype(BF16)
    lo = (a - hi.astype(F32)).astype(BF16)
    return hi, lo


def _tile_rows(a, reps, axis=0):
    assert axis == 0
    return jnp.concatenate([a] * reps, axis=0)


def _norm_mod(x, g, sc, sh):
    ms = jnp.mean(x * x, axis=-1, keepdims=True)
    y = x * lax.rsqrt(ms + EPS) * g
    return y * (1.0 + sc) + sh


def _mod_row(i, tm):
    nct = T_CTX // tm
    lpb = DEC_SEQ // tm
    return jnp.where(i < nct, 0, 8 + (i - nct) // lpb)


def _mod_spec(layer, tm):
    return pl.BlockSpec((None, None, 1, N_MOD * D), lambda i, *_: (layer, _mod_row(i, tm), 0, 0))


def _row_spec():
    return pl.BlockSpec((1, D), lambda *_: (0, 0))


def _full_spec(shape):
    nd = len(shape)
    return pl.BlockSpec(shape, lambda *_: (0,) * nd, pipeline_mode=pl.Buffered(1))


def _mod_kernel(c_ref, w_ref, b_ref, o_ref):
    c = c_ref[...]
    s = (c * jax.nn.sigmoid(c)).astype(BF16)
    o_ref[...] = _dot(s, w_ref[...].astype(BF16)) + b_ref[...]


def _mod_all(cond16, w_ada, b_ada):
    tn = 1024
    return pl.pallas_call(
        _mod_kernel,
        grid=(DEPTH, N_MOD * D // tn),
        in_specs=[pl.BlockSpec((16, D), lambda l, n: (0, 0)),
                  pl.BlockSpec((None, D, tn), lambda l, n: (l, 0, n)),
                  pl.BlockSpec((None, 1, tn), lambda l, n: (l, 0, n))],
        out_specs=pl.BlockSpec((None, 16, tn), lambda l, n: (l, 0, n)),
        out_shape=jax.ShapeDtypeStruct((DEPTH, 16, N_MOD * D), F32),
        compiler_params=_cparams(("arbitrary", "arbitrary")),
        name="adaln_mod",
    )(cond16, w_ada, b_ada.reshape(DEPTH, 1, N_MOD * D))


def _conv_kernel(x_ref, mod_ref, g_ref, win_ref, cw_ref, wout_ref, o_ref, gb_s, u_s, z_s):
    i = pl.program_id(0)
    is_ctx = i < (T_CTX // TM)
    sh = mod_ref[:, 0:D]
    sc = mod_ref[:, D:2 * D]
    gate = mod_ref[:, 2 * D:3 * D]
    g = g_ref[...]
    zero8 = jnp.zeros((8, D), F32)
    u_s[0:8, :] = zero8
    u_s[8 + TM:16 + TM, :] = zero8
    for c in range(TM // MM_CHUNK):
        r0 = c * MM_CHUNK
        h = _norm_mod(x_ref[r0:r0 + MM_CHUNK, :], g, sc, sh).astype(BF16)
        proj = _dot(h, win_ref[...])
        gb_s[r0:r0 + MM_CHUNK, :] = proj[:, 0:D]
        u_s[8 + r0:8 + r0 + MM_CHUNK, :] = proj[:, D:2 * D] * proj[:, 2 * D:3 * D]
    row = lax.broadcasted_iota(I32, (ROW_CHUNK, 1), 0)
    first = jnp.logical_and(is_ctx, row == 0)
    last = jnp.logical_and(is_ctx, row == ROW_CHUNK - 1)
    for c in range(TM // ROW_CHUNK):
        r0 = c * ROW_CHUNK
        up = jnp.where(first, 0.0, u_s[7 + r0:7 + r0 + ROW_CHUNK, :])
        mid = u_s[8 + r0:8 + r0 + ROW_CHUNK, :]
        dn = jnp.where(last, 0.0, u_s[9 + r0:9 + r0 + ROW_CHUNK, :])
        conv = up * cw_ref[0:1, :] + mid * cw_ref[1:2, :] + dn * cw_ref[2:3, :]
        z_s[r0:r0 + ROW_CHUNK, :] = (gb_s[r0:r0 + ROW_CHUNK, :] * conv).astype(BF16)
    for c in range(TM // MM_CHUNK):
        r0 = c * MM_CHUNK
        y = _dot(z_s[r0:r0 + MM_CHUNK, :], wout_ref[...])
        o_ref[r0:r0 + MM_CHUNK, :] = x_ref[r0:r0 + MM_CHUNK, :] + gate * y


def _conv_mixer(x, mod4, layer, g_mix, w_in, conv_w, w_out):
    assert SEQ == ROW_CHUNK and DEC_SEQ == TM
    return pl.pallas_call(
        _conv_kernel,
        grid=(T_ALL // TM,),
        in_specs=[pl.BlockSpec((TM, D), lambda i: (i, 0)),
                  _mod_spec(layer, TM), _row_spec(),
                  _full_spec((D, 3 * D)), _full_spec((3, D)), _full_spec((D, D))],
        out_specs=pl.BlockSpec((TM, D), lambda i: (i, 0)),
        out_shape=jax.ShapeDtypeStruct((T_ALL, D), F32),
        scratch_shapes=[pltpu.VMEM((TM, D), F32), pltpu.VMEM((TM + 16, D), F32), pltpu.VMEM((TM, D), BF16)],
        compiler_params=_cparams(("arbitrary",)),
        name="conv_mixer",
    )(x, mod4, g_mix.reshape(1, D), w_in, conv_w, w_out)


FFN_TM = 512
MXU_WIDTH_V7X = 256
FFN_SPLITS = ((0, 6 * MXU_WIDTH_V7X), (6 * MXU_WIDTH_V7X, D_FF))
assert D_FF % MXU_WIDTH_V7X == 0


def _ffn_kernel(x_ref, mod_ref, g_ref, wgu_ref, wd_ref, o_ref):
    sh = mod_ref[:, 3 * D:4 * D]
    sc = mod_ref[:, 4 * D:5 * D]
    gate = mod_ref[:, 5 * D:6 * D]
    x = x_ref[...]
    h = _norm_mod(x, g_ref[...], sc, sh).astype(BF16)
    acc = None
    for lo, hi in FFN_SPLITS:
        gg = _dot(h, wgu_ref[:, lo:hi])
        uu = _dot(h, wgu_ref[:, D_FF + lo:D_FF + hi])
        a = (gg * jax.nn.sigmoid(gg) * uu).astype(BF16)
        contrib = _dot(a, wd_ref[lo:hi, :])
        acc = contrib if acc is None else acc + contrib
    o_ref[...] = x + gate * acc


def _dense_ffn(x, mod4, layer, g_ffn, w_gu, w_down):
    tok = pl.BlockSpec((FFN_TM, D), lambda i: (i, 0))
    return pl.pallas_call(
        _ffn_kernel,
        grid=(T_ALL // FFN_TM,),
        in_specs=[tok, _mod_spec(layer, FFN_TM), _row_spec(),
                  _full_spec((D, 2 * D_FF)), _full_spec((D_FF, D))],
        out_specs=tok,
        out_shape=jax.ShapeDtypeStruct((T_ALL, D), F32),
        compiler_params=_cparams(("arbitrary",)),
        name="dense_swiglu",
    )(x, mod4, g_ffn.reshape(1, D), w_gu, w_down)


def _s5_prep_kernel(lr_ref, li_ref, ldt_ref, br_ref, bi_ref, lbr_ref, lbi_ref, bbr_ref, bbi_ref):
    lr = lr_ref[...]
    li = li_ref[...]
    dt = jnp.exp(ldt_ref[...])
    mag = jnp.exp(lr * dt)
    ar = mag * jnp.cos(li * dt)
    ai = mag * jnp.sin(li * dt)
    nr = ar - 1.0
    den = lr * lr + li * li
    fr = (nr * lr + ai * li) / den
    fi = (ai * lr - nr * li) / den
    br = br_ref[...]
    bi = bi_ref[...]
    lbr_ref[...] = ar
    lbi_ref[...] = ai
    bbr_ref[...] = fr * br - fi * bi
    bbi_ref[...] = fr * bi + fi * br


def _s5_prep(lam_re, lam_im, log_dt, b_re, b_im):
    rows = 2 * S5_GROUPS
    cols = S5_STATE * S5_GROUP
    exp = lambda a: jnp.repeat(a.reshape(rows, S5_STATE), S5_GROUP, axis=1)
    ldt = jnp.broadcast_to(log_dt.reshape(rows, 1), (rows, cols))
    outs = pl.pallas_call(
        _s5_prep_kernel,
        out_shape=[jax.ShapeDtypeStruct((rows, cols), F32)] * 4,
        name="s5_discretize",
    )(exp(lam_re), exp(lam_im), ldt, b_re.reshape(rows, cols), b_im.reshape(rows, cols))
    lbr, lbi, bbr, bbi = outs
    shp = (2, S5_GROUPS, S5_STATE, S5_GROUP)
    return lbr.reshape(shp)[..., 0], lbi.reshape(shp)[..., 0], bbr.reshape(shp), bbi.reshape(shp)


S5_JT = 8
S5_GPT = LANES // S5_GROUP
S5_HALF = S5_GPT * S5_STATE
S5_ROWS = 512


def _s5_scan_kernel(x_ref, mod_ref, g_ref, wb_ref, wc_ref, lam_ref, h0_ref,
                    y_ref, fin_ref, bu_s, st_s, *, nb, jgroup):
    d = pl.program_id(0)
    c = pl.program_id(1)
    lc = S5_ROWS // nb

    @pl.when(c == 0)
    def _():
        st_s[...] = h0_ref[...]

    rep = S5_ROWS // 8
    sh = _tile_rows(mod_ref[:, 0:D], rep, axis=0)
    sc = _tile_rows(mod_ref[:, D:2 * D], rep, axis=0)
    u = _norm_mod(x_ref[...], g_ref[...], sc, sh).astype(BF16)
    for j in range(S5_JT):
        bu_s[j] = _dot(u[:, j * LANES:(j + 1) * LANES], wb_ref[j])

    for j0 in range(0, S5_JT, jgroup):
        js = list(range(j0, j0 + jgroup))
        lam = [(jnp.broadcast_to(lam_ref[j][:, 0:S5_HALF], (nb, S5_HALF)),
                jnp.broadcast_to(lam_ref[j][:, S5_HALF:], (nb, S5_HALF))) for j in js]

        def body(t, carry):
            l = jnp.where(d == 0, t, lc - 1 - t)
            r0 = pl.multiple_of(l * nb, nb)
            out = []
            for k, j in enumerate(js):
                sr, si = carry[k]
                ar, ai = lam[k]
                bu = bu_s[j, pl.ds(r0, nb), :]
                hr = ar * sr - ai * si + bu[:, 0:S5_HALF]
                hi = ar * si + ai * sr + bu[:, S5_HALF:]
                bu_s[j, pl.ds(r0, nb), 0:S5_HALF] = hr
                bu_s[j, pl.ds(r0, nb), S5_HALF:] = hi
                out.append((hr, hi))
            return tuple(out)

        init = tuple((st_s[j][:, 0:S5_HALF], st_s[j][:, S5_HALF:]) for j in js)
        fin = lax.fori_loop(0, lc, body, init)
        for k, j in enumerate(js):
            st_s[j, :, 0:S5_HALF] = fin[k][0]
            st_s[j, :, S5_HALF:] = fin[k][1]

    for j in range(S5_JT):
        y_ref[:, j * LANES:(j + 1) * LANES] = _dot(bu_s[j].astype(BF16), wc_ref[j])
    fin_ref[...] = st_s[...]


def _s5_scan(xt, nb, mod3, layer, path, g_mix, w_b, w_c, lam_s, h0):
    rows = xt.shape[0]
    nc = rows // S5_ROWS
    chunk = lambda d, c: c + d * (nc - 1 - 2 * c)
    kern = functools.partial(_s5_scan_kernel, nb=nb, jgroup=2 if nb == 8 else 1)
    return pl.pallas_call(
        kern,
        grid=(2, nc),
        in_specs=[pl.BlockSpec((S5_ROWS, D), lambda d, c: (chunk(d, c), 0)),
                  pl.BlockSpec((None, 8, N_MOD * D), lambda d, c: (layer, path, 0)),
                  _row_spec(),
                  pl.BlockSpec((None, S5_JT, LANES, 2 * S5_HALF), lambda d, c: (d, 0, 0, 0)),
                  pl.BlockSpec((None, S5_JT, 2 * S5_HALF, LANES), lambda d, c: (d, 0, 0, 0)),
                  pl.BlockSpec((None, S5_JT, 1, 2 * S5_HALF), lambda d, c: (d, 0, 0, 0)),
                  pl.BlockSpec((None, S5_JT, nb, 2 * S5_HALF), lambda d, c: (d, 0, 0, 0))],
        out_specs=[pl.BlockSpec((None, S5_ROWS, D), lambda d, c: (d, chunk(d, c), 0)),
                   pl.BlockSpec((None, S5_JT, nb, 2 * S5_HALF), lambda d, c: (d, 0, 0, 0))],
        out_shape=[jax.ShapeDtypeStruct((2, rows, D), F32),
                   jax.ShapeDtypeStruct((2, S5_JT, nb, 2 * S5_HALF), F32)],
        scratch_shapes=[pltpu.VMEM((S5_JT, S5_ROWS, 2 * S5_HALF), F32),
                        pltpu.VMEM((S5_JT, nb, 2 * S5_HALF), F32)],
        compiler_params=_cparams(("arbitrary", "arbitrary")),
        name="s5_scan_b%d" % nb,
    )(xt, mod3, g_mix.reshape(1, D), w_b, w_c, lam_s, h0)


S5_GLU_ROWS = 512


def _s5_glu_kernel(x_ref, yf_ref, yb_ref, mod_ref, g_ref, dsk_ref, w_ref, o_ref):
    rep = MM_CHUNK // 8
    sh = _tile_rows(mod_ref[:, 0:D], rep, axis=0)
    sc = _tile_rows(mod_ref[:, D:2 * D], rep, axis=0)
    gate = _tile_rows(mod_ref[:, 2 * D:3 * D], rep, axis=0)
    for c in range(S5_GLU_ROWS // MM_CHUNK):
        r0 = c * MM_CHUNK
        x = x_ref[r0:r0 + MM_CHUNK, :]
        u = _norm_mod(x, g_ref[...], sc, sh)
        y = u * dsk_ref[...] + yf_ref[r0:r0 + MM_CHUNK, :] + yb_ref[r0:r0 + MM_CHUNK, :]
        z = jax.nn.gelu(y).astype(BF16)
        ag = _dot(z, w_ref[...])
        out = ag[:, 0:D] * jax.nn.sigmoid(ag[:, D:2 * D])
        o_ref[r0:r0 + MM_CHUNK, :] = x + gate * out


def _s5_glu(xt, y2, mod3, layer, path, g_mix, d_skip, w_glu):
    rows = xt.shape[0]
    return pl.pallas_call(
        _s5_glu_kernel,
        grid=(rows // S5_GLU_ROWS,),
        in_specs=[pl.BlockSpec((S5_GLU_ROWS, D), lambda i: (i, 0)),
                  pl.BlockSpec((None, S5_GLU_ROWS, D), lambda i: (0, i, 0)),
                  pl.BlockSpec((None, S5_GLU_ROWS, D), lambda i: (1, i, 0)),
                  pl.BlockSpec((None, 8, N_MOD * D), lambda i: (layer, path, 0)),
                  _row_spec(), _row_spec(), _full_spec((D, 2 * D))],
        out_specs=pl.BlockSpec((S5_GLU_ROWS, D), lambda i: (i, 0)),
        out_shape=jax.ShapeDtypeStruct((rows, D), F32),
        compiler_params=_cparams(("arbitrary",)),
        name="s5_glu",
    )(xt, y2, y2, mod3, g_mix.reshape(1, D), d_skip.reshape(1, D), w_glu)


def _s5_weights(lbr, lbi, bbr, bbi, c_re, c_im):
    eye = jnp.eye(S5_GPT, dtype=F32)
    bb = jnp.stack([bbr, bbi]).reshape(2, 2, S5_JT, S5_GPT, S5_STATE, S5_GROUP)
    w_bu = jnp.einsum('rdjgps,gh->djgsrhp', bb, eye).reshape(2, S5_JT, LANES, 2 * S5_HALF)
    w_b = w_bu.astype(BF16)
    cc =jnp.stack([c_re, -c_im]).reshape(2, 2, S5_JT, S5_GPT, S5_GROUP, S5_STATE)
    w_c = jnp.einsum('rdjgsp,gh->djrgphs', cc, eye).reshape(2, S5_JT, 2 * S5_HALF, LANES).astype(BF16)
    lam_s = jnp.concatenate([lbr.reshape(2, S5_JT, S5_HALF), lbi.reshape(2, S5_JT, S5_HALF)], axis=-1)
    return w_b, w_c, lam_s.reshape(2, S5_JT, 1, 2 * S5_HALF)


def _s5_state_in(st_re, st_im):
    def lay(a):
        b = a.shape[0]
        return a.transpose(1, 0, 2, 3).reshape(2, b, S5_JT, S5_HALF).transpose(0, 2, 1, 3)
    return jnp.concatenate([lay(st_re), lay(st_im)], axis=-1)


def _s5_state_out(fin):
    def lay(a):
        b = a.shape[2]
        return a.transpose(2, 0, 1, 3).reshape(b, 2, S5_GROUPS, S5_STATE)
    return lay(fin[..., 0:S5_HALF]), lay(fin[..., S5_HALF:])


QKV_TM = 512


def _qkv_ctx_kernel(x_ref, mod_ref, g_ref, w_ref, q_ref, k_ref, v_ref, kv_ref):
    sh = mod_ref[:, 0:D]
    sc = mod_ref[:, D:2 * D]
    for c in range(QKV_TM // MM_CHUNK):
        r0 = c * MM_CHUNK
        h = _norm_mod(x_ref[r0:r0 + MM_CHUNK, :], g_ref[...], sc, sh).astype(BF16)
        q_ref[r0:r0 + MM_CHUNK, :] = _dot(h, w_ref[:, 0:D]).astype(BF16)
        k_ref[r0:r0 + MM_CHUNK, :] = _dot(h, w_ref[:, D:2 * D]).astype(BF16)
        v_ref[r0:r0 + MM_CHUNK, :] = _dot(h, w_ref[:, 2 * D:3 * D]).astype(BF16)
        kv_ref[r0:r0 + MM_CHUNK, :] = _dot(h, w_ref[:, 3 * D:3 * D + 2 * N_KV_HEADS * HEAD_DIM])


def _qkv_ctx(x, mod4, layer, g_mix, w_all):
    n = w_all.shape[1]
    bf = jax.ShapeDtypeStruct((T_CTX, D), BF16)
    tok = pl.BlockSpec((QKV_TM, D), lambda i: (i, 0))
    return pl.pallas_call(
        _qkv_ctx_kernel,
        grid=(T_CTX // QKV_TM,),
        in_specs=[tok, _mod_spec(layer, QKV_TM), _row_spec(), _full_spec((D, n))],
        out_specs=[tok, tok, tok, pl.BlockSpec((QKV_TM, 512), lambda i: (i, 0))],
        out_shape=[bf, bf, bf, jax.ShapeDtypeStruct((T_CTX, 512), F32)],
        compiler_params=_cparams(("arbitrary",)),
        name="qkv_ctx",
    )(x, mod4, g_mix.reshape(1, D), w_all)


def _qkv_lat_kernel(x_ref, mod_ref, g_ref, w_ref, cos_ref, sin_ref, q_ref, k_ref, v_ref):
    sh = mod_ref[:, 0:D]
    sc = mod_ref[:, D:2 * D]
    for c in range(QKV_TM // MM_CHUNK):
        r0 = c * MM_CHUNK
        h = _norm_mod(x_ref[r0:r0 + MM_CHUNK, :], g_ref[...], sc, sh).astype(BF16)
        cos = cos_ref[r0:r0 + MM_CHUNK, :]
        sin = sin_ref[r0:r0 + MM_CHUNK, :]
        q_ref[r0:r0 + MM_CHUNK, :] = (_dot(h, w_ref[:, 0:D]) * cos + _dot(h, w_ref[:, D:2 * D]) * sin).astype(BF16)
        k_ref[r0:r0 + MM_CHUNK, :] = (_dot(h, w_ref[:, 2 * D:3 * D]) * cos
                                      + _dot(h, w_ref[:, 3 * D:4 * D]) * sin).astype(BF16)
        v_ref[r0:r0 + MM_CHUNK, :] = _dot(h, w_ref[:, 4 * D:5 * D]).astype(BF16)


def _qkv_lat(x, mod4, layer, g_mix, w_all, cos_t, sin_t):
    n = w_all.shape[1]
    nct = T_CTX // QKV_TM
    lpb = DEC_SEQ // QKV_TM
    bf = jax.ShapeDtypeStruct((T_LAT, D), BF16)
    tok_out = pl.BlockSpec((QKV_TM, D), lambda i: (i, 0))
    rope = pl.BlockSpec((QKV_TM, D), lambda i: (i % lpb, 0))
    return pl.pallas_call(
        _qkv_lat_kernel,
        grid=(T_LAT // QKV_TM,),
        in_specs=[pl.BlockSpec((QKV_TM, D), lambda i: (i + nct, 0)),
                  pl.BlockSpec((None, None, 1, N_MOD * D), lambda i: (layer, 8 + i // lpb, 0, 0)),
                  _row_spec(), _full_spec((D, n)), rope, rope],
        out_specs=[tok_out, tok_out, tok_out],
        out_shape=[bf, bf, bf],
        compiler_params=_cparams(("arbitrary",)),
        name="qkv_lat",
    )(x, mod4, g_mix.reshape(1, D), w_all, cos_t, sin_t)


KVW = Q_PER_KV * HEAD_DIM


def _head_masks(rows):
    lane = lax.broadcasted_iota(I32, (rows, KVW), 1)
    return [jnp.logical_and(lane >= g * HEAD_DIM, lane < (g + 1) * HEAD_DIM) for g in range(Q_PER_KV)]


def _attn_ctx_kernel(sink_ref, q_ref, k_ref, v_ref, o_ref):
    scale = HEAD_DIM ** -0.5
    masks = _head_masks(SEQ)
    for kv in range(N_KV_HEADS):
        c0 = kv * KVW
        q = q_ref[:, c0:c0 + KVW]
        k = k_ref[:, c0:c0 + KVW]
        v = v_ref[:, c0:c0 + KVW]
        acc = jnp.zeros((SEQ, KVW), F32)
        for g in range(Q_PER_KV):
            sink = sink_ref[kv * Q_PER_KV + g]
            qg = jnp.where(masks[g], q, jnp.zeros_like(q))
            s = _dot_nt(qg, k) * scale
            m = jnp.maximum(jnp.max(s, axis=-1, keepdims=True), sink)
            e = jnp.exp(s - m)
            den = jnp.sum(e, axis=-1, keepdims=True) + jnp.exp(sink - m)
            og = _dot(e.astype(BF16), v) / den
            acc = jnp.where(masks[g], og, acc)
        o_ref[:, c0:c0 + KVW] = acc.astype(BF16)


def _attn_ctx(sink, q, k, v):
    tok = pl.BlockSpec((SEQ, D), lambda b, *_: (b, 0))
    return pl.pallas_call(
        _attn_ctx_kernel,
        grid_spec=pltpu.PrefetchScalarGridSpec(
            num_scalar_prefetch=1, grid=(BATCH,),
            in_specs=[tok, tok, tok], out_specs=tok),
        out_shape=jax.ShapeDtypeStruct((T_CTX, D), BF16),
        compiler_params=_cparams(("arbitrary",)),
        name="attn_ctx",
    )(sink, q, k, v)


ATT_TQ = 128
ATT_SPAN = ATT_TQ + 2 * WINDOW


def _attn_lat_kernel(sink_ref, q_ref, k_ref, v_ref, ck_ref, cv_ref, o_ref):
    qb = pl.program_id(1)
    scale = HEAD_DIM ** -0.5
    w0 = pl.multiple_of(jnp.clip(qb * ATT_TQ - WINDOW, 0, DEC_SEQ - ATT_SPAN), ATT_TQ)
    rows = Q_PER_KV * ATT_TQ
    ridx = lax.broadcasted_iota(I32, (rows, ATT_SPAN), 0)
    qpos = qb * ATT_TQ + (ridx & (ATT_TQ - 1))
    kpos = w0 + lax.broadcasted_iota(I32, (rows, ATT_SPAN), 1)
    valid = jnp.abs(qpos - kpos) <= WINDOW
    rcol = lax.broadcasted_iota(I32, (rows, 1), 0)
    masks = _head_masks(ATT_TQ)
    for kv in range(N_KV_HEADS):
        c0 = kv * KVW
        q = q_ref[:, c0:c0 + KVW]
        qs = jnp.concatenate([jnp.where(masks[g], q, jnp.zeros_like(q)) for g in range(Q_PER_KV)], axis=0)
        sink = jnp.zeros((rows, 1), F32)
        for g in range(Q_PER_KV):
            sink = jnp.where(rcol >= g * ATT_TQ, sink_ref[kv * Q_PER_KV + g], sink)
        s_ctx = _dot_nt(qs, ck_ref[:, c0:c0 + KVW]) * scale
        s_win = _dot_nt(qs, k_ref[pl.ds(w0, ATT_SPAN), c0:c0 + KVW]) * scale
        s_win = jnp.where(valid, s_win, NEG_INF)
        m = jnp.maximum(jnp.maximum(jnp.max(s_ctx, axis=-1, keepdims=True),
                                    jnp.max(s_win, axis=-1, keepdims=True)), sink)
        e_ctx = jnp.exp(s_ctx - m)
        e_win = jnp.exp(s_win - m)
        den = (jnp.exp(sink - m) + jnp.sum(e_ctx, axis=-1, keepdims=True)
               + jnp.sum(e_win, axis=-1, keepdims=True))
        o = (_dot(e_ctx.astype(BF16), cv_ref[:, c0:c0 + KVW])
             + _dot(e_win.astype(BF16), v_ref[pl.ds(w0, ATT_SPAN), c0:c0 + KVW])) / den
        acc = jnp.zeros((ATT_TQ, KVW), F32)
        for g in range(Q_PER_KV):
            acc = jnp.where(masks[g], o[g * ATT_TQ:(g + 1) * ATT_TQ, :], acc)
        o_ref[:, c0:c0 + KVW] = acc.astype(BF16)


def _attn_lat(sink, q, k, v, ck, cv):
    nqb = DEC_SEQ // ATT_TQ
    qspec = pl.BlockSpec((ATT_TQ, D), lambda b, i, *_: (b * nqb + i, 0))
    seq = pl.BlockSpec((DEC_SEQ, D), lambda b, i, *_: (b, 0))
    ctx = pl.BlockSpec((PAST_LEN, D), lambda b, i, *_: (b, 0))
    return pl.pallas_call(
        _attn_lat_kernel,
        grid_spec=pltpu.PrefetchScalarGridSpec(
            num_scalar_prefetch=1, grid=(DEC_BATCH, nqb),
            in_specs=[qspec, seq, seq, ctx, ctx], out_specs=qspec),
        out_shape=jax.ShapeDtypeStruct((T_LAT, D), BF16),
        compiler_params=_cparams(("arbitrary", "arbitrary")),
        name="attn_lat",
    )(sink, q, k, v, ck, cv)


def _resproj_kernel(x_ref, a_ref, mod_ref, w_ref, o_ref):
    gate = mod_ref[:, 2 * D:3 * D]
    o_ref[...] = x_ref[...] + gate * _dot(a_ref[...], w_ref[...])


def _resproj(x, a, mod4, layer, w):
    tok = pl.BlockSpec((TM, D), lambda i: (i, 0))
    return pl.pallas_call(
        _resproj_kernel,
        grid=(T_ALL // TM,),
        in_specs=[tok, tok, _mod_spec(layer, TM), _full_spec((D, D))],
        out_specs=tok,
        out_shape=jax.ShapeDtypeStruct((T_ALL, D), F32),
        compiler_params=_cparams(("arbitrary",)),
        name="attn_out_proj",
    )(x, a, mod4, w)


def _rope_tables():
    rows = DEC_SEQ // GRID_W
    row = jnp.repeat(jnp.arange(rows), GRID_W).astype(F32)
    col = jnp.tile(jnp.arange(GRID_W), rows).astype(F32)
    inv = ROPE_THETA ** (-jnp.arange(ROPE_FREQS, dtype=F32) / ROPE_FREQS)
    ang = jnp.concatenate([row[:, None] * inv, col[:, None] * inv], axis=-1)
    cos = jnp.cos(ang)
    sin = jnp.sin(ang)
    cos_h = jnp.concatenate([cos, cos], axis=-1)
    sin_h = jnp.concatenate([sin, sin], axis=-1)
    return jnp.tile(cos_h, (1, N_HEADS)), jnp.tile(sin_h, (1, N_HEADS))


def _rot_half_cols(w):
    k = w.shape[0]
    w4 = w.reshape(k, -1, 2, HEAD_DIM // 2)
    return jnp.stack([-w4[:, :, 1], w4[:, :, 0]], axis=2).reshape(k, -1)


def _expand_kv_cols(w):
    k = w.shape[0]
    w3 = w.reshape(k, N_KV_HEADS, 1, HEAD_DIM)
    return jnp.broadcast_to(w3, (k, N_KV_HEADS, Q_PER_KV, HEAD_DIM)).reshape(k, N_HEADS * HEAD_DIM)


FG = D // FNET_GROUPS


def _fnet_kernel(x_ref, mod_ref, g_ref, cs_ref, fl_ref, w_ref, o_ref, h_s, ab_s):
    sh = mod_ref[:, 0:D]
    sc = mod_ref[:, D:2 * D]
    gate = mod_ref[:, 2 * D:3 * D]
    for c in range(TM // ROW_CHUNK):
        r0 = c * ROW_CHUNK
        h_s[r0:r0 + ROW_CHUNK, :] = _norm_mod(x_ref[r0:r0 + ROW_CHUNK, :], g_ref[...], sc, sh).astype(BF16)
    for g in range(FNET_GROUPS):
        ab = _dot(h_s[:, g * FG:(g + 1) * FG], cs_ref[...])
        ab_s[0:TM, g * FG:(g + 1) * FG] = ab[:, 0:FG].astype(BF16)
        ab_s[TM:2 * TM, g * FG:(g + 1) * FG] = ab[:, FG:2 * FG].astype(BF16)
    for c in range(TM // MM_CHUNK):
        r0 = c * MM_CHUNK
        f = _dot(fl_ref[r0:r0 + MM_CHUNK, :], ab_s[...])
        y = _dot(f.astype(BF16), w_ref[...])
        o_ref[r0:r0 + MM_CHUNK, :] = x_ref[r0:r0 + MM_CHUNK, :] + gate * y


def _fnet_mixer(x, mod4, layer, g_mix, cs, fl, w_out):
    nct = T_CTX // TM
    tok = pl.BlockSpec((TM, D), lambda i: (i, 0))
    return pl.pallas_call(
        _fnet_kernel,
        grid=(T_ALL // TM,),
        in_specs=[tok, _mod_spec(layer, TM), _row_spec(), _full_spec((FG, 2 * FG)),
                  pl.BlockSpec((None, TM, 2 * TM), lambda i: (jnp.where(i < nct, 0, 1), 0, 0)),
                  _full_spec((D, D))],
        out_specs=tok,
        out_shape=jax.ShapeDtypeStruct((T_ALL, D), F32),
        scratch_shapes=[pltpu.VMEM((TM, D), BF16), pltpu.VMEM((2 * TM, D), BF16)],
        compiler_params=_cparams(("arbitrary",)),
        name="fnet_mixer",
    )(x, mod4, g_mix.reshape(1, D), cs, fl, w_out)


def _dft_cos_sin(n):
    k = np.arange(n)
    ang = 2.0 * np.pi * ((k[:, None] * k[None, :]) % n) / n
    return np.cos(ang), np.sin(ang)


def _fnet_tables():
    cc, sc = _dft_cos_sin(FG)
    cs = np.concatenate([cc, sc], axis=1) / math.sqrt(FG)
    mats = []
    for seq in (SEQ, DEC_SEQ):
        cl, sl = _dft_cos_sin(seq)
        reps = TM // seq
        eye = np.eye(reps)
        mats.append(np.concatenate([np.kron(eye, cl), -np.kron(eye, sl)], axis=1) / math.sqrt(seq))
    return jnp.asarray(cs, F32).astype(BF16), jnp.asarray(np.stack(mats), F32).astype(BF16)


ROUTER_TM = 512
TOK_SUB = D // LANES


def _store_token_tiles(ref, r0, val):
    rows = val.shape[0]
    for c in range(TOK_SUB):
        ref[pl.ds(r0 * TOK_SUB + c, rows, stride=TOK_SUB), :] = val[:, c * LANES:(c + 1) * LANES]


def _load_token_tiles(ref, r0, rows, c, lead=None):
    idx = pl.ds(r0 * TOK_SUB + c, rows, stride=TOK_SUB)
    return ref[idx, :] if lead is None else ref[lead, idx, :]


def _router_kernel(x_ref, mod_ref, g_ref, wr_ref, h_ref, sel_ref, gates_ref):
    sh = mod_ref[:, 3 * D:4 * D]
    sc = mod_ref[:, 4 * D:5 * D]
    w_hi = wr_ref[0]
    w_lo = wr_ref[1]
    for c in range(ROUTER_TM // ROW_CHUNK):
        r0 = c * ROW_CHUNK
        h = _norm_mod(x_ref[r0:r0 + ROW_CHUNK, :], g_ref[...], sc, sh)
        _store_token_tiles(h_ref, r0, h)
        h_hi, h_lo = _split_bf16(h)
        logits = _dot(h_hi, w_hi) + _dot(h_lo, w_hi) + _dot(h_hi, w_lo)
        lane = lax.broadcasted_iota(I32, logits.shape, 1)
        lg = jnp.where(lane < N_EXPERTS, logits, -jnp.inf)
        m1 = jnp.max(lg, axis=-1, keepdims=True)
        i1 = jnp.min(jnp.where(lg == m1, lane, LANES), axis=-1, keepdims=True)
        lg2 = jnp.where(lane == i1, -jnp.inf, lg)
        m2 = jnp.max(lg2, axis=-1, keepdims=True)
        i2 = jnp.min(jnp.where(lg2 == m2, lane, LANES), axis=-1, keepdims=True)
        e2 = jnp.exp(m2 - m1)
        den = 1.0 + e2
        sel_ref[r0:r0 + ROW_CHUNK, :] = jnp.where(lane == i1, 1, jnp.where(lane == i2, 2, 0)).astype(I32)
        gates_ref[r0:r0 + ROW_CHUNK, :] = jnp.where(lane == 0, 1.0 / den, jnp.where(lane == 1, e2 / den, 0.0))


def _router(x, mod4, layer, g_ffn, w_router):
    wr = jnp.zeros((D, LANES), F32).at[:, 0:N_EXPERTS].set(w_router)
    wr_hi = wr.astype(BF16)
    wr_lo = (wr - wr_hi.astype(F32)).astype(BF16)
    tok = pl.BlockSpec((ROUTER_TM, D), lambda i: (i, 0))
    nar = pl.BlockSpec((ROUTER_TM, LANES), lambda i: (i, 0))
    return pl.pallas_call(
        _router_kernel,
        grid=(T_ALL // ROUTER_TM,),
        in_specs=[tok, _mod_spec(layer, ROUTER_TM), _row_spec(), _full_spec((2, D, LANES))],
        out_specs=[pl.BlockSpec((ROUTER_TM * TOK_SUB, LANES), lambda i: (i, 0)), nar, nar],
        out_shape=[jax.ShapeDtypeStruct((T_ALL * TOK_SUB, LANES), F32),
                   jax.ShapeDtypeStruct((T_ALL, LANES), I32),
                   jax.ShapeDtypeStruct((T_ALL, LANES), F32)],
        compiler_params=_cparams(("arbitrary",)),
        name="moe_router",
    )(x, mod4, g_ffn.reshape(1, D), jnp.stack([wr_hi, wr_lo]))


def _route_plan(sel):
    s8 = sel[:, 0:N_EXPERTS]
    onehot = (s8 > 0).astype(I32)
    csum = jnp.cumsum(onehot, axis=0)
    rank = csum - onehot
    counts = csum[-1]
    padded = ((counts + TME - 1) // TME) * TME
    gend = jnp.cumsum(padded)
    gstart = gend - padded
    pos_te = gstart[None, :] + rank
    pos1 = jnp.sum(jnp.where(s8 == 1, pos_te, 0), axis=1).astype(I32)
    pos2 = jnp.sum(jnp.where(s8 == 2, pos_te, 0), axis=1).astype(I32)
    total = gend[-1:]
    pad_lo = jnp.concatenate([gstart + counts, total])
    pad_hi = jnp.concatenate([gend, jnp.full((1,), P_MAX, I32)])
    src = _inverse_map(pos1, pos2, jnp.concatenate([pad_lo, pad_hi]).astype(I32))
    n_valid = gend[-1] // TME
    tile_start = jnp.arange(N_EXPERT_TILES, dtype=I32) * TME
    eff_start = jnp.minimum(tile_start, jnp.maximum(n_valid - 1, 0) * TME)
    tile_expert = jnp.minimum(jnp.sum((eff_start[:, None] >= gend[None, :]).astype(I32), axis=1),
                              N_EXPERTS - 1).astype(I32)
    return pos1, pos2, src, tile_expert, n_valid.astype(I32).reshape(1)


N_PAD_RANGES = N_EXPERTS + 1


def _inverse_map_kernel(p1_ref, p2_ref, pad_ref, src_ref):
    def clear(p, carry):
        src_ref[p] = 0
        return carry
    for i in range(N_PAD_RANGES):
        lax.fori_loop(pad_ref[i], pad_ref[N_PAD_RANGES + i], clear, 0)

    def body(t, carry):
        src_ref[p1_ref[t]] = t
        src_ref[p2_ref[t]] = t
        return carry
    lax.fori_loop(0, T_ALL, body, 0, unroll=8)


def _inverse_map(pos1, pos2, pad_ranges):
    smem = pl.BlockSpec(memory_space=pltpu.SMEM)
    return pl.pallas_call(
        _inverse_map_kernel,
        in_specs=[smem, smem, smem], out_specs=smem,
        out_shape=jax.ShapeDtypeStruct((P_MAX,), I32),
        name="moe_inverse_map",
    )(pos1, pos2, pad_ranges)


EXP_TF = 512
EXP_NF = D_FF_EXPERT // EXP_TF
GATHER_ROWS = 152
XS_ROWS = EXP_NF * GATHER_ROWS
assert XS_ROWS >= TME


def _expert_kernel(te_ref, nv_ref, src_ref, h_hbm, wg_ref, wu_ref, wd_ref, o_ref, xs_buf, xb_s, acc_s, sem):
    m = pl.program_id(0)
    f = pl.program_id(1)
    nt = pl.num_programs(0)
    nf = pl.num_programs(1)
    n_valid = nv_ref[0]
    slot = m % 2
    nslot = 1 - slot

    def row_copy(tile, r, dst_slot):
        t = src_ref[tile * TME + jnp.minimum(r, TME - 1)]
        return pltpu.make_async_copy(h_hbm.at[pl.ds(pl.multiple_of(t * TOK_SUB, TOK_SUB), TOK_SUB), :],
                                     xs_buf.at[dst_slot, pl.ds(pl.multiple_of(r * TOK_SUB, TOK_SUB), TOK_SUB), :],
                                     sem.at[dst_slot])

    def wait_slot(s):
        pltpu.make_async_copy(h_hbm.at[pl.ds(0, XS_ROWS * TOK_SUB), :], xs_buf.at[s], sem.at[s]).wait()

    @pl.when(m < n_valid)
    def _():
        @pl.when(f == 0)
        def _():
            @pl.when(m == 0)
            def _():
                acc_s[...] = jnp.zeros_like(acc_s)

                def body(r, carry):
                    row_copy(0, r, 0).start()
                    return carry
                lax.fori_loop(0, XS_ROWS, body, 0, unroll=8)

            wait_slot(slot)
            for c in range(TOK_SUB):
                xb_s[:, c * LANES:(c + 1) * LANES] = _load_token_tiles(xs_buf, 0, TME, c, lead=slot).astype(BF16)

        nxt = jnp.minimum(m + 1, nt - 1)
        base = f * GATHER_ROWS
        for k in range(GATHER_ROWS):
            row_copy(nxt, base + k, nslot).start()

        xb = xb_s[...]
        gg = _dot(xb, wg_ref[...].astype(BF16))
        uu = _dot(xb, wu_ref[...].astype(BF16))
        a = (gg * jax.nn.sigmoid(gg) * uu).astype(BF16)
        contrib = _dot(a, wd_ref[...].astype(BF16))

        acc_s[...] = jnp.where(f == 0, 0.0, acc_s[...]) + contrib

        @pl.when(f == nf - 1)
        def _():
            _store_token_tiles(o_ref, 0, acc_s[...])

        @pl.when(jnp.logical_and(m == nt - 1, f == nf - 1))
        def _():
            wait_slot(nslot)

    @pl.when(jnp.logical_and(m >= n_valid, f == 0))
    def _():
        @pl.when(m == n_valid)
        def _():
            wait_slot(slot)
        o_ref[...] = jnp.zeros_like(o_ref)


def _expert_ffn(tile_expert, n_valid, src, h, w_gu, w_down, li):
    def feff(m, f, nv):
        return jnp.where(m < nv[0], f, EXP_NF - 1)
    return pl.pallas_call(
        _expert_kernel,
        grid_spec=pltpu.PrefetchScalarGridSpec(
            num_scalar_prefetch=3, grid=(N_EXPERT_TILES, EXP_NF),
            in_specs=[pl.BlockSpec(memory_space=pl.ANY),
                      pl.BlockSpec((None, None, D, EXP_TF),
                                   lambda m, f, te, nv, sr: (li, te[m], 0, feff(m, f, nv))),
                      pl.BlockSpec((None, None, D, EXP_TF),
                                   lambda m, f, te, nv, sr: (li, te[m], 0, EXP_NF + feff(m, f, nv))),
                      pl.BlockSpec((None, None, EXP_TF, D),
                                   lambda m, f, te, nv, sr: (li, te[m], feff(m, f, nv), 0))],
            out_specs=pl.BlockSpec((TME * TOK_SUB, LANES), lambda m, f, te, nv, sr: (m, 0)),
            scratch_shapes=[pltpu.VMEM((2, XS_ROWS * TOK_SUB, LANES), F32), pltpu.VMEM((TME, D), BF16),
                            pltpu.VMEM((TME, D), F32), pltpu.SemaphoreType.DMA((2,))]),
        out_shape=jax.ShapeDtypeStruct((P_MAX * TOK_SUB, LANES), F32),
        compiler_params=_cparams(("arbitrary", "arbitrary")),
        name="moe_expert_swiglu",
    )(tile_expert, n_valid, src, h, w_gu, w_gu, w_down)


COMB_TM = 512
COMB_UNROLL = 8


def _combine_kernel(p1_ref, p2_ref, x_ref, gates_ref, mod_ref, gf_ref, y_hbm, o_ref, a_s, b_s, sem, *, final):
    i = pl.program_id(0)

    def tile_copy(p, r, dst, s):
        return pltpu.make_async_copy(y_hbm.at[pl.ds(pl.multiple_of(p * TOK_SUB, TOK_SUB), TOK_SUB), :],
                                     dst.at[pl.ds(pl.multiple_of(r * TOK_SUB, TOK_SUB), TOK_SUB), :], s)

    def body(g, carry):
        for k in range(COMB_UNROLL):
            r = g * COMB_UNROLL + k
            t = i * COMB_TM + r
            tile_copy(p1_ref[t], r, a_s, sem.at[0]).start(priority=0)
            tile_copy(p2_ref[t], r, b_s, sem.at[1]).start(priority=1)
        return carry
    lax.fori_loop(0, COMB_TM // COMB_UNROLL, body, 0)
    pltpu.make_async_copy(y_hbm.at[pl.ds(0, COMB_TM * TOK_SUB), :], a_s, sem.at[0]).wait()
    pltpu.make_async_copy(y_hbm.at[pl.ds(0, COMB_TM * TOK_SUB), :], b_s, sem.at[1]).wait()

    for c in range(COMB_TM // ROW_CHUNK):
        r0 = c * ROW_CHUNK
        w1 = gates_ref[r0:r0 + ROW_CHUNK, 0:1]
        w2 = gates_ref[r0:r0 + ROW_CHUNK, 1:2]
        for j in range(TOK_SUB):
            cols = slice(j * LANES, (j + 1) * LANES)
            y = (w1 * _load_token_tiles(a_s, r0, ROW_CHUNK, j) + w2 * _load_token_tiles(b_s, r0, ROW_CHUNK, j))
            o_ref[r0:r0 + ROW_CHUNK, cols] = x_ref[r0:r0 + ROW_CHUNK, cols] + mod_ref[:, 5 * D + j * LANES:5 * D + (j + 1) * LANES] * y
        if final:
            xn = o_ref[r0:r0 + ROW_CHUNK, :]
            ms = jnp.mean(xn * xn, axis=-1, keepdims=True)
            o_ref[r0:r0 + ROW_CHUNK, :] = xn * lax.rsqrt(ms + EPS) * gf_ref[...]


def _combine(pos1, pos2, x, gates, mod4, layer, g_final, y, final):
    tok = pl.BlockSpec((COMB_TM, D), lambda i, *_: (i, 0))
    return pl.pallas_call(
        functools.partial(_combine_kernel, final=final),
        grid_spec=pltpu.PrefetchScalarGridSpec(
            num_scalar_prefetch=2, grid=(T_ALL // COMB_TM,),
            in_specs=[tok, pl.BlockSpec((COMB_TM, LANES), lambda i, *_: (i, 0)),
                      _mod_spec(layer, COMB_TM), _row_spec(),
                      pl.BlockSpec(memory_space=pl.ANY)],
            out_specs=tok,
            scratch_shapes=[pltpu.VMEM((COMB_TM * TOK_SUB, LANES), F32), pltpu.VMEM((COMB_TM * TOK_SUB, LANES), F32),
                            pltpu.SemaphoreType.DMA((2,))]),
        out_shape=jax.ShapeDtypeStruct((T_ALL, D), F32),
        compiler_params=_cparams(("arbitrary",)),
        name="moe_combine_final" if final else "moe_combine",
    )(pos1, pos2, x, gates, mod4, g_final.reshape(1, D), y)


def _moe_layer(x, mod4, layer, g_ffn, w_router, w_gu, w_down, li, g_final, final):
    h, sel, gates = _router(x, mod4, layer, g_ffn, w_router)
    pos1, pos2, src, tile_expert, n_valid = _route_plan(sel)
    ys = _expert_ffn(tile_expert, n_valid, src, h, w_gu, w_down, li)
    return _combine(pos1, pos2, x, gates, mod4, layer, g_final, ys, final)


def kernel(x_prompt, x_sample, cache_k, cache_v, state_ssm_re, state_ssm_im, c, c_ctx, w_ada, b_ada, g_mix, g_ffn, g_final, conv_w_in, conv_w, conv_w_out, s5_lambda_re, s5_lambda_im, s5_b_re, s5_b_im, s5_c_re, s5_c_im, s5_log_dt, s5_d, s5_w_glu, attn_w_q, attn_w_kv, attn_w_o, attn_sink, fnet_w_out, ffn_w_gu, ffn_w_down, moe_w_router, moe_w_gu, moe_w_down):
    x = jnp.concatenate([x_prompt.reshape(T_CTX, D), x_sample.reshape(T_LAT, D)], axis=0)
    cond16 = jnp.concatenate([jnp.broadcast_to(c_ctx[None, :], (8, D)), c], axis=0)
    mod3 = _mod_all(cond16, w_ada, b_ada)
    mod4 = mod3.reshape(DEPTH, 16, 1, N_MOD * D)

    x = _conv_mixer(x, mod4, 0, g_mix[0], conv_w_in[0].astype(BF16), conv_w[0], conv_w_out[0].astype(BF16))
    x = _dense_ffn(x, mod4, 0, g_ffn[0], ffn_w_gu[0].astype(BF16), ffn_w_down[0].astype(BF16))

    lbr, lbi, bbr, bbi = _s5_prep(s5_lambda_re[0], s5_lambda_im[0], s5_log_dt[0], s5_b_re[0], s5_b_im[0])
    w_b, w_c, lam_s = _s5_weights(lbr, lbi, bbr, bbi, s5_c_re[0], s5_c_im[0])
    w_glu = s5_w_glu[0].astype(BF16)
    xc = x[0:T_CTX].reshape(BATCH, SEQ, D).transpose(1, 0, 2).reshape(T_CTX, D)
    xl = x[T_CTX:].reshape(DEC_BATCH, DEC_SEQ, D).transpose(1, 0, 2).reshape(T_LAT, D)
    h0_ctx = jnp.zeros((2, S5_JT, BATCH, 2 * S5_HALF), F32)
    h0_lat = _s5_state_in(state_ssm_re[:, 0], state_ssm_im[:, 0])
    yc, fin_c = _s5_scan(xc, BATCH, mod3, 1, 0, g_mix[1], w_b, w_c, lam_s, h0_ctx)
    yl, _ = _s5_scan(xl, DEC_BATCH, mod3, 1, 1, g_mix[1], w_b, w_c, lam_s, h0_lat)
    xc = _s5_glu(xc, yc, mod3, 1, 0, g_mix[1], s5_d[0], w_glu)
    xl = _s5_glu(xl, yl, mod3, 1, 1, g_mix[1], s5_d[0], w_glu)
    x = jnp.concatenate([xc.reshape(SEQ, BATCH, D).transpose(1, 0, 2).reshape(T_CTX, D),
                         xl.reshape(DEC_SEQ, DEC_BATCH, D).transpose(1, 0, 2).reshape(T_LAT, D)], axis=0)
    new_re, new_im = _s5_state_out(fin_c)
    x = _moe_layer(x, mod4, 1, g_ffn[1], moe_w_router[0], moe_w_gu, moe_w_down, 0, g_final, False)

    wq = attn_w_q[0]
    wk = _expand_kv_cols(attn_w_kv[0][:, 0:N_KV_HEADS * HEAD_DIM])
    wv = _expand_kv_cols(attn_w_kv[0][:, N_KV_HEADS * HEAD_DIM:])
    w_ctx = jnp.concatenate([wq, wk, wv, attn_w_kv[0]], axis=1).astype(BF16)
    w_lat = jnp.concatenate([wq, _rot_half_cols(wq), wk, _rot_half_cols(wk), wv], axis=1).astype(BF16)
    cos_t, sin_t = _rope_tables()
    q_c, k_c, v_c, kv_c = _qkv_ctx(x, mod4, 2, g_mix[2], w_ctx)
    q_l, k_l, v_l = _qkv_lat(x, mod4, 2, g_mix[2], w_lat, cos_t, sin_t)
    sink = attn_sink[0]
    o_c = _attn_ctx(sink, q_c, k_c, v_c)
    expand = lambda a: jnp.broadcast_to(
        a.reshape(DEC_BATCH * PAST_LEN, N_KV_HEADS, 1, HEAD_DIM),
        (DEC_BATCH * PAST_LEN, N_KV_HEADS, Q_PER_KV, HEAD_DIM)).reshape(DEC_BATCH * PAST_LEN, D).astype(BF16)
    o_l = _attn_lat(sink, q_l, k_l, v_l, expand(cache_k[:, 0]), expand(cache_v[:, 0]))
    x = _resproj(x, jnp.concatenate([o_c, o_l], axis=0), mod4, 2, attn_w_o[0].astype(BF16))
    kvw = N_KV_HEADS * HEAD_DIM
    new_k = kv_c[:, 0:kvw].reshape(BATCH, 1, SEQ, N_KV_HEADS, HEAD_DIM)
    new_v = kv_c[:, kvw:].reshape(BATCH, 1, SEQ, N_KV_HEADS, HEAD_DIM)
    x = _dense_ffn(x, mod4, 2, g_ffn[2], ffn_w_gu[1].astype(BF16), ffn_w_down[1].astype(BF16))

    cs, fl = _fnet_tables()
    x = _fnet_mixer(x, mod4, 3, g_mix[3], cs, fl, fnet_w_out[0].astype(BF16))
    x = _moe_layer(x, mod4, 3, g_ffn[3], moe_w_router[1], moe_w_gu, moe_w_down, 1, g_final, True)

    y_prompt = x[0:T_CTX].reshape(BATCH, SEQ, D)
    y_sample = x[T_CTX:].reshape(DEC_BATCH, DEC_SEQ, D)
    return (y_prompt, y_sample, new_k, new_v, new_re[:, None], new_im[:, None])
```

```python
import functools
import math

import numpy as np
import jax
import jax.numpy as jnp
from jax import lax
from jax.experimental import pallas as pl
from jax.experimental.pallas import tpu as pltpu

F32 = jnp.float32
BF16 = jnp.bfloat16
I32 = jnp.int32

D = 1024
BATCH = 16
SEQ = 256
DEPTH = 4
DEC_BATCH = 8
DEC_SEQ = 1024
PAST_LEN = 256
GRID_W = 64
EPS = 1e-6
N_MOD = 6
S5_GROUP = 16
S5_GROUPS = D // S5_GROUP
S5_STATE = 64
HEAD_DIM = 64
N_HEADS = D // HEAD_DIM
N_KV_HEADS = 4
Q_PER_KV = N_HEADS // N_KV_HEADS
WINDOW = 128
ROPE_THETA = 10000.0
ROPE_FREQS = HEAD_DIM // 4
FNET_GROUPS = 4
D_FF = 2816
N_EXPERTS = 8
TOP_K = 2
D_FF_EXPERT = 3584
NEG_INF = -1e30

T_CTX = BATCH * SEQ
T_LAT = DEC_BATCH * DEC_SEQ
T_ALL = T_CTX + T_LAT

VMEM_LIMIT_V7X = 56 * 1024 * 1024
LANES = 128

TM = 1024
ROW_CHUNK = 256
MM_CHUNK = 512
TME = 1024
N_EXPERT_TILES = (T_ALL * TOP_K) // TME + N_EXPERTS
P_MAX = N_EXPERT_TILES * TME


def _cparams(sem):
    return pltpu.CompilerParams(dimension_semantics=sem, vmem_limit_bytes=VMEM_LIMIT_V7X)


def _dot(a, b):
    return jnp.dot(a, b, preferred_element_type=F32)


def _dot_nt(a, b):
    return lax.dot_general(a, b, (((1,), (1,)), ((), ())), preferred_element_type=F32)


def _split_bf16(a):
    hi = a.astype(BF16)
    lo = (a - hi.astype(F32)).astype(BF16)
    return hi, lo


def _tile_rows(a, reps, axis=0):
    assert axis == 0
    return jnp.concatenate([a] * reps, axis=0)


def _norm_mod(x, g, sc, sh):
    ms = jnp.mean(x * x, axis=-1, keepdims=True)
    y = x * lax.rsqrt(ms + EPS) * g
    return y * (1.0 + sc) + sh


def _mod_row(i, tm):
    nct = T_CTX // tm
    lpb = DEC_SEQ // tm
    return jnp.where(i < nct, 0, 8 + (i - nct) // lpb)


def _mod_spec(layer, tm):
    return pl.BlockSpec((None, None, 1, N_MOD * D), lambda i, *_: (layer, _mod_row(i, tm), 0, 0))


def _row_spec():
    return pl.BlockSpec((1, D), lambda *_: (0, 0))


def _full_spec(shape):
    nd = len(shape)
    return pl.BlockSpec(shape, lambda *_: (0,) * nd, pipeline_mode=pl.Buffered(1))


def _mod_kernel(c_ref, w_ref, b_ref, o_ref):
    c = c_ref[...]
    s = (c * jax.nn.sigmoid(c)).astype(BF16)
    o_ref[...] = _dot(s, w_ref[...].astype(BF16)) + b_ref[...]


def _mod_all(cond16, w_ada, b_ada):
    tn = 1024
    return pl.pallas_call(
        _mod_kernel,
        grid=(DEPTH, N_MOD * D // tn),
        in_specs=[pl.BlockSpec((16, D), lambda l, n: (0, 0)),
                  pl.BlockSpec((None, D, tn), lambda l, n: (l, 0, n)),
                  pl.BlockSpec((None, 1, tn), lambda l, n: (l, 0, n))],
        out_specs=pl.BlockSpec((None, 16, tn), lambda l, n: (l, 0, n)),
        out_shape=jax.ShapeDtypeStruct((DEPTH, 16, N_MOD * D), F32),
        compiler_params=_cparams(("arbitrary", "arbitrary")),
        name="adaln_mod",
    )(cond16, w_ada, b_ada.reshape(DEPTH, 1, N_MOD * D))


def _conv_kernel(x_ref, mod_ref, g_ref, win_ref, cw_ref, wout_ref, o_ref, gb_s, u_s, z_s):
    i = pl.program_id(0)
    is_ctx = i < (T_CTX // TM)
    sh = mod_ref[:, 0:D]
    sc = mod_ref[:, D:2 * D]
    gate = mod_ref[:, 2 * D:3 * D]
    g = g_ref[...]
    zero8 = jnp.zeros((8, D), F32)
    u_s[0:8, :] = zero8
    u_s[8 + TM:16 + TM, :] = zero8
    for c in range(TM // MM_CHUNK):
        r0 = c * MM_CHUNK
        h = _norm_mod(x_ref[r0:r0 + MM_CHUNK, :], g, sc, sh).astype(BF16)
        proj = _dot(h, win_ref[...])
        gb_s[r0:r0 + MM_CHUNK, :] = proj[:, 0:D]
        u_s[8 + r0:8 + r0 + MM_CHUNK, :] = proj[:, D:2 * D] * proj[:, 2 * D:3 * D]
    row = lax.broadcasted_iota(I32, (ROW_CHUNK, 1), 0)
    first = jnp.logical_and(is_ctx, row == 0)
    last = jnp.logical_and(is_ctx, row == ROW_CHUNK - 1)
    for c in range(TM // ROW_CHUNK):
        r0 = c * ROW_CHUNK
        up = jnp.where(first, 0.0, u_s[7 + r0:7 + r0 + ROW_CHUNK, :])
        mid = u_s[8 + r0:8 + r0 + ROW_CHUNK, :]
        dn = jnp.where(last, 0.0, u_s[9 + r0:9 + r0 + ROW_CHUNK, :])
        conv = up * cw_ref[0:1, :] + mid * cw_ref[1:2, :] + dn * cw_ref[2:3, :]
        z_s[r0:r0 + ROW_CHUNK, :] = (gb_s[r0:r0 + ROW_CHUNK, :] * conv).astype(BF16)
    for c in range(TM // MM_CHUNK):
        r0 = c * MM_CHUNK
        y = _dot(z_s[r0:r0 + MM_CHUNK, :], wout_ref[...])
        o_ref[r0:r0 + MM_CHUNK, :] = x_ref[r0:r0 + MM_CHUNK, :] + gate * y


def _conv_mixer(x, mod4, layer, g_mix, w_in, conv_w, w_out):
    assert SEQ == ROW_CHUNK and DEC_SEQ == TM
    return pl.pallas_call(
        _conv_kernel,
        grid=(T_ALL // TM,),
        in_specs=[pl.BlockSpec((TM, D), lambda i: (i, 0)),
                  _mod_spec(layer, TM), _row_spec(),
                  _full_spec((D, 3 * D)), _full_spec((3, D)), _full_spec((D, D))],
        out_specs=pl.BlockSpec((TM, D), lambda i: (i, 0)),
        out_shape=jax.ShapeDtypeStruct((T_ALL, D), F32),
        scratch_shapes=[pltpu.VMEM((TM, D), F32), pltpu.VMEM((TM + 16, D), F32), pltpu.VMEM((TM, D), BF16)],
        compiler_params=_cparams(("arbitrary",)),
        name="conv_mixer",
    )(x, mod4, g_mix.reshape(1, D), w_in, conv_w, w_out)


FFN_TM = 512
MXU_WIDTH_V7X = 256
FFN_SPLITS = ((0, 6 * MXU_WIDTH_V7X), (6 * MXU_WIDTH_V7X, D_FF))
assert D_FF % MXU_WIDTH_V7X == 0


def _ffn_kernel(x_ref, mod_ref, g_ref, wgu_ref, wd_ref, o_ref):
    sh = mod_ref[:, 3 * D:4 * D]
    sc = mod_ref[:, 4 * D:5 * D]
    gate = mod_ref[:, 5 * D:6 * D]
    x = x_ref[...]
    h = _norm_mod(x, g_ref[...], sc, sh).astype(BF16)
    acc = None
    for lo, hi in FFN_SPLITS:
        gg = _dot(h, wgu_ref[:, lo:hi])
        uu = _dot(h, wgu_ref[:, D_FF + lo:D_FF + hi])
        a = (gg * jax.nn.sigmoid(gg) * uu).astype(BF16)
        contrib = _dot(a, wd_ref[lo:hi, :])
        acc = contrib if acc is None else acc + contrib
    o_ref[...] = x + gate * acc


def _dense_ffn(x, mod4, layer, g_ffn, w_gu, w_down):
    tok = pl.BlockSpec((FFN_TM, D), lambda i: (i, 0))
    return pl.pallas_call(
        _ffn_kernel,
        grid=(T_ALL // FFN_TM,),
        in_specs=[tok, _mod_spec(layer, FFN_TM), _row_spec(),
                  _full_spec((D, 2 * D_FF)), _full_spec((D_FF, D))],
        out_specs=tok,
        out_shape=jax.ShapeDtypeStruct((T_ALL, D), F32),
        compiler_params=_cparams(("arbitrary",)),
        name="dense_swiglu",
    )(x, mod4, g_ffn.reshape(1, D), w_gu, w_down)


def _s5_prep_kernel(lr_ref, li_ref, ldt_ref, br_ref, bi_ref, lbr_ref, lbi_ref, bbr_ref, bbi_ref):
    lr = lr_ref[...]
    li = li_ref[...]
    dt = jnp.exp(ldt_ref[...])
    mag = jnp.exp(lr * dt)
    ar = mag * jnp.cos(li * dt)
    ai = mag * jnp.sin(li * dt)
    nr = ar - 1.0
    den = lr * lr + li * li
    fr = (nr * lr + ai * li) / den
    fi = (ai * lr - nr * li) / den
    br = br_ref[...]
    bi = bi_ref[...]
    lbr_ref[...] = ar
    lbi_ref[...] = ai
    bbr_ref[...] = fr * br - fi * bi
    bbi_ref[...] = fr * bi + fi * br


def _s5_prep(lam_re, lam_im, log_dt, b_re, b_im):
    rows = 2 * S5_GROUPS
    cols = S5_STATE * S5_GROUP
    exp = lambda a: jnp.repeat(a.reshape(rows, S5_STATE), S5_GROUP, axis=1)
    ldt = jnp.broadcast_to(log_dt.reshape(rows, 1), (rows, cols))
    outs = pl.pallas_call(
        _s5_prep_kernel,
        out_shape=[jax.ShapeDtypeStruct((rows, cols), F32)] * 4,
        name="s5_discretize",
    )(exp(lam_re), exp(lam_im), ldt, b_re.reshape(rows, cols), b_im.reshape(rows, cols))
    lbr, lbi, bbr, bbi = outs
    shp = (2, S5_GROUPS, S5_STATE, S5_GROUP)
    return lbr.reshape(shp)[..., 0], lbi.reshape(shp)[..., 0], bbr.reshape(shp), bbi.reshape(shp)


S5_JT = 8
S5_GPT = LANES // S5_GROUP
S5_HALF = S5_GPT * S5_STATE
S5_ROWS = 512


def _s5_scan_kernel(x_ref, mod_ref, g_ref, wb_ref, wc_ref, lam_ref, h0_ref,
                    y_ref, fin_ref, bu_s, st_s, *, nb, jgroup):
    d = pl.program_id(0)
    c = pl.program_id(1)
    lc = S5_ROWS // nb

    @pl.when(c == 0)
    def _():
        st_s[...] = h0_ref[...]

    rep = S5_ROWS // 8
    sh = _tile_rows(mod_ref[:, 0:D], rep, axis=0)
    sc = _tile_rows(mod_ref[:, D:2 * D], rep, axis=0)
    u = _norm_mod(x_ref[...], g_ref[...], sc, sh).astype(BF16)
    for j in range(S5_JT):
        bu_s[j] = _dot(u[:, j * LANES:(j + 1) * LANES], wb_ref[j])

    for j0 in range(0, S5_JT, jgroup):
        js = list(range(j0, j0 + jgroup))
        lam = [(jnp.broadcast_to(lam_ref[j][:, 0:S5_HALF], (nb, S5_HALF)),
                jnp.broadcast_to(lam_ref[j][:, S5_HALF:], (nb, S5_HALF))) for j in js]

        def body(t, carry):
            l = jnp.where(d == 0, t, lc - 1 - t)
            r0 = pl.multiple_of(l * nb, nb)
            out = []
            for k, j in enumerate(js):
                sr, si = carry[k]
                ar, ai = lam[k]
                bu = bu_s[j, pl.ds(r0, nb), :]
                hr = ar * sr - ai * si + bu[:, 0:S5_HALF]
                hi = ar * si + ai * sr + bu[:, S5_HALF:]
                bu_s[j, pl.ds(r0, nb), 0:S5_HALF] = hr
                bu_s[j, pl.ds(r0, nb), S5_HALF:] = hi
                out.append((hr, hi))
            return tuple(out)

        init = tuple((st_s[j][:, 0:S5_HALF], st_s[j][:, S5_HALF:]) for j in js)
        fin = lax.fori_loop(0, lc, body, init)
        for k, j in enumerate(js):
            st_s[j, :, 0:S5_HALF] = fin[k][0]
            st_s[j, :, S5_HALF:] = fin[k][1]

    for j in range(S5_JT):
        y_ref[:, j * LANES:(j + 1) * LANES] = _dot(bu_s[j].astype(BF16), wc_ref[j])
    fin_ref[...] = st_s[...]


def _s5_scan(xt, nb, mod3, layer, path, g_mix, w_b, w_c, lam_s, h0):
    rows = xt.shape[0]
    nc = rows // S5_ROWS
    chunk = lambda d, c: c + d * (nc - 1 - 2 * c)
    kern = functools.partial(_s5_scan_kernel, nb=nb, jgroup=2 if nb == 8 else 1)
    return pl.pallas_call(
        kern,
        grid=(2, nc),
        in_specs=[pl.BlockSpec((S5_ROWS, D), lambda d, c: (chunk(d, c), 0)),
                  pl.BlockSpec((None, 8, N_MOD * D), lambda d, c: (layer, path, 0)),
                  _row_spec(),
                  pl.BlockSpec((None, S5_JT, LANES, 2 * S5_HALF), lambda d, c: (d, 0, 0, 0)),
                  pl.BlockSpec((None, S5_JT, 2 * S5_HALF, LANES), lambda d, c: (d, 0, 0, 0)),
                  pl.BlockSpec((None, S5_JT, 1, 2 * S5_HALF), lambda d, c: (d, 0, 0, 0)),
                  pl.BlockSpec((None, S5_JT, nb, 2 * S5_HALF), lambda d, c: (d, 0, 0, 0))],
        out_specs=[pl.BlockSpec((None, S5_ROWS, D), lambda d, c: (d, chunk(d, c), 0)),
                   pl.BlockSpec((None, S5_JT, nb, 2 * S5_HALF), lambda d, c: (d, 0, 0, 0))],
        out_shape=[jax.ShapeDtypeStruct((2, rows, D), F32),
                   jax.ShapeDtypeStruct((2, S5_JT, nb, 2 * S5_HALF), F32)],
        scratch_shapes=[pltpu.VMEM((S5_JT, S5_ROWS, 2 * S5_HALF), F32),
                        pltpu.VMEM((S5_JT, nb, 2 * S5_HALF), F32)],
        compiler_params=_cparams(("arbitrary", "arbitrary")),
        name="s5_scan_b%d" % nb,
    )(xt, mod3, g_mix.reshape(1, D), w_b, w_c, lam_s, h0)


S5_GLU_ROWS = 512


def _s5_glu_kernel(x_ref, yf_ref, yb_ref, mod_ref, g_ref, dsk_ref, w_ref, o_ref):
    rep = MM_CHUNK // 8
    sh = _tile_rows(mod_ref[:, 0:D], rep, axis=0)
    sc = _tile_rows(mod_ref[:, D:2 * D], rep, axis=0)
    gate = _tile_rows(mod_ref[:, 2 * D:3 * D], rep, axis=0)
    for c in range(S5_GLU_ROWS // MM_CHUNK):
        r0 = c * MM_CHUNK
        x = x_ref[r0:r0 + MM_CHUNK, :]
        u = _norm_mod(x, g_ref[...], sc, sh)
        y = u * dsk_ref[...] + yf_ref[r0:r0 + MM_CHUNK, :] + yb_ref[r0:r0 + MM_CHUNK, :]
        z = jax.nn.gelu(y).astype(BF16)
        ag = _dot(z, w_ref[...])
        out = ag[:, 0:D] * jax.nn.sigmoid(ag[:, D:2 * D])
        o_ref[r0:r0 + MM_CHUNK, :] = x + gate * out


def _s5_glu(xt, y2, mod3, layer, path, g_mix, d_skip, w_glu):
    rows = xt.shape[0]
    return pl.pallas_call(
        _s5_glu_kernel,
        grid=(rows // S5_GLU_ROWS,),
        in_specs=[pl.BlockSpec((S5_GLU_ROWS, D), lambda i: (i, 0)),
                  pl.BlockSpec((None, S5_GLU_ROWS, D), lambda i: (0, i, 0)),
                  pl.BlockSpec((None, S5_GLU_ROWS, D), lambda i: (1, i, 0)),
                  pl.BlockSpec((None, 8, N_MOD * D), lambda i: (layer, path, 0)),
                  _row_spec(), _row_spec(), _full_spec((D, 2 * D))],
        out_specs=pl.BlockSpec((S5_GLU_ROWS, D), lambda i: (i, 0)),
        out_shape=jax.ShapeDtypeStruct((rows, D), F32),
        compiler_params=_cparams(("arbitrary",)),
        name="s5_glu",
    )(xt, y2, y2, mod3, g_mix.reshape(1, D), d_skip.reshape(1, D), w_glu)


def _s5_weights(lbr, lbi, bbr, bbi, c_re, c_im):
    eye = jnp.eye(S5_GPT, dtype=F32)
    bb = jnp.stack([bbr, bbi]).reshape(2, 2, S5_JT, S5_GPT, S5_STATE, S5_GROUP)
    w_bu = jnp.einsum('rdjgps,gh->djgsrhp', bb, eye).reshape(2, S5_JT, LANES, 2 * S5_HALF)
    w_b = w_bu.astype(BF16)
    cc =jnp.stack([c_re, -c_im]).reshape(2, 2, S5_JT, S5_GPT, S5_GROUP, S5_STATE)
    w_c = jnp.einsum('rdjgsp,gh->djrgphs', cc, eye).reshape(2, S5_JT, 2 * S5_HALF, LANES).astype(BF16)
    lam_s = jnp.concatenate([lbr.reshape(2, S5_JT, S5_HALF), lbi.reshape(2, S5_JT, S5_HALF)], axis=-1)
    return w_b, w_c, lam_s.reshape(2, S5_JT, 1, 2 * S5_HALF)


def _s5_state_in(st_re, st_im):
    def lay(a):
        b = a.shape[0]
        return a.transpose(1, 0, 2, 3).reshape(2, b, S5_JT, S5_HALF).transpose(0, 2, 1, 3)
    return jnp.concatenate([lay(st_re), lay(st_im)], axis=-1)


def _s5_state_out(fin):
    def lay(a):
        b = a.shape[2]
        return a.transpose(2, 0, 1, 3).reshape(b, 2, S5_GROUPS, S5_STATE)
    return lay(fin[..., 0:S5_HALF]), lay(fin[..., S5_HALF:])


QKV_TM = 512


def _qkv_ctx_kernel(x_ref, mod_ref, g_ref, w_ref, q_ref, k_ref, v_ref, kv_ref):
    sh = mod_ref[:, 0:D]
    sc = mod_ref[:, D:2 * D]
    for c in range(QKV_TM // MM_CHUNK):
        r0 = c * MM_CHUNK
        h = _norm_mod(x_ref[r0:r0 + MM_CHUNK, :], g_ref[...], sc, sh).astype(BF16)
        q_ref[r0:r0 + MM_CHUNK, :] = _dot(h, w_ref[:, 0:D]).astype(BF16)
        k_ref[r0:r0 + MM_CHUNK, :] = _dot(h, w_ref[:, D:2 * D]).astype(BF16)
        v_ref[r0:r0 + MM_CHUNK, :] = _dot(h, w_ref[:, 2 * D:3 * D]).astype(BF16)
        kv_ref[r0:r0 + MM_CHUNK, :] = _dot(h, w_ref[:, 3 * D:3 * D + 2 * N_KV_HEADS * HEAD_DIM])


def _qkv_ctx(x, mod4, layer, g_mix, w_all):
    n = w_all.shape[1]
    bf = jax.ShapeDtypeStruct((T_CTX, D), BF16)
    tok = pl.BlockSpec((QKV_TM, D), lambda i: (i, 0))
    return pl.pallas_call(
        _qkv_ctx_kernel,
        grid=(T_CTX // QKV_TM,),
        in_specs=[tok, _mod_spec(layer, QKV_TM), _row_spec(), _full_spec((D, n))],
        out_specs=[tok, tok, tok, pl.BlockSpec((QKV_TM, 512), lambda i: (i, 0))],
        out_shape=[bf, bf, bf, jax.ShapeDtypeStruct((T_CTX, 512), F32)],
        compiler_params=_cparams(("arbitrary",)),
        name="qkv_ctx",
    )(x, mod4, g_mix.reshape(1, D), w_all)


def _qkv_lat_kernel(x_ref, mod_ref, g_ref, w_ref, cos_ref, sin_ref, q_ref, k_ref, v_ref):
    sh = mod_ref[:, 0:D]
    sc = mod_ref[:, D:2 * D]
    for c in range(QKV_TM // MM_CHUNK):
        r0 = c * MM_CHUNK
        h = _norm_mod(x_ref[r0:r0 + MM_CHUNK, :], g_ref[...], sc, sh).astype(BF16)
        cos = cos_ref[r0:r0 + MM_CHUNK, :]
        sin = sin_ref[r0:r0 + MM_CHUNK, :]
        q_ref[r0:r0 + MM_CHUNK, :] = (_dot(h, w_ref[:, 0:D]) * cos + _dot(h, w_ref[:, D:2 * D]) * sin).astype(BF16)
        k_ref[r0:r0 + MM_CHUNK, :] = (_dot(h, w_ref[:, 2 * D:3 * D]) * cos
                                      + _dot(h, w_ref[:, 3 * D:4 * D]) * sin).astype(BF16)
        v_ref[r0:r0 + MM_CHUNK, :] = _dot(h, w_ref[:, 4 * D:5 * D]).astype(BF16)


def _qkv_lat(x, mod4, layer, g_mix, w_all, cos_t, sin_t):
    n = w_all.shape[1]
    nct = T_CTX // QKV_TM
    lpb = DEC_SEQ // QKV_TM
    bf = jax.ShapeDtypeStruct((T_LAT, D), BF16)
    tok_out = pl.BlockSpec((QKV_TM, D), lambda i: (i, 0))
    rope = pl.BlockSpec((QKV_TM, D), lambda i: (i % lpb, 0))
    return pl.pallas_call(
        _qkv_lat_kernel,
        grid=(T_LAT // QKV_TM,),
        in_specs=[pl.BlockSpec((QKV_TM, D), lambda i: (i + nct, 0)),
                  pl.BlockSpec((None, None, 1, N_MOD * D), lambda i: (layer, 8 + i // lpb, 0, 0)),
                  _row_spec(), _full_spec((D, n)), rope, rope],
        out_specs=[tok_out, tok_out, tok_out],
        out_shape=[bf, bf, bf],
        compiler_params=_cparams(("arbitrary",)),
        name="qkv_lat",
    )(x, mod4, g_mix.reshape(1, D), w_all, cos_t, sin_t)


KVW = Q_PER_KV * HEAD_DIM


def _head_masks(rows):
    lane = lax.broadcasted_iota(I32, (rows, KVW), 1)
    return [jnp.logical_and(lane >= g * HEAD_DIM, lane < (g + 1) * HEAD_DIM) for g in range(Q_PER_KV)]


def _attn_ctx_kernel(sink_ref, q_ref, k_ref, v_ref, o_ref):
    scale = HEAD_DIM ** -0.5
    masks = _head_masks(SEQ)
    for kv in range(N_KV_HEADS):
        c0 = kv * KVW
        q = q_ref[:, c0:c0 + KVW]
        k = k_ref[:, c0:c0 + KVW]
        v = v_ref[:, c0:c0 + KVW]
        acc = jnp.zeros((SEQ, KVW), F32)
        for g in range(Q_PER_KV):
            sink = sink_ref[kv * Q_PER_KV + g]
            qg = jnp.where(masks[g], q, jnp.zeros_like(q))
            s = _dot_nt(qg, k) * scale
            m = jnp.maximum(jnp.max(s, axis=-1, keepdims=True), sink)
            e = jnp.exp(s - m)
            den = jnp.sum(e, axis=-1, keepdims=True) + jnp.exp(sink - m)
            og = _dot(e.astype(BF16), v) / den
            acc = jnp.where(masks[g], og, acc)
        o_ref[:, c0:c0 + KVW] = acc.astype(BF16)


def _attn_ctx(sink, q, k, v):
    tok = pl.BlockSpec((SEQ, D), lambda b, *_: (b, 0))
    return pl.pallas_call(
        _attn_ctx_kernel,
        grid_spec=pltpu.PrefetchScalarGridSpec(
            num_scalar_prefetch=1, grid=(BATCH,),
            in_specs=[tok, tok, tok], out_specs=tok),
        out_shape=jax.ShapeDtypeStruct((T_CTX, D), BF16),
        compiler_params=_cparams(("arbitrary",)),
        name="attn_ctx",
    )(sink, q, k, v)


ATT_TQ = 128
ATT_SPAN = ATT_TQ + 2 * WINDOW


def _attn_lat_kernel(sink_ref, q_ref, k_ref, v_ref, ck_ref, cv_ref, o_ref):
    qb = pl.program_id(1)
    scale = HEAD_DIM ** -0.5
    w0 = pl.multiple_of(jnp.clip(qb * ATT_TQ - WINDOW, 0, DEC_SEQ - ATT_SPAN), ATT_TQ)
    rows = Q_PER_KV * ATT_TQ
    ridx = lax.broadcasted_iota(I32, (rows, ATT_SPAN), 0)
    qpos = qb * ATT_TQ + (ridx & (ATT_TQ - 1))
    kpos = w0 + lax.broadcasted_iota(I32, (rows, ATT_SPAN), 1)
    valid = jnp.abs(qpos - kpos) <= WINDOW
    rcol = lax.broadcasted_iota(I32, (rows, 1), 0)
    masks = _head_masks(ATT_TQ)
    for kv in range(N_KV_HEADS):
        c0 = kv * KVW
        q = q_ref[:, c0:c0 + KVW]
        qs = jnp.concatenate([jnp.where(masks[g], q, jnp.zeros_like(q)) for g in range(Q_PER_KV)], axis=0)
        sink = jnp.zeros((rows, 1), F32)
        for g in range(Q_PER_KV):
            sink = jnp.where(rcol >= g * ATT_TQ, sink_ref[kv * Q_PER_KV + g], sink)
        s_ctx = _dot_nt(qs, ck_ref[:, c0:c0 + KVW]) * scale
        s_win = _dot_nt(qs, k_ref[pl.ds(w0, ATT_SPAN), c0:c0 + KVW]) * scale
        s_win = jnp.where(valid, s_win, NEG_INF)
        m = jnp.maximum(jnp.maximum(jnp.max(s_ctx, axis=-1, keepdims=True),
                                    jnp.max(s_win, axis=-1, keepdims=True)), sink)
        e_ctx = jnp.exp(s_ctx - m)
        e_win = jnp.exp(s_win - m)
        den = (jnp.exp(sink - m) + jnp.sum(e_ctx, axis=-1, keepdims=True)
               + jnp.sum(e_win, axis=-1, keepdims=True))
        o = (_dot(e_ctx.astype(BF16), cv_ref[:, c0:c0 + KVW])
             + _dot(e_win.astype(BF16), v_ref[pl.ds(w0, ATT_SPAN), c0:c0 + KVW])) / den
        acc = jnp.zeros((ATT_TQ, KVW), F32)
        for g in range(Q_PER_KV):
            acc = jnp.where(masks[g], o[g * ATT_TQ:(g + 1) * ATT_TQ, :], acc)
        o_ref[:, c0:c0 + KVW] = acc.astype(BF16)


def _attn_lat(sink, q, k, v, ck, cv):
    nqb = DEC_SEQ // ATT_TQ
    qspec = pl.BlockSpec((ATT_TQ, D), lambda b, i, *_: (b * nqb + i, 0))
    seq = pl.BlockSpec((DEC_SEQ, D), lambda b, i, *_: (b, 0))
    ctx = pl.BlockSpec((PAST_LEN, D), lambda b, i, *_: (b, 0))
    return pl.pallas_call(
        _attn_lat_kernel,
        grid_spec=pltpu.PrefetchScalarGridSpec(
            num_scalar_prefetch=1, grid=(DEC_BATCH, nqb),
            in_specs=[qspec, seq, seq, ctx, ctx], out_specs=qspec),
        out_shape=jax.ShapeDtypeStruct((T_LAT, D), BF16),
        compiler_params=_cparams(("arbitrary", "arbitrary")),
        name="attn_lat",
    )(sink, q, k, v, ck, cv)


def _resproj_kernel(x_ref, a_ref, mod_ref, w_ref, o_ref):
    gate = mod_ref[:, 2 * D:3 * D]
    o_ref[...] = x_ref[...] + gate * _dot(a_ref[...], w_ref[...])


def _resproj(x, a, mod4, layer, w):
    tok = pl.BlockSpec((TM, D), lambda i: (i, 0))
    return pl.pallas_call(
        _resproj_kernel,
        grid=(T_ALL // TM,),
        in_specs=[tok, tok, _mod_spec(layer, TM), _full_spec((D, D))],
        out_specs=tok,
        out_shape=jax.ShapeDtypeStruct((T_ALL, D), F32),
        compiler_params=_cparams(("arbitrary",)),
        name="attn_out_proj",
    )(x, a, mod4, w)


def _rope_tables():
    rows = DEC_SEQ // GRID_W
    row = jnp.repeat(jnp.arange(rows), GRID_W).astype(F32)
    col = jnp.tile(jnp.arange(GRID_W), rows).astype(F32)
    inv = ROPE_THETA ** (-jnp.arange(ROPE_FREQS, dtype=F32) / ROPE_FREQS)
    ang = jnp.concatenate([row[:, None] * inv, col[:, None] * inv], axis=-1)
    cos = jnp.cos(ang)
    sin = jnp.sin(ang)
    cos_h = jnp.concatenate([cos, cos], axis=-1)
    sin_h = jnp.concatenate([sin, sin], axis=-1)
    return jnp.tile(cos_h, (1, N_HEADS)), jnp.tile(sin_h, (1, N_HEADS))


def _rot_half_cols(w):
    k = w.shape[0]
    w4 = w.reshape(k, -1, 2, HEAD_DIM // 2)
    return jnp.stack([-w4[:, :, 1], w4[:, :, 0]], axis=2).reshape(k, -1)


def _expand_kv_cols(w):
    k = w.shape[0]
    w3 = w.reshape(k, N_KV_HEADS, 1, HEAD_DIM)
    return jnp.broadcast_to(w3, (k, N_KV_HEADS, Q_PER_KV, HEAD_DIM)).reshape(k, N_HEADS * HEAD_DIM)


FG = D // FNET_GROUPS


def _fnet_kernel(x_ref, mod_ref, g_ref, cs_ref, fl_ref, w_ref, o_ref, h_s, ab_s):
    sh = mod_ref[:, 0:D]
    sc = mod_ref[:, D:2 * D]
    gate = mod_ref[:, 2 * D:3 * D]
    for c in range(TM // ROW_CHUNK):
        r0 = c * ROW_CHUNK
        h_s[r0:r0 + ROW_CHUNK, :] = _norm_mod(x_ref[r0:r0 + ROW_CHUNK, :], g_ref[...], sc, sh).astype(BF16)
    for g in range(FNET_GROUPS):
        ab = _dot(h_s[:, g * FG:(g + 1) * FG], cs_ref[...])
        ab_s[0:TM, g * FG:(g + 1) * FG] = ab[:, 0:FG].astype(BF16)
        ab_s[TM:2 * TM, g * FG:(g + 1) * FG] = ab[:, FG:2 * FG].astype(BF16)
    for c in range(TM // MM_CHUNK):
        r0 = c * MM_CHUNK
        f = _dot(fl_ref[r0:r0 + MM_CHUNK, :], ab_s[...])
        y = _dot(f.astype(BF16), w_ref[...])
        o_ref[r0:r0 + MM_CHUNK, :] = x_ref[r0:r0 + MM_CHUNK, :] + gate * y


def _fnet_mixer(x, mod4, layer, g_mix, cs, fl, w_out):
    nct = T_CTX // TM
    tok = pl.BlockSpec((TM, D), lambda i: (i, 0))
    return pl.pallas_call(
        _fnet_kernel,
        grid=(T_ALL // TM,),
        in_specs=[tok, _mod_spec(layer, TM), _row_spec(), _full_spec((FG, 2 * FG)),
                  pl.BlockSpec((None, TM, 2 * TM), lambda i: (jnp.where(i < nct, 0, 1), 0, 0)),
                  _full_spec((D, D))],
        out_specs=tok,
        out_shape=jax.ShapeDtypeStruct((T_ALL, D), F32),
        scratch_shapes=[pltpu.VMEM((TM, D), BF16), pltpu.VMEM((2 * TM, D), BF16)],
        compiler_params=_cparams(("arbitrary",)),
        name="fnet_mixer",
    )(x, mod4, g_mix.reshape(1, D), cs, fl, w_out)


def _dft_cos_sin(n):
    k = np.arange(n)
    ang = 2.0 * np.pi * ((k[:, None] * k[None, :]) % n) / n
    return np.cos(ang), np.sin(ang)


def _fnet_tables():
    cc, sc = _dft_cos_sin(FG)
    cs = np.concatenate([cc, sc], axis=1) / math.sqrt(FG)
    mats = []
    for seq in (SEQ, DEC_SEQ):
        cl, sl = _dft_cos_sin(seq)
        reps = TM // seq
        eye = np.eye(reps)
        mats.append(np.concatenate([np.kron(eye, cl), -np.kron(eye, sl)], axis=1) / math.sqrt(seq))
    return jnp.asarray(cs, F32).astype(BF16), jnp.asarray(np.stack(mats), F32).astype(BF16)


ROUTER_TM = 512
TOK_SUB = D // LANES


def _store_token_tiles(ref, r0, val):
    rows = val.shape[0]
    for c in range(TOK_SUB):
        ref[pl.ds(r0 * TOK_SUB + c, rows, stride=TOK_SUB), :] = val[:, c * LANES:(c + 1) * LANES]


def _load_token_tiles(ref, r0, rows, c, lead=None):
    idx = pl.ds(r0 * TOK_SUB + c, rows, stride=TOK_SUB)
    return ref[idx, :] if lead is None else ref[lead, idx, :]


def _router_kernel(x_ref, mod_ref, g_ref, wr_ref, tri_ref, h_ref, info_ref, gates_ref, cnt_ref, cnt_s):
    sh = mod_ref[:, 3 * D:4 * D]
    sc = mod_ref[:, 4 * D:5 * D]
    w_hi = wr_ref[0]
    w_lo = wr_ref[1]

    @pl.when(pl.program_id(0) == 0)
    def _():
        cnt_s[...] = jnp.zeros_like(cnt_s)

    for c in range(ROUTER_TM // ROW_CHUNK):
        r0 = c * ROW_CHUNK
        h = _norm_mod(x_ref[r0:r0 + ROW_CHUNK, :], g_ref[...], sc, sh)
        _store_token_tiles(h_ref, r0, h)
        h_hi, h_lo = _split_bf16(h)
        logits = _dot(h_hi, w_hi) + _dot(h_lo, w_hi) + _dot(h_hi, w_lo)
        lane = lax.broadcasted_iota(I32, logits.shape, 1)
        lg = jnp.where(lane < N_EXPERTS, logits, -jnp.inf)
        m1 = jnp.max(lg, axis=-1, keepdims=True)
        i1 = jnp.min(jnp.where(lg == m1, lane, LANES), axis=-1, keepdims=True)
        lg2 = jnp.where(lane == i1, -jnp.inf, lg)
        m2 = jnp.max(lg2, axis=-1, keepdims=True)
        i2 = jnp.min(jnp.where(lg2 == m2, lane, LANES), axis=-1, keepdims=True)
        e2 = jnp.exp(m2 - m1)
        den = 1.0 + e2
        gates_ref[r0:r0 + ROW_CHUNK, :] = jnp.where(lane == 0, 1.0 / den, jnp.where(lane == 1, e2 / den, 0.0))
        onehot = jnp.where(jnp.logical_or(lane == i1, lane == i2), 1.0, 0.0)
        cnt = cnt_s[0:1, :]
        rank = _dot(tri_ref[...], onehot.astype(BF16)) + cnt
        cnt_s[0:1, :] = rank[ROW_CHUNK - 1:ROW_CHUNK, :] + onehot[ROW_CHUNK - 1:ROW_CHUNK, :]
        r1 = jnp.sum(jnp.where(lane == i1, rank, 0.0), axis=-1, keepdims=True)
        r2 = jnp.sum(jnp.where(lane == i2, rank, 0.0), axis=-1, keepdims=True)
        info = jnp.where(lane == 0, i1.astype(F32), jnp.where(lane == 1, i2.astype(F32),
                         jnp.where(lane == 2, r1, jnp.where(lane == 3, r2, 0.0))))
        info_ref[r0:r0 + ROW_CHUNK, :] = info.astype(I32)
    cnt_ref[...] = cnt_s[...]


def _router(x, mod4, layer, g_ffn, w_router):
    wr = jnp.zeros((D, LANES), F32).at[:, 0:N_EXPERTS].set(w_router)
    wr_hi = wr.astype(BF16)
    wr_lo = (wr - wr_hi.astype(F32)).astype(BF16)
    tok = pl.BlockSpec((ROUTER_TM, D), lambda i: (i, 0))
    nar = pl.BlockSpec((ROUTER_TM, LANES), lambda i: (i, 0))
    tri = jnp.tril(jnp.ones((ROW_CHUNK, ROW_CHUNK), F32), -1).astype(BF16)
    return pl.pallas_call(
        _router_kernel,
        grid=(T_ALL // ROUTER_TM,),
        in_specs=[tok, _mod_spec(layer, ROUTER_TM), _row_spec(), _full_spec((2, D, LANES)),
                  _full_spec((ROW_CHUNK, ROW_CHUNK))],
        out_specs=[pl.BlockSpec((ROUTER_TM * TOK_SUB, LANES), lambda i: (i, 0)), nar, nar,
                   pl.BlockSpec((8, LANES), lambda i: (0, 0))],
        out_shape=[jax.ShapeDtypeStruct((T_ALL * TOK_SUB, LANES), F32),
                   jax.ShapeDtypeStruct((T_ALL, LANES), I32),
                   jax.ShapeDtypeStruct((T_ALL, LANES), F32),
                   jax.ShapeDtypeStruct((8, LANES), F32)],
        scratch_shapes=[pltpu.VMEM((8, LANES), F32)],
        compiler_params=_cparams(("arbitrary",)),
        name="moe_router",
    )(x, mod4, g_ffn.reshape(1, D), jnp.stack([wr_hi, wr_lo]), tri)


def _route_plan(info, cnt):
    counts = cnt[0, 0:N_EXPERTS].astype(I32)
    padded = ((counts + TME - 1) // TME) * TME
    gend = jnp.cumsum(padded)
    gstart = gend - padded
    eid = jnp.arange(N_EXPERTS, dtype=I32)[None, :]
    pos1 = (jnp.sum(jnp.where(info[:, 0:1] == eid, gstart[None, :], 0), axis=1) + info[:, 2]).astype(I32)
    pos2 = (jnp.sum(jnp.where(info[:, 1:2] == eid, gstart[None, :], 0), axis=1) + info[:, 3]).astype(I32)
    total = gend[-1:]
    pad_lo = jnp.concatenate([gstart + counts, total])
    pad_hi = jnp.concatenate([gend, jnp.full((1,), P_MAX, I32)])
    src = _inverse_map(pos1, pos2, jnp.concatenate([pad_lo, pad_hi]).astype(I32))
    n_valid = gend[-1] // TME
    tile_start = jnp.arange(N_EXPERT_TILES, dtype=I32) * TME
    eff_start = jnp.minimum(tile_start, jnp.maximum(n_valid - 1, 0) * TME)
    tile_expert = jnp.minimum(jnp.sum((eff_start[:, None] >= gend[None, :]).astype(I32), axis=1),
                              N_EXPERTS - 1).astype(I32)
    return pos1, pos2, src, tile_expert, n_valid.astype(I32).reshape(1)


N_PAD_RANGES = N_EXPERTS + 1


def _inverse_map_kernel(p1_ref, p2_ref, pad_ref, src_ref):
    def clear(p, carry):
        src_ref[p] = 0
        return carry
    for i in range(N_PAD_RANGES):
        lax.fori_loop(pad_ref[i], pad_ref[N_PAD_RANGES + i], clear, 0)

    def body(t, carry):
        src_ref[p1_ref[t]] = t
        src_ref[p2_ref[t]] = t
        return carry
    lax.fori_loop(0, T_ALL, body, 0, unroll=8)


def _inverse_map(pos1, pos2, pad_ranges):
    smem = pl.BlockSpec(memory_space=pltpu.SMEM)
    return pl.pallas_call(
        _inverse_map_kernel,
        in_specs=[smem, smem, smem], out_specs=smem,
        out_shape=jax.ShapeDtypeStruct((P_MAX,), I32),
        name="moe_inverse_map",
    )(pos1, pos2, pad_ranges)


EXP_TF = 512
EXP_NF = D_FF_EXPERT // EXP_TF
GATHER_ROWS = 152
XS_ROWS = EXP_NF * GATHER_ROWS
assert XS_ROWS >= TME
GATHER_DMA_PRIORITY = 1


def _expert_kernel(te_ref, nv_ref, src_ref, h_hbm, wg_ref, wu_ref, wd_ref, o_ref, xs_buf, xb_s, acc_s, sem):
    m = pl.program_id(0)
    f = pl.program_id(1)
    nt = pl.num_programs(0)
    nf = pl.num_programs(1)
    n_valid = nv_ref[0]
    slot = m % 2
    nslot = 1 - slot

    def row_copy(tile, r, dst_slot):
        t = src_ref[tile * TME + jnp.minimum(r, TME - 1)]
        return pltpu.make_async_copy(h_hbm.at[pl.ds(pl.multiple_of(t * TOK_SUB, TOK_SUB), TOK_SUB), :],
                                     xs_buf.at[dst_slot, pl.ds(pl.multiple_of(r * TOK_SUB, TOK_SUB), TOK_SUB), :],
                                     sem.at[dst_slot])

    def wait_slot(s):
        pltpu.make_async_copy(h_hbm.at[pl.ds(0, XS_ROWS * TOK_SUB), :], xs_buf.at[s], sem.at[s]).wait()

    @pl.when(m < n_valid)
    def _():
        @pl.when(f == 0)
        def _():
            @pl.when(m == 0)
            def _():
                acc_s[...] = jnp.zeros_like(acc_s)

                def body(r, carry):
                    row_copy(0, r, 0).start()
                    return carry
                lax.fori_loop(0, XS_ROWS, body, 0, unroll=8)

            wait_slot(slot)
            for c in range(TOK_SUB):
                xb_s[:, c * LANES:(c + 1) * LANES] = _load_token_tiles(xs_buf, 0, TME, c, lead=slot).astype(BF16)

        nxt = jnp.minimum(m + 1, nt - 1)
        base = f * GATHER_ROWS
        for k in range(GATHER_ROWS):
            row_copy(nxt, base + k, nslot).start(priority=GATHER_DMA_PRIORITY)

        xb = xb_s[...]
        gg = _dot(xb, wg_ref[...].astype(BF16))
        uu = _dot(xb, wu_ref[...].astype(BF16))
        a = (gg * jax.nn.sigmoid(gg) * uu).astype(BF16)
        contrib = _dot(a, wd_ref[...].astype(BF16))

        acc_s[...] = jnp.where(f == 0, 0.0, acc_s[...]) + contrib

        @pl.when(f == nf - 1)
        def _():
            _store_token_tiles(o_ref, 0, acc_s[...])

        @pl.when(jnp.logical_and(m == nt - 1, f == nf - 1))
        def _():
            wait_slot(nslot)

    @pl.when(jnp.logical_and(m >= n_valid, f == 0))
    def _():
        @pl.when(m == n_valid)
        def _():
            wait_slot(slot)
        o_ref[...] = jnp.zeros_like(o_ref)


def _expert_ffn(tile_expert, n_valid, src, h, w_gu, w_down, li):
    def feff(m, f, nv):
        return jnp.where(m < nv[0], f, EXP_NF - 1)
    return pl.pallas_call(
        _expert_kernel,
        grid_spec=pltpu.PrefetchScalarGridSpec(
            num_scalar_prefetch=3, grid=(N_EXPERT_TILES, EXP_NF),
            in_specs=[pl.BlockSpec(memory_space=pl.ANY),
                      pl.BlockSpec((None, None, D, EXP_TF),
                                   lambda m, f, te, nv, sr: (li, te[m], 0, feff(m, f, nv))),
                      pl.BlockSpec((None, None, D, EXP_TF),
                                   lambda m, f, te, nv, sr: (li, te[m], 0, EXP_NF + feff(m, f, nv))),
                      pl.BlockSpec((None, None, EXP_TF, D),
                                   lambda m, f, te, nv, sr: (li, te[m], feff(m, f, nv), 0))],
            out_specs=pl.BlockSpec((TME * TOK_SUB, LANES), lambda m, f, te, nv, sr: (m, 0)),
            scratch_shapes=[pltpu.VMEM((2, XS_ROWS * TOK_SUB, LANES), F32), pltpu.VMEM((TME, D), BF16),
                            pltpu.VMEM((TME, D), F32), pltpu.SemaphoreType.DMA((2,))]),
        out_shape=jax.ShapeDtypeStruct((P_MAX * TOK_SUB, LANES), F32),
        compiler_params=_cparams(("arbitrary", "arbitrary")),
        name="moe_expert_swiglu",
    )(tile_expert, n_valid, src, h, w_gu, w_gu, w_down)


COMB_TM = 512
COMB_UNROLL = 8


def _combine_kernel(p1_ref, p2_ref, x_ref, gates_ref, mod_ref, gf_ref, y_hbm, o_ref, a_s, b_s, sem, *, final):
    i = pl.program_id(0)

    def tile_copy(p, r, dst, s):
        return pltpu.make_async_copy(y_hbm.at[pl.ds(pl.multiple_of(p * TOK_SUB, TOK_SUB), TOK_SUB), :],
                                     dst.at[pl.ds(pl.multiple_of(r * TOK_SUB, TOK_SUB), TOK_SUB), :], s)

    def body(g, carry):
        for k in range(COMB_UNROLL):
            r = g * COMB_UNROLL + k
            t = i * COMB_TM + r
            tile_copy(p1_ref[t], r, a_s, sem.at[0]).start(priority=0)
            tile_copy(p2_ref[t], r, b_s, sem.at[1]).start(priority=1)
        return carry
    lax.fori_loop(0, COMB_TM // COMB_UNROLL, body, 0)
    pltpu.make_async_copy(y_hbm.at[pl.ds(0, COMB_TM * TOK_SUB), :], a_s, sem.at[0]).wait()
    pltpu.make_async_copy(y_hbm.at[pl.ds(0, COMB_TM * TOK_SUB), :], b_s, sem.at[1]).wait()

    for c in range(COMB_TM // ROW_CHUNK):
        r0 = c * ROW_CHUNK
        w1 = gates_ref[r0:r0 + ROW_CHUNK, 0:1]
        w2 = gates_ref[r0:r0 + ROW_CHUNK, 1:2]
        for j in range(TOK_SUB):
            cols = slice(j * LANES, (j + 1) * LANES)
            y = (w1 * _load_token_tiles(a_s, r0, ROW_CHUNK, j) + w2 * _load_token_tiles(b_s, r0, ROW_CHUNK, j))
            o_ref[r0:r0 + ROW_CHUNK, cols] = x_ref[r0:r0 + ROW_CHUNK, cols] + mod_ref[:, 5 * D + j * LANES:5 * D + (j + 1) * LANES] * y
        if final:
            xn = o_ref[r0:r0 + ROW_CHUNK, :]
            ms = jnp.mean(xn * xn, axis=-1, keepdims=True)
            o_ref[r0:r0 + ROW_CHUNK, :] = xn * lax.rsqrt(ms + EPS) * gf_ref[...]


def _combine(pos1, pos2, x, gates, mod4, layer, g_final, y, final):
    tok = pl.BlockSpec((COMB_TM, D), lambda i, *_: (i, 0))
    return pl.pallas_call(
        functools.partial(_combine_kernel, final=final),
        grid_spec=pltpu.PrefetchScalarGridSpec(
            num_scalar_prefetch=2, grid=(T_ALL // COMB_TM,),
            in_specs=[tok, pl.BlockSpec((COMB_TM, LANES), lambda i, *_: (i, 0)),
                      _mod_spec(layer, COMB_TM), _row_spec(),
                      pl.BlockSpec(memory_space=pl.ANY)],
            out_specs=tok,
            scratch_shapes=[pltpu.VMEM((COMB_TM * TOK_SUB, LANES), F32), pltpu.VMEM((COMB_TM * TOK_SUB, LANES), F32),
                            pltpu.SemaphoreType.DMA((2,))]),
        out_shape=jax.ShapeDtypeStruct((T_ALL, D), F32),
        compiler_params=_cparams(("arbitrary",)),
        name="moe_combine_final" if final else "moe_combine",
    )(pos1, pos2, x, gates, mod4, g_final.reshape(1, D), y)


def _moe_layer(x, mod4, layer, g_ffn, w_router, w_gu, w_down, li, g_final, final):
    h, info, gates, cnt = _router(x, mod4, layer, g_ffn, w_router)
    pos1, pos2, src, tile_expert, n_valid = _route_plan(info, cnt)
    ys = _expert_ffn(tile_expert, n_valid, src, h, w_gu, w_down, li)
    return _combine(pos1, pos2, x, gates, mod4, layer, g_final, ys, final)


def kernel(x_prompt, x_sample, cache_k, cache_v, state_ssm_re, state_ssm_im, c, c_ctx, w_ada, b_ada, g_mix, g_ffn, g_final, conv_w_in, conv_w, conv_w_out, s5_lambda_re, s5_lambda_im, s5_b_re, s5_b_im, s5_c_re, s5_c_im, s5_log_dt, s5_d, s5_w_glu, attn_w_q, attn_w_kv, attn_w_o, attn_sink, fnet_w_out, ffn_w_gu, ffn_w_down, moe_w_router, moe_w_gu, moe_w_down):
    x = jnp.concatenate([x_prompt.reshape(T_CTX, D), x_sample.reshape(T_LAT, D)], axis=0)
    cond16 = jnp.concatenate([jnp.broadcast_to(c_ctx[None, :], (8, D)), c], axis=0)
    mod3 = _mod_all(cond16, w_ada, b_ada)
    mod4 = mod3.reshape(DEPTH, 16, 1, N_MOD * D)

    x = _conv_mixer(x, mod4, 0, g_mix[0], conv_w_in[0].astype(BF16), conv_w[0], conv_w_out[0].astype(BF16))
    x = _dense_ffn(x, mod4, 0, g_ffn[0], ffn_w_gu[0].astype(BF16), ffn_w_down[0].astype(BF16))

    lbr, lbi, bbr, bbi = _s5_prep(s5_lambda_re[0], s5_lambda_im[0], s5_log_dt[0], s5_b_re[0], s5_b_im[0])
    w_b, w_c, lam_s = _s5_weights(lbr, lbi, bbr, bbi, s5_c_re[0], s5_c_im[0])
    w_glu = s5_w_glu[0].astype(BF16)
    xc = x[0:T_CTX].reshape(BATCH, SEQ, D).transpose(1, 0, 2).reshape(T_CTX, D)
    xl = x[T_CTX:].reshape(DEC_BATCH, DEC_SEQ, D).transpose(1, 0, 2).reshape(T_LAT, D)
    h0_ctx = jnp.zeros((2, S5_JT, BATCH, 2 * S5_HALF), F32)
    h0_lat = _s5_state_in(state_ssm_re[:, 0], state_ssm_im[:, 0])
    yc, fin_c = _s5_scan(xc, BATCH, mod3, 1, 0, g_mix[1], w_b, w_c, lam_s, h0_ctx)
    yl, _ = _s5_scan(xl, DEC_BATCH, mod3, 1, 1, g_mix[1], w_b, w_c, lam_s, h0_lat)
    xc = _s5_glu(xc, yc, mod3, 1, 0, g_mix[1], s5_d[0], w_glu)
    xl = _s5_glu(xl, yl, mod3, 1, 1, g_mix[1], s5_d[0], w_glu)
    x = jnp.concatenate([xc.reshape(SEQ, BATCH, D).transpose(1, 0, 2).reshape(T_CTX, D),
                         xl.reshape(DEC_SEQ, DEC_BATCH, D).transpose(1, 0, 2).reshape(T_LAT, D)], axis=0)
    new_re, new_im = _s5_state_out(fin_c)
    x = _moe_layer(x, mod4, 1, g_ffn[1], moe_w_router[0], moe_w_gu, moe_w_down, 0, g_final, False)

    wq = attn_w_q[0]
    wk = _expand_kv_cols(attn_w_kv[0][:, 0:N_KV_HEADS * HEAD_DIM])
    wv = _expand_kv_cols(attn_w_kv[0][:, N_KV_HEADS * HEAD_DIM:])
    w_ctx = jnp.concatenate([wq, wk, wv, attn_w_kv[0]], axis=1).astype(BF16)
    w_lat = jnp.concatenate([wq, _rot_half_cols(wq), wk, _rot_half_cols(wk), wv], axis=1).astype(BF16)
    cos_t, sin_t = _rope_tables()
    q_c, k_c, v_c, kv_c = _qkv_ctx(x, mod4, 2, g_mix[2], w_ctx)
    q_l, k_l, v_l = _qkv_lat(x, mod4, 2, g_mix[2], w_lat, cos_t, sin_t)
    sink = attn_sink[0]
    o_c = _attn_ctx(sink, q_c, k_c, v_c)
    expand = lambda a: jnp.broadcast_to(
        a.reshape(DEC_BATCH * PAST_LEN, N_KV_HEADS, 1, HEAD_DIM),
        (DEC_BATCH * PAST_LEN, N_KV_HEADS, Q_PER_KV, HEAD_DIM)).reshape(DEC_BATCH * PAST_LEN, D).astype(BF16)
    o_l = _attn_lat(sink, q_l, k_l, v_l, expand(cache_k[:, 0]), expand(cache_v[:, 0]))
    x = _resproj(x, jnp.concatenate([o_c, o_l], axis=0), mod4, 2, attn_w_o[0].astype(BF16))
    kvw = N_KV_HEADS * HEAD_DIM
    new_k = kv_c[:, 0:kvw].reshape(BATCH, 1, SEQ, N_KV_HEADS, HEAD_DIM)
    new_v = kv_c[:, kvw:].reshape(BATCH, 1, SEQ, N_KV_HEADS, HEAD_DIM)
    x = _dense_ffn(x, mod4, 2, g_ffn[2], ffn_w_gu[1].astype(BF16), ffn_w_down[1].astype(BF16))

    cs, fl = _fnet_tables()
    x = _fnet_mixer(x, mod4, 3, g_mix[3], cs, fl, fnet_w_out[0].astype(BF16))
    x = _moe_layer(x, mod4, 3, g_ffn[3], moe_w_router[1], moe_w_gu, moe_w_down, 1, g_final, True)

    y_prompt = x[0:T_CTX].reshape(BATCH, SEQ, D)
    y_sample = x[T_CTX:].reshape(DEC_BATCH, DEC_SEQ, D)
    return (y_prompt, y_sample, new_k, new_v, new_re[:, None], new_im[:, None])
```

```python
import functools
import math

import numpy as np
import jax
import jax.numpy as jnp
from jax import lax
from jax.experimental import pallas as pl
from jax.experimental.pallas import tpu as pltpu

F32 = jnp.float32
BF16 = jnp.bfloat16
I32 = jnp.int32

D = 1024
BATCH = 16
SEQ = 256
DEPTH = 4
DEC_BATCH = 8
DEC_SEQ = 1024
PAST_LEN = 256
GRID_W = 64
EPS = 1e-6
N_MOD = 6
S5_GROUP = 16
S5_GROUPS = D // S5_GROUP
S5_STATE = 64
HEAD_DIM = 64
N_HEADS = D // HEAD_DIM
N_KV_HEADS = 4
Q_PER_KV = N_HEADS // N_KV_HEADS
WINDOW = 128
ROPE_THETA = 10000.0
ROPE_FREQS = HEAD_DIM // 4
FNET_GROUPS = 4
D_FF = 2816
N_EXPERTS = 8
TOP_K = 2
D_FF_EXPERT = 3584
NEG_INF = -1e30

T_CTX = BATCH * SEQ
T_LAT = DEC_BATCH * DEC_SEQ
T_ALL = T_CTX + T_LAT

VMEM_LIMIT_V7X = 56 * 1024 * 1024
LANES = 128

TM = 1024
ROW_CHUNK = 256
MM_CHUNK = 512
TME = 1024
N_EXPERT_TILES = (T_ALL * TOP_K) // TME + N_EXPERTS
P_MAX = N_EXPERT_TILES * TME


def _cparams(sem):
    return pltpu.CompilerParams(dimension_semantics=sem, vmem_limit_bytes=VMEM_LIMIT_V7X)


def _dot(a, b):
    return jnp.dot(a, b, preferred_element_type=F32)


def _dot_nt(a, b):
    return lax.dot_general(a, b, (((1,), (1,)), ((), ())), preferred_element_type=F32)


def _split_bf16(a):
    hi = a.astype(BF16)
    lo = (a - hi.astype(F32)).astype(BF16)
    return hi, lo


def _tile_rows(a, reps, axis=0):
    assert axis == 0
    return jnp.concatenate([a] * reps, axis=0)


def _norm_mod(x, g, sc, sh):
    ms = jnp.mean(x * x, axis=-1, keepdims=True)
    y = x * lax.rsqrt(ms + EPS) * g
    return y * (1.0 + sc) + sh


def _mod_row(i, tm):
    nct = T_CTX // tm
    lpb = DEC_SEQ // tm
    return jnp.where(i < nct, 0, 8 + (i - nct) // lpb)


def _mod_spec(layer, tm):
    return pl.BlockSpec((None, None, 1, N_MOD * D), lambda i, *_: (layer, _mod_row(i, tm), 0, 0))


def _row_spec():
    return pl.BlockSpec((1, D), lambda *_: (0, 0))


def _full_spec(shape):
    nd = len(shape)
    return pl.BlockSpec(shape, lambda *_: (0,) * nd, pipeline_mode=pl.Buffered(1))


def _mod_kernel(c_ref, w_ref, b_ref, o_ref):
    c = c_ref[...]
    s = (c * jax.nn.sigmoid(c)).astype(BF16)
    o_ref[...] = _dot(s, w_ref[...].astype(BF16)) + b_ref[...]


def _mod_all(cond16, w_ada, b_ada):
    tn = 1024
    return pl.pallas_call(
        _mod_kernel,
        grid=(DEPTH, N_MOD * D // tn),
        in_specs=[pl.BlockSpec((16, D), lambda l, n: (0, 0)),
                  pl.BlockSpec((None, D, tn), lambda l, n: (l, 0, n)),
                  pl.BlockSpec((None, 1, tn), lambda l, n: (l, 0, n))],
        out_specs=pl.BlockSpec((None, 16, tn), lambda l, n: (l, 0, n)),
        out_shape=jax.ShapeDtypeStruct((DEPTH, 16, N_MOD * D), F32),
        compiler_params=_cparams(("arbitrary", "arbitrary")),
        name="adaln_mod",
    )(cond16, w_ada, b_ada.reshape(DEPTH, 1, N_MOD * D))


def _conv_kernel(x_ref, mod_ref, g_ref, win_ref, cw_ref, wout_ref, o_ref, gb_s, u_s, z_s):
    i = pl.program_id(0)
    is_ctx = i < (T_CTX // TM)
    sh = mod_ref[:, 0:D]
    sc = mod_ref[:, D:2 * D]
    gate = mod_ref[:, 2 * D:3 * D]
    g = g_ref[...]
    zero8 = jnp.zeros((8, D), F32)
    u_s[0:8, :] = zero8
    u_s[8 + TM:16 + TM, :] = zero8
    for c in range(TM // MM_CHUNK):
        r0 = c * MM_CHUNK
        h = _norm_mod(x_ref[r0:r0 + MM_CHUNK, :], g, sc, sh).astype(BF16)
        proj = _dot(h, win_ref[...])
        gb_s[r0:r0 + MM_CHUNK, :] = proj[:, 0:D]
        u_s[8 + r0:8 + r0 + MM_CHUNK, :] = proj[:, D:2 * D] * proj[:, 2 * D:3 * D]
    row = lax.broadcasted_iota(I32, (ROW_CHUNK, 1), 0)
    first = jnp.logical_and(is_ctx, row == 0)
    last = jnp.logical_and(is_ctx, row == ROW_CHUNK - 1)
    for c in range(TM // ROW_CHUNK):
        r0 = c * ROW_CHUNK
        up = jnp.where(first, 0.0, u_s[7 + r0:7 + r0 + ROW_CHUNK, :])
        mid = u_s[8 + r0:8 + r0 + ROW_CHUNK, :]
        dn = jnp.where(last, 0.0, u_s[9 + r0:9 + r0 + ROW_CHUNK, :])
        conv = up * cw_ref[0:1, :] + mid * cw_ref[1:2, :] + dn * cw_ref[2:3, :]
        z_s[r0:r0 + ROW_CHUNK, :] = (gb_s[r0:r0 + ROW_CHUNK, :] * conv).astype(BF16)
    for c in range(TM // MM_CHUNK):
        r0 = c * MM_CHUNK
        y = _dot(z_s[r0:r0 + MM_CHUNK, :], wout_ref[...])
        o_ref[r0:r0 + MM_CHUNK, :] = x_ref[r0:r0 + MM_CHUNK, :] + gate * y


def _conv_mixer(x, mod4, layer, g_mix, w_in, conv_w, w_out):
    assert SEQ == ROW_CHUNK and DEC_SEQ == TM
    return pl.pallas_call(
        _conv_kernel,
        grid=(T_ALL // TM,),
        in_specs=[pl.BlockSpec((TM, D), lambda i: (i, 0)),
                  _mod_spec(layer, TM), _row_spec(),
                  _full_spec((D, 3 * D)), _full_spec((3, D)), _full_spec((D, D))],
        out_specs=pl.BlockSpec((TM, D), lambda i: (i, 0)),
        out_shape=jax.ShapeDtypeStruct((T_ALL, D), F32),
        scratch_shapes=[pltpu.VMEM((TM, D), F32), pltpu.VMEM((TM + 16, D), F32), pltpu.VMEM((TM, D), BF16)],
        compiler_params=_cparams(("arbitrary",)),
        name="conv_mixer",
    )(x, mod4, g_mix.reshape(1, D), w_in, conv_w, w_out)


FFN_TM = 512
MXU_WIDTH_V7X = 256
FFN_SPLITS = ((0, 6 * MXU_WIDTH_V7X), (6 * MXU_WIDTH_V7X, D_FF))
assert D_FF % MXU_WIDTH_V7X == 0


def _ffn_kernel(x_ref, mod_ref, g_ref, wgu_ref, wd_ref, o_ref):
    sh = mod_ref[:, 3 * D:4 * D]
    sc = mod_ref[:, 4 * D:5 * D]
    gate = mod_ref[:, 5 * D:6 * D]
    x = x_ref[...]
    h = _norm_mod(x, g_ref[...], sc, sh).astype(BF16)
    acc = None
    for lo, hi in FFN_SPLITS:
        gg = _dot(h, wgu_ref[:, lo:hi])
        uu = _dot(h, wgu_ref[:, D_FF + lo:D_FF + hi])
        a = (gg * jax.nn.sigmoid(gg) * uu).astype(BF16)
        contrib = _dot(a, wd_ref[lo:hi, :])
        acc = contrib if acc is None else acc + contrib
    o_ref[...] = x + gate * acc


def _dense_ffn(x, mod4, layer, g_ffn, w_gu, w_down):
    tok = pl.BlockSpec((FFN_TM, D), lambda i: (i, 0))
    return pl.pallas_call(
        _ffn_kernel,
        grid=(T_ALL // FFN_TM,),
        in_specs=[tok, _mod_spec(layer, FFN_TM), _row_spec(),
                  _full_spec((D, 2 * D_FF)), _full_spec((D_FF, D))],
        out_specs=tok,
        out_shape=jax.ShapeDtypeStruct((T_ALL, D), F32),
        compiler_params=_cparams(("arbitrary",)),
        name="dense_swiglu",
    )(x, mod4, g_ffn.reshape(1, D), w_gu, w_down)


def _s5_prep_kernel(lr_ref, li_ref, ldt_ref, br_ref, bi_ref, lbr_ref, lbi_ref, bbr_ref, bbi_ref):
    lr = lr_ref[...]
    li = li_ref[...]
    dt = jnp.exp(ldt_ref[...])
    mag = jnp.exp(lr * dt)
    ar = mag * jnp.cos(li * dt)
    ai = mag * jnp.sin(li * dt)
    nr = ar - 1.0
    den = lr * lr + li * li
    fr = (nr * lr + ai * li) / den
    fi = (ai * lr - nr * li) / den
    br = br_ref[...]
    bi = bi_ref[...]
    lbr_ref[...] = ar
    lbi_ref[...] = ai
    bbr_ref[...] = fr * br - fi * bi
    bbi_ref[...] = fr * bi + fi * br


def _s5_prep(lam_re, lam_im, log_dt, b_re, b_im):
    rows = 2 * S5_GROUPS
    cols = S5_STATE * S5_GROUP
    exp = lambda a: jnp.repeat(a.reshape(rows, S5_STATE), S5_GROUP, axis=1)
    ldt = jnp.broadcast_to(log_dt.reshape(rows, 1), (rows, cols))
    outs = pl.pallas_call(
        _s5_prep_kernel,
        out_shape=[jax.ShapeDtypeStruct((rows, cols), F32)] * 4,
        name="s5_discretize",
    )(exp(lam_re), exp(lam_im), ldt, b_re.reshape(rows, cols), b_im.reshape(rows, cols))
    lbr, lbi, bbr, bbi = outs
    shp = (2, S5_GROUPS, S5_STATE, S5_GROUP)
    return lbr.reshape(shp)[..., 0], lbi.reshape(shp)[..., 0], bbr.reshape(shp), bbi.reshape(shp)


S5_JT = 8
S5_GPT = LANES // S5_GROUP
S5_HALF = S5_GPT * S5_STATE
S5_ROWS = 512


def _s5_scan_kernel(x_ref, mod_ref, g_ref, wb_ref, wc_ref, lam_ref, h0_ref,
                    y_ref, fin_ref, bu_s, st_s, *, nb, jgroup):
    d = pl.program_id(0)
    c = pl.program_id(1)
    lc = S5_ROWS // nb

    @pl.when(c == 0)
    def _():
        st_s[...] = h0_ref[...]

    rep = S5_ROWS // 8
    sh = _tile_rows(mod_ref[:, 0:D], rep, axis=0)
    sc = _tile_rows(mod_ref[:, D:2 * D], rep, axis=0)
    u = _norm_mod(x_ref[...], g_ref[...], sc, sh).astype(BF16)
    for j in range(S5_JT):
        bu_s[j] = _dot(u[:, j * LANES:(j + 1) * LANES], wb_ref[j])

    for j0 in range(0, S5_JT, jgroup):
        js = list(range(j0, j0 + jgroup))
        lam = [(jnp.broadcast_to(lam_ref[j][:, 0:S5_HALF], (nb, S5_HALF)),
                jnp.broadcast_to(lam_ref[j][:, S5_HALF:], (nb, S5_HALF))) for j in js]

        def body(t, carry):
            l = jnp.where(d == 0, t, lc - 1 - t)
            r0 = pl.multiple_of(l * nb, nb)
            out = []
            for k, j in enumerate(js):
                sr, si = carry[k]
                ar, ai = lam[k]
                bu = bu_s[j, pl.ds(r0, nb), :]
                hr = ar * sr - ai * si + bu[:, 0:S5_HALF]
                hi = ar * si + ai * sr + bu[:, S5_HALF:]
                bu_s[j, pl.ds(r0, nb), 0:S5_HALF] = hr
                bu_s[j, pl.ds(r0, nb), S5_HALF:] = hi
                out.append((hr, hi))
            return tuple(out)

        init = tuple((st_s[j][:, 0:S5_HALF], st_s[j][:, S5_HALF:]) for j in js)
        fin = lax.fori_loop(0, lc, body, init)
        for k, j in enumerate(js):
            st_s[j, :, 0:S5_HALF] = fin[k][0]
            st_s[j, :, S5_HALF:] = fin[k][1]

    for j in range(S5_JT):
        y_ref[:, j * LANES:(j + 1) * LANES] = _dot(bu_s[j].astype(BF16), wc_ref[j])
    fin_ref[...] = st_s[...]


def _s5_scan(xt, nb, mod3, layer, path, g_mix, w_b, w_c, lam_s, h0):
    rows = xt.shape[0]
    nc = rows // S5_ROWS
    chunk = lambda d, c: c + d * (nc - 1 - 2 * c)
    kern = functools.partial(_s5_scan_kernel, nb=nb, jgroup=2 if nb == 8 else 1)
    return pl.pallas_call(
        kern,
        grid=(2, nc),
        in_specs=[pl.BlockSpec((S5_ROWS, D), lambda d, c: (chunk(d, c), 0)),
                  pl.BlockSpec((None, 8, N_MOD * D), lambda d, c: (layer, path, 0)),
                  _row_spec(),
                  pl.BlockSpec((None, S5_JT, LANES, 2 * S5_HALF), lambda d, c: (d, 0, 0, 0)),
                  pl.BlockSpec((None, S5_JT, 2 * S5_HALF, LANES), lambda d, c: (d, 0, 0, 0)),
                  pl.BlockSpec((None, S5_JT, 1, 2 * S5_HALF), lambda d, c: (d, 0, 0, 0)),
                  pl.BlockSpec((None, S5_JT, nb, 2 * S5_HALF), lambda d, c: (d, 0, 0, 0))],
        out_specs=[pl.BlockSpec((None, S5_ROWS, D), lambda d, c: (d, chunk(d, c), 0)),
                   pl.BlockSpec((None, S5_JT, nb, 2 * S5_HALF), lambda d, c: (d, 0, 0, 0))],
        out_shape=[jax.ShapeDtypeStruct((2, rows, D), F32),
                   jax.ShapeDtypeStruct((2, S5_JT, nb, 2 * S5_HALF), F32)],
        scratch_shapes=[pltpu.VMEM((S5_JT, S5_ROWS, 2 * S5_HALF), F32),
                        pltpu.VMEM((S5_JT, nb, 2 * S5_HALF), F32)],
        compiler_params=_cparams(("arbitrary", "arbitrary")),
        name="s5_scan_b%d" % nb,
    )(xt, mod3, g_mix.reshape(1, D), w_b, w_c, lam_s, h0)


S5_GLU_ROWS = 512


def _s5_glu_kernel(x_ref, yf_ref, yb_ref, mod_ref, g_ref, dsk_ref, w_ref, o_ref):
    rep = MM_CHUNK // 8
    sh = _tile_rows(mod_ref[:, 0:D], rep, axis=0)
    sc = _tile_rows(mod_ref[:, D:2 * D], rep, axis=0)
    gate = _tile_rows(mod_ref[:, 2 * D:3 * D], rep, axis=0)
    for c in range(S5_GLU_ROWS // MM_CHUNK):
        r0 = c * MM_CHUNK
        x = x_ref[r0:r0 + MM_CHUNK, :]
        u = _norm_mod(x, g_ref[...], sc, sh)
        y = u * dsk_ref[...] + yf_ref[r0:r0 + MM_CHUNK, :] + yb_ref[r0:r0 + MM_CHUNK, :]
        z = jax.nn.gelu(y).astype(BF16)
        ag = _dot(z, w_ref[...])
        out = ag[:, 0:D] * jax.nn.sigmoid(ag[:, D:2 * D])
        o_ref[r0:r0 + MM_CHUNK, :] = x + gate * out


def _s5_glu(xt, y2, mod3, layer, path, g_mix, d_skip, w_glu):
    rows = xt.shape[0]
    return pl.pallas_call(
        _s5_glu_kernel,
        grid=(rows // S5_GLU_ROWS,),
        in_specs=[pl.BlockSpec((S5_GLU_ROWS, D), lambda i: (i, 0)),
                  pl.BlockSpec((None, S5_GLU_ROWS, D), lambda i: (0, i, 0)),
                  pl.BlockSpec((None, S5_GLU_ROWS, D), lambda i: (1, i, 0)),
                  pl.BlockSpec((None, 8, N_MOD * D), lambda i: (layer, path, 0)),
                  _row_spec(), _row_spec(), _full_spec((D, 2 * D))],
        out_specs=pl.BlockSpec((S5_GLU_ROWS, D), lambda i: (i, 0)),
        out_shape=jax.ShapeDtypeStruct((rows, D), F32),
        compiler_params=_cparams(("arbitrary",)),
        name="s5_glu",
    )(xt, y2, y2, mod3, g_mix.reshape(1, D), d_skip.reshape(1, D), w_glu)


def _s5_weights(lbr, lbi, bbr, bbi, c_re, c_im):
    eye = jnp.eye(S5_GPT, dtype=F32)
    bb = jnp.stack([bbr, bbi]).reshape(2, 2, S5_JT, S5_GPT, S5_STATE, S5_GROUP)
    w_bu = jnp.einsum('rdjgps,gh->djgsrhp', bb, eye).reshape(2, S5_JT, LANES, 2 * S5_HALF)
    w_b = w_bu.astype(BF16)
    cc =jnp.stack([c_re, -c_im]).reshape(2, 2, S5_JT, S5_GPT, S5_GROUP, S5_STATE)
    w_c = jnp.einsum('rdjgsp,gh->djrgphs', cc, eye).reshape(2, S5_JT, 2 * S5_HALF, LANES).astype(BF16)
    lam_s = jnp.concatenate([lbr.reshape(2, S5_JT, S5_HALF), lbi.reshape(2, S5_JT, S5_HALF)], axis=-1)
    return w_b, w_c, lam_s.reshape(2, S5_JT, 1, 2 * S5_HALF)


def _s5_state_in(st_re, st_im):
    def lay(a):
        b = a.shape[0]
        return a.transpose(1, 0, 2, 3).reshape(2, b, S5_JT, S5_HALF).transpose(0, 2, 1, 3)
    return jnp.concatenate([lay(st_re), lay(st_im)], axis=-1)


def _s5_state_out(fin):
    def lay(a):
        b = a.shape[2]
        return a.transpose(2, 0, 1, 3).reshape(b, 2, S5_GROUPS, S5_STATE)
    return lay(fin[..., 0:S5_HALF]), lay(fin[..., S5_HALF:])


QKV_TM = 512


def _qkv_ctx_kernel(x_ref, mod_ref, g_ref, w_ref, q_ref, k_ref, v_ref, kv_ref):
    sh = mod_ref[:, 0:D]
    sc = mod_ref[:, D:2 * D]
    for c in range(QKV_TM // MM_CHUNK):
        r0 = c * MM_CHUNK
        h = _norm_mod(x_ref[r0:r0 + MM_CHUNK, :], g_ref[...], sc, sh).astype(BF16)
        q_ref[r0:r0 + MM_CHUNK, :] = _dot(h, w_ref[:, 0:D]).astype(BF16)
        k_ref[r0:r0 + MM_CHUNK, :] = _dot(h, w_ref[:, D:2 * D]).astype(BF16)
        v_ref[r0:r0 + MM_CHUNK, :] = _dot(h, w_ref[:, 2 * D:3 * D]).astype(BF16)
        kv_ref[r0:r0 + MM_CHUNK, :] = _dot(h, w_ref[:, 3 * D:3 * D + 2 * N_KV_HEADS * HEAD_DIM])


def _qkv_ctx(x, mod4, layer, g_mix, w_all):
    n = w_all.shape[1]
    bf = jax.ShapeDtypeStruct((T_CTX, D), BF16)
    tok = pl.BlockSpec((QKV_TM, D), lambda i: (i, 0))
    return pl.pallas_call(
        _qkv_ctx_kernel,
        grid=(T_CTX // QKV_TM,),
        in_specs=[tok, _mod_spec(layer, QKV_TM), _row_spec(), _full_spec((D, n))],
        out_specs=[tok, tok, tok, pl.BlockSpec((QKV_TM, 512), lambda i: (i, 0))],
        out_shape=[bf, bf, bf, jax.ShapeDtypeStruct((T_CTX, 512), F32)],
        compiler_params=_cparams(("arbitrary",)),
        name="qkv_ctx",
    )(x, mod4, g_mix.reshape(1, D), w_all)


def _qkv_lat_kernel(x_ref, mod_ref, g_ref, w_ref, cos_ref, sin_ref, q_ref, k_ref, v_ref):
    sh = mod_ref[:, 0:D]
    sc = mod_ref[:, D:2 * D]
    for c in range(QKV_TM // MM_CHUNK):
        r0 = c * MM_CHUNK
        h = _norm_mod(x_ref[r0:r0 + MM_CHUNK, :], g_ref[...], sc, sh).astype(BF16)
        cos = cos_ref[r0:r0 + MM_CHUNK, :]
        sin = sin_ref[r0:r0 + MM_CHUNK, :]
        q_ref[r0:r0 + MM_CHUNK, :] = (_dot(h, w_ref[:, 0:D]) * cos + _dot(h, w_ref[:, D:2 * D]) * sin).astype(BF16)
        k_ref[r0:r0 + MM_CHUNK, :] = (_dot(h, w_ref[:, 2 * D:3 * D]) * cos
                                      + _dot(h, w_ref[:, 3 * D:4 * D]) * sin).astype(BF16)
        v_ref[r0:r0 + MM_CHUNK, :] = _dot(h, w_ref[:, 4 * D:5 * D]).astype(BF16)


def _qkv_lat(x, mod4, layer, g_mix, w_all, cos_t, sin_t):
    n = w_all.shape[1]
    nct = T_CTX // QKV_TM
    lpb = DEC_SEQ // QKV_TM
    bf = jax.ShapeDtypeStruct((T_LAT, D), BF16)
    tok_out = pl.BlockSpec((QKV_TM, D), lambda i: (i, 0))
    rope = pl.BlockSpec((QKV_TM, D), lambda i: (i % lpb, 0))
    return pl.pallas_call(
        _qkv_lat_kernel,
        grid=(T_LAT // QKV_TM,),
        in_specs=[pl.BlockSpec((QKV_TM, D), lambda i: (i + nct, 0)),
                  pl.BlockSpec((None, None, 1, N_MOD * D), lambda i: (layer, 8 + i // lpb, 0, 0)),
                  _row_spec(), _full_spec((D, n)), rope, rope],
        out_specs=[tok_out, tok_out, tok_out],
        out_shape=[bf, bf, bf],
        compiler_params=_cparams(("arbitrary",)),
        name="qkv_lat",
    )(x, mod4, g_mix.reshape(1, D), w_all, cos_t, sin_t)


KVW = Q_PER_KV * HEAD_DIM


def _head_masks(rows):
    lane = lax.broadcasted_iota(I32, (rows, KVW), 1)
    return [jnp.logical_and(lane >= g * HEAD_DIM, lane < (g + 1) * HEAD_DIM) for g in range(Q_PER_KV)]


def _attn_ctx_kernel(sink_ref, q_ref, k_ref, v_ref, o_ref):
    scale = HEAD_DIM ** -0.5
    masks = _head_masks(SEQ)
    for kv in range(N_KV_HEADS):
        c0 = kv * KVW
        q = q_ref[:, c0:c0 + KVW]
        k = k_ref[:, c0:c0 + KVW]
        v = v_ref[:, c0:c0 + KVW]
        acc = jnp.zeros((SEQ, KVW), F32)
        for g in range(Q_PER_KV):
            sink = sink_ref[kv * Q_PER_KV + g]
            qg = jnp.where(masks[g], q, jnp.zeros_like(q))
            s = _dot_nt(qg, k) * scale
            m = jnp.maximum(jnp.max(s, axis=-1, keepdims=True), sink)
            e = jnp.exp(s - m)
            den = jnp.sum(e, axis=-1, keepdims=True) + jnp.exp(sink - m)
            og = _dot(e.astype(BF16), v) / den
            acc = jnp.where(masks[g], og, acc)
        o_ref[:, c0:c0 + KVW] = acc.astype(BF16)


def _attn_ctx(sink, q, k, v):
    tok = pl.BlockSpec((SEQ, D), lambda b, *_: (b, 0))
    return pl.pallas_call(
        _attn_ctx_kernel,
        grid_spec=pltpu.PrefetchScalarGridSpec(
            num_scalar_prefetch=1, grid=(BATCH,),
            in_specs=[tok, tok, tok], out_specs=tok),
        out_shape=jax.ShapeDtypeStruct((T_CTX, D), BF16),
        compiler_params=_cparams(("arbitrary",)),
        name="attn_ctx",
    )(sink, q, k, v)


ATT_TQ = 128
ATT_SPAN = ATT_TQ + 2 * WINDOW


def _attn_lat_kernel(sink_ref, q_ref, k_ref, v_ref, ck_ref, cv_ref, o_ref):
    qb = pl.program_id(1)
    scale = HEAD_DIM ** -0.5
    w0 = pl.multiple_of(jnp.clip(qb * ATT_TQ - WINDOW, 0, DEC_SEQ - ATT_SPAN), ATT_TQ)
    rows = Q_PER_KV * ATT_TQ
    ridx = lax.broadcasted_iota(I32, (rows, ATT_SPAN), 0)
    qpos = qb * ATT_TQ + (ridx & (ATT_TQ - 1))
    kpos = w0 + lax.broadcasted_iota(I32, (rows, ATT_SPAN), 1)
    valid = jnp.abs(qpos - kpos) <= WINDOW
    rcol = lax.broadcasted_iota(I32, (rows, 1), 0)
    masks = _head_masks(ATT_TQ)
    for kv in range(N_KV_HEADS):
        c0 = kv * KVW
        q = q_ref[:, c0:c0 + KVW]
        qs = jnp.concatenate([jnp.where(masks[g], q, jnp.zeros_like(q)) for g in range(Q_PER_KV)], axis=0)
        sink = jnp.zeros((rows, 1), F32)
        for g in range(Q_PER_KV):
            sink = jnp.where(rcol >= g * ATT_TQ, sink_ref[kv * Q_PER_KV + g], sink)
        s_ctx = _dot_nt(qs, ck_ref[:, c0:c0 + KVW]) * scale
        s_win = _dot_nt(qs, k_ref[pl.ds(w0, ATT_SPAN), c0:c0 + KVW]) * scale
        s_win = jnp.where(valid, s_win, NEG_INF)
        m = jnp.maximum(jnp.maximum(jnp.max(s_ctx, axis=-1, keepdims=True),
                                    jnp.max(s_win, axis=-1, keepdims=True)), sink)
        e_ctx = jnp.exp(s_ctx - m)
        e_win = jnp.exp(s_win - m)
        den = (jnp.exp(sink - m) + jnp.sum(e_ctx, axis=-1, keepdims=True)
               + jnp.sum(e_win, axis=-1, keepdims=True))
        o = (_dot(e_ctx.astype(BF16), cv_ref[:, c0:c0 + KVW])
             + _dot(e_win.astype(BF16), v_ref[pl.ds(w0, ATT_SPAN), c0:c0 + KVW])) / den
        acc = jnp.zeros((ATT_TQ, KVW), F32)
        for g in range(Q_PER_KV):
            acc = jnp.where(masks[g], o[g * ATT_TQ:(g + 1) * ATT_TQ, :], acc)
        o_ref[:, c0:c0 + KVW] = acc.astype(BF16)


def _attn_lat(sink, q, k, v, ck, cv):
    nqb = DEC_SEQ // ATT_TQ
    qspec = pl.BlockSpec((ATT_TQ, D), lambda b, i, *_: (b * nqb + i, 0))
    seq = pl.BlockSpec((DEC_SEQ, D), lambda b, i, *_: (b, 0))
    ctx = pl.BlockSpec((PAST_LEN, D), lambda b, i, *_: (b, 0))
    return pl.pallas_call(
        _attn_lat_kernel,
        grid_spec=pltpu.PrefetchScalarGridSpec(
            num_scalar_prefetch=1, grid=(DEC_BATCH, nqb),
            in_specs=[qspec, seq, seq, ctx, ctx], out_specs=qspec),
        out_shape=jax.ShapeDtypeStruct((T_LAT, D), BF16),
        compiler_params=_cparams(("arbitrary", "arbitrary")),
        name="attn_lat",
    )(sink, q, k, v, ck, cv)


def _resproj_kernel(x_ref, a_ref, mod_ref, w_ref, o_ref):
    gate = mod_ref[:, 2 * D:3 * D]
    o_ref[...] = x_ref[...] + gate * _dot(a_ref[...], w_ref[...])


def _resproj(x, a, mod4, layer, w):
    tok = pl.BlockSpec((TM, D), lambda i: (i, 0))
    return pl.pallas_call(
        _resproj_kernel,
        grid=(T_ALL // TM,),
        in_specs=[tok, tok, _mod_spec(layer, TM), _full_spec((D, D))],
        out_specs=tok,
        out_shape=jax.ShapeDtypeStruct((T_ALL, D), F32),
        compiler_params=_cparams(("arbitrary",)),
        name="attn_out_proj",
    )(x, a, mod4, w)


def _rope_tables():
    rows = DEC_SEQ // GRID_W
    row = jnp.repeat(jnp.arange(rows), GRID_W).astype(F32)
    col = jnp.tile(jnp.arange(GRID_W), rows).astype(F32)
    inv = ROPE_THETA ** (-jnp.arange(ROPE_FREQS, dtype=F32) / ROPE_FREQS)
    ang = jnp.concatenate([row[:, None] * inv, col[:, None] * inv], axis=-1)
    cos = jnp.cos(ang)
    sin = jnp.sin(ang)
    cos_h = jnp.concatenate([cos, cos], axis=-1)
    sin_h = jnp.concatenate([sin, sin], axis=-1)
    return jnp.tile(cos_h, (1, N_HEADS)), jnp.tile(sin_h, (1, N_HEADS))


def _rot_half_cols(w):
    k = w.shape[0]
    w4 = w.reshape(k, -1, 2, HEAD_DIM // 2)
    return jnp.stack([-w4[:, :, 1], w4[:, :, 0]], axis=2).reshape(k, -1)


def _expand_kv_cols(w):
    k = w.shape[0]
    w3 = w.reshape(k, N_KV_HEADS, 1, HEAD_DIM)
    return jnp.broadcast_to(w3, (k, N_KV_HEADS, Q_PER_KV, HEAD_DIM)).reshape(k, N_HEADS * HEAD_DIM)


FG = D // FNET_GROUPS


def _fnet_kernel(x_ref, mod_ref, g_ref, cs_ref, fl_ref, w_ref, o_ref, h_s, ab_s):
    sh = mod_ref[:, 0:D]
    sc = mod_ref[:, D:2 * D]
    gate = mod_ref[:, 2 * D:3 * D]
    for c in range(TM // ROW_CHUNK):
        r0 = c * ROW_CHUNK
        h_s[r0:r0 + ROW_CHUNK, :] = _norm_mod(x_ref[r0:r0 + ROW_CHUNK, :], g_ref[...], sc, sh).astype(BF16)
    for g in range(FNET_GROUPS):
        ab = _dot(h_s[:, g * FG:(g + 1) * FG], cs_ref[...])
        ab_s[0:TM, g * FG:(g + 1) * FG] = ab[:, 0:FG].astype(BF16)
        ab_s[TM:2 * TM, g * FG:(g + 1) * FG] = ab[:, FG:2 * FG].astype(BF16)
    for c in range(TM // MM_CHUNK):
        r0 = c * MM_CHUNK
        f = _dot(fl_ref[r0:r0 + MM_CHUNK, :], ab_s[...])
        y = _dot(f.astype(BF16), w_ref[...])
        o_ref[r0:r0 + MM_CHUNK, :] = x_ref[r0:r0 + MM_CHUNK, :] + gate * y


def _fnet_mixer(x, mod4, layer, g_mix, cs, fl, w_out):
    nct = T_CTX // TM
    tok = pl.BlockSpec((TM, D), lambda i: (i, 0))
    return pl.pallas_call(
        _fnet_kernel,
        grid=(T_ALL // TM,),
        in_specs=[tok, _mod_spec(layer, TM), _row_spec(), _full_spec((FG, 2 * FG)),
                  pl.BlockSpec((None, TM, 2 * TM), lambda i: (jnp.where(i < nct, 0, 1), 0, 0)),
                  _full_spec((D, D))],
        out_specs=tok,
        out_shape=jax.ShapeDtypeStruct((T_ALL, D), F32),
        scratch_shapes=[pltpu.VMEM((TM, D), BF16), pltpu.VMEM((2 * TM, D), BF16)],
        compiler_params=_cparams(("arbitrary",)),
        name="fnet_mixer",
    )(x, mod4, g_mix.reshape(1, D), cs, fl, w_out)


def _dft_cos_sin(n):
    k = np.arange(n)
    ang = 2.0 * np.pi * ((k[:, None] * k[None, :]) % n) / n
    return np.cos(ang), np.sin(ang)


def _fnet_tables():
    cc, sc = _dft_cos_sin(FG)
    cs = np.concatenate([cc, sc], axis=1) / math.sqrt(FG)
    mats = []
    for seq in (SEQ, DEC_SEQ):
        cl, sl = _dft_cos_sin(seq)
        reps = TM // seq
        eye = np.eye(reps)
        mats.append(np.concatenate([np.kron(eye, cl), -np.kron(eye, sl)], axis=1) / math.sqrt(seq))
    return jnp.asarray(cs, F32).astype(BF16), jnp.asarray(np.stack(mats), F32).astype(BF16)


ROUTER_TM = 512
TOK_SUB = D // LANES


def _store_token_tiles(ref, r0, val):
    rows = val.shape[0]
    for c in range(TOK_SUB):
        ref[pl.ds(r0 * TOK_SUB + c, rows, stride=TOK_SUB), :] = val[:, c * LANES:(c + 1) * LANES]


def _load_token_tiles(ref, r0, rows, c, lead=None):
    idx = pl.ds(r0 * TOK_SUB + c, rows, stride=TOK_SUB)
    return ref[idx, :] if lead is None else ref[lead, idx, :]


def _router_kernel(x_ref, mod_ref, g_ref, wr_ref, tri_ref, h_ref, info_ref, gates_ref, cnt_ref, cnt_s):
    sh = mod_ref[:, 3 * D:4 * D]
    sc = mod_ref[:, 4 * D:5 * D]
    w_hi = wr_ref[0]
    w_lo = wr_ref[1]

    @pl.when(pl.program_id(0) == 0)
    def _():
        cnt_s[...] = jnp.zeros_like(cnt_s)

    for c in range(ROUTER_TM // ROW_CHUNK):
        r0 = c * ROW_CHUNK
        h = _norm_mod(x_ref[r0:r0 + ROW_CHUNK, :], g_ref[...], sc, sh)
        _store_token_tiles(h_ref, r0, h)
        h_hi, h_lo = _split_bf16(h)
        logits = _dot(h_hi, w_hi) + _dot(h_lo, w_hi) + _dot(h_hi, w_lo)
        lane = lax.broadcasted_iota(I32, logits.shape, 1)
        lg = jnp.where(lane < N_EXPERTS, logits, -jnp.inf)
        m1 = jnp.max(lg, axis=-1, keepdims=True)
        i1 = jnp.min(jnp.where(lg == m1, lane, LANES), axis=-1, keepdims=True)
        lg2 = jnp.where(lane == i1, -jnp.inf, lg)
        m2 = jnp.max(lg2, axis=-1, keepdims=True)
        i2 = jnp.min(jnp.where(lg2 == m2, lane, LANES), axis=-1, keepdims=True)
        e2 = jnp.exp(m2 - m1)
        den = 1.0 + e2
        gates_ref[r0:r0 + ROW_CHUNK, :] = jnp.where(lane == 0, 1.0 / den, jnp.where(lane == 1, e2 / den, 0.0))
        onehot = jnp.where(jnp.logical_or(lane == i1, lane == i2), 1.0, 0.0)
        cnt = cnt_s[0:1, :]
        rank = _dot(tri_ref[...], onehot.astype(BF16)) + cnt
        cnt_s[0:1, :] = rank[ROW_CHUNK - 1:ROW_CHUNK, :] + onehot[ROW_CHUNK - 1:ROW_CHUNK, :]
        r1 = jnp.sum(jnp.where(lane == i1, rank, 0.0), axis=-1, keepdims=True)
        r2 = jnp.sum(jnp.where(lane == i2, rank, 0.0), axis=-1, keepdims=True)
        info = jnp.where(lane == 0, i1.astype(F32), jnp.where(lane == 1, i2.astype(F32),
                         jnp.where(lane == 2, r1, jnp.where(lane == 3, r2, 0.0))))
        info_ref[r0:r0 + ROW_CHUNK, :] = info.astype(I32)
    cnt_ref[...] = cnt_s[...]


def _router(x, mod4, layer, g_ffn, w_router):
    wr = jnp.zeros((D, LANES), F32).at[:, 0:N_EXPERTS].set(w_router)
    wr_hi = wr.astype(BF16)
    wr_lo = (wr - wr_hi.astype(F32)).astype(BF16)
    tok = pl.BlockSpec((ROUTER_TM, D), lambda i: (i, 0))
    nar = pl.BlockSpec((ROUTER_TM, LANES), lambda i: (i, 0))
    tri = jnp.tril(jnp.ones((ROW_CHUNK, ROW_CHUNK), F32), -1).astype(BF16)
    return pl.pallas_call(
        _router_kernel,
        grid=(T_ALL // ROUTER_TM,),
        in_specs=[tok, _mod_spec(layer, ROUTER_TM), _row_spec(), _full_spec((2, D, LANES)),
                  _full_spec((ROW_CHUNK, ROW_CHUNK))],
        out_specs=[pl.BlockSpec((ROUTER_TM * TOK_SUB, LANES), lambda i: (i, 0)), nar, nar,
                   pl.BlockSpec((8, LANES), lambda i: (0, 0))],
        out_shape=[jax.ShapeDtypeStruct((T_ALL * TOK_SUB, LANES), F32),
                   jax.ShapeDtypeStruct((T_ALL, LANES), I32),
                   jax.ShapeDtypeStruct((T_ALL, LANES), F32),
                   jax.ShapeDtypeStruct((8, LANES), F32)],
        scratch_shapes=[pltpu.VMEM((8, LANES), F32)],
        compiler_params=_cparams(("arbitrary",)),
        name="moe_router",
    )(x, mod4, g_ffn.reshape(1, D), jnp.stack([wr_hi, wr_lo]), tri)


def _route_plan(info, cnt):
    counts = cnt[0, 0:N_EXPERTS].astype(I32)
    padded = ((counts + TME - 1) // TME) * TME
    gend = jnp.cumsum(padded)
    gstart = gend - padded
    eid = jnp.arange(N_EXPERTS, dtype=I32)[None, :]
    pos1 = (jnp.sum(jnp.where(info[:, 0:1] == eid, gstart[None, :], 0), axis=1) + info[:, 2]).astype(I32)
    pos2 = (jnp.sum(jnp.where(info[:, 1:2] == eid, gstart[None, :], 0), axis=1) + info[:, 3]).astype(I32)
    n_valid = gend[-1] // TME
    tile_start = jnp.arange(N_EXPERT_TILES, dtype=I32) * TME
    eff_start = jnp.minimum(tile_start, jnp.maximum(n_valid - 1, 0) * TME)
    tile_expert = jnp.minimum(jnp.sum((eff_start[:, None] >= gend[None, :]).astype(I32), axis=1),
                              N_EXPERTS - 1).astype(I32)
    return pos1, pos2, tile_expert, n_valid.astype(I32).reshape(1)


DISP_TM = 512
DMA_UNROLL = 8


def _dispatch_kernel(p1_ref, p2_ref, h_ref, xs_zero, xs_hbm, sem):
    del xs_zero
    i = pl.program_id(0)

    def body(g, carry):
        for k in range(DMA_UNROLL):
            r = g * DMA_UNROLL + k
            t = i * DISP_TM + r
            tile = h_ref.at[pl.ds(pl.multiple_of(r * TOK_SUB, TOK_SUB), TOK_SUB), :]
            for j, p_ref in enumerate((p1_ref, p2_ref)):
                dst = xs_hbm.at[pl.ds(pl.multiple_of(p_ref[t] * TOK_SUB, TOK_SUB), TOK_SUB), :]
                pltpu.make_async_copy(tile, dst, sem.at[j]).start(priority=j)
        return carry
    lax.fori_loop(0, DISP_TM // DMA_UNROLL, body, 0)
    for j in range(2):
        pltpu.make_async_copy(h_ref, xs_hbm.at[pl.ds(0, DISP_TM * TOK_SUB), :], sem.at[j]).wait()


def _dispatch(pos1, pos2, h):
    xs_zero = jnp.zeros((P_MAX * TOK_SUB, LANES), F32)
    return pl.pallas_call(
        _dispatch_kernel,
        grid_spec=pltpu.PrefetchScalarGridSpec(
            num_scalar_prefetch=2, grid=(T_ALL // DISP_TM,),
            in_specs=[pl.BlockSpec((DISP_TM * TOK_SUB, LANES), lambda i, *_: (i, 0)),
                      pl.BlockSpec(memory_space=pl.ANY)],
            out_specs=pl.BlockSpec(memory_space=pl.ANY),
            scratch_shapes=[pltpu.SemaphoreType.DMA((2,))]),
        out_shape=jax.ShapeDtypeStruct((P_MAX * TOK_SUB, LANES), F32),
        input_output_aliases={3: 0},
        compiler_params=_cparams(("arbitrary",)),
        name="moe_dispatch",
    )(pos1, pos2, h, xs_zero)


EXP_TF = 512
EXP_NF = D_FF_EXPERT // EXP_TF


def _expert_kernel(te_ref, nv_ref, xs_ref, wg_ref, wu_ref, wd_ref, o_ref, xb_s, acc_s):
    m = pl.program_id(0)
    f = pl.program_id(1)
    nf = pl.num_programs(1)
    n_valid = nv_ref[0]

    @pl.when(m < n_valid)
    def _():
        @pl.when(f == 0)
        def _():
            @pl.when(m == 0)
            def _():
                acc_s[...] = jnp.zeros_like(acc_s)

            for c in range(TOK_SUB):
                xb_s[:, c * LANES:(c + 1) * LANES] = _load_token_tiles(xs_ref, 0, TME, c).astype(BF16)

        xb = xb_s[...]
        gg = _dot(xb, wg_ref[...].astype(BF16))
        uu = _dot(xb, wu_ref[...].astype(BF16))
        a = (gg * jax.nn.sigmoid(gg) * uu).astype(BF16)
        contrib = _dot(a, wd_ref[...].astype(BF16))

        acc_s[...] = jnp.where(f == 0, 0.0, acc_s[...]) + contrib

        @pl.when(f == nf - 1)
        def _():
            _store_token_tiles(o_ref, 0, acc_s[...])

    @pl.when(jnp.logical_and(m >= n_valid, f == 0))
    def _():
        o_ref[...] = jnp.zeros_like(o_ref)


def _expert_ffn(tile_expert, n_valid, xs, w_gu, w_down, li):
    def feff(m, f, nv):
        return jnp.where(m < nv[0], f, EXP_NF - 1)

    def meff(m, nv):
        return jnp.minimum(m, nv[0] - 1)
    return pl.pallas_call(
        _expert_kernel,
        grid_spec=pltpu.PrefetchScalarGridSpec(
            num_scalar_prefetch=2, grid=(N_EXPERT_TILES, EXP_NF),
            in_specs=[pl.BlockSpec((TME * TOK_SUB, LANES), lambda m, f, te, nv: (meff(m, nv), 0)),
                      pl.BlockSpec((None, None, D, EXP_TF),
                                   lambda m, f, te, nv: (li, te[m], 0, feff(m, f, nv))),
                      pl.BlockSpec((None, None, D, EXP_TF),
                                   lambda m, f, te, nv: (li, te[m], 0, EXP_NF + feff(m, f, nv))),
                      pl.BlockSpec((None, None, EXP_TF, D),
                                   lambda m, f, te, nv: (li, te[m], feff(m, f, nv), 0))],
            out_specs=pl.BlockSpec((TME * TOK_SUB, LANES), lambda m, f, te, nv: (m, 0)),
            scratch_shapes=[pltpu.VMEM((TME, D), BF16), pltpu.VMEM((TME, D), F32)]),
        out_shape=jax.ShapeDtypeStruct((P_MAX * TOK_SUB, LANES), F32),
        compiler_params=_cparams(("arbitrary", "arbitrary")),
        name="moe_expert_swiglu",
    )(tile_expert, n_valid, xs, w_gu, w_gu, w_down)


COMB_TM = 512


def _combine_kernel(p1_ref, p2_ref, x_ref, gates_ref, mod_ref, gf_ref, y_hbm, o_ref, a_s, b_s, sem, *, final):
    i = pl.program_id(0)

    def tile_copy(p, r, dst, s):
        return pltpu.make_async_copy(y_hbm.at[pl.ds(pl.multiple_of(p * TOK_SUB, TOK_SUB), TOK_SUB), :],
                                     dst.at[pl.ds(pl.multiple_of(r * TOK_SUB, TOK_SUB), TOK_SUB), :], s)

    def body(g, carry):
        for k in range(DMA_UNROLL):
            r = g * DMA_UNROLL + k
            t = i * COMB_TM + r
            tile_copy(p1_ref[t], r, a_s, sem.at[0]).start(priority=0)
            tile_copy(p2_ref[t], r, b_s, sem.at[1]).start(priority=1)
        return carry
    lax.fori_loop(0, COMB_TM // DMA_UNROLL, body, 0)
    pltpu.make_async_copy(y_hbm.at[pl.ds(0, COMB_TM * TOK_SUB), :], a_s, sem.at[0]).wait()
    pltpu.make_async_copy(y_hbm.at[pl.ds(0, COMB_TM * TOK_SUB), :], b_s, sem.at[1]).wait()

    for c in range(COMB_TM // ROW_CHUNK):
        r0 = c * ROW_CHUNK
        w1 = gates_ref[r0:r0 + ROW_CHUNK, 0:1]
        w2 = gates_ref[r0:r0 + ROW_CHUNK, 1:2]
        for j in range(TOK_SUB):
            cols = slice(j * LANES, (j + 1) * LANES)
            y = (w1 * _load_token_tiles(a_s, r0, ROW_CHUNK, j) + w2 * _load_token_tiles(b_s, r0, ROW_CHUNK, j))
            o_ref[r0:r0 + ROW_CHUNK, cols] = x_ref[r0:r0 + ROW_CHUNK, cols] + mod_ref[:, 5 * D + j * LANES:5 * D + (j + 1) * LANES] * y
        if final:
            xn = o_ref[r0:r0 + ROW_CHUNK, :]
            ms = jnp.mean(xn * xn, axis=-1, keepdims=True)
            o_ref[r0:r0 + ROW_CHUNK, :] = xn * lax.rsqrt(ms + EPS) * gf_ref[...]


def _combine(pos1, pos2, x, gates, mod4, layer, g_final, y, final):
    tok = pl.BlockSpec((COMB_TM, D), lambda i, *_: (i, 0))
    return pl.pallas_call(
        functools.partial(_combine_kernel, final=final),
        grid_spec=pltpu.PrefetchScalarGridSpec(
            num_scalar_prefetch=2, grid=(T_ALL // COMB_TM,),
            in_specs=[tok, pl.BlockSpec((COMB_TM, LANES), lambda i, *_: (i, 0)),
                      _mod_spec(layer, COMB_TM), _row_spec(),
                      pl.BlockSpec(memory_space=pl.ANY)],
            out_specs=tok,
            scratch_shapes=[pltpu.VMEM((COMB_TM * TOK_SUB, LANES), F32), pltpu.VMEM((COMB_TM * TOK_SUB, LANES), F32),
                            pltpu.SemaphoreType.DMA((2,))]),
        out_shape=jax.ShapeDtypeStruct((T_ALL, D), F32),
        compiler_params=_cparams(("arbitrary",)),
        name="moe_combine_final" if final else "moe_combine",
    )(pos1, pos2, x, gates, mod4, g_final.reshape(1, D), y)


def _moe_layer(x, mod4, layer, g_ffn, w_router, w_gu, w_down, li, g_final, final):
    h, info, gates, cnt = _router(x, mod4, layer, g_ffn, w_router)
    pos1, pos2, tile_expert, n_valid = _route_plan(info, cnt)
    xs = _dispatch(pos1, pos2, h)
    ys = _expert_ffn(tile_expert, n_valid, xs, w_gu, w_down, li)
    return _combine(pos1, pos2, x, gates, mod4, layer, g_final, ys, final)


def kernel(x_prompt, x_sample, cache_k, cache_v, state_ssm_re, state_ssm_im, c, c_ctx, w_ada, b_ada, g_mix, g_ffn, g_final, conv_w_in, conv_w, conv_w_out, s5_lambda_re, s5_lambda_im, s5_b_re, s5_b_im, s5_c_re, s5_c_im, s5_log_dt, s5_d, s5_w_glu, attn_w_q, attn_w_kv, attn_w_o, attn_sink, fnet_w_out, ffn_w_gu, ffn_w_down, moe_w_router, moe_w_gu, moe_w_down):
    x = jnp.concatenate([x_prompt.reshape(T_CTX, D), x_sample.reshape(T_LAT, D)], axis=0)
    cond16 = jnp.concatenate([jnp.broadcast_to(c_ctx[None, :], (8, D)), c], axis=0)
    mod3 = _mod_all(cond16, w_ada, b_ada)
    mod4 = mod3.reshape(DEPTH, 16, 1, N_MOD * D)

    x = _conv_mixer(x, mod4, 0, g_mix[0], conv_w_in[0].astype(BF16), conv_w[0], conv_w_out[0].astype(BF16))
    x = _dense_ffn(x, mod4, 0, g_ffn[0], ffn_w_gu[0].astype(BF16), ffn_w_down[0].astype(BF16))

    lbr, lbi, bbr, bbi = _s5_prep(s5_lambda_re[0], s5_lambda_im[0], s5_log_dt[0], s5_b_re[0], s5_b_im[0])
    w_b, w_c, lam_s = _s5_weights(lbr, lbi, bbr, bbi, s5_c_re[0], s5_c_im[0])
    w_glu = s5_w_glu[0].astype(BF16)
    xc = x[0:T_CTX].reshape(BATCH, SEQ, D).transpose(1, 0, 2).reshape(T_CTX, D)
    xl = x[T_CTX:].reshape(DEC_BATCH, DEC_SEQ, D).transpose(1, 0, 2).reshape(T_LAT, D)
    h0_ctx = jnp.zeros((2, S5_JT, BATCH, 2 * S5_HALF), F32)
    h0_lat = _s5_state_in(state_ssm_re[:, 0], state_ssm_im[:, 0])
    yc, fin_c = _s5_scan(xc, BATCH, mod3, 1, 0, g_mix[1], w_b, w_c, lam_s, h0_ctx)
    yl, _ = _s5_scan(xl, DEC_BATCH, mod3, 1, 1, g_mix[1], w_b, w_c, lam_s, h0_lat)
    xc = _s5_glu(xc, yc, mod3, 1, 0, g_mix[1], s5_d[0], w_glu)
    xl = _s5_glu(xl, yl, mod3, 1, 1, g_mix[1], s5_d[0], w_glu)
    x = jnp.concatenate([xc.reshape(SEQ, BATCH, D).transpose(1, 0, 2).reshape(T_CTX, D),
                         xl.reshape(DEC_SEQ, DEC_BATCH, D).transpose(1, 0, 2).reshape(T_LAT, D)], axis=0)
    new_re, new_im = _s5_state_out(fin_c)
    x = _moe_layer(x, mod4, 1, g_ffn[1], moe_w_router[0], moe_w_gu, moe_w_down, 0, g_final, False)

    wq = attn_w_q[0]
    wk = _expand_kv_cols(attn_w_kv[0][:, 0:N_KV_HEADS * HEAD_DIM])
    wv = _expand_kv_cols(attn_w_kv[0][:, N_KV_HEADS * HEAD_DIM:])
    w_ctx = jnp.concatenate([wq, wk, wv, attn_w_kv[0]], axis=1).astype(BF16)
    w_lat = jnp.concatenate([wq, _rot_half_cols(wq), wk, _rot_half_cols(wk), wv], axis=1).astype(BF16)
    cos_t, sin_t = _rope_tables()
    q_c, k_c, v_c, kv_c = _qkv_ctx(x, mod4, 2, g_mix[2], w_ctx)
    q_l, k_l, v_l = _qkv_lat(x, mod4, 2, g_mix[2], w_lat, cos_t, sin_t)
    sink = attn_sink[0]
    o_c = _attn_ctx(sink, q_c, k_c, v_c)
    expand = lambda a: jnp.broadcast_to(
        a.reshape(DEC_BATCH * PAST_LEN, N_KV_HEADS, 1, HEAD_DIM),
        (DEC_BATCH * PAST_LEN, N_KV_HEADS, Q_PER_KV, HEAD_DIM)).reshape(DEC_BATCH * PAST_LEN, D).astype(BF16)
    o_l = _attn_lat(sink, q_l, k_l, v_l, expand(cache_k[:, 0]), expand(cache_v[:, 0]))
    x = _resproj(x, jnp.concatenate([o_c, o_l], axis=0), mod4, 2, attn_w_o[0].astype(BF16))
    kvw = N_KV_HEADS * HEAD_DIM
    new_k = kv_c[:, 0:kvw].reshape(BATCH, 1, SEQ, N_KV_HEADS, HEAD_DIM)
    new_v = kv_c[:, kvw:].reshape(BATCH, 1, SEQ, N_KV_HEADS, HEAD_DIM)
    x = _dense_ffn(x, mod4, 2, g_ffn[2], ffn_w_gu[1].astype(BF16), ffn_w_down[1].astype(BF16))

    cs, fl = _fnet_tables()
    x = _fnet_mixer(x, mod4, 3, g_mix[3], cs, fl, fnet_w_out[0].astype(BF16))
    x = _moe_layer(x, mod4, 3, g_ffn[3], moe_w_router[1], moe_w_gu, moe_w_down, 1, g_final, True)

    y_prompt = x[0:T_CTX].reshape(BATCH, SEQ, D)
    y_sample = x[T_CTX:].reshape(DEC_BATCH, DEC_SEQ, D)
    return (y_prompt, y_sample, new_k, new_v, new_re[:, None], new_im[:, None])
```

```python
import functools
import math

import numpy as np
import jax
import jax.numpy as jnp
from jax import lax
from jax.experimental import pallas as pl
from jax.experimental.pallas import tpu as pltpu

F32 = jnp.float32
BF16 = jnp.bfloat16
I32 = jnp.int32

D = 1024
BATCH = 16
SEQ = 256
DEPTH = 4
DEC_BATCH = 8
DEC_SEQ = 1024
PAST_LEN = 256
GRID_W = 64
EPS = 1e-6
N_MOD = 6
S5_GROUP = 16
S5_GROUPS = D // S5_GROUP
S5_STATE = 64
HEAD_DIM = 64
N_HEADS = D // HEAD_DIM
N_KV_HEADS = 4
Q_PER_KV = N_HEADS // N_KV_HEADS
WINDOW = 128
ROPE_THETA = 10000.0
ROPE_FREQS = HEAD_DIM // 4
FNET_GROUPS = 4
D_FF = 2816
N_EXPERTS = 8
TOP_K = 2
D_FF_EXPERT = 3584
NEG_INF = -1e30

T_CTX = BATCH * SEQ
T_LAT = DEC_BATCH * DEC_SEQ
T_ALL = T_CTX + T_LAT

VMEM_LIMIT_V7X = 56 * 1024 * 1024
LANES = 128

TM = 1024
ROW_CHUNK = 256
MM_CHUNK = 512
TME = 1024
N_EXPERT_TILES = (T_ALL * TOP_K) // TME + N_EXPERTS
P_MAX = N_EXPERT_TILES * TME


def _cparams(sem):
    return pltpu.CompilerParams(dimension_semantics=sem, vmem_limit_bytes=VMEM_LIMIT_V7X)


def _dot(a, b):
    return jnp.dot(a, b, preferred_element_type=F32)


def _dot_nt(a, b):
    return lax.dot_general(a, b, (((1,), (1,)), ((), ())), preferred_element_type=F32)


def _split_bf16(a):
    hi = a.astype(BF16)
    lo = (a - hi.astype(F32)).astype(BF16)
    return hi, lo


def _tile_rows(a, reps, axis=0):
    assert axis == 0
    return jnp.concatenate([a] * reps, axis=0)


def _norm_mod(x, g, sc, sh):
    ms = jnp.mean(x * x, axis=-1, keepdims=True)
    y = x * lax.rsqrt(ms + EPS) * g
    return y * (1.0 + sc) + sh


def _mod_row(i, tm):
    nct = T_CTX // tm
    lpb = DEC_SEQ // tm
    return jnp.where(i < nct, 0, 8 + (i - nct) // lpb)


def _mod_spec(layer, tm):
    return pl.BlockSpec((None, None, 1, N_MOD * D), lambda i, *_: (layer, _mod_row(i, tm), 0, 0))


def _row_spec():
    return pl.BlockSpec((1, D), lambda *_: (0, 0))


def _full_spec(shape):
    nd = len(shape)
    return pl.BlockSpec(shape, lambda *_: (0,) * nd, pipeline_mode=pl.Buffered(1))


def _mod_kernel(c_ref, w_ref, b_ref, o_ref):
    c = c_ref[...]
    s = (c * jax.nn.sigmoid(c)).astype(BF16)
    o_ref[...] = _dot(s, w_ref[...].astype(BF16)) + b_ref[...]


def _mod_all(cond16, w_ada, b_ada):
    tn = 1024
    return pl.pallas_call(
        _mod_kernel,
        grid=(DEPTH, N_MOD * D // tn),
        in_specs=[pl.BlockSpec((16, D), lambda l, n: (0, 0)),
                  pl.BlockSpec((None, D, tn), lambda l, n: (l, 0, n)),
                  pl.BlockSpec((None, 1, tn), lambda l, n: (l, 0, n))],
        out_specs=pl.BlockSpec((None, 16, tn), lambda l, n: (l, 0, n)),
        out_shape=jax.ShapeDtypeStruct((DEPTH, 16, N_MOD * D), F32),
        compiler_params=_cparams(("arbitrary", "arbitrary")),
        name="adaln_mod",
    )(cond16, w_ada, b_ada.reshape(DEPTH, 1, N_MOD * D))


def _conv_kernel(x_ref, mod_ref, g_ref, win_ref, cw_ref, wout_ref, o_ref, gb_s, u_s, z_s):
    i = pl.program_id(0)
    is_ctx = i < (T_CTX // TM)
    sh = mod_ref[:, 0:D]
    sc = mod_ref[:, D:2 * D]
    gate = mod_ref[:, 2 * D:3 * D]
    g = g_ref[...]
    zero8 = jnp.zeros((8, D), F32)
    u_s[0:8, :] = zero8
    u_s[8 + TM:16 + TM, :] = zero8
    for c in range(TM // MM_CHUNK):
        r0 = c * MM_CHUNK
        h = _norm_mod(x_ref[r0:r0 + MM_CHUNK, :], g, sc, sh).astype(BF16)
        proj = _dot(h, win_ref[...])
        gb_s[r0:r0 + MM_CHUNK, :] = proj[:, 0:D]
        u_s[8 + r0:8 + r0 + MM_CHUNK, :] = proj[:, D:2 * D] * proj[:, 2 * D:3 * D]
    row = lax.broadcasted_iota(I32, (ROW_CHUNK, 1), 0)
    first = jnp.logical_and(is_ctx, row == 0)
    last = jnp.logical_and(is_ctx, row == ROW_CHUNK - 1)
    for c in range(TM // ROW_CHUNK):
        r0 = c * ROW_CHUNK
        up = jnp.where(first, 0.0, u_s[7 + r0:7 + r0 + ROW_CHUNK, :])
        mid = u_s[8 + r0:8 + r0 + ROW_CHUNK, :]
        dn = jnp.where(last, 0.0, u_s[9 + r0:9 + r0 + ROW_CHUNK, :])
        conv = up * cw_ref[0:1, :] + mid * cw_ref[1:2, :] + dn * cw_ref[2:3, :]
        z_s[r0:r0 + ROW_CHUNK, :] = (gb_s[r0:r0 + ROW_CHUNK, :] * conv).astype(BF16)
    for c in range(TM // MM_CHUNK):
        r0 = c * MM_CHUNK
        y = _dot(z_s[r0:r0 + MM_CHUNK, :], wout_ref[...])
        o_ref[r0:r0 + MM_CHUNK, :] = x_ref[r0:r0 + MM_CHUNK, :] + gate * y


def _conv_mixer(x, mod4, layer, g_mix, w_in, conv_w, w_out):
    assert SEQ == ROW_CHUNK and DEC_SEQ == TM
    return pl.pallas_call(
        _conv_kernel,
        grid=(T_ALL // TM,),
        in_specs=[pl.BlockSpec((TM, D), lambda i: (i, 0)),
                  _mod_spec(layer, TM), _row_spec(),
                  _full_spec((D, 3 * D)), _full_spec((3, D)), _full_spec((D, D))],
        out_specs=pl.BlockSpec((TM, D), lambda i: (i, 0)),
        out_shape=jax.ShapeDtypeStruct((T_ALL, D), F32),
        scratch_shapes=[pltpu.VMEM((TM, D), F32), pltpu.VMEM((TM + 16, D), F32), pltpu.VMEM((TM, D), BF16)],
        compiler_params=_cparams(("arbitrary",)),
        name="conv_mixer",
    )(x, mod4, g_mix.reshape(1, D), w_in, conv_w, w_out)


FFN_TM = 512
MXU_WIDTH_V7X = 256
FFN_SPLITS = ((0, 6 * MXU_WIDTH_V7X), (6 * MXU_WIDTH_V7X, D_FF))
assert D_FF % MXU_WIDTH_V7X == 0


def _ffn_kernel(x_ref, mod_ref, g_ref, wgu_ref, wd_ref, o_ref):
    sh = mod_ref[:, 3 * D:4 * D]
    sc = mod_ref[:, 4 * D:5 * D]
    gate = mod_ref[:, 5 * D:6 * D]
    x = x_ref[...]
    h = _norm_mod(x, g_ref[...], sc, sh).astype(BF16)
    acc = None
    for lo, hi in FFN_SPLITS:
        gg = _dot(h, wgu_ref[:, lo:hi])
        uu = _dot(h, wgu_ref[:, D_FF + lo:D_FF + hi])
        a = (gg * jax.nn.sigmoid(gg) * uu).astype(BF16)
        contrib = _dot(a, wd_ref[lo:hi, :])
        acc = contrib if acc is None else acc + contrib
    o_ref[...] = x + gate * acc


def _dense_ffn(x, mod4, layer, g_ffn, w_gu, w_down):
    tok = pl.BlockSpec((FFN_TM, D), lambda i: (i, 0))
    return pl.pallas_call(
        _ffn_kernel,
        grid=(T_ALL // FFN_TM,),
        in_specs=[tok, _mod_spec(layer, FFN_TM), _row_spec(),
                  _full_spec((D, 2 * D_FF)), _full_spec((D_FF, D))],
        out_specs=tok,
        out_shape=jax.ShapeDtypeStruct((T_ALL, D), F32),
        compiler_params=_cparams(("arbitrary",)),
        name="dense_swiglu",
    )(x, mod4, g_ffn.reshape(1, D), w_gu, w_down)


def _s5_prep_kernel(lr_ref, li_ref, ldt_ref, br_ref, bi_ref, lbr_ref, lbi_ref, bbr_ref, bbi_ref):
    lr = lr_ref[...]
    li = li_ref[...]
    dt = jnp.exp(ldt_ref[...])
    mag = jnp.exp(lr * dt)
    ar = mag * jnp.cos(li * dt)
    ai = mag * jnp.sin(li * dt)
    nr = ar - 1.0
    den = lr * lr + li * li
    fr = (nr * lr + ai * li) / den
    fi = (ai * lr - nr * li) / den
    br = br_ref[...]
    bi = bi_ref[...]
    lbr_ref[...] = ar
    lbi_ref[...] = ai
    bbr_ref[...] = fr * br - fi * bi
    bbi_ref[...] = fr * bi + fi * br


def _s5_prep(lam_re, lam_im, log_dt, b_re, b_im):
    rows = 2 * S5_GROUPS
    cols = S5_STATE * S5_GROUP
    exp = lambda a: jnp.repeat(a.reshape(rows, S5_STATE), S5_GROUP, axis=1)
    ldt = jnp.broadcast_to(log_dt.reshape(rows, 1), (rows, cols))
    outs = pl.pallas_call(
        _s5_prep_kernel,
        out_shape=[jax.ShapeDtypeStruct((rows, cols), F32)] * 4,
        name="s5_discretize",
    )(exp(lam_re), exp(lam_im), ldt, b_re.reshape(rows, cols), b_im.reshape(rows, cols))
    lbr, lbi, bbr, bbi = outs
    shp = (2, S5_GROUPS, S5_STATE, S5_GROUP)
    return lbr.reshape(shp)[..., 0], lbi.reshape(shp)[..., 0], bbr.reshape(shp), bbi.reshape(shp)


S5_JT = 8
S5_GPT = LANES // S5_GROUP
S5_HALF = S5_GPT * S5_STATE
S5_ROWS = 512


def _s5_scan_kernel(x_ref, mod_ref, g_ref, wb_ref, wc_ref, lam_ref, h0_ref,
                    y_ref, fin_ref, bu_s, st_s, *, nb, jgroup):
    d = pl.program_id(0)
    c = pl.program_id(1)
    lc = S5_ROWS // nb

    @pl.when(c == 0)
    def _():
        st_s[...] = h0_ref[...]

    rep = S5_ROWS // 8
    sh = _tile_rows(mod_ref[:, 0:D], rep, axis=0)
    sc = _tile_rows(mod_ref[:, D:2 * D], rep, axis=0)
    u = _norm_mod(x_ref[...], g_ref[...], sc, sh).astype(BF16)
    for j in range(S5_JT):
        bu_s[j] = _dot(u[:, j * LANES:(j + 1) * LANES], wb_ref[j])

    for j0 in range(0, S5_JT, jgroup):
        js = list(range(j0, j0 + jgroup))
        lam = [(jnp.broadcast_to(lam_ref[j][:, 0:S5_HALF], (nb, S5_HALF)),
                jnp.broadcast_to(lam_ref[j][:, S5_HALF:], (nb, S5_HALF))) for j in js]

        def body(t, carry):
            l = jnp.where(d == 0, t, lc - 1 - t)
            r0 = pl.multiple_of(l * nb, nb)
            out = []
            for k, j in enumerate(js):
                sr, si = carry[k]
                ar, ai = lam[k]
                bu = bu_s[j, pl.ds(r0, nb), :]
                hr = ar * sr - ai * si + bu[:, 0:S5_HALF]
                hi = ar * si + ai * sr + bu[:, S5_HALF:]
                bu_s[j, pl.ds(r0, nb), 0:S5_HALF] = hr
                bu_s[j, pl.ds(r0, nb), S5_HALF:] = hi
                out.append((hr, hi))
            return tuple(out)

        init = tuple((st_s[j][:, 0:S5_HALF], st_s[j][:, S5_HALF:]) for j in js)
        fin = lax.fori_loop(0, lc, body, init)
        for k, j in enumerate(js):
            st_s[j, :, 0:S5_HALF] = fin[k][0]
            st_s[j, :, S5_HALF:] = fin[k][1]

    for j in range(S5_JT):
        y_ref[:, j * LANES:(j + 1) * LANES] = _dot(bu_s[j].astype(BF16), wc_ref[j])
    fin_ref[...] = st_s[...]


def _s5_scan(xt, nb, mod3, layer, path, g_mix, w_b, w_c, lam_s, h0):
    rows = xt.shape[0]
    nc = rows // S5_ROWS
    chunk = lambda d, c: c + d * (nc - 1 - 2 * c)
    kern = functools.partial(_s5_scan_kernel, nb=nb, jgroup=2 if nb == 8 else 1)
    return pl.pallas_call(
        kern,
        grid=(2, nc),
        in_specs=[pl.BlockSpec((S5_ROWS, D), lambda d, c: (chunk(d, c), 0)),
                  pl.BlockSpec((None, 8, N_MOD * D), lambda d, c: (layer, path, 0)),
                  _row_spec(),
                  pl.BlockSpec((None, S5_JT, LANES, 2 * S5_HALF), lambda d, c: (d, 0, 0, 0)),
                  pl.BlockSpec((None, S5_JT, 2 * S5_HALF, LANES), lambda d, c: (d, 0, 0, 0)),
                  pl.BlockSpec((None, S5_JT, 1, 2 * S5_HALF), lambda d, c: (d, 0, 0, 0)),
                  pl.BlockSpec((None, S5_JT, nb, 2 * S5_HALF), lambda d, c: (d, 0, 0, 0))],
        out_specs=[pl.BlockSpec((None, S5_ROWS, D), lambda d, c: (d, chunk(d, c), 0)),
                   pl.BlockSpec((None, S5_JT, nb, 2 * S5_HALF), lambda d, c: (d, 0, 0, 0))],
        out_shape=[jax.ShapeDtypeStruct((2, rows, D), F32),
                   jax.ShapeDtypeStruct((2, S5_JT, nb, 2 * S5_HALF), F32)],
        scratch_shapes=[pltpu.VMEM((S5_JT, S5_ROWS, 2 * S5_HALF), F32),
                        pltpu.VMEM((S5_JT, nb, 2 * S5_HALF), F32)],
        compiler_params=_cparams(("arbitrary", "arbitrary")),
        name="s5_scan_b%d" % nb,
    )(xt, mod3, g_mix.reshape(1, D), w_b, w_c, lam_s, h0)


S5_GLU_ROWS = 512


def _s5_glu_kernel(x_ref, yf_ref, yb_ref, mod_ref, g_ref, dsk_ref, w_ref, o_ref):
    rep = MM_CHUNK // 8
    sh = _tile_rows(mod_ref[:, 0:D], rep, axis=0)
    sc = _tile_rows(mod_ref[:, D:2 * D], rep, axis=0)
    gate = _tile_rows(mod_ref[:, 2 * D:3 * D], rep, axis=0)
    for c in range(S5_GLU_ROWS // MM_CHUNK):
        r0 = c * MM_CHUNK
        x = x_ref[r0:r0 + MM_CHUNK, :]
        u = _norm_mod(x, g_ref[...], sc, sh)
        y = u * dsk_ref[...] + yf_ref[r0:r0 + MM_CHUNK, :] + yb_ref[r0:r0 + MM_CHUNK, :]
        z = jax.nn.gelu(y).astype(BF16)
        ag = _dot(z, w_ref[...])
        out = ag[:, 0:D] * jax.nn.sigmoid(ag[:, D:2 * D])
        o_ref[r0:r0 + MM_CHUNK, :] = x + gate * out


def _s5_glu(xt, y2, mod3, layer, path, g_mix, d_skip, w_glu):
    rows = xt.shape[0]
    return pl.pallas_call(
        _s5_glu_kernel,
        grid=(rows // S5_GLU_ROWS,),
        in_specs=[pl.BlockSpec((S5_GLU_ROWS, D), lambda i: (i, 0)),
                  pl.BlockSpec((None, S5_GLU_ROWS, D), lambda i: (0, i, 0)),
                  pl.BlockSpec((None, S5_GLU_ROWS, D), lambda i: (1, i, 0)),
                  pl.BlockSpec((None, 8, N_MOD * D), lambda i: (layer, path, 0)),
                  _row_spec(), _row_spec(), _full_spec((D, 2 * D))],
        out_specs=pl.BlockSpec((S5_GLU_ROWS, D), lambda i: (i, 0)),
        out_shape=jax.ShapeDtypeStruct((rows, D), F32),
        compiler_params=_cparams(("arbitrary",)),
        name="s5_glu",
    )(xt, y2, y2, mod3, g_mix.reshape(1, D), d_skip.reshape(1, D), w_glu)


def _s5_weights(lbr, lbi, bbr, bbi, c_re, c_im):
    eye = jnp.eye(S5_GPT, dtype=F32)
    bb = jnp.stack([bbr, bbi]).reshape(2, 2, S5_JT, S5_GPT, S5_STATE, S5_GROUP)
    w_bu = jnp.einsum('rdjgps,gh->djgsrhp', bb, eye).reshape(2, S5_JT, LANES, 2 * S5_HALF)
    w_b = w_bu.astype(BF16)
    cc =jnp.stack([c_re, -c_im]).reshape(2, 2, S5_JT, S5_GPT, S5_GROUP, S5_STATE)
    w_c = jnp.einsum('rdjgsp,gh->djrgphs', cc, eye).reshape(2, S5_JT, 2 * S5_HALF, LANES).astype(BF16)
    lam_s = jnp.concatenate([lbr.reshape(2, S5_JT, S5_HALF), lbi.reshape(2, S5_JT, S5_HALF)], axis=-1)
    return w_b, w_c, lam_s.reshape(2, S5_JT, 1, 2 * S5_HALF)


def _s5_state_in(st_re, st_im):
    def lay(a):
        b = a.shape[0]
        return a.transpose(1, 0, 2, 3).reshape(2, b, S5_JT, S5_HALF).transpose(0, 2, 1, 3)
    return jnp.concatenate([lay(st_re), lay(st_im)], axis=-1)


def _s5_state_out(fin):
    def lay(a):
        b = a.shape[2]
        return a.transpose(2, 0, 1, 3).reshape(b, 2, S5_GROUPS, S5_STATE)
    return lay(fin[..., 0:S5_HALF]), lay(fin[..., S5_HALF:])


QKV_TM = 512


def _qkv_ctx_kernel(x_ref, mod_ref, g_ref, wq_ref, wk_ref, wv_ref, wkv_ref, q_ref, k_ref, v_ref, kv_ref):
    sh = mod_ref[:, 0:D]
    sc = mod_ref[:, D:2 * D]
    for c in range(QKV_TM // MM_CHUNK):
        r0 = c * MM_CHUNK
        h = _norm_mod(x_ref[r0:r0 + MM_CHUNK, :], g_ref[...], sc, sh).astype(BF16)
        q_ref[r0:r0 + MM_CHUNK, :] = _dot(h, wq_ref[...]).astype(BF16)
        k_ref[r0:r0 + MM_CHUNK, :] = _dot(h, wk_ref[...]).astype(BF16)
        v_ref[r0:r0 + MM_CHUNK, :] = _dot(h, wv_ref[...]).astype(BF16)
        kv_ref[r0:r0 + MM_CHUNK, :] = _dot(h, wkv_ref[...])


def _qkv_ctx(x, mod4, layer, g_mix, wq, wk, wv, wkv):
    bf = jax.ShapeDtypeStruct((T_CTX, D), BF16)
    tok = pl.BlockSpec((QKV_TM, D), lambda i: (i, 0))
    kvw = wkv.shape[1]
    return pl.pallas_call(
        _qkv_ctx_kernel,
        grid=(T_CTX // QKV_TM,),
        in_specs=[tok, _mod_spec(layer, QKV_TM), _row_spec(),
                  _full_spec((D, D)), _full_spec((D, D)), _full_spec((D, D)), _full_spec((D, kvw))],
        out_specs=[tok, tok, tok, pl.BlockSpec((QKV_TM, kvw), lambda i: (i, 0))],
        out_shape=[bf, bf, bf, jax.ShapeDtypeStruct((T_CTX, kvw), F32)],
        compiler_params=_cparams(("arbitrary",)),
        name="qkv_ctx",
    )(x, mod4, g_mix.reshape(1, D), wq, wk, wv, wkv)


def _qkv_lat_kernel(x_ref, mod_ref, g_ref, wq_ref, wqr_ref, wk_ref, wkr_ref, wv_ref, cos_ref, sin_ref,
                    q_ref, k_ref, v_ref):
    sh = mod_ref[:, 0:D]
    sc = mod_ref[:, D:2 * D]
    for c in range(QKV_TM // MM_CHUNK):
        r0 = c * MM_CHUNK
        h = _norm_mod(x_ref[r0:r0 + MM_CHUNK, :], g_ref[...], sc, sh).astype(BF16)
        cos = cos_ref[r0:r0 + MM_CHUNK, :]
        sin = sin_ref[r0:r0 + MM_CHUNK, :]
        q_ref[r0:r0 + MM_CHUNK, :] = (_dot(h, wq_ref[...]) * cos + _dot(h, wqr_ref[...]) * sin).astype(BF16)
        k_ref[r0:r0 + MM_CHUNK, :] = (_dot(h, wk_ref[...]) * cos + _dot(h, wkr_ref[...]) * sin).astype(BF16)
        v_ref[r0:r0 + MM_CHUNK, :] = _dot(h, wv_ref[...]).astype(BF16)


def _qkv_lat(x, mod4, layer, g_mix, wq, wq_rot, wk, wk_rot, wv, cos_t, sin_t):
    nct = T_CTX // QKV_TM
    lpb = DEC_SEQ // QKV_TM
    bf = jax.ShapeDtypeStruct((T_LAT, D), BF16)
    tok_out = pl.BlockSpec((QKV_TM, D), lambda i: (i, 0))
    rope = pl.BlockSpec((QKV_TM, D), lambda i: (i % lpb, 0))
    return pl.pallas_call(
        _qkv_lat_kernel,
        grid=(T_LAT // QKV_TM,),
        in_specs=[pl.BlockSpec((QKV_TM, D), lambda i: (i + nct, 0)),
                  pl.BlockSpec((None, None, 1, N_MOD * D), lambda i: (layer, 8 + i // lpb, 0, 0)),
                  _row_spec()] + [_full_spec((D, D))] * 5 + [rope, rope],
        out_specs=[tok_out, tok_out, tok_out],
        out_shape=[bf, bf, bf],
        compiler_params=_cparams(("arbitrary",)),
        name="qkv_lat",
    )(x, mod4, g_mix.reshape(1, D), wq, wq_rot, wk, wk_rot, wv, cos_t, sin_t)


KVW = Q_PER_KV * HEAD_DIM


def _head_masks(rows):
    lane = lax.broadcasted_iota(I32, (rows, KVW), 1)
    return [jnp.logical_and(lane >= g * HEAD_DIM, lane < (g + 1) * HEAD_DIM) for g in range(Q_PER_KV)]


def _attn_ctx_kernel(sink_ref, q_ref, k_ref, v_ref, o_ref):
    scale = HEAD_DIM ** -0.5
    masks = _head_masks(SEQ)
    for kv in range(N_KV_HEADS):
        c0 = kv * KVW
        q = q_ref[:, c0:c0 + KVW]
        k = k_ref[:, c0:c0 + KVW]
        v = v_ref[:, c0:c0 + KVW]
        acc = jnp.zeros((SEQ, KVW), F32)
        for g in range(Q_PER_KV):
            sink = sink_ref[kv * Q_PER_KV + g]
            qg = jnp.where(masks[g], q, jnp.zeros_like(q))
            s = _dot_nt(qg, k) * scale
            m = jnp.maximum(jnp.max(s, axis=-1, keepdims=True), sink)
            e = jnp.exp(s - m)
            den = jnp.sum(e, axis=-1, keepdims=True) + jnp.exp(sink - m)
            og = _dot(e.astype(BF16), v) / den
            acc = jnp.where(masks[g], og, acc)
        o_ref[:, c0:c0 + KVW] = acc.astype(BF16)


def _attn_ctx(sink, q, k, v):
    tok = pl.BlockSpec((SEQ, D), lambda b, *_: (b, 0))
    return pl.pallas_call(
        _attn_ctx_kernel,
        grid_spec=pltpu.PrefetchScalarGridSpec(
            num_scalar_prefetch=1, grid=(BATCH,),
            in_specs=[tok, tok, tok], out_specs=tok),
        out_shape=jax.ShapeDtypeStruct((T_CTX, D), BF16),
        compiler_params=_cparams(("arbitrary",)),
        name="attn_ctx",
    )(sink, q, k, v)


ATT_TQ = 128
ATT_SPAN = ATT_TQ + 2 * WINDOW


def _attn_lat_kernel(sink_ref, q_ref, k_ref, v_ref, ck_ref, cv_ref, o_ref):
    qb = pl.program_id(1)
    scale = HEAD_DIM ** -0.5
    w0 = pl.multiple_of(jnp.clip(qb * ATT_TQ - WINDOW, 0, DEC_SEQ - ATT_SPAN), ATT_TQ)
    rows = Q_PER_KV * ATT_TQ
    ridx = lax.broadcasted_iota(I32, (rows, ATT_SPAN), 0)
    qpos = qb * ATT_TQ + (ridx & (ATT_TQ - 1))
    kpos = w0 + lax.broadcasted_iota(I32, (rows, ATT_SPAN), 1)
    valid = jnp.abs(qpos - kpos) <= WINDOW
    rcol = lax.broadcasted_iota(I32, (rows, 1), 0)
    masks = _head_masks(ATT_TQ)
    for kv in range(N_KV_HEADS):
        c0 = kv * KVW
        q = q_ref[:, c0:c0 + KVW]
        qs = jnp.concatenate([jnp.where(masks[g], q, jnp.zeros_like(q)) for g in range(Q_PER_KV)], axis=0)
        sink = jnp.zeros((rows, 1), F32)
        for g in range(Q_PER_KV):
            sink = jnp.where(rcol >= g * ATT_TQ, sink_ref[kv * Q_PER_KV + g], sink)
        s_ctx = _dot_nt(qs, ck_ref[:, c0:c0 + KVW]) * scale
        s_win = _dot_nt(qs, k_ref[pl.ds(w0, ATT_SPAN), c0:c0 + KVW]) * scale
        s_win = jnp.where(valid, s_win, NEG_INF)
        m = jnp.maximum(jnp.maximum(jnp.max(s_ctx, axis=-1, keepdims=True),
                                    jnp.max(s_win, axis=-1, keepdims=True)), sink)
        e_ctx = jnp.exp(s_ctx - m)
        e_win = jnp.exp(s_win - m)
        den = (jnp.exp(sink - m) + jnp.sum(e_ctx, axis=-1, keepdims=True)
               + jnp.sum(e_win, axis=-1, keepdims=True))
        o = (_dot(e_ctx.astype(BF16), cv_ref[:, c0:c0 + KVW])
             + _dot(e_win.astype(BF16), v_ref[pl.ds(w0, ATT_SPAN), c0:c0 + KVW])) / den
        acc = jnp.zeros((ATT_TQ, KVW), F32)
        for g in range(Q_PER_KV):
            acc = jnp.where(masks[g], o[g * ATT_TQ:(g + 1) * ATT_TQ, :], acc)
        o_ref[:, c0:c0 + KVW] = acc.astype(BF16)


def _attn_lat(sink, q, k, v, ck, cv):
    nqb = DEC_SEQ // ATT_TQ
    qspec = pl.BlockSpec((ATT_TQ, D), lambda b, i, *_: (b * nqb + i, 0))
    seq = pl.BlockSpec((DEC_SEQ, D), lambda b, i, *_: (b, 0))
    ctx = pl.BlockSpec((PAST_LEN, D), lambda b, i, *_: (b, 0))
    return pl.pallas_call(
        _attn_lat_kernel,
        grid_spec=pltpu.PrefetchScalarGridSpec(
            num_scalar_prefetch=1, grid=(DEC_BATCH, nqb),
            in_specs=[qspec, seq, seq, ctx, ctx], out_specs=qspec),
        out_shape=jax.ShapeDtypeStruct((T_LAT, D), BF16),
        compiler_params=_cparams(("arbitrary", "arbitrary")),
        name="attn_lat",
    )(sink, q, k, v, ck, cv)


def _resproj_kernel(x_ref, a_ref, mod_ref, w_ref, o_ref):
    gate = mod_ref[:, 2 * D:3 * D]
    o_ref[...] = x_ref[...] + gate * _dot(a_ref[...], w_ref[...])


def _resproj(x, a, mod4, layer, w):
    tok = pl.BlockSpec((TM, D), lambda i: (i, 0))
    return pl.pallas_call(
        _resproj_kernel,
        grid=(T_ALL // TM,),
        in_specs=[tok, tok, _mod_spec(layer, TM), _full_spec((D, D))],
        out_specs=tok,
        out_shape=jax.ShapeDtypeStruct((T_ALL, D), F32),
        compiler_params=_cparams(("arbitrary",)),
        name="attn_out_proj",
    )(x, a, mod4, w)


def _rope_tables():
    rows = DEC_SEQ // GRID_W
    row = jnp.repeat(jnp.arange(rows), GRID_W).astype(F32)
    col = jnp.tile(jnp.arange(GRID_W), rows).astype(F32)
    inv = ROPE_THETA ** (-jnp.arange(ROPE_FREQS, dtype=F32) / ROPE_FREQS)
    ang = jnp.concatenate([row[:, None] * inv, col[:, None] * inv], axis=-1)
    cos = jnp.cos(ang)
    sin = jnp.sin(ang)
    cos_h = jnp.concatenate([cos, cos], axis=-1)
    sin_h = jnp.concatenate([sin, sin], axis=-1)
    return jnp.tile(cos_h, (1, N_HEADS)), jnp.tile(sin_h, (1, N_HEADS))


def _rot_half_cols(w):
    k = w.shape[0]
    w4 = w.reshape(k, -1, 2, HEAD_DIM // 2)
    return jnp.stack([-w4[:, :, 1], w4[:, :, 0]], axis=2).reshape(k, -1)


def _expand_kv_cols(w):
    k = w.shape[0]
    w3 = w.reshape(k, N_KV_HEADS, 1, HEAD_DIM)
    return jnp.broadcast_to(w3, (k, N_KV_HEADS, Q_PER_KV, HEAD_DIM)).reshape(k, N_HEADS * HEAD_DIM)


FG = D // FNET_GROUPS


def _fnet_kernel(x_ref, mod_ref, g_ref, cs_ref, fl_ref, w_ref, o_ref, h_s, ab_s):
    sh = mod_ref[:, 0:D]
    sc = mod_ref[:, D:2 * D]
    gate = mod_ref[:, 2 * D:3 * D]
    for c in range(TM // ROW_CHUNK):
        r0 = c * ROW_CHUNK
        h_s[r0:r0 + ROW_CHUNK, :] = _norm_mod(x_ref[r0:r0 + ROW_CHUNK, :], g_ref[...], sc, sh).astype(BF16)
    for g in range(FNET_GROUPS):
        ab = _dot(h_s[:, g * FG:(g + 1) * FG], cs_ref[...])
        ab_s[0:TM, g * FG:(g + 1) * FG] = ab[:, 0:FG].astype(BF16)
        ab_s[TM:2 * TM, g * FG:(g + 1) * FG] = ab[:, FG:2 * FG].astype(BF16)
    for c in range(TM // MM_CHUNK):
        r0 = c * MM_CHUNK
        f = _dot(fl_ref[r0:r0 + MM_CHUNK, :], ab_s[...])
        y = _dot(f.astype(BF16), w_ref[...])
        o_ref[r0:r0 + MM_CHUNK, :] = x_ref[r0:r0 + MM_CHUNK, :] + gate * y


def _fnet_mixer(x, mod4, layer, g_mix, cs, fl, w_out):
    nct = T_CTX // TM
    tok = pl.BlockSpec((TM, D), lambda i: (i, 0))
    return pl.pallas_call(
        _fnet_kernel,
        grid=(T_ALL // TM,),
        in_specs=[tok, _mod_spec(layer, TM), _row_spec(), _full_spec((FG, 2 * FG)),
                  pl.BlockSpec((None, TM, 2 * TM), lambda i: (jnp.where(i < nct, 0, 1), 0, 0)),
                  _full_spec((D, D))],
        out_specs=tok,
        out_shape=jax.ShapeDtypeStruct((T_ALL, D), F32),
        scratch_shapes=[pltpu.VMEM((TM, D), BF16), pltpu.VMEM((2 * TM, D), BF16)],
        compiler_params=_cparams(("arbitrary",)),
        name="fnet_mixer",
    )(x, mod4, g_mix.reshape(1, D), cs, fl, w_out)


def _dft_cos_sin(n):
    k = np.arange(n)
    ang = 2.0 * np.pi * ((k[:, None] * k[None, :]) % n) / n
    return np.cos(ang), np.sin(ang)


def _fnet_tables():
    cc, sc = _dft_cos_sin(FG)
    cs = np.concatenate([cc, sc], axis=1) / math.sqrt(FG)
    mats = []
    for seq in (SEQ, DEC_SEQ):
        cl, sl = _dft_cos_sin(seq)
        reps = TM // seq
        eye = np.eye(reps)
        mats.append(np.concatenate([np.kron(eye, cl), -np.kron(eye, sl)], axis=1) / math.sqrt(seq))
    return jnp.asarray(cs, F32).astype(BF16), jnp.asarray(np.stack(mats), F32).astype(BF16)


ROUTER_TM = 512
TOK_SUB = D // LANES


def _store_token_tiles(ref, r0, val):
    rows = val.shape[0]
    for c in range(TOK_SUB):
        ref[pl.ds(r0 * TOK_SUB + c, rows, stride=TOK_SUB), :] = val[:, c * LANES:(c + 1) * LANES]


def _load_token_tiles(ref, r0, rows, c, lead=None):
    idx = pl.ds(r0 * TOK_SUB + c, rows, stride=TOK_SUB)
    return ref[idx, :] if lead is None else ref[lead, idx, :]


def _router_kernel(x_ref, mod_ref, g_ref, wr_ref, h_ref, info_ref, gates_ref):
    sh = mod_ref[:, 3 * D:4 * D]
    sc = mod_ref[:, 4 * D:5 * D]
    w_hi = wr_ref[0]
    w_lo = wr_ref[1]
    for c in range(ROUTER_TM // ROW_CHUNK):
        r0 = c * ROW_CHUNK
        h = _norm_mod(x_ref[r0:r0 + ROW_CHUNK, :], g_ref[...], sc, sh)
        _store_token_tiles(h_ref, r0, h)
        h_hi, h_lo = _split_bf16(h)
        logits = _dot(h_hi, w_hi) + _dot(h_lo, w_hi) + _dot(h_hi, w_lo)
        lane = lax.broadcasted_iota(I32, logits.shape, 1)
        lg = jnp.where(lane < N_EXPERTS, logits, -jnp.inf)
        m1 = jnp.max(lg, axis=-1, keepdims=True)
        i1 = jnp.min(jnp.where(lg == m1, lane, LANES), axis=-1, keepdims=True)
        lg2 = jnp.where(lane == i1, -jnp.inf, lg)
        m2 = jnp.max(lg2, axis=-1, keepdims=True)
        i2 = jnp.min(jnp.where(lg2 == m2, lane, LANES), axis=-1, keepdims=True)
        e2 = jnp.exp(m2 - m1)
        den = 1.0 + e2
        gates_ref[r0:r0 + ROW_CHUNK, :] = jnp.where(lane == 0, 1.0 / den, jnp.where(lane == 1, e2 / den, 0.0))
        info_ref[r0:r0 + ROW_CHUNK, :] = jnp.where(lane == 0, i1, jnp.where(lane == 1, i2, 0)).astype(I32)


def _router(x, mod4, layer, g_ffn, w_router):
    wr = jnp.zeros((D, LANES), F32).at[:, 0:N_EXPERTS].set(w_router)
    wr_hi = wr.astype(BF16)
    wr_lo = (wr - wr_hi.astype(F32)).astype(BF16)
    tok = pl.BlockSpec((ROUTER_TM, D), lambda i: (i, 0))
    nar = pl.BlockSpec((ROUTER_TM, LANES), lambda i: (i, 0))
    return pl.pallas_call(
        _router_kernel,
        grid=(T_ALL // ROUTER_TM,),
        in_specs=[tok, _mod_spec(layer, ROUTER_TM), _row_spec(), _full_spec((2, D, LANES))],
        out_specs=[pl.BlockSpec((ROUTER_TM * TOK_SUB, LANES), lambda i: (i, 0)), nar, nar],
        out_shape=[jax.ShapeDtypeStruct((T_ALL * TOK_SUB, LANES), F32),
                   jax.ShapeDtypeStruct((T_ALL, LANES), I32),
                   jax.ShapeDtypeStruct((T_ALL, LANES), F32)],
        compiler_params=_cparams(("arbitrary",)),
        name="moe_router",
    )(x, mod4, g_ffn.reshape(1, D), jnp.stack([wr_hi, wr_lo]))


def _route_plan(info):
    eid = jnp.arange(N_EXPERTS, dtype=I32)[None, :]
    first = info[:, 0:1] == eid
    second = info[:, 1:2] == eid
    onehot = jnp.logical_or(first, second).astype(I32)
    csum = jnp.cumsum(onehot, axis=0)
    rank = csum - onehot
    counts = csum[-1]
    padded = ((counts + TME - 1) // TME) * TME
    gend = jnp.cumsum(padded)
    gstart = gend - padded
    pos_te = gstart[None, :] + rank
    pos1 = jnp.sum(jnp.where(first, pos_te, 0), axis=1).astype(I32)
    pos2 = jnp.sum(jnp.where(second, pos_te, 0), axis=1).astype(I32)
    n_valid = gend[-1] // TME
    tile_start = jnp.arange(N_EXPERT_TILES, dtype=I32) * TME
    eff_start = jnp.minimum(tile_start, jnp.maximum(n_valid - 1, 0) * TME)
    tile_expert = jnp.minimum(jnp.sum((eff_start[:, None] >= gend[None, :]).astype(I32), axis=1),
                              N_EXPERTS - 1).astype(I32)
    valid_end = (gstart + counts)[tile_expert]
    tile_rows = jnp.where(tile_start < gend[-1], jnp.clip(valid_end - tile_start, 0, TME), 0).astype(I32)
    return pos1, pos2, tile_expert, n_valid.astype(I32).reshape(1), tile_rows


DISP_TM = 512
DMA_UNROLL = 8


def _dispatch_kernel(p1_ref, p2_ref, h_ref, xs_init, xs_hbm, sem):
    del xs_init
    i = pl.program_id(0)

    def body(g, carry):
        for k in range(DMA_UNROLL):
            r = g * DMA_UNROLL + k
            t = i * DISP_TM + r
            tile = h_ref.at[pl.ds(pl.multiple_of(r * TOK_SUB, TOK_SUB), TOK_SUB), :]
            for j, p_ref in enumerate((p1_ref, p2_ref)):
                dst = xs_hbm.at[pl.ds(pl.multiple_of(p_ref[t] * TOK_SUB, TOK_SUB), TOK_SUB), :]
                pltpu.make_async_copy(tile, dst, sem.at[j]).start(priority=j)
        return carry
    lax.fori_loop(0, DISP_TM // DMA_UNROLL, body, 0)
    for j in range(2):
        pltpu.make_async_copy(h_ref, xs_hbm.at[pl.ds(0, DISP_TM * TOK_SUB), :], sem.at[j]).wait()


def _dispatch(pos1, pos2, h, xs_init):
    return pl.pallas_call(
        _dispatch_kernel,
        grid_spec=pltpu.PrefetchScalarGridSpec(
            num_scalar_prefetch=2, grid=(T_ALL // DISP_TM,),
            in_specs=[pl.BlockSpec((DISP_TM * TOK_SUB, LANES), lambda i, *_: (i, 0)),
                      pl.BlockSpec(memory_space=pl.ANY)],
            out_specs=pl.BlockSpec(memory_space=pl.ANY),
            scratch_shapes=[pltpu.SemaphoreType.DMA((2,))]),
        out_shape=jax.ShapeDtypeStruct((P_MAX * TOK_SUB, LANES), F32),
        input_output_aliases={3: 0},
        compiler_params=_cparams(("arbitrary",)),
        name="moe_dispatch",
    )(pos1, pos2, h, xs_init)


EXP_TF = 512
EXP_NF = D_FF_EXPERT // EXP_TF


def _expert_kernel(te_ref, nv_ref, rows_ref, xs_ref, wg_ref, wu_ref, wd_ref, o_ref, xb_s, acc_s):
    m = pl.program_id(0)
    f = pl.program_id(1)
    nf = pl.num_programs(1)
    n_valid = nv_ref[0]
    half = TME // 2
    small = rows_ref[m] <= half

    @pl.when(jnp.logical_and(m == 0, f == 0))
    def _():
        acc_s[...] = jnp.zeros_like(acc_s)

    def compute(mrows):
        @pl.when(f == 0)
        def _():
            for c in range(TOK_SUB):
                xb_s[0:mrows, c * LANES:(c + 1) * LANES] = _load_token_tiles(xs_ref, 0, mrows, c).astype(BF16)

        xb = xb_s[0:mrows, :]
        gg = _dot(xb, wg_ref[...].astype(BF16))
        uu = _dot(xb, wu_ref[...].astype(BF16))
        a = (gg * jax.nn.sigmoid(gg) * uu).astype(BF16)
        contrib = _dot(a, wd_ref[...].astype(BF16))

        acc_s[0:mrows, :] = jnp.where(f == 0, 0.0, acc_s[0:mrows, :]) + contrib

        @pl.when(f == nf - 1)
        def _():
            _store_token_tiles(o_ref, 0, acc_s[0:mrows, :])
            if mrows < TME:
                o_ref[mrows * TOK_SUB:TME * TOK_SUB, :] = jnp.zeros(((TME - mrows) * TOK_SUB, LANES), F32)

    @pl.when(jnp.logical_and(m < n_valid, jnp.logical_not(small)))
    def _():
        compute(TME)

    @pl.when(jnp.logical_and(m < n_valid, small))
    def _():
        compute(half)

    @pl.when(jnp.logical_and(m >= n_valid, f == 0))
    def _():
        o_ref[...] = jnp.zeros_like(o_ref)


def _expert_ffn(tile_expert, n_valid, tile_rows, xs, w_gu, w_down, li):
    def feff(m, f, nv):
        return jnp.where(m < nv[0], f, EXP_NF - 1)

    def meff(m, nv):
        return jnp.minimum(m, nv[0] - 1)
    return pl.pallas_call(
        _expert_kernel,
        grid_spec=pltpu.PrefetchScalarGridSpec(
            num_scalar_prefetch=3, grid=(N_EXPERT_TILES, EXP_NF),
            in_specs=[pl.BlockSpec((TME * TOK_SUB, LANES), lambda m, f, te, nv, tr: (meff(m, nv), 0)),
                      pl.BlockSpec((None, None, D, EXP_TF),
                                   lambda m, f, te, nv, tr: (li, te[m], 0, feff(m, f, nv))),
                      pl.BlockSpec((None, None, D, EXP_TF),
                                   lambda m, f, te, nv, tr: (li, te[m], 0, EXP_NF + feff(m, f, nv))),
                      pl.BlockSpec((None, None, EXP_TF, D),
                                   lambda m, f, te, nv, tr: (li, te[m], feff(m, f, nv), 0))],
            out_specs=pl.BlockSpec((TME * TOK_SUB, LANES), lambda m, f, te, nv, tr: (m, 0)),
            scratch_shapes=[pltpu.VMEM((TME, D), BF16), pltpu.VMEM((TME, D), F32)]),
        out_shape=jax.ShapeDtypeStruct((P_MAX * TOK_SUB, LANES), F32),
        compiler_params=_cparams(("arbitrary", "arbitrary")),
        name="moe_expert_swiglu",
    )(tile_expert, n_valid, tile_rows, xs, w_gu, w_gu, w_down)


COMB_TM = 512


def _combine_kernel(p1_ref, p2_ref, x_ref, gates_ref, mod_ref, gf_ref, y_hbm, o_ref, a_s, b_s, sem, *, final):
    i = pl.program_id(0)

    def tile_copy(p, r, dst, s):
        return pltpu.make_async_copy(y_hbm.at[pl.ds(pl.multiple_of(p * TOK_SUB, TOK_SUB), TOK_SUB), :],
                                     dst.at[pl.ds(pl.multiple_of(r * TOK_SUB, TOK_SUB), TOK_SUB), :], s)

    def body(g, carry):
        for k in range(DMA_UNROLL):
            r = g * DMA_UNROLL + k
            t = i * COMB_TM + r
            tile_copy(p1_ref[t], r, a_s, sem.at[0]).start(priority=0)
            tile_copy(p2_ref[t], r, b_s, sem.at[1]).start(priority=1)
        return carry
    lax.fori_loop(0, COMB_TM // DMA_UNROLL, body, 0)
    pltpu.make_async_copy(y_hbm.at[pl.ds(0, COMB_TM * TOK_SUB), :], a_s, sem.at[0]).wait()
    pltpu.make_async_copy(y_hbm.at[pl.ds(0, COMB_TM * TOK_SUB), :], b_s, sem.at[1]).wait()

    for c in range(COMB_TM // ROW_CHUNK):
        r0 = c * ROW_CHUNK
        w1 = gates_ref[r0:r0 + ROW_CHUNK, 0:1]
        w2 = gates_ref[r0:r0 + ROW_CHUNK, 1:2]
        for j in range(TOK_SUB):
            cols = slice(j * LANES, (j + 1) * LANES)
            y = (w1 * _load_token_tiles(a_s, r0, ROW_CHUNK, j) + w2 * _load_token_tiles(b_s, r0, ROW_CHUNK, j))
            o_ref[r0:r0 + ROW_CHUNK, cols] = x_ref[r0:r0 + ROW_CHUNK, cols] + mod_ref[:, 5 * D + j * LANES:5 * D + (j + 1) * LANES] * y
        if final:
            xn = o_ref[r0:r0 + ROW_CHUNK, :]
            ms = jnp.mean(xn * xn, axis=-1, keepdims=True)
            o_ref[r0:r0 + ROW_CHUNK, :] = xn * lax.rsqrt(ms + EPS) * gf_ref[...]


def _combine(pos1, pos2, x, gates, mod4, layer, g_final, y, final):
    tok = pl.BlockSpec((COMB_TM, D), lambda i, *_: (i, 0))
    return pl.pallas_call(
        functools.partial(_combine_kernel, final=final),
        grid_spec=pltpu.PrefetchScalarGridSpec(
            num_scalar_prefetch=2, grid=(T_ALL // COMB_TM,),
            in_specs=[tok, pl.BlockSpec((COMB_TM, LANES), lambda i, *_: (i, 0)),
                      _mod_spec(layer, COMB_TM), _row_spec(),
                      pl.BlockSpec(memory_space=pl.ANY)],
            out_specs=tok,
            scratch_shapes=[pltpu.VMEM((COMB_TM * TOK_SUB, LANES), F32), pltpu.VMEM((COMB_TM * TOK_SUB, LANES), F32),
                            pltpu.SemaphoreType.DMA((2,))]),
        out_shape=jax.ShapeDtypeStruct((T_ALL, D), F32),
        compiler_params=_cparams(("arbitrary",)),
        name="moe_combine_final" if final else "moe_combine",
    )(pos1, pos2, x, gates, mod4, g_final.reshape(1, D), y)


def _moe_layer(x, mod4, layer, g_ffn, w_router, w_gu, w_down, li, g_final, final, xs_init):
    h, info, gates = _router(x, mod4, layer, g_ffn, w_router)
    pos1, pos2, tile_expert, n_valid, tile_rows = _route_plan(info)
    xs = _dispatch(pos1, pos2, h, xs_init)
    ys = _expert_ffn(tile_expert, n_valid, tile_rows, xs, w_gu, w_down, li)
    return _combine(pos1, pos2, x, gates, mod4, layer, g_final, ys, final), xs


def kernel(x_prompt, x_sample, cache_k, cache_v, state_ssm_re, state_ssm_im, c, c_ctx, w_ada, b_ada, g_mix, g_ffn, g_final, conv_w_in, conv_w, conv_w_out, s5_lambda_re, s5_lambda_im, s5_b_re, s5_b_im, s5_c_re, s5_c_im, s5_log_dt, s5_d, s5_w_glu, attn_w_q, attn_w_kv, attn_w_o, attn_sink, fnet_w_out, ffn_w_gu, ffn_w_down, moe_w_router, moe_w_gu, moe_w_down):
    x = jnp.concatenate([x_prompt.reshape(T_CTX, D), x_sample.reshape(T_LAT, D)], axis=0)
    cond16 = jnp.concatenate([jnp.broadcast_to(c_ctx[None, :], (8, D)), c], axis=0)
    mod3 = _mod_all(cond16, w_ada, b_ada)
    mod4 = mod3.reshape(DEPTH, 16, 1, N_MOD * D)

    x = _conv_mixer(x, mod4, 0, g_mix[0], conv_w_in[0].astype(BF16), conv_w[0], conv_w_out[0].astype(BF16))
    x = _dense_ffn(x, mod4, 0, g_ffn[0], ffn_w_gu[0].astype(BF16), ffn_w_down[0].astype(BF16))

    lbr, lbi, bbr, bbi = _s5_prep(s5_lambda_re[0], s5_lambda_im[0], s5_log_dt[0], s5_b_re[0], s5_b_im[0])
    w_b, w_c, lam_s = _s5_weights(lbr, lbi, bbr, bbi, s5_c_re[0], s5_c_im[0])
    w_glu = s5_w_glu[0].astype(BF16)
    xc = x[0:T_CTX].reshape(BATCH, SEQ, D).transpose(1, 0, 2).reshape(T_CTX, D)
    xl = x[T_CTX:].reshape(DEC_BATCH, DEC_SEQ, D).transpose(1, 0, 2).reshape(T_LAT, D)
    h0_ctx = jnp.zeros((2, S5_JT, BATCH, 2 * S5_HALF), F32)
    h0_lat = _s5_state_in(state_ssm_re[:, 0], state_ssm_im[:, 0])
    yc, fin_c = _s5_scan(xc, BATCH, mod3, 1, 0, g_mix[1], w_b, w_c, lam_s, h0_ctx)
    yl, _ = _s5_scan(xl, DEC_BATCH, mod3, 1, 1, g_mix[1], w_b, w_c, lam_s, h0_lat)
    xc = _s5_glu(xc, yc, mod3, 1, 0, g_mix[1], s5_d[0], w_glu)
    xl = _s5_glu(xl, yl, mod3, 1, 1, g_mix[1], s5_d[0], w_glu)
    x = jnp.concatenate([xc.reshape(SEQ, BATCH, D).transpose(1, 0, 2).reshape(T_CTX, D),
                         xl.reshape(DEC_SEQ, DEC_BATCH, D).transpose(1, 0, 2).reshape(T_LAT, D)], axis=0)
    new_re, new_im = _s5_state_out(fin_c)
    xs_zero = jnp.zeros((P_MAX * TOK_SUB, LANES), F32)
    x, xs_buf = _moe_layer(x, mod4, 1, g_ffn[1], moe_w_router[0], moe_w_gu, moe_w_down, 0, g_final, False, xs_zero)

    wq = attn_w_q[0]
    wk = _expand_kv_cols(attn_w_kv[0][:, 0:N_KV_HEADS * HEAD_DIM])
    wv = _expand_kv_cols(attn_w_kv[0][:, N_KV_HEADS * HEAD_DIM:])
    wq_b, wk_b, wv_b = wq.astype(BF16), wk.astype(BF16), wv.astype(BF16)
    cos_t, sin_t = _rope_tables()
    q_c, k_c, v_c, kv_c = _qkv_ctx(x, mod4, 2, g_mix[2], wq_b, wk_b, wv_b, attn_w_kv[0].astype(BF16))
    q_l, k_l, v_l = _qkv_lat(x, mod4, 2, g_mix[2], wq_b, _rot_half_cols(wq).astype(BF16), wk_b,
                             _rot_half_cols(wk).astype(BF16), wv_b, cos_t, sin_t)
    sink = attn_sink[0]
    o_c = _attn_ctx(sink, q_c, k_c, v_c)
    expand = lambda a: jnp.broadcast_to(
        a.reshape(DEC_BATCH * PAST_LEN, N_KV_HEADS, 1, HEAD_DIM),
        (DEC_BATCH * PAST_LEN, N_KV_HEADS, Q_PER_KV, HEAD_DIM)).reshape(DEC_BATCH * PAST_LEN, D).astype(BF16)
    o_l = _attn_lat(sink, q_l, k_l, v_l, expand(cache_k[:, 0]), expand(cache_v[:, 0]))
    x = _resproj(x, jnp.concatenate([o_c, o_l], axis=0), mod4, 2, attn_w_o[0].astype(BF16))
    kvw = N_KV_HEADS * HEAD_DIM
    new_k = kv_c[:, 0:kvw].reshape(BATCH, 1, SEQ, N_KV_HEADS, HEAD_DIM)
    new_v = kv_c[:, kvw:].reshape(BATCH, 1, SEQ, N_KV_HEADS, HEAD_DIM)
    x = _dense_ffn(x, mod4, 2, g_ffn[2], ffn_w_gu[1].astype(BF16), ffn_w_down[1].astype(BF16))

    cs, fl = _fnet_tables()
    x = _fnet_mixer(x, mod4, 3, g_mix[3], cs, fl, fnet_w_out[0].astype(BF16))
    x, _ = _moe_layer(x, mod4, 3, g_ffn[3], moe_w_router[1], moe_w_gu, moe_w_down, 1, g_final, True, xs_buf)

    y_prompt = x[0:T_CTX].reshape(BATCH, SEQ, D)
    y_sample = x[T_CTX:].reshape(DEC_BATCH, DEC_SEQ, D)
    return (y_prompt, y_sample, new_k, new_v, new_re[:, None], new_im[:, None])
```

```python
import functools
import math

import numpy as np
import jax
import jax.numpy as jnp
from jax import lax
from jax.experimental import pallas as pl
from jax.experimental.pallas import tpu as pltpu

F32 = jnp.float32
BF16 = jnp.bfloat16
I32 = jnp.int32

D = 1024
BATCH = 16
SEQ = 256
DEPTH = 4
DEC_BATCH = 8
DEC_SEQ = 1024
PAST_LEN = 256
GRID_W = 64
EPS = 1e-6
N_MOD = 6
S5_GROUP = 16
S5_GROUPS = D // S5_GROUP
S5_STATE = 64
HEAD_DIM = 64
N_HEADS = D // HEAD_DIM
N_KV_HEADS = 4
Q_PER_KV = N_HEADS // N_KV_HEADS
WINDOW = 128
ROPE_THETA = 10000.0
ROPE_FREQS = HEAD_DIM // 4
FNET_GROUPS = 4
D_FF = 2816
N_EXPERTS = 8
TOP_K = 2
D_FF_EXPERT = 3584
NEG_INF = -1e30

T_CTX = BATCH * SEQ
T_LAT = DEC_BATCH * DEC_SEQ
T_ALL = T_CTX + T_LAT

VMEM_LIMIT_V7X = 56 * 1024 * 1024
LANES = 128

TM = 1024
ROW_CHUNK = 256
MM_CHUNK = 512
TME = 1024
N_EXPERT_TILES = (T_ALL * TOP_K) // TME + N_EXPERTS
P_MAX = N_EXPERT_TILES * TME


def _cparams(sem):
    return pltpu.CompilerParams(dimension_semantics=sem, vmem_limit_bytes=VMEM_LIMIT_V7X)


def _dot(a, b):
    return jnp.dot(a, b, preferred_element_type=F32)


def _dot_nt(a, b):
    return lax.dot_general(a, b, (((1,), (1,)), ((), ())), preferred_element_type=F32)


def _split_bf16(a):
    hi = a.astype(BF16)
    lo = (a - hi.astype(F32)).astype(BF16)
    return hi, lo


def _tile_rows(a, reps, axis=0):
    assert axis == 0
    return jnp.concatenate([a] * reps, axis=0)


def _norm_mod(x, g, sc, sh):
    ms = jnp.mean(x * x, axis=-1, keepdims=True)
    y = x * lax.rsqrt(ms + EPS) * g
    return y * (1.0 + sc) + sh


def _mod_row(i, tm):
    nct = T_CTX // tm
    lpb = DEC_SEQ // tm
    return jnp.where(i < nct, 0, 8 + (i - nct) // lpb)


def _mod_spec(layer, tm):
    return pl.BlockSpec((None, None, 1, N_MOD * D), lambda i, *_: (layer, _mod_row(i, tm), 0, 0))


def _row_spec():
    return pl.BlockSpec((1, D), lambda *_: (0, 0))


def _full_spec(shape):
    nd = len(shape)
    return pl.BlockSpec(shape, lambda *_: (0,) * nd, pipeline_mode=pl.Buffered(1))


def _mod_kernel(c_ref, w_ref, b_ref, o_ref):
    c = c_ref[...]
    s = (c * jax.nn.sigmoid(c)).astype(BF16)
    o_ref[...] = _dot(s, w_ref[...].astype(BF16)) + b_ref[...]


def _mod_all(cond16, w_ada, b_ada):
    tn = 1024
    return pl.pallas_call(
        _mod_kernel,
        grid=(DEPTH, N_MOD * D // tn),
        in_specs=[pl.BlockSpec((16, D), lambda l, n: (0, 0)),
                  pl.BlockSpec((None, D, tn), lambda l, n: (l, 0, n)),
                  pl.BlockSpec((None, 1, tn), lambda l, n: (l, 0, n))],
        out_specs=pl.BlockSpec((None, 16, tn), lambda l, n: (l, 0, n)),
        out_shape=jax.ShapeDtypeStruct((DEPTH, 16, N_MOD * D), F32),
        compiler_params=_cparams(("arbitrary", "arbitrary")),
        name="adaln_mod",
    )(cond16, w_ada, b_ada.reshape(DEPTH, 1, N_MOD * D))


def _conv_kernel(xc_ref, xl_ref, mod_ref, g_ref, win_ref, cw_ref, wout_ref, o_ref, gb_s, u_s, z_s):
    i = pl.program_id(0)
    is_ctx = i < (T_CTX // TM)

    def x_rows(r0, n):
        return jnp.where(is_ctx, xc_ref[r0:r0 + n, :], xl_ref[r0:r0 + n, :])

    sh = mod_ref[:, 0:D]
    sc = mod_ref[:, D:2 * D]
    gate = mod_ref[:, 2 * D:3 * D]
    g = g_ref[...]
    zero8 = jnp.zeros((8, D), F32)
    u_s[0:8, :] = zero8
    u_s[8 + TM:16 + TM, :] = zero8
    for c in range(TM // MM_CHUNK):
        r0 = c * MM_CHUNK
        h = _norm_mod(x_rows(r0, MM_CHUNK), g, sc, sh).astype(BF16)
        proj = _dot(h, win_ref[...])
        gb_s[r0:r0 + MM_CHUNK, :] = proj[:, 0:D]
        u_s[8 + r0:8 + r0 + MM_CHUNK, :] = proj[:, D:2 * D] * proj[:, 2 * D:3 * D]
    row = lax.broadcasted_iota(I32, (ROW_CHUNK, 1), 0)
    first = jnp.logical_and(is_ctx, row == 0)
    last = jnp.logical_and(is_ctx, row == ROW_CHUNK - 1)
    for c in range(TM // ROW_CHUNK):
        r0 = c * ROW_CHUNK
        up = jnp.where(first, 0.0, u_s[7 + r0:7 + r0 + ROW_CHUNK, :])
        mid = u_s[8 + r0:8 + r0 + ROW_CHUNK, :]
        dn = jnp.where(last, 0.0, u_s[9 + r0:9 + r0 + ROW_CHUNK, :])
        conv = up * cw_ref[0:1, :] + mid * cw_ref[1:2, :] + dn * cw_ref[2:3, :]
        z_s[r0:r0 + ROW_CHUNK, :] = (gb_s[r0:r0 + ROW_CHUNK, :] * conv).astype(BF16)
    for c in range(TM // MM_CHUNK):
        r0 = c * MM_CHUNK
        y = _dot(z_s[r0:r0 + MM_CHUNK, :], wout_ref[...])
        o_ref[r0:r0 + MM_CHUNK, :] = x_rows(r0, MM_CHUNK) + gate * y


def _conv_mixer(x_ctx, x_lat, mod4, layer, g_mix, w_in, conv_w, w_out):
    assert SEQ == ROW_CHUNK and DEC_SEQ == TM
    ctx, lat = _ctx_lat_specs(TM)
    return pl.pallas_call(
        _conv_kernel,
        grid=(T_ALL // TM,),
        in_specs=[ctx, lat, _mod_spec(layer, TM), _row_spec(),
                  _full_spec((D, 3 * D)), _full_spec((3, D)), _full_spec((D, D))],
        out_specs=pl.BlockSpec((TM, D), lambda i: (i, 0)),
        out_shape=jax.ShapeDtypeStruct((T_ALL, D), F32),
        scratch_shapes=[pltpu.VMEM((TM, D), F32), pltpu.VMEM((TM + 16, D), F32), pltpu.VMEM((TM, D), BF16)],
        compiler_params=_cparams(("arbitrary",)),
        name="conv_mixer",
    )(x_ctx, x_lat, mod4, g_mix.reshape(1, D), w_in, conv_w, w_out)


FFN_TM = 512
MXU_WIDTH_V7X = 256
FFN_SPLITS = ((0, 6 * MXU_WIDTH_V7X), (6 * MXU_WIDTH_V7X, D_FF))
assert D_FF % MXU_WIDTH_V7X == 0


def _ffn_kernel(x_ref, mod_ref, g_ref, wgu_ref, wd_ref, o_ref):
    sh = mod_ref[:, 3 * D:4 * D]
    sc = mod_ref[:, 4 * D:5 * D]
    gate = mod_ref[:, 5 * D:6 * D]
    x = x_ref[...]
    h = _norm_mod(x, g_ref[...], sc, sh).astype(BF16)
    acc = None
    for lo, hi in FFN_SPLITS:
        gg = _dot(h, wgu_ref[:, lo:hi])
        uu = _dot(h, wgu_ref[:, D_FF + lo:D_FF + hi])
        a = (gg * jax.nn.sigmoid(gg) * uu).astype(BF16)
        contrib = _dot(a, wd_ref[lo:hi, :])
        acc = contrib if acc is None else acc + contrib
    o_ref[...] = x + gate * acc


def _dense_ffn(x, mod4, layer, g_ffn, w_gu, w_down):
    tok = pl.BlockSpec((FFN_TM, D), lambda i: (i, 0))
    return pl.pallas_call(
        _ffn_kernel,
        grid=(T_ALL // FFN_TM,),
        in_specs=[tok, _mod_spec(layer, FFN_TM), _row_spec(),
                  _full_spec((D, 2 * D_FF)), _full_spec((D_FF, D))],
        out_specs=tok,
        out_shape=jax.ShapeDtypeStruct((T_ALL, D), F32),
        compiler_params=_cparams(("arbitrary",)),
        name="dense_swiglu",
    )(x, mod4, g_ffn.reshape(1, D), w_gu, w_down)


def _s5_prep_kernel(lr_ref, li_ref, ldt_ref, br_ref, bi_ref, lbr_ref, lbi_ref, bbr_ref, bbi_ref):
    lr = lr_ref[...]
    li = li_ref[...]
    dt = jnp.exp(ldt_ref[...])
    mag = jnp.exp(lr * dt)
    ar = mag * jnp.cos(li * dt)
    ai = mag * jnp.sin(li * dt)
    nr = ar - 1.0
    den = lr * lr + li * li
    fr = (nr * lr + ai * li) / den
    fi = (ai * lr - nr * li) / den
    br = br_ref[...]
    bi = bi_ref[...]
    lbr_ref[...] = ar
    lbi_ref[...] = ai
    bbr_ref[...] = fr * br - fi * bi
    bbi_ref[...] = fr * bi + fi * br


def _s5_prep(lam_re, lam_im, log_dt, b_re, b_im):
    rows = 2 * S5_GROUPS
    cols = S5_STATE * S5_GROUP
    exp = lambda a: jnp.repeat(a.reshape(rows, S5_STATE), S5_GROUP, axis=1)
    ldt = jnp.broadcast_to(log_dt.reshape(rows, 1), (rows, cols))
    outs = pl.pallas_call(
        _s5_prep_kernel,
        out_shape=[jax.ShapeDtypeStruct((rows, cols), F32)] * 4,
        name="s5_discretize",
    )(exp(lam_re), exp(lam_im), ldt, b_re.reshape(rows, cols), b_im.reshape(rows, cols))
    lbr, lbi, bbr, bbi = outs
    shp = (2, S5_GROUPS, S5_STATE, S5_GROUP)
    return lbr.reshape(shp)[..., 0], lbi.reshape(shp)[..., 0], bbr.reshape(shp), bbi.reshape(shp)


S5_JT = 8
S5_GPT = LANES // S5_GROUP
S5_HALF = S5_GPT * S5_STATE
S5_ROWS = 512


def _s5_scan_kernel(x_ref, mod_ref, g_ref, wb_ref, wc_ref, lam_ref, h0_ref,
                    y_ref, fin_ref, bu_s, st_s, *, nb, jgroup):
    d = pl.program_id(0)
    c = pl.program_id(1)
    lc = S5_ROWS // nb

    @pl.when(c == 0)
    def _():
        st_s[...] = h0_ref[...]

    rep = S5_ROWS // 8
    sh = _tile_rows(mod_ref[:, 0:D], rep, axis=0)
    sc = _tile_rows(mod_ref[:, D:2 * D], rep, axis=0)
    u = _norm_mod(x_ref[...], g_ref[...], sc, sh).astype(BF16)
    for j in range(S5_JT):
        bu_s[j] = _dot(u[:, j * LANES:(j + 1) * LANES], wb_ref[j])

    for j0 in range(0, S5_JT, jgroup):
        js = list(range(j0, j0 + jgroup))
        lam = [(jnp.broadcast_to(lam_ref[j][:, 0:S5_HALF], (nb, S5_HALF)),
                jnp.broadcast_to(lam_ref[j][:, S5_HALF:], (nb, S5_HALF))) for j in js]

        def body(t, carry):
            l = jnp.where(d == 0, t, lc - 1 - t)
            r0 = pl.multiple_of(l * nb, nb)
            out = []
            for k, j in enumerate(js):
                sr, si = carry[k]
                ar, ai = lam[k]
                bu = bu_s[j, pl.ds(r0, nb), :]
                hr = ar * sr - ai * si + bu[:, 0:S5_HALF]
                hi = ar * si + ai * sr + bu[:, S5_HALF:]
                bu_s[j, pl.ds(r0, nb), 0:S5_HALF] = hr
                bu_s[j, pl.ds(r0, nb), S5_HALF:] = hi
                out.append((hr, hi))
            return tuple(out)

        init = tuple((st_s[j][:, 0:S5_HALF], st_s[j][:, S5_HALF:]) for j in js)
        fin = lax.fori_loop(0, lc, body, init)
        for k, j in enumerate(js):
            st_s[j, :, 0:S5_HALF] = fin[k][0]
            st_s[j, :, S5_HALF:] = fin[k][1]

    for j in range(S5_JT):
        y_ref[:, j * LANES:(j + 1) * LANES] = _dot(bu_s[j].astype(BF16), wc_ref[j])
    fin_ref[...] = st_s[...]


def _s5_scan(xt, nb, mod3, layer, path, g_mix, w_b, w_c, lam_s, h0):
    rows = xt.shape[0]
    nc = rows // S5_ROWS
    chunk = lambda d, c: c + d * (nc - 1 - 2 * c)
    kern = functools.partial(_s5_scan_kernel, nb=nb, jgroup=2 if nb == 8 else 1)
    return pl.pallas_call(
        kern,
        grid=(2, nc),
        in_specs=[pl.BlockSpec((S5_ROWS, D), lambda d, c: (chunk(d, c), 0)),
                  pl.BlockSpec((None, 8, N_MOD * D), lambda d, c: (layer, path, 0)),
                  _row_spec(),
                  pl.BlockSpec((None, S5_JT, LANES, 2 * S5_HALF), lambda d, c: (d, 0, 0, 0)),
                  pl.BlockSpec((None, S5_JT, 2 * S5_HALF, LANES), lambda d, c: (d, 0, 0, 0)),
                  pl.BlockSpec((None, S5_JT, 1, 2 * S5_HALF), lambda d, c: (d, 0, 0, 0)),
                  pl.BlockSpec((None, S5_JT, nb, 2 * S5_HALF), lambda d, c: (d, 0, 0, 0))],
        out_specs=[pl.BlockSpec((None, S5_ROWS, D), lambda d, c: (d, chunk(d, c), 0)),
                   pl.BlockSpec((None, S5_JT, nb, 2 * S5_HALF), lambda d, c: (d, 0, 0, 0))],
        out_shape=[jax.ShapeDtypeStruct((2, rows, D), F32),
                   jax.ShapeDtypeStruct((2, S5_JT, nb, 2 * S5_HALF), F32)],
        scratch_shapes=[pltpu.VMEM((S5_JT, S5_ROWS, 2 * S5_HALF), F32),
                        pltpu.VMEM((S5_JT, nb, 2 * S5_HALF), F32)],
        compiler_params=_cparams(("arbitrary", "arbitrary")),
        name="s5_scan_b%d" % nb,
    )(xt, mod3, g_mix.reshape(1, D), w_b, w_c, lam_s, h0)


S5_GLU_ROWS = 512


def _s5_glu_kernel(x_ref, yf_ref, yb_ref, mod_ref, g_ref, dsk_ref, w_ref, o_ref):
    rep = MM_CHUNK // 8
    sh = _tile_rows(mod_ref[:, 0:D], rep, axis=0)
    sc = _tile_rows(mod_ref[:, D:2 * D], rep, axis=0)
    gate = _tile_rows(mod_ref[:, 2 * D:3 * D], rep, axis=0)
    for c in range(S5_GLU_ROWS // MM_CHUNK):
        r0 = c * MM_CHUNK
        x = x_ref[r0:r0 + MM_CHUNK, :]
        u = _norm_mod(x, g_ref[...], sc, sh)
        y = u * dsk_ref[...] + yf_ref[r0:r0 + MM_CHUNK, :] + yb_ref[r0:r0 + MM_CHUNK, :]
        z = jax.nn.gelu(y).astype(BF16)
        ag = _dot(z, w_ref[...])
        out = ag[:, 0:D] * jax.nn.sigmoid(ag[:, D:2 * D])
        o_ref[r0:r0 + MM_CHUNK, :] = x + gate * out


def _s5_glu(xt, y2, mod3, layer, path, g_mix, d_skip, w_glu):
    rows = xt.shape[0]
    return pl.pallas_call(
        _s5_glu_kernel,
        grid=(rows // S5_GLU_ROWS,),
        in_specs=[pl.BlockSpec((S5_GLU_ROWS, D), lambda i: (i, 0)),
                  pl.BlockSpec((None, S5_GLU_ROWS, D), lambda i: (0, i, 0)),
                  pl.BlockSpec((None, S5_GLU_ROWS, D), lambda i: (1, i, 0)),
                  pl.BlockSpec((None, 8, N_MOD * D), lambda i: (layer, path, 0)),
                  _row_spec(), _row_spec(), _full_spec((D, 2 * D))],
        out_specs=pl.BlockSpec((S5_GLU_ROWS, D), lambda i: (i, 0)),
        out_shape=jax.ShapeDtypeStruct((rows, D), F32),
        compiler_params=_cparams(("arbitrary",)),
        name="s5_glu",
    )(xt, y2, y2, mod3, g_mix.reshape(1, D), d_skip.reshape(1, D), w_glu)


def _s5_weights(lbr, lbi, bbr, bbi, c_re, c_im):
    eye = jnp.eye(S5_GPT, dtype=F32)
    bb = jnp.stack([bbr, bbi]).reshape(2, 2, S5_JT, S5_GPT, S5_STATE, S5_GROUP)
    w_bu = jnp.einsum('rdjgps,gh->djgsrhp', bb, eye).reshape(2, S5_JT, LANES, 2 * S5_HALF)
    w_b = w_bu.astype(BF16)
    cc =jnp.stack([c_re, -c_im]).reshape(2, 2, S5_JT, S5_GPT, S5_GROUP, S5_STATE)
    w_c = jnp.einsum('rdjgsp,gh->djrgphs', cc, eye).reshape(2, S5_JT, 2 * S5_HALF, LANES).astype(BF16)
    lam_s = jnp.concatenate([lbr.reshape(2, S5_JT, S5_HALF), lbi.reshape(2, S5_JT, S5_HALF)], axis=-1)
    return w_b, w_c, lam_s.reshape(2, S5_JT, 1, 2 * S5_HALF)


def _s5_state_in(st_re, st_im):
    def lay(a):
        b = a.shape[0]
        return a.transpose(1, 0, 2, 3).reshape(2, b, S5_JT, S5_HALF).transpose(0, 2, 1, 3)
    return jnp.concatenate([lay(st_re), lay(st_im)], axis=-1)


def _s5_state_out(fin):
    def lay(a):
        b = a.shape[2]
        return a.transpose(2, 0, 1, 3).reshape(b, 2, S5_GROUPS, S5_STATE)
    return lay(fin[..., 0:S5_HALF]), lay(fin[..., S5_HALF:])


QKV_TM = 512


def _qkv_ctx_kernel(x_ref, mod_ref, g_ref, wq_ref, wk_ref, wv_ref, wkv_ref, q_ref, k_ref, v_ref, kv_ref):
    sh = mod_ref[:, 0:D]
    sc = mod_ref[:, D:2 * D]
    for c in range(QKV_TM // MM_CHUNK):
        r0 = c * MM_CHUNK
        h = _norm_mod(x_ref[r0:r0 + MM_CHUNK, :], g_ref[...], sc, sh).astype(BF16)
        q_ref[r0:r0 + MM_CHUNK, :] = _dot(h, wq_ref[...]).astype(BF16)
        k_ref[r0:r0 + MM_CHUNK, :] = _dot(h, wk_ref[...]).astype(BF16)
        v_ref[r0:r0 + MM_CHUNK, :] = _dot(h, wv_ref[...]).astype(BF16)
        kv_ref[r0:r0 + MM_CHUNK, :] = _dot(h, wkv_ref[...])


def _qkv_ctx(x, mod4, layer, g_mix, wq, wk, wv, wkv):
    bf = jax.ShapeDtypeStruct((T_CTX, D), BF16)
    tok = pl.BlockSpec((QKV_TM, D), lambda i: (i, 0))
    kvw = wkv.shape[1]
    return pl.pallas_call(
        _qkv_ctx_kernel,
        grid=(T_CTX // QKV_TM,),
        in_specs=[tok, _mod_spec(layer, QKV_TM), _row_spec(),
                  _full_spec((D, D)), _full_spec((D, D)), _full_spec((D, D)), _full_spec((D, kvw))],
        out_specs=[tok, tok, tok, pl.BlockSpec((QKV_TM, kvw), lambda i: (i, 0))],
        out_shape=[bf, bf, bf, jax.ShapeDtypeStruct((T_CTX, kvw), F32)],
        compiler_params=_cparams(("arbitrary",)),
        name="qkv_ctx",
    )(x, mod4, g_mix.reshape(1, D), wq, wk, wv, wkv)


def _qkv_lat_kernel(x_ref, mod_ref, g_ref, wq_ref, wqr_ref, wk_ref, wkr_ref, wv_ref, cos_ref, sin_ref,
                    q_ref, k_ref, v_ref):
    sh = mod_ref[:, 0:D]
    sc = mod_ref[:, D:2 * D]
    for c in range(QKV_TM // MM_CHUNK):
        r0 = c * MM_CHUNK
        h = _norm_mod(x_ref[r0:r0 + MM_CHUNK, :], g_ref[...], sc, sh).astype(BF16)
        cos = cos_ref[r0:r0 + MM_CHUNK, :]
        sin = sin_ref[r0:r0 + MM_CHUNK, :]
        q_ref[r0:r0 + MM_CHUNK, :] = (_dot(h, wq_ref[...]) * cos + _dot(h, wqr_ref[...]) * sin).astype(BF16)
        k_ref[r0:r0 + MM_CHUNK, :] = (_dot(h, wk_ref[...]) * cos + _dot(h, wkr_ref[...]) * sin).astype(BF16)
        v_ref[r0:r0 + MM_CHUNK, :] = _dot(h, wv_ref[...]).astype(BF16)


def _qkv_lat(x, mod4, layer, g_mix, wq, wq_rot, wk, wk_rot, wv, cos_t, sin_t):
    nct = T_CTX // QKV_TM
    lpb = DEC_SEQ // QKV_TM
    bf = jax.ShapeDtypeStruct((T_LAT, D), BF16)
    tok_out = pl.BlockSpec((QKV_TM, D), lambda i: (i, 0))
    rope = pl.BlockSpec((QKV_TM, D), lambda i: (i % lpb, 0))
    return pl.pallas_call(
        _qkv_lat_kernel,
        grid=(T_LAT // QKV_TM,),
        in_specs=[pl.BlockSpec((QKV_TM, D), lambda i: (i + nct, 0)),
                  pl.BlockSpec((None, None, 1, N_MOD * D), lambda i: (layer, 8 + i // lpb, 0, 0)),
                  _row_spec()] + [_full_spec((D, D))] * 5 + [rope, rope],
        out_specs=[tok_out, tok_out, tok_out],
        out_shape=[bf, bf, bf],
        compiler_params=_cparams(("arbitrary",)),
        name="qkv_lat",
    )(x, mod4, g_mix.reshape(1, D), wq, wq_rot, wk, wk_rot, wv, cos_t, sin_t)


KVW = Q_PER_KV * HEAD_DIM


def _head_masks(rows):
    lane = lax.broadcasted_iota(I32, (rows, KVW), 1)
    return [jnp.logical_and(lane >= g * HEAD_DIM, lane < (g + 1) * HEAD_DIM) for g in range(Q_PER_KV)]


def _attn_ctx_kernel(sink_ref, q_ref, k_ref, v_ref, o_ref):
    scale = HEAD_DIM ** -0.5
    masks = _head_masks(SEQ)
    for kv in range(N_KV_HEADS):
        c0 = kv * KVW
        q = q_ref[:, c0:c0 + KVW]
        k = k_ref[:, c0:c0 + KVW]
        v = v_ref[:, c0:c0 + KVW]
        acc = jnp.zeros((SEQ, KVW), F32)
        for g in range(Q_PER_KV):
            sink = sink_ref[kv * Q_PER_KV + g]
            qg = jnp.where(masks[g], q, jnp.zeros_like(q))
            s = _dot_nt(qg, k) * scale
            m = jnp.maximum(jnp.max(s, axis=-1, keepdims=True), sink)
            e = jnp.exp(s - m)
            den = jnp.sum(e, axis=-1, keepdims=True) + jnp.exp(sink - m)
            og = _dot(e.astype(BF16), v) / den
            acc = jnp.where(masks[g], og, acc)
        o_ref[:, c0:c0 + KVW] = acc.astype(BF16)


def _attn_ctx(sink, q, k, v):
    tok = pl.BlockSpec((SEQ, D), lambda b, *_: (b, 0))
    return pl.pallas_call(
        _attn_ctx_kernel,
        grid_spec=pltpu.PrefetchScalarGridSpec(
            num_scalar_prefetch=1, grid=(BATCH,),
            in_specs=[tok, tok, tok], out_specs=tok),
        out_shape=jax.ShapeDtypeStruct((T_CTX, D), BF16),
        compiler_params=_cparams(("arbitrary",)),
        name="attn_ctx",
    )(sink, q, k, v)


ATT_TQ = 128
ATT_SPAN = ATT_TQ + 2 * WINDOW


def _attn_lat_kernel(sink_ref, q_ref, k_ref, v_ref, ck_ref, cv_ref, o_ref):
    qb = pl.program_id(1)
    scale = HEAD_DIM ** -0.5
    w0 = pl.multiple_of(jnp.clip(qb * ATT_TQ - WINDOW, 0, DEC_SEQ - ATT_SPAN), ATT_TQ)
    rows = Q_PER_KV * ATT_TQ
    ridx = lax.broadcasted_iota(I32, (rows, ATT_SPAN), 0)
    qpos = qb * ATT_TQ + (ridx & (ATT_TQ - 1))
    kpos = w0 + lax.broadcasted_iota(I32, (rows, ATT_SPAN), 1)
    valid = jnp.abs(qpos - kpos) <= WINDOW
    rcol = lax.broadcasted_iota(I32, (rows, 1), 0)
    masks = _head_masks(ATT_TQ)
    for kv in range(N_KV_HEADS):
        c0 = kv * KVW
        q = q_ref[:, c0:c0 + KVW]
        qs = jnp.concatenate([jnp.where(masks[g], q, jnp.zeros_like(q)) for g in range(Q_PER_KV)], axis=0)
        sink = jnp.zeros((rows, 1), F32)
        for g in range(Q_PER_KV):
            sink = jnp.where(rcol >= g * ATT_TQ, sink_ref[kv * Q_PER_KV + g], sink)
        s_ctx = _dot_nt(qs, ck_ref[:, c0:c0 + KVW]) * scale
        s_win = _dot_nt(qs, k_ref[pl.ds(w0, ATT_SPAN), c0:c0 + KVW]) * scale
        s_win = jnp.where(valid, s_win, NEG_INF)
        m = jnp.maximum(jnp.maximum(jnp.max(s_ctx, axis=-1, keepdims=True),
                                    jnp.max(s_win, axis=-1, keepdims=True)), sink)
        e_ctx = jnp.exp(s_ctx - m)
        e_win = jnp.exp(s_win - m)
        den = (jnp.exp(sink - m) + jnp.sum(e_ctx, axis=-1, keepdims=True)
               + jnp.sum(e_win, axis=-1, keepdims=True))
        o = (_dot(e_ctx.astype(BF16), cv_ref[:, c0:c0 + KVW])
             + _dot(e_win.astype(BF16), v_ref[pl.ds(w0, ATT_SPAN), c0:c0 + KVW])) / den
        acc = jnp.zeros((ATT_TQ, KVW), F32)
        for g in range(Q_PER_KV):
            acc = jnp.where(masks[g], o[g * ATT_TQ:(g + 1) * ATT_TQ, :], acc)
        o_ref[:, c0:c0 + KVW] = acc.astype(BF16)


def _attn_lat(sink, q, k, v, ck, cv):
    nqb = DEC_SEQ // ATT_TQ
    qspec = pl.BlockSpec((ATT_TQ, D), lambda b, i, *_: (b * nqb + i, 0))
    seq = pl.BlockSpec((DEC_SEQ, D), lambda b, i, *_: (b, 0))
    ctx = pl.BlockSpec((PAST_LEN, D), lambda b, i, *_: (b, 0))
    return pl.pallas_call(
        _attn_lat_kernel,
        grid_spec=pltpu.PrefetchScalarGridSpec(
            num_scalar_prefetch=1, grid=(DEC_BATCH, nqb),
            in_specs=[qspec, seq, seq, ctx, ctx], out_specs=qspec),
        out_shape=jax.ShapeDtypeStruct((T_LAT, D), BF16),
        compiler_params=_cparams(("arbitrary", "arbitrary")),
        name="attn_lat",
    )(sink, q, k, v, ck, cv)


def _ctx_lat_specs(tm):
    nct = T_CTX // tm
    ctx = pl.BlockSpec((tm, D), lambda i, *_: (jnp.minimum(i, nct - 1), 0))
    lat = pl.BlockSpec((tm, D), lambda i, *_: (jnp.maximum(i - nct, 0), 0))
    return ctx, lat


def _resproj_kernel(x_ref, ac_ref, al_ref, mod_ref, w_ref, o_ref):
    gate = mod_ref[:, 2 * D:3 * D]
    is_ctx = pl.program_id(0) < T_CTX // TM
    a = jnp.where(is_ctx, ac_ref[...], al_ref[...])
    o_ref[...] = x_ref[...] + gate * _dot(a, w_ref[...])


def _resproj(x, a_ctx, a_lat, mod4, layer, w):
    tok = pl.BlockSpec((TM, D), lambda i: (i, 0))
    ctx, lat = _ctx_lat_specs(TM)
    return pl.pallas_call(
        _resproj_kernel,
        grid=(T_ALL // TM,),
        in_specs=[tok, ctx, lat, _mod_spec(layer, TM), _full_spec((D, D))],
        out_specs=tok,
        out_shape=jax.ShapeDtypeStruct((T_ALL, D), F32),
        compiler_params=_cparams(("arbitrary",)),
        name="attn_out_proj",
    )(x, a_ctx, a_lat, mod4, w)


def _rope_tables():
    rows = DEC_SEQ // GRID_W
    row = jnp.repeat(jnp.arange(rows), GRID_W).astype(F32)
    col = jnp.tile(jnp.arange(GRID_W), rows).astype(F32)
    inv = ROPE_THETA ** (-jnp.arange(ROPE_FREQS, dtype=F32) / ROPE_FREQS)
    ang = jnp.concatenate([row[:, None] * inv, col[:, None] * inv], axis=-1)
    cos = jnp.cos(ang)
    sin = jnp.sin(ang)
    cos_h = jnp.concatenate([cos, cos], axis=-1)
    sin_h = jnp.concatenate([sin, sin], axis=-1)
    return jnp.tile(cos_h, (1, N_HEADS)), jnp.tile(sin_h, (1, N_HEADS))


def _rot_half_cols(w):
    k = w.shape[0]
    w4 = w.reshape(k, -1, 2, HEAD_DIM // 2)
    return jnp.stack([-w4[:, :, 1], w4[:, :, 0]], axis=2).reshape(k, -1)


def _expand_kv_cols(w):
    k = w.shape[0]
    w3 = w.reshape(k, N_KV_HEADS, 1, HEAD_DIM)
    return jnp.broadcast_to(w3, (k, N_KV_HEADS, Q_PER_KV, HEAD_DIM)).reshape(k, N_HEADS * HEAD_DIM)


FG = D // FNET_GROUPS


def _fnet_kernel(x_ref, mod_ref, g_ref, cs_ref, fl_ref, w_ref, o_ref, h_s, ab_s):
    sh = mod_ref[:, 0:D]
    sc = mod_ref[:, D:2 * D]
    gate = mod_ref[:, 2 * D:3 * D]
    for c in range(TM // ROW_CHUNK):
        r0 = c * ROW_CHUNK
        h_s[r0:r0 + ROW_CHUNK, :] = _norm_mod(x_ref[r0:r0 + ROW_CHUNK, :], g_ref[...], sc, sh).astype(BF16)
    for g in range(FNET_GROUPS):
        ab = _dot(h_s[:, g * FG:(g + 1) * FG], cs_ref[...])
        ab_s[0:TM, g * FG:(g + 1) * FG] = ab[:, 0:FG].astype(BF16)
        ab_s[TM:2 * TM, g * FG:(g + 1) * FG] = ab[:, FG:2 * FG].astype(BF16)
    for c in range(TM // MM_CHUNK):
        r0 = c * MM_CHUNK
        f = _dot(fl_ref[r0:r0 + MM_CHUNK, :], ab_s[...])
        y = _dot(f.astype(BF16), w_ref[...])
        o_ref[r0:r0 + MM_CHUNK, :] = x_ref[r0:r0 + MM_CHUNK, :] + gate * y


def _fnet_mixer(x, mod4, layer, g_mix, cs, fl, w_out):
    nct = T_CTX // TM
    tok = pl.BlockSpec((TM, D), lambda i: (i, 0))
    return pl.pallas_call(
        _fnet_kernel,
        grid=(T_ALL // TM,),
        in_specs=[tok, _mod_spec(layer, TM), _row_spec(), _full_spec((FG, 2 * FG)),
                  pl.BlockSpec((None, TM, 2 * TM), lambda i: (jnp.where(i < nct, 0, 1), 0, 0)),
                  _full_spec((D, D))],
        out_specs=tok,
        out_shape=jax.ShapeDtypeStruct((T_ALL, D), F32),
        scratch_shapes=[pltpu.VMEM((TM, D), BF16), pltpu.VMEM((2 * TM, D), BF16)],
        compiler_params=_cparams(("arbitrary",)),
        name="fnet_mixer",
    )(x, mod4, g_mix.reshape(1, D), cs, fl, w_out)


def _dft_cos_sin(n):
    k = np.arange(n)
    ang = 2.0 * np.pi * ((k[:, None] * k[None, :]) % n) / n
    return np.cos(ang), np.sin(ang)


def _fnet_tables():
    cc, sc = _dft_cos_sin(FG)
    cs = np.concatenate([cc, sc], axis=1) / math.sqrt(FG)
    mats = []
    for seq in (SEQ, DEC_SEQ):
        cl, sl = _dft_cos_sin(seq)
        reps = TM // seq
        eye = np.eye(reps)
        mats.append(np.concatenate([np.kron(eye, cl), -np.kron(eye, sl)], axis=1) / math.sqrt(seq))
    return jnp.asarray(cs, F32).astype(BF16), jnp.asarray(np.stack(mats), F32).astype(BF16)


ROUTER_TM = 512
TOK_SUB = D // LANES


def _store_token_tiles(ref, r0, val):
    rows = val.shape[0]
    for c in range(TOK_SUB):
        ref[pl.ds(r0 * TOK_SUB + c, rows, stride=TOK_SUB), :] = val[:, c * LANES:(c + 1) * LANES]


def _load_token_tiles(ref, r0, rows, c, lead=None):
    idx = pl.ds(r0 * TOK_SUB + c, rows, stride=TOK_SUB)
    return ref[idx, :] if lead is None else ref[lead, idx, :]


def _router_kernel(x_ref, mod_ref, g_ref, wr_ref, h_ref, info_ref, gates_ref):
    sh = mod_ref[:, 3 * D:4 * D]
    sc = mod_ref[:, 4 * D:5 * D]
    w_hi = wr_ref[0]
    w_lo = wr_ref[1]
    for c in range(ROUTER_TM // ROW_CHUNK):
        r0 = c * ROW_CHUNK
        h = _norm_mod(x_ref[r0:r0 + ROW_CHUNK, :], g_ref[...], sc, sh)
        _store_token_tiles(h_ref, r0, h)
        h_hi, h_lo = _split_bf16(h)
        logits = _dot(h_hi, w_hi) + _dot(h_lo, w_hi) + _dot(h_hi, w_lo)
        lane = lax.broadcasted_iota(I32, logits.shape, 1)
        lg = jnp.where(lane < N_EXPERTS, logits, -jnp.inf)
        m1 = jnp.max(lg, axis=-1, keepdims=True)
        i1 = jnp.min(jnp.where(lg == m1, lane, LANES), axis=-1, keepdims=True)
        lg2 = jnp.where(lane == i1, -jnp.inf, lg)
        m2 = jnp.max(lg2, axis=-1, keepdims=True)
        i2 = jnp.min(jnp.where(lg2 == m2, lane, LANES), axis=-1, keepdims=True)
        e2 = jnp.exp(m2 - m1)
        den = 1.0 + e2
        gates_ref[r0:r0 + ROW_CHUNK, :] = jnp.where(lane == 0, 1.0 / den, jnp.where(lane == 1, e2 / den, 0.0))
        info_ref[r0:r0 + ROW_CHUNK, :] = jnp.where(lane == 0, i1, jnp.where(lane == 1, i2, 0)).astype(I32)


def _router(x, mod4, layer, g_ffn, w_router):
    wr = jnp.zeros((D, LANES), F32).at[:, 0:N_EXPERTS].set(w_router)
    wr_hi = wr.astype(BF16)
    wr_lo = (wr - wr_hi.astype(F32)).astype(BF16)
    tok = pl.BlockSpec((ROUTER_TM, D), lambda i: (i, 0))
    nar = pl.BlockSpec((ROUTER_TM, LANES), lambda i: (i, 0))
    return pl.pallas_call(
        _router_kernel,
        grid=(T_ALL // ROUTER_TM,),
        in_specs=[tok, _mod_spec(layer, ROUTER_TM), _row_spec(), _full_spec((2, D, LANES))],
        out_specs=[pl.BlockSpec((ROUTER_TM * TOK_SUB, LANES), lambda i: (i, 0)), nar, nar],
        out_shape=[jax.ShapeDtypeStruct((T_ALL * TOK_SUB, LANES), F32),
                   jax.ShapeDtypeStruct((T_ALL, LANES), I32),
                   jax.ShapeDtypeStruct((T_ALL, LANES), F32)],
        compiler_params=_cparams(("arbitrary",)),
        name="moe_router",
    )(x, mod4, g_ffn.reshape(1, D), jnp.stack([wr_hi, wr_lo]))


def _route_plan(info):
    eid = jnp.arange(N_EXPERTS, dtype=I32)[None, :]
    first = info[:, 0:1] == eid
    second = info[:, 1:2] == eid
    onehot = jnp.logical_or(first, second).astype(I32)
    csum = jnp.cumsum(onehot, axis=0)
    rank = csum - onehot
    counts = csum[-1]
    padded = ((counts + TME - 1) // TME) * TME
    gend = jnp.cumsum(padded)
    gstart = gend - padded
    pos_te = gstart[None, :] + rank
    pos1 = jnp.sum(jnp.where(first, pos_te, 0), axis=1).astype(I32)
    pos2 = jnp.sum(jnp.where(second, pos_te, 0), axis=1).astype(I32)
    n_valid = gend[-1] // TME
    tile_start = jnp.arange(N_EXPERT_TILES, dtype=I32) * TME
    eff_start = jnp.minimum(tile_start, jnp.maximum(n_valid - 1, 0) * TME)
    tile_expert = jnp.minimum(jnp.sum((eff_start[:, None] >= gend[None, :]).astype(I32), axis=1),
                              N_EXPERTS - 1).astype(I32)
    valid_end = (gstart + counts)[tile_expert]
    tile_rows = jnp.where(tile_start < gend[-1], jnp.clip(valid_end - tile_start, 0, TME), 0).astype(I32)
    return pos1, pos2, tile_expert, n_valid.astype(I32).reshape(1), tile_rows


DISP_TM = 512
DMA_UNROLL = 8


def _dispatch_kernel(p1_ref, p2_ref, h_ref, xs_init, xs_hbm, sem):
    del xs_init
    i = pl.program_id(0)

    def body(g, carry):
        for k in range(DMA_UNROLL):
            r = g * DMA_UNROLL + k
            t = i * DISP_TM + r
            tile = h_ref.at[pl.ds(pl.multiple_of(r * TOK_SUB, TOK_SUB), TOK_SUB), :]
            for j, p_ref in enumerate((p1_ref, p2_ref)):
                dst = xs_hbm.at[pl.ds(pl.multiple_of(p_ref[t] * TOK_SUB, TOK_SUB), TOK_SUB), :]
                pltpu.make_async_copy(tile, dst, sem.at[j]).start(priority=j)
        return carry
    lax.fori_loop(0, DISP_TM // DMA_UNROLL, body, 0)
    for j in range(2):
        pltpu.make_async_copy(h_ref, xs_hbm.at[pl.ds(0, DISP_TM * TOK_SUB), :], sem.at[j]).wait()


def _dispatch(pos1, pos2, h, xs_init):
    return pl.pallas_call(
        _dispatch_kernel,
        grid_spec=pltpu.PrefetchScalarGridSpec(
            num_scalar_prefetch=2, grid=(T_ALL // DISP_TM,),
            in_specs=[pl.BlockSpec((DISP_TM * TOK_SUB, LANES), lambda i, *_: (i, 0)),
                      pl.BlockSpec(memory_space=pl.ANY)],
            out_specs=pl.BlockSpec(memory_space=pl.ANY),
            scratch_shapes=[pltpu.SemaphoreType.DMA((2,))]),
        out_shape=jax.ShapeDtypeStruct((P_MAX * TOK_SUB, LANES), F32),
        input_output_aliases={3: 0},
        compiler_params=_cparams(("arbitrary",)),
        name="moe_dispatch",
    )(pos1, pos2, h, xs_init)


EXP_TF = 512
EXP_NF = D_FF_EXPERT // EXP_TF


def _expert_kernel(te_ref, nv_ref, rows_ref, xs_ref, wg_ref, wu_ref, wd_ref, o_ref, xb_s, acc_s):
    m = pl.program_id(0)
    f = pl.program_id(1)
    nf = pl.num_programs(1)
    n_valid = nv_ref[0]
    half = TME // 2
    small = rows_ref[m] <= half

    @pl.when(jnp.logical_and(m == 0, f == 0))
    def _():
        acc_s[...] = jnp.zeros_like(acc_s)

    def compute(mrows):
        @pl.when(f == 0)
        def _():
            for c in range(TOK_SUB):
                xb_s[0:mrows, c * LANES:(c + 1) * LANES] = _load_token_tiles(xs_ref, 0, mrows, c).astype(BF16)

        xb = xb_s[0:mrows, :]
        gg = _dot(xb, wg_ref[...].astype(BF16))
        uu = _dot(xb, wu_ref[...].astype(BF16))
        a = (gg * jax.nn.sigmoid(gg) * uu).astype(BF16)
        contrib = _dot(a, wd_ref[...].astype(BF16))

        acc_s[0:mrows, :] = jnp.where(f == 0, 0.0, acc_s[0:mrows, :]) + contrib

        @pl.when(f == nf - 1)
        def _():
            _store_token_tiles(o_ref, 0, acc_s[0:mrows, :])
            if mrows < TME:
                o_ref[mrows * TOK_SUB:TME * TOK_SUB, :] = jnp.zeros(((TME - mrows) * TOK_SUB, LANES), F32)

    @pl.when(jnp.logical_and(m < n_valid, jnp.logical_not(small)))
    def _():
        compute(TME)

    @pl.when(jnp.logical_and(m < n_valid, small))
    def _():
        compute(half)

    @pl.when(jnp.logical_and(m >= n_valid, f == 0))
    def _():
        o_ref[...] = jnp.zeros_like(o_ref)


def _expert_ffn(tile_expert, n_valid, tile_rows, xs, w_gu, w_down, li):
    def feff(m, f, nv):
        return jnp.where(m < nv[0], f, EXP_NF - 1)

    def meff(m, nv):
        return jnp.minimum(m, nv[0] - 1)
    return pl.pallas_call(
        _expert_kernel,
        grid_spec=pltpu.PrefetchScalarGridSpec(
            num_scalar_prefetch=3, grid=(N_EXPERT_TILES, EXP_NF),
            in_specs=[pl.BlockSpec((TME * TOK_SUB, LANES), lambda m, f, te, nv, tr: (meff(m, nv), 0)),
                      pl.BlockSpec((None, None, D, EXP_TF),
                                   lambda m, f, te, nv, tr: (li, te[m], 0, feff(m, f, nv))),
                      pl.BlockSpec((None, None, D, EXP_TF),
                                   lambda m, f, te, nv, tr: (li, te[m], 0, EXP_NF + feff(m, f, nv))),
                      pl.BlockSpec((None, None, EXP_TF, D),
                                   lambda m, f, te, nv, tr: (li, te[m], feff(m, f, nv), 0))],
            out_specs=pl.BlockSpec((TME * TOK_SUB, LANES), lambda m, f, te, nv, tr: (m, 0)),
            scratch_shapes=[pltpu.VMEM((TME, D), BF16), pltpu.VMEM((TME, D), F32)]),
        out_shape=jax.ShapeDtypeStruct((P_MAX * TOK_SUB, LANES), F32),
        compiler_params=_cparams(("arbitrary", "arbitrary")),
        name="moe_expert_swiglu",
    )(tile_expert, n_valid, tile_rows, xs, w_gu, w_gu, w_down)


COMB_TM = 512


def _combine_kernel(p1_ref, p2_ref, x_ref, gates_ref, mod_ref, gf_ref, y_hbm, *rest, final):
    i = pl.program_id(0)
    if final:
        oc_ref, ol_ref, a_s, b_s, sem = rest
    else:
        o_ref, a_s, b_s, sem = rest

    def tile_copy(p, r, dst, s):
        return pltpu.make_async_copy(y_hbm.at[pl.ds(pl.multiple_of(p * TOK_SUB, TOK_SUB), TOK_SUB), :],
                                     dst.at[pl.ds(pl.multiple_of(r * TOK_SUB, TOK_SUB), TOK_SUB), :], s)

    def body(g, carry):
        for k in range(DMA_UNROLL):
            r = g * DMA_UNROLL + k
            t = i * COMB_TM + r
            tile_copy(p1_ref[t], r, a_s, sem.at[0]).start(priority=0)
            tile_copy(p2_ref[t], r, b_s, sem.at[1]).start(priority=1)
        return carry
    lax.fori_loop(0, COMB_TM // DMA_UNROLL, body, 0)
    pltpu.make_async_copy(y_hbm.at[pl.ds(0, COMB_TM * TOK_SUB), :], a_s, sem.at[0]).wait()
    pltpu.make_async_copy(y_hbm.at[pl.ds(0, COMB_TM * TOK_SUB), :], b_s, sem.at[1]).wait()

    def finish(out_ref):
        for c in range(COMB_TM // ROW_CHUNK):
            r0 = c * ROW_CHUNK
            w1 = gates_ref[r0:r0 + ROW_CHUNK, 0:1]
            w2 = gates_ref[r0:r0 + ROW_CHUNK, 1:2]
            for j in range(TOK_SUB):
                cols = slice(j * LANES, (j + 1) * LANES)
                y = (w1 * _load_token_tiles(a_s, r0, ROW_CHUNK, j) + w2 * _load_token_tiles(b_s, r0, ROW_CHUNK, j))
                gate = mod_ref[:, 5 * D + j * LANES:5 * D + (j + 1) * LANES]
                out_ref[r0:r0 + ROW_CHUNK, cols] = x_ref[r0:r0 + ROW_CHUNK, cols] + gate * y
            if final:
                xn = out_ref[r0:r0 + ROW_CHUNK, :]
                ms = jnp.mean(xn * xn, axis=-1, keepdims=True)
                out_ref[r0:r0 + ROW_CHUNK, :] = xn * lax.rsqrt(ms + EPS) * gf_ref[...]

    if final:
        is_ctx = i < T_CTX // COMB_TM

        @pl.when(i == 0)
        def _():
            ol_ref[...] = jnp.zeros_like(ol_ref)

        pl.when(is_ctx)(lambda: finish(oc_ref))
        pl.when(jnp.logical_not(is_ctx))(lambda: finish(ol_ref))
    else:
        finish(o_ref)


def _combine(pos1, pos2, x, gates, mod4, layer, g_final, y, final):
    tok = pl.BlockSpec((COMB_TM, D), lambda i, *_: (i, 0))
    if final:
        out_specs = list(_ctx_lat_specs(COMB_TM))
        out_shape = [jax.ShapeDtypeStruct((T_CTX, D), F32), jax.ShapeDtypeStruct((T_LAT, D), F32)]
    else:
        out_specs = tok
        out_shape = jax.ShapeDtypeStruct((T_ALL, D), F32)
    return pl.pallas_call(
        functools.partial(_combine_kernel, final=final),
        grid_spec=pltpu.PrefetchScalarGridSpec(
            num_scalar_prefetch=2, grid=(T_ALL // COMB_TM,),
            in_specs=[tok, pl.BlockSpec((COMB_TM, LANES), lambda i, *_: (i, 0)),
                      _mod_spec(layer, COMB_TM), _row_spec(),
                      pl.BlockSpec(memory_space=pl.ANY)],
            out_specs=out_specs,
            scratch_shapes=[pltpu.VMEM((COMB_TM * TOK_SUB, LANES), F32), pltpu.VMEM((COMB_TM * TOK_SUB, LANES), F32),
                            pltpu.SemaphoreType.DMA((2,))]),
        out_shape=out_shape,
        compiler_params=_cparams(("arbitrary",)),
        name="moe_combine_final" if final else "moe_combine",
    )(pos1, pos2, x, gates, mod4, g_final.reshape(1, D), y)


def _moe_layer(x, mod4, layer, g_ffn, w_router, w_gu, w_down, li, g_final, final, xs_init):
    h, info, gates = _router(x, mod4, layer, g_ffn, w_router)
    pos1, pos2, tile_expert, n_valid, tile_rows = _route_plan(info)
    xs = _dispatch(pos1, pos2, h, xs_init)
    ys = _expert_ffn(tile_expert, n_valid, tile_rows, xs, w_gu, w_down, li)
    return _combine(pos1, pos2, x, gates, mod4, layer, g_final, ys, final), xs


def kernel(x_prompt, x_sample, cache_k, cache_v, state_ssm_re, state_ssm_im, c, c_ctx, w_ada, b_ada, g_mix, g_ffn, g_final, conv_w_in, conv_w, conv_w_out, s5_lambda_re, s5_lambda_im, s5_b_re, s5_b_im, s5_c_re, s5_c_im, s5_log_dt, s5_d, s5_w_glu, attn_w_q, attn_w_kv, attn_w_o, attn_sink, fnet_w_out, ffn_w_gu, ffn_w_down, moe_w_router, moe_w_gu, moe_w_down):
    cond16 =jnp.concatenate([jnp.broadcast_to(c_ctx[None, :], (8, D)), c], axis=0)
    mod3 = _mod_all(cond16, w_ada, b_ada)
    mod4 = mod3.reshape(DEPTH, 16, 1, N_MOD * D)

    x = _conv_mixer(x_prompt.reshape(T_CTX, D), x_sample.reshape(T_LAT, D), mod4, 0, g_mix[0],
                    conv_w_in[0].astype(BF16), conv_w[0], conv_w_out[0].astype(BF16))
    x = _dense_ffn(x, mod4, 0, g_ffn[0], ffn_w_gu[0].astype(BF16), ffn_w_down[0].astype(BF16))

    lbr, lbi, bbr, bbi = _s5_prep(s5_lambda_re[0], s5_lambda_im[0], s5_log_dt[0], s5_b_re[0], s5_b_im[0])
    w_b, w_c, lam_s = _s5_weights(lbr, lbi, bbr, bbi, s5_c_re[0], s5_c_im[0])
    w_glu = s5_w_glu[0].astype(BF16)
    xc = x[0:T_CTX].reshape(BATCH, SEQ, D).transpose(1, 0, 2).reshape(T_CTX, D)
    xl = x[T_CTX:].reshape(DEC_BATCH, DEC_SEQ, D).transpose(1, 0, 2).reshape(T_LAT, D)
    h0_ctx = jnp.zeros((2, S5_JT, BATCH, 2 * S5_HALF), F32)
    h0_lat = _s5_state_in(state_ssm_re[:, 0], state_ssm_im[:, 0])
    yc, fin_c = _s5_scan(xc, BATCH, mod3, 1, 0, g_mix[1], w_b, w_c, lam_s, h0_ctx)
    yl, _ = _s5_scan(xl, DEC_BATCH, mod3, 1, 1, g_mix[1], w_b, w_c, lam_s, h0_lat)
    xc = _s5_glu(xc, yc, mod3, 1, 0, g_mix[1], s5_d[0], w_glu)
    xl = _s5_glu(xl, yl, mod3, 1, 1, g_mix[1], s5_d[0], w_glu)
    x = jnp.concatenate([xc.reshape(SEQ, BATCH, D).transpose(1, 0, 2).reshape(T_CTX, D),
                         xl.reshape(DEC_SEQ, DEC_BATCH, D).transpose(1, 0, 2).reshape(T_LAT, D)], axis=0)
    new_re, new_im = _s5_state_out(fin_c)
    xs_zero = jnp.zeros((P_MAX * TOK_SUB, LANES), F32)
    x, xs_buf = _moe_layer(x, mod4, 1, g_ffn[1], moe_w_router[0], moe_w_gu, moe_w_down, 0, g_final, False, xs_zero)

    wq = attn_w_q[0]
    wk = _expand_kv_cols(attn_w_kv[0][:, 0:N_KV_HEADS * HEAD_DIM])
    wv = _expand_kv_cols(attn_w_kv[0][:, N_KV_HEADS * HEAD_DIM:])
    wq_b, wk_b, wv_b = wq.astype(BF16), wk.astype(BF16), wv.astype(BF16)
    cos_t, sin_t = _rope_tables()
    q_c, k_c, v_c, kv_c = _qkv_ctx(x, mod4, 2, g_mix[2], wq_b, wk_b, wv_b, attn_w_kv[0].astype(BF16))
    q_l, k_l, v_l = _qkv_lat(x, mod4, 2, g_mix[2], wq_b, _rot_half_cols(wq).astype(BF16), wk_b,
                             _rot_half_cols(wk).astype(BF16), wv_b, cos_t, sin_t)
    sink = attn_sink[0]
    o_c = _attn_ctx(sink, q_c, k_c, v_c)
    expand = lambda a: jnp.broadcast_to(
        a.reshape(DEC_BATCH * PAST_LEN, N_KV_HEADS, 1, HEAD_DIM),
        (DEC_BATCH * PAST_LEN, N_KV_HEADS, Q_PER_KV, HEAD_DIM)).reshape(DEC_BATCH * PAST_LEN, D).astype(BF16)
    o_l = _attn_lat(sink, q_l, k_l, v_l, expand(cache_k[:, 0]), expand(cache_v[:, 0]))
    x = _resproj(x, o_c, o_l, mod4, 2, attn_w_o[0].astype(BF16))
    kvw = N_KV_HEADS * HEAD_DIM
    new_k = kv_c[:, 0:kvw].reshape(BATCH, 1, SEQ, N_KV_HEADS, HEAD_DIM)
    new_v = kv_c[:, kvw:].reshape(BATCH, 1, SEQ, N_KV_HEADS, HEAD_DIM)
    x = _dense_ffn(x, mod4, 2, g_ffn[2], ffn_w_gu[1].astype(BF16), ffn_w_down[1].astype(BF16))

    cs, fl = _fnet_tables()
    x = _fnet_mixer(x, mod4, 3, g_mix[3], cs, fl, fnet_w_out[0].astype(BF16))
    (y_ctx, y_lat), _ = _moe_layer(x, mod4, 3, g_ffn[3], moe_w_router[1], moe_w_gu, moe_w_down, 1, g_final, True,
                                   xs_buf)

    y_prompt = y_ctx.reshape(BATCH, SEQ, D)
    y_sample = y_lat.reshape(DEC_BATCH, DEC_SEQ, D)
    return (y_prompt, y_sample, new_k, new_v, new_re[:, None], new_im[:, None])
```

```python
import functools
import math

import numpy as np
import jax
import jax.numpy as jnp
from jax import lax
from jax.experimental import pallas as pl
from jax.experimental.pallas import tpu as pltpu

F32 = jnp.float32
BF16 = jnp.bfloat16
I32 = jnp.int32

D = 1024
BATCH = 16
SEQ = 256
DEPTH = 4
DEC_BATCH = 8
DEC_SEQ = 1024
PAST_LEN = 256
GRID_W = 64
EPS = 1e-6
N_MOD = 6
S5_GROUP = 16
S5_GROUPS = D // S5_GROUP
S5_STATE = 64
HEAD_DIM = 64
N_HEADS = D // HEAD_DIM
N_KV_HEADS = 4
Q_PER_KV = N_HEADS // N_KV_HEADS
WINDOW = 128
ROPE_THETA = 10000.0
ROPE_FREQS = HEAD_DIM // 4
FNET_GROUPS = 4
D_FF = 2816
N_EXPERTS = 8
TOP_K = 2
D_FF_EXPERT = 3584
NEG_INF = -1e30

T_CTX = BATCH * SEQ
T_LAT = DEC_BATCH * DEC_SEQ
T_ALL = T_CTX + T_LAT

VMEM_LIMIT_V7X = 56 * 1024 * 1024
LANES = 128

TM = 1024
ROW_CHUNK = 256
MM_CHUNK = 512
TME = 1024
N_EXPERT_TILES = (T_ALL * TOP_K) // TME + N_EXPERTS
P_MAX = N_EXPERT_TILES * TME


def _cparams(sem):
    return pltpu.CompilerParams(dimension_semantics=sem, vmem_limit_bytes=VMEM_LIMIT_V7X)


def _dot(a, b):
    return jnp.dot(a, b, preferred_element_type=F32)


def _dot_nt(a, b):
    return lax.dot_general(a, b, (((1,), (1,)), ((), ())), preferred_element_type=F32)


def _split_bf16(a):
    hi = a.astype(BF16)
    lo = (a - hi.astype(F32)).astype(BF16)
    return hi, lo


def _tile_rows(a, reps, axis=0):
    assert axis == 0
    return jnp.concatenate([a] * reps, axis=0)


def _norm_mod(x, g, sc, sh):
    ms = jnp.mean(x * x, axis=-1, keepdims=True)
    y = x * lax.rsqrt(ms + EPS) * g
    return y * (1.0 + sc) + sh


def _mod_row(i, tm):
    nct = T_CTX // tm
    lpb = DEC_SEQ // tm
    return jnp.where(i < nct, 0, 8 + (i - nct) // lpb)


def _mod_spec(layer, tm):
    return pl.BlockSpec((None, None, 1, N_MOD * D), lambda i, *_: (layer, _mod_row(i, tm), 0, 0))


def _row_spec():
    return pl.BlockSpec((1, D), lambda *_: (0, 0))


def _full_spec(shape):
    nd = len(shape)
    return pl.BlockSpec(shape, lambda *_: (0,) * nd, pipeline_mode=pl.Buffered(1))


def _mod_kernel(c_ref, w_ref, b_ref, o_ref):
    c = c_ref[...]
    s = (c * jax.nn.sigmoid(c)).astype(BF16)
    o_ref[...] = _dot(s, w_ref[...].astype(BF16)) + b_ref[...]


def _mod_all(cond16, w_ada, b_ada):
    tn = 1024
    return pl.pallas_call(
        _mod_kernel,
        grid=(DEPTH, N_MOD * D // tn),
        in_specs=[pl.BlockSpec((16, D), lambda l, n: (0, 0)),
                  pl.BlockSpec((None, D, tn), lambda l, n: (l, 0, n)),
                  pl.BlockSpec((None, 1, tn), lambda l, n: (l, 0, n))],
        out_specs=pl.BlockSpec((None, 16, tn), lambda l, n: (l, 0, n)),
        out_shape=jax.ShapeDtypeStruct((DEPTH, 16, N_MOD * D), F32),
        compiler_params=_cparams(("arbitrary", "arbitrary")),
        name="adaln_mod",
    )(cond16, w_ada, b_ada.reshape(DEPTH, 1, N_MOD * D))


def _conv_kernel(xc_ref, xl_ref, mod_ref, g_ref, win_ref, cw_ref, wout_ref, o_ref, gb_s, u_s, z_s):
    i = pl.program_id(0)
    is_ctx = i < (T_CTX // TM)

    def x_rows(r0, n):
        return jnp.where(is_ctx, xc_ref[r0:r0 + n, :], xl_ref[r0:r0 + n, :])

    sh = mod_ref[:, 0:D]
    sc = mod_ref[:, D:2 * D]
    gate = mod_ref[:, 2 * D:3 * D]
    g = g_ref[...]
    zero8 = jnp.zeros((8, D), F32)
    u_s[0:8, :] = zero8
    u_s[8 + TM:16 + TM, :] = zero8
    for c in range(TM // MM_CHUNK):
        r0 = c * MM_CHUNK
        h = _norm_mod(x_rows(r0, MM_CHUNK), g, sc, sh).astype(BF16)
        proj = _dot(h, win_ref[...])
        gb_s[r0:r0 + MM_CHUNK, :] = proj[:, 0:D]
        u_s[8 + r0:8 + r0 + MM_CHUNK, :] = proj[:, D:2 * D] * proj[:, 2 * D:3 * D]
    row = lax.broadcasted_iota(I32, (ROW_CHUNK, 1), 0)
    first = jnp.logical_and(is_ctx, row == 0)
    last = jnp.logical_and(is_ctx, row == ROW_CHUNK - 1)
    for c in range(TM // ROW_CHUNK):
        r0 = c * ROW_CHUNK
        up = jnp.where(first, 0.0, u_s[7 + r0:7 + r0 + ROW_CHUNK, :])
        mid = u_s[8 + r0:8 + r0 + ROW_CHUNK, :]
        dn = jnp.where(last, 0.0, u_s[9 + r0:9 + r0 + ROW_CHUNK, :])
        conv = up * cw_ref[0:1, :] + mid * cw_ref[1:2, :] + dn * cw_ref[2:3, :]
        z_s[r0:r0 + ROW_CHUNK, :] = (gb_s[r0:r0 + ROW_CHUNK, :] * conv).astype(BF16)
    for c in range(TM // MM_CHUNK):
        r0 = c * MM_CHUNK
        y = _dot(z_s[r0:r0 + MM_CHUNK, :], wout_ref[...])
        o_ref[r0:r0 + MM_CHUNK, :] = x_rows(r0, MM_CHUNK) + gate * y


def _conv_mixer(x_ctx, x_lat, mod4, layer, g_mix, w_in, conv_w, w_out):
    assert SEQ == ROW_CHUNK and DEC_SEQ == TM
    ctx, lat = _ctx_lat_specs(TM)
    return pl.pallas_call(
        _conv_kernel,
        grid=(T_ALL // TM,),
        in_specs=[ctx, lat, _mod_spec(layer, TM), _row_spec(),
                  _full_spec((D, 3 * D)), _full_spec((3, D)), _full_spec((D, D))],
        out_specs=pl.BlockSpec((TM, D), lambda i: (i, 0)),
        out_shape=jax.ShapeDtypeStruct((T_ALL, D), F32),
        scratch_shapes=[pltpu.VMEM((TM, D), F32), pltpu.VMEM((TM + 16, D), F32), pltpu.VMEM((TM, D), BF16)],
        compiler_params=_cparams(("arbitrary",)),
        name="conv_mixer",
    )(x_ctx, x_lat, mod4, g_mix.reshape(1, D), w_in, conv_w, w_out)


FFN_TM = 512
MXU_WIDTH_V7X = 256
FFN_SPLITS = ((0, 6 * MXU_WIDTH_V7X), (6 * MXU_WIDTH_V7X, D_FF))
assert D_FF % MXU_WIDTH_V7X == 0


def _ffn_kernel(x_ref, mod_ref, g_ref, wgu_ref, wd_ref, o_ref):
    sh = mod_ref[:, 3 * D:4 * D]
    sc = mod_ref[:, 4 * D:5 * D]
    gate = mod_ref[:, 5 * D:6 * D]
    x = x_ref[...]
    h = _norm_mod(x, g_ref[...], sc, sh).astype(BF16)
    acc = None
    for lo, hi in FFN_SPLITS:
        gg = _dot(h, wgu_ref[:, lo:hi])
        uu = _dot(h, wgu_ref[:, D_FF + lo:D_FF + hi])
        a = (gg * jax.nn.sigmoid(gg) * uu).astype(BF16)
        contrib = _dot(a, wd_ref[lo:hi, :])
        acc = contrib if acc is None else acc + contrib
    o_ref[...] = x + gate * acc


def _dense_ffn(x, mod4, layer, g_ffn, w_gu, w_down):
    tok = pl.BlockSpec((FFN_TM, D), lambda i: (i, 0))
    return pl.pallas_call(
        _ffn_kernel,
        grid=(T_ALL // FFN_TM,),
        in_specs=[tok, _mod_spec(layer, FFN_TM), _row_spec(),
                  _full_spec((D, 2 * D_FF)), _full_spec((D_FF, D))],
        out_specs=tok,
        out_shape=jax.ShapeDtypeStruct((T_ALL, D), F32),
        compiler_params=_cparams(("arbitrary",)),
        name="dense_swiglu",
    )(x, mod4, g_ffn.reshape(1, D), w_gu, w_down)


def _s5_prep_kernel(lr_ref, li_ref, ldt_ref, br_ref, bi_ref, lbr_ref, lbi_ref, bbr_ref, bbi_ref):
    lr = lr_ref[...]
    li = li_ref[...]
    dt = jnp.exp(ldt_ref[...])
    mag = jnp.exp(lr * dt)
    ar = mag * jnp.cos(li * dt)
    ai = mag * jnp.sin(li * dt)
    nr = ar - 1.0
    den = lr * lr + li * li
    fr = (nr * lr + ai * li) / den
    fi = (ai * lr - nr * li) / den
    br = br_ref[...]
    bi = bi_ref[...]
    lbr_ref[...] = ar
    lbi_ref[...] = ai
    bbr_ref[...] = fr * br - fi * bi
    bbi_ref[...] = fr * bi + fi * br


def _s5_prep(lam_re, lam_im, log_dt, b_re, b_im):
    rows = 2 * S5_GROUPS
    cols = S5_STATE * S5_GROUP
    exp = lambda a: jnp.repeat(a.reshape(rows, S5_STATE), S5_GROUP, axis=1)
    ldt = jnp.broadcast_to(log_dt.reshape(rows, 1), (rows, cols))
    outs = pl.pallas_call(
        _s5_prep_kernel,
        out_shape=[jax.ShapeDtypeStruct((rows, cols), F32)] * 4,
        name="s5_discretize",
    )(exp(lam_re), exp(lam_im), ldt, b_re.reshape(rows, cols), b_im.reshape(rows, cols))
    lbr, lbi, bbr, bbi = outs
    shp = (2, S5_GROUPS, S5_STATE, S5_GROUP)
    return lbr.reshape(shp)[..., 0], lbi.reshape(shp)[..., 0], bbr.reshape(shp), bbi.reshape(shp)


S5_JT = 8
S5_GPT = LANES // S5_GROUP
S5_HALF = S5_GPT * S5_STATE
S5_ROWS = 512


def _s5_scan_kernel(x_ref, mod_ref, g_ref, wb_ref, wc_ref, lam_ref, h0_ref,
                    y_ref, fin_ref, bu_s, st_s, *, nb, jgroup):
    d = pl.program_id(0)
    c = pl.program_id(1)
    lc = S5_ROWS // nb

    @pl.when(c == 0)
    def _():
        st_s[...] = h0_ref[...]

    rep = S5_ROWS // 8
    sh = _tile_rows(mod_ref[:, 0:D], rep, axis=0)
    sc = _tile_rows(mod_ref[:, D:2 * D], rep, axis=0)
    u = _norm_mod(x_ref[...], g_ref[...], sc, sh).astype(BF16)
    for j in range(S5_JT):
        bu_s[j] = _dot(u[:, j * LANES:(j + 1) * LANES], wb_ref[j])

    for j0 in range(0, S5_JT, jgroup):
        js = list(range(j0, j0 + jgroup))
        lam = [(jnp.broadcast_to(lam_ref[j][:, 0:S5_HALF], (nb, S5_HALF)),
                jnp.broadcast_to(lam_ref[j][:, S5_HALF:], (nb, S5_HALF))) for j in js]

        def body(t, carry):
            l = jnp.where(d == 0, t, lc - 1 - t)
            r0 = pl.multiple_of(l * nb, nb)
            out = []
            for k, j in enumerate(js):
                sr, si = carry[k]
                ar, ai = lam[k]
                bu = bu_s[j, pl.ds(r0, nb), :]
                hr = ar * sr - ai * si + bu[:, 0:S5_HALF]
                hi = ar * si + ai * sr + bu[:, S5_HALF:]
                bu_s[j, pl.ds(r0, nb), 0:S5_HALF] = hr
                bu_s[j, pl.ds(r0, nb), S5_HALF:] = hi
                out.append((hr, hi))
            return tuple(out)

        init = tuple((st_s[j][:, 0:S5_HALF], st_s[j][:, S5_HALF:]) for j in js)
        fin = lax.fori_loop(0, lc, body, init)
        for k, j in enumerate(js):
            st_s[j, :, 0:S5_HALF] = fin[k][0]
            st_s[j, :, S5_HALF:] = fin[k][1]

    for j in range(S5_JT):
        y_ref[:, j * LANES:(j + 1) * LANES] = _dot(bu_s[j].astype(BF16), wc_ref[j])
    fin_ref[...] = st_s[...]


def _s5_scan(xt, nb, mod3, layer, path, g_mix, w_b, w_c, lam_s, h0):
    rows = xt.shape[0]
    nc = rows // S5_ROWS
    chunk = lambda d, c: c + d * (nc - 1 - 2 * c)
    kern = functools.partial(_s5_scan_kernel, nb=nb, jgroup=2 if nb == 8 else 1)
    return pl.pallas_call(
        kern,
        grid=(2, nc),
        in_specs=[pl.BlockSpec((S5_ROWS, D), lambda d, c: (chunk(d, c), 0)),
                  pl.BlockSpec((None, 8, N_MOD * D), lambda d, c: (layer, path, 0)),
                  _row_spec(),
                  pl.BlockSpec((None, S5_JT, LANES, 2 * S5_HALF), lambda d, c: (d, 0, 0, 0)),
                  pl.BlockSpec((None, S5_JT, 2 * S5_HALF, LANES), lambda d, c: (d, 0, 0, 0)),
                  pl.BlockSpec((None, S5_JT, 1, 2 * S5_HALF), lambda d, c: (d, 0, 0, 0)),
                  pl.BlockSpec((None, S5_JT, nb, 2 * S5_HALF), lambda d, c: (d, 0, 0, 0))],
        out_specs=[pl.BlockSpec((None, S5_ROWS, D), lambda d, c: (d, chunk(d, c), 0)),
                   pl.BlockSpec((None, S5_JT, nb, 2 * S5_HALF), lambda d, c: (d, 0, 0, 0))],
        out_shape=[jax.ShapeDtypeStruct((2, rows, D), F32),
                   jax.ShapeDtypeStruct((2, S5_JT, nb, 2 * S5_HALF), F32)],
        scratch_shapes=[pltpu.VMEM((S5_JT, S5_ROWS, 2 * S5_HALF), F32),
                        pltpu.VMEM((S5_JT, nb, 2 * S5_HALF), F32)],
        compiler_params=_cparams(("arbitrary", "arbitrary")),
        name="s5_scan_b%d" % nb,
    )(xt, mod3, g_mix.reshape(1, D), w_b, w_c, lam_s, h0)


S5_GLU_ROWS = 512


def _s5_glu_kernel(x_ref, yf_ref, yb_ref, mod_ref, g_ref, dsk_ref, w_ref, o_ref):
    rep = MM_CHUNK // 8
    sh = _tile_rows(mod_ref[:, 0:D], rep, axis=0)
    sc = _tile_rows(mod_ref[:, D:2 * D], rep, axis=0)
    gate = _tile_rows(mod_ref[:, 2 * D:3 * D], rep, axis=0)
    for c in range(S5_GLU_ROWS // MM_CHUNK):
        r0 = c * MM_CHUNK
        x = x_ref[r0:r0 + MM_CHUNK, :]
        u = _norm_mod(x, g_ref[...], sc, sh)
        y = u * dsk_ref[...] + yf_ref[r0:r0 + MM_CHUNK, :] + yb_ref[r0:r0 + MM_CHUNK, :]
        z = jax.nn.gelu(y).astype(BF16)
        ag = _dot(z, w_ref[...])
        out = ag[:, 0:D] * jax.nn.sigmoid(ag[:, D:2 * D])
        o_ref[r0:r0 + MM_CHUNK, :] = x + gate * out


def _s5_glu(xt, y2, mod3, layer, path, g_mix, d_skip, w_glu):
    rows = xt.shape[0]
    return pl.pallas_call(
        _s5_glu_kernel,
        grid=(rows // S5_GLU_ROWS,),
        in_specs=[pl.BlockSpec((S5_GLU_ROWS, D), lambda i: (i, 0)),
                  pl.BlockSpec((None, S5_GLU_ROWS, D), lambda i: (0, i, 0)),
                  pl.BlockSpec((None, S5_GLU_ROWS, D), lambda i: (1, i, 0)),
                  pl.BlockSpec((None, 8, N_MOD * D), lambda i: (layer, path, 0)),
                  _row_spec(), _row_spec(), _full_spec((D, 2 * D))],
        out_specs=pl.BlockSpec((S5_GLU_ROWS, D), lambda i: (i, 0)),
        out_shape=jax.ShapeDtypeStruct((rows, D), F32),
        compiler_params=_cparams(("arbitrary",)),
        name="s5_glu",
    )(xt, y2, y2, mod3, g_mix.reshape(1, D), d_skip.reshape(1, D), w_glu)


def _s5_weights(lbr, lbi, bbr, bbi, c_re, c_im):
    eye = jnp.eye(S5_GPT, dtype=F32)
    bb = jnp.stack([bbr, bbi]).reshape(2, 2, S5_JT, S5_GPT, S5_STATE, S5_GROUP)
    w_bu = jnp.einsum('rdjgps,gh->djgsrhp', bb, eye).reshape(2, S5_JT, LANES, 2 * S5_HALF)
    w_b = w_bu.astype(BF16)
    cc =jnp.stack([c_re, -c_im]).reshape(2, 2, S5_JT, S5_GPT, S5_GROUP, S5_STATE)
    w_c = jnp.einsum('rdjgsp,gh->djrgphs', cc, eye).reshape(2, S5_JT, 2 * S5_HALF, LANES).astype(BF16)
    lam_s = jnp.concatenate([lbr.reshape(2, S5_JT, S5_HALF), lbi.reshape(2, S5_JT, S5_HALF)], axis=-1)
    return w_b, w_c, lam_s.reshape(2, S5_JT, 1, 2 * S5_HALF)


def _s5_state_in(st_re, st_im):
    def lay(a):
        b = a.shape[0]
        return a.transpose(1, 0, 2, 3).reshape(2, b, S5_JT, S5_HALF).transpose(0, 2, 1, 3)
    return jnp.concatenate([lay(st_re), lay(st_im)], axis=-1)


def _s5_state_out(fin):
    def lay(a):
        b = a.shape[2]
        return a.transpose(2, 0, 1, 3).reshape(b, 2, S5_GROUPS, S5_STATE)
    return lay(fin[..., 0:S5_HALF]), lay(fin[..., S5_HALF:])


QKV_TM = 512


def _qkv_ctx_kernel(x_ref, mod_ref, g_ref, wq_ref, wk_ref, wv_ref, wkv_ref, q_ref, k_ref, v_ref, kv_ref):
    sh = mod_ref[:, 0:D]
    sc = mod_ref[:, D:2 * D]
    for c in range(QKV_TM // MM_CHUNK):
        r0 = c * MM_CHUNK
        h = _norm_mod(x_ref[r0:r0 + MM_CHUNK, :], g_ref[...], sc, sh).astype(BF16)
        q_ref[r0:r0 + MM_CHUNK, :] = _dot(h, wq_ref[...]).astype(BF16)
        k_ref[r0:r0 + MM_CHUNK, :] = _dot(h, wk_ref[...]).astype(BF16)
        v_ref[r0:r0 + MM_CHUNK, :] = _dot(h, wv_ref[...]).astype(BF16)
        kv_ref[r0:r0 + MM_CHUNK, :] = _dot(h, wkv_ref[...])


def _qkv_ctx(x, mod4, layer, g_mix, wq, wk, wv, wkv):
    bf = jax.ShapeDtypeStruct((T_CTX, D), BF16)
    tok = pl.BlockSpec((QKV_TM, D), lambda i: (i, 0))
    kvw = wkv.shape[1]
    return pl.pallas_call(
        _qkv_ctx_kernel,
        grid=(T_CTX // QKV_TM,),
        in_specs=[tok, _mod_spec(layer, QKV_TM), _row_spec(),
                  _full_spec((D, D)), _full_spec((D, D)), _full_spec((D, D)), _full_spec((D, kvw))],
        out_specs=[tok, tok, tok, pl.BlockSpec((QKV_TM, kvw), lambda i: (i, 0))],
        out_shape=[bf, bf, bf, jax.ShapeDtypeStruct((T_CTX, kvw), F32)],
        compiler_params=_cparams(("arbitrary",)),
        name="qkv_ctx",
    )(x, mod4, g_mix.reshape(1, D), wq, wk, wv, wkv)


def _qkv_lat_kernel(x_ref, mod_ref, g_ref, wq_ref, wqr_ref, wk_ref, wkr_ref, wv_ref, cos_ref, sin_ref,
                    q_ref, k_ref, v_ref):
    sh = mod_ref[:, 0:D]
    sc = mod_ref[:, D:2 * D]
    for c in range(QKV_TM // MM_CHUNK):
        r0 = c * MM_CHUNK
        h = _norm_mod(x_ref[r0:r0 + MM_CHUNK, :], g_ref[...], sc, sh).astype(BF16)
        cos = cos_ref[r0:r0 + MM_CHUNK, :]
        sin = sin_ref[r0:r0 + MM_CHUNK, :]
        q_ref[r0:r0 + MM_CHUNK, :] = (_dot(h, wq_ref[...]) * cos + _dot(h, wqr_ref[...]) * sin).astype(BF16)
        k_ref[r0:r0 + MM_CHUNK, :] = (_dot(h, wk_ref[...]) * cos + _dot(h, wkr_ref[...]) * sin).astype(BF16)
        v_ref[r0:r0 + MM_CHUNK, :] = _dot(h, wv_ref[...]).astype(BF16)


def _qkv_lat(x, mod4, layer, g_mix, wq, wq_rot, wk, wk_rot, wv, cos_t, sin_t):
    nct = T_CTX // QKV_TM
    lpb = DEC_SEQ // QKV_TM
    bf = jax.ShapeDtypeStruct((T_LAT, D), BF16)
    tok_out = pl.BlockSpec((QKV_TM, D), lambda i: (i, 0))
    rope = pl.BlockSpec((QKV_TM, D), lambda i: (i % lpb, 0))
    return pl.pallas_call(
        _qkv_lat_kernel,
        grid=(T_LAT // QKV_TM,),
        in_specs=[pl.BlockSpec((QKV_TM, D), lambda i: (i + nct, 0)),
                  pl.BlockSpec((None, None, 1, N_MOD * D), lambda i: (layer, 8 + i // lpb, 0, 0)),
                  _row_spec()] + [_full_spec((D, D))] * 5 + [rope, rope],
        out_specs=[tok_out, tok_out, tok_out],
        out_shape=[bf, bf, bf],
        compiler_params=_cparams(("arbitrary",)),
        name="qkv_lat",
    )(x, mod4, g_mix.reshape(1, D), wq, wq_rot, wk, wk_rot, wv, cos_t, sin_t)


KVW = Q_PER_KV * HEAD_DIM


def _head_masks(rows):
    lane = lax.broadcasted_iota(I32, (rows, KVW), 1)
    return [jnp.logical_and(lane >= g * HEAD_DIM, lane < (g + 1) * HEAD_DIM) for g in range(Q_PER_KV)]


def _attn_ctx_kernel(sink_ref, q_ref, k_ref, v_ref, o_ref):
    scale = HEAD_DIM ** -0.5
    masks = _head_masks(SEQ)
    for kv in range(N_KV_HEADS):
        c0 = kv * KVW
        q = q_ref[:, c0:c0 + KVW]
        k = k_ref[:, c0:c0 + KVW]
        v = v_ref[:, c0:c0 + KVW]
        acc = jnp.zeros((SEQ, KVW), F32)
        for g in range(Q_PER_KV):
            sink = sink_ref[kv * Q_PER_KV + g]
            qg = jnp.where(masks[g], q, jnp.zeros_like(q))
            s = _dot_nt(qg, k) * scale
            m = jnp.maximum(jnp.max(s, axis=-1, keepdims=True), sink)
            e = jnp.exp(s - m)
            den = jnp.sum(e, axis=-1, keepdims=True) + jnp.exp(sink - m)
            og = _dot(e.astype(BF16), v) / den
            acc = jnp.where(masks[g], og, acc)
        o_ref[:, c0:c0 + KVW] = acc.astype(BF16)


def _attn_ctx(sink, q, k, v):
    tok = pl.BlockSpec((SEQ, D), lambda b, *_: (b, 0))
    return pl.pallas_call(
        _attn_ctx_kernel,
        grid_spec=pltpu.PrefetchScalarGridSpec(
            num_scalar_prefetch=1, grid=(BATCH,),
            in_specs=[tok, tok, tok], out_specs=tok),
        out_shape=jax.ShapeDtypeStruct((T_CTX, D), BF16),
        compiler_params=_cparams(("arbitrary",)),
        name="attn_ctx",
    )(sink, q, k, v)


ATT_TQ = 128
ATT_SPAN = ATT_TQ + 2 * WINDOW
assert math.log2(HEAD_DIM ** -0.5).is_integer()


def _attn_lat_kernel(sink_ref, q_ref, k_ref, v_ref, ck_ref, cv_ref, o_ref):
    qb = pl.program_id(1)
    scale = HEAD_DIM ** -0.5
    w0 = pl.multiple_of(jnp.clip(qb * ATT_TQ - WINDOW, 0, DEC_SEQ - ATT_SPAN), ATT_TQ)
    rows = Q_PER_KV * ATT_TQ
    ridx = lax.broadcasted_iota(I32, (rows, ATT_SPAN), 0)
    qpos = qb * ATT_TQ + (ridx & (ATT_TQ - 1))
    kpos = w0 + lax.broadcasted_iota(I32, (rows, ATT_SPAN), 1)
    valid = jnp.abs(qpos - kpos) <= WINDOW
    rcol = lax.broadcasted_iota(I32, (rows, 1), 0)
    masks = _head_masks(ATT_TQ)
    for kv in range(N_KV_HEADS):
        c0 = kv * KVW
        q = q_ref[:, c0:c0 + KVW] * scale
        qs = jnp.concatenate([jnp.where(masks[g], q, jnp.zeros_like(q)) for g in range(Q_PER_KV)], axis=0)
        sink = jnp.zeros((rows, 1), F32)
        for g in range(Q_PER_KV):
            sink = jnp.where(rcol >= g * ATT_TQ, sink_ref[kv * Q_PER_KV + g], sink)
        s_ctx = _dot_nt(qs, ck_ref[:, c0:c0 + KVW])
        s_win = _dot_nt(qs, k_ref[pl.ds(w0, ATT_SPAN), c0:c0 + KVW])
        s_win = jnp.where(valid, s_win, NEG_INF)
        m = jnp.maximum(jnp.maximum(jnp.max(s_ctx, axis=-1, keepdims=True),
                                    jnp.max(s_win, axis=-1, keepdims=True)), sink)
        e_ctx = jnp.exp(s_ctx - m)
        e_win = jnp.exp(s_win - m)
        den = (jnp.exp(sink - m) + jnp.sum(e_ctx, axis=-1, keepdims=True)
               + jnp.sum(e_win, axis=-1, keepdims=True))
        o = (_dot(e_ctx.astype(BF16), cv_ref[:, c0:c0 + KVW])
             + _dot(e_win.astype(BF16), v_ref[pl.ds(w0, ATT_SPAN), c0:c0 + KVW])) / den
        acc = jnp.zeros((ATT_TQ, KVW), F32)
        for g in range(Q_PER_KV):
            acc = jnp.where(masks[g], o[g * ATT_TQ:(g + 1) * ATT_TQ, :], acc)
        o_ref[:, c0:c0 + KVW] = acc.astype(BF16)


def _attn_lat(sink, q, k, v, ck, cv):
    nqb = DEC_SEQ // ATT_TQ
    qspec = pl.BlockSpec((ATT_TQ, D), lambda b, i, *_: (b * nqb + i, 0))
    seq = pl.BlockSpec((DEC_SEQ, D), lambda b, i, *_: (b, 0))
    ctx = pl.BlockSpec((PAST_LEN, D), lambda b, i, *_: (b, 0))
    return pl.pallas_call(
        _attn_lat_kernel,
        grid_spec=pltpu.PrefetchScalarGridSpec(
            num_scalar_prefetch=1, grid=(DEC_BATCH, nqb),
            in_specs=[qspec, seq, seq, ctx, ctx], out_specs=qspec),
        out_shape=jax.ShapeDtypeStruct((T_LAT, D), BF16),
        compiler_params=_cparams(("arbitrary", "arbitrary")),
        name="attn_lat",
    )(sink, q, k, v, ck, cv)


def _ctx_lat_specs(tm):
    nct = T_CTX // tm
    ctx = pl.BlockSpec((tm, D), lambda i, *_: (jnp.minimum(i, nct - 1), 0))
    lat = pl.BlockSpec((tm, D), lambda i, *_: (jnp.maximum(i - nct, 0), 0))
    return ctx, lat


def _resproj_kernel(x_ref, ac_ref, al_ref, mod_ref, w_ref, o_ref):
    gate = mod_ref[:, 2 * D:3 * D]
    is_ctx = pl.program_id(0) < T_CTX // TM
    a = jnp.where(is_ctx, ac_ref[...], al_ref[...])
    o_ref[...] = x_ref[...] + gate * _dot(a, w_ref[...])


def _resproj(x, a_ctx, a_lat, mod4, layer, w):
    tok = pl.BlockSpec((TM, D), lambda i: (i, 0))
    ctx, lat = _ctx_lat_specs(TM)
    return pl.pallas_call(
        _resproj_kernel,
        grid=(T_ALL // TM,),
        in_specs=[tok, ctx, lat, _mod_spec(layer, TM), _full_spec((D, D))],
        out_specs=tok,
        out_shape=jax.ShapeDtypeStruct((T_ALL, D), F32),
        compiler_params=_cparams(("arbitrary",)),
        name="attn_out_proj",
    )(x, a_ctx, a_lat, mod4, w)


def _rope_tables():
    rows = DEC_SEQ // GRID_W
    row = jnp.repeat(jnp.arange(rows), GRID_W).astype(F32)
    col = jnp.tile(jnp.arange(GRID_W), rows).astype(F32)
    inv = ROPE_THETA ** (-jnp.arange(ROPE_FREQS, dtype=F32) / ROPE_FREQS)
    ang = jnp.concatenate([row[:, None] * inv, col[:, None] * inv], axis=-1)
    cos = jnp.cos(ang)
    sin = jnp.sin(ang)
    cos_h = jnp.concatenate([cos, cos], axis=-1)
    sin_h = jnp.concatenate([sin, sin], axis=-1)
    return jnp.tile(cos_h, (1, N_HEADS)), jnp.tile(sin_h, (1, N_HEADS))


def _rot_half_cols(w):
    k = w.shape[0]
    w4 = w.reshape(k, -1, 2, HEAD_DIM // 2)
    return jnp.stack([-w4[:, :, 1], w4[:, :, 0]], axis=2).reshape(k, -1)


def _expand_kv_cols(w):
    k = w.shape[0]
    w3 = w.reshape(k, N_KV_HEADS, 1, HEAD_DIM)
    return jnp.broadcast_to(w3, (k, N_KV_HEADS, Q_PER_KV, HEAD_DIM)).reshape(k, N_HEADS * HEAD_DIM)


FG = D // FNET_GROUPS


def _fnet_kernel(x_ref, mod_ref, g_ref, cs_ref, fl_ref, w_ref, o_ref, h_s, ab_s):
    sh = mod_ref[:, 0:D]
    sc = mod_ref[:, D:2 * D]
    gate = mod_ref[:, 2 * D:3 * D]
    for c in range(TM // ROW_CHUNK):
        r0 = c * ROW_CHUNK
        h_s[r0:r0 + ROW_CHUNK, :] = _norm_mod(x_ref[r0:r0 + ROW_CHUNK, :], g_ref[...], sc, sh).astype(BF16)
    for g in range(FNET_GROUPS):
        ab = _dot(h_s[:, g * FG:(g + 1) * FG], cs_ref[...])
        ab_s[0:TM, g * FG:(g + 1) * FG] = ab[:, 0:FG].astype(BF16)
        ab_s[TM:2 * TM, g * FG:(g + 1) * FG] = ab[:, FG:2 * FG].astype(BF16)
    for c in range(TM // MM_CHUNK):
        r0 = c * MM_CHUNK
        f = _dot(fl_ref[r0:r0 + MM_CHUNK, :], ab_s[...])
        y = _dot(f.astype(BF16), w_ref[...])
        o_ref[r0:r0 + MM_CHUNK, :] = x_ref[r0:r0 + MM_CHUNK, :] + gate * y


def _fnet_mixer(x, mod4, layer, g_mix, cs, fl, w_out):
    nct = T_CTX // TM
    tok = pl.BlockSpec((TM, D), lambda i: (i, 0))
    return pl.pallas_call(
        _fnet_kernel,
        grid=(T_ALL // TM,),
        in_specs=[tok, _mod_spec(layer, TM), _row_spec(), _full_spec((FG, 2 * FG)),
                  pl.BlockSpec((None, TM, 2 * TM), lambda i: (jnp.where(i < nct, 0, 1), 0, 0)),
                  _full_spec((D, D))],
        out_specs=tok,
        out_shape=jax.ShapeDtypeStruct((T_ALL, D), F32),
        scratch_shapes=[pltpu.VMEM((TM, D), BF16), pltpu.VMEM((2 * TM, D), BF16)],
        compiler_params=_cparams(("arbitrary",)),
        name="fnet_mixer",
    )(x, mod4, g_mix.reshape(1, D), cs, fl, w_out)


def _dft_cos_sin(n):
    k = np.arange(n)
    ang = 2.0 * np.pi * ((k[:, None] * k[None, :]) % n) / n
    return np.cos(ang), np.sin(ang)


def _fnet_tables():
    cc, sc = _dft_cos_sin(FG)
    cs = np.concatenate([cc, sc], axis=1) / math.sqrt(FG)
    mats = []
    for seq in (SEQ, DEC_SEQ):
        cl, sl = _dft_cos_sin(seq)
        reps = TM // seq
        eye = np.eye(reps)
        mats.append(np.concatenate([np.kron(eye, cl), -np.kron(eye, sl)], axis=1) / math.sqrt(seq))
    return jnp.asarray(cs, F32).astype(BF16), jnp.asarray(np.stack(mats), F32).astype(BF16)


ROUTER_TM = 512
TOK_SUB = D // LANES


def _store_token_tiles(ref, r0, val):
    rows = val.shape[0]
    for c in range(TOK_SUB):
        ref[pl.ds(r0 * TOK_SUB + c, rows, stride=TOK_SUB), :] = val[:, c * LANES:(c + 1) * LANES]


def _load_token_tiles(ref, r0, rows, c, lead=None):
    idx = pl.ds(r0 * TOK_SUB + c, rows, stride=TOK_SUB)
    return ref[idx, :] if lead is None else ref[lead, idx, :]


def _router_kernel(x_ref, mod_ref, g_ref, wr_ref, h_ref, info_ref, gates_ref):
    sh = mod_ref[:, 3 * D:4 * D]
    sc = mod_ref[:, 4 * D:5 * D]
    w_hi = wr_ref[0]
    w_lo = wr_ref[1]
    for c in range(ROUTER_TM // ROW_CHUNK):
        r0 = c * ROW_CHUNK
        h = _norm_mod(x_ref[r0:r0 + ROW_CHUNK, :], g_ref[...], sc, sh)
        _store_token_tiles(h_ref, r0, h)
        h_hi, h_lo = _split_bf16(h)
        logits = _dot(h_hi, w_hi) + _dot(h_lo, w_hi) + _dot(h_hi, w_lo)
        lane = lax.broadcasted_iota(I32, logits.shape, 1)
        lg = jnp.where(lane < N_EXPERTS, logits, -jnp.inf)
        m1 = jnp.max(lg, axis=-1, keepdims=True)
        i1 = jnp.min(jnp.where(lg == m1, lane, LANES), axis=-1, keepdims=True)
        lg2 = jnp.where(lane == i1, -jnp.inf, lg)
        m2 = jnp.max(lg2, axis=-1, keepdims=True)
        i2 = jnp.min(jnp.where(lg2 == m2, lane, LANES), axis=-1, keepdims=True)
        e2 = jnp.exp(m2 - m1)
        den = 1.0 + e2
        gates_ref[r0:r0 + ROW_CHUNK, :] = jnp.where(lane == 0, 1.0 / den, jnp.where(lane == 1, e2 / den, 0.0))
        info_ref[r0:r0 + ROW_CHUNK, :] = jnp.where(lane == 0, i1, jnp.where(lane == 1, i2, 0)).astype(I32)


def _router(x, mod4, layer, g_ffn, w_router):
    wr = jnp.zeros((D, LANES), F32).at[:, 0:N_EXPERTS].set(w_router)
    wr_hi = wr.astype(BF16)
    wr_lo = (wr - wr_hi.astype(F32)).astype(BF16)
    tok = pl.BlockSpec((ROUTER_TM, D), lambda i: (i, 0))
    nar = pl.BlockSpec((ROUTER_TM, LANES), lambda i: (i, 0))
    return pl.pallas_call(
        _router_kernel,
        grid=(T_ALL // ROUTER_TM,),
        in_specs=[tok, _mod_spec(layer, ROUTER_TM), _row_spec(), _full_spec((2, D, LANES))],
        out_specs=[pl.BlockSpec((ROUTER_TM * TOK_SUB, LANES), lambda i: (i, 0)), nar, nar],
        out_shape=[jax.ShapeDtypeStruct((T_ALL * TOK_SUB, LANES), F32),
                   jax.ShapeDtypeStruct((T_ALL, LANES), I32),
                   jax.ShapeDtypeStruct((T_ALL, LANES), F32)],
        compiler_params=_cparams(("arbitrary",)),
        name="moe_router",
    )(x, mod4, g_ffn.reshape(1, D), jnp.stack([wr_hi, wr_lo]))


def _route_plan(info):
    eid = jnp.arange(N_EXPERTS, dtype=I32)[None, :]
    first = info[:, 0:1] == eid
    second = info[:, 1:2] == eid
    onehot = jnp.logical_or(first, second).astype(I32)
    csum = jnp.cumsum(onehot, axis=0)
    rank = csum - onehot
    counts = csum[-1]
    padded = ((counts + TME - 1) // TME) * TME
    gend = jnp.cumsum(padded)
    gstart = gend - padded
    pos_te = gstart[None, :] + rank
    pos1 = jnp.sum(jnp.where(first, pos_te, 0), axis=1).astype(I32)
    pos2 = jnp.sum(jnp.where(second, pos_te, 0), axis=1).astype(I32)
    n_valid = gend[-1] // TME
    tile_start = jnp.arange(N_EXPERT_TILES, dtype=I32) * TME
    eff_start = jnp.minimum(tile_start, jnp.maximum(n_valid - 1, 0) * TME)
    tile_expert = jnp.minimum(jnp.sum((eff_start[:, None] >= gend[None, :]).astype(I32), axis=1),
                              N_EXPERTS - 1).astype(I32)
    valid_end = (gstart + counts)[tile_expert]
    tile_rows = jnp.where(tile_start < gend[-1], jnp.clip(valid_end - tile_start, 0, TME), 0).astype(I32)
    return pos1, pos2, tile_expert, n_valid.astype(I32).reshape(1), tile_rows


DISP_TM = 512
DMA_UNROLL = 8


def _dispatch_kernel(p1_ref, p2_ref, h_ref, xs_init, xs_hbm, sem):
    del xs_init
    i = pl.program_id(0)

    def body(g, carry):
        for k in range(DMA_UNROLL):
            r = g * DMA_UNROLL + k
            t = i * DISP_TM + r
            tile = h_ref.at[pl.ds(pl.multiple_of(r * TOK_SUB, TOK_SUB), TOK_SUB), :]
            for j, p_ref in enumerate((p1_ref, p2_ref)):
                dst = xs_hbm.at[pl.ds(pl.multiple_of(p_ref[t] * TOK_SUB, TOK_SUB), TOK_SUB), :]
                pltpu.make_async_copy(tile, dst, sem.at[j]).start(priority=j)
        return carry
    lax.fori_loop(0, DISP_TM // DMA_UNROLL, body, 0)
    for j in range(2):
        pltpu.make_async_copy(h_ref, xs_hbm.at[pl.ds(0, DISP_TM * TOK_SUB), :], sem.at[j]).wait()


def _dispatch(pos1, pos2, h, xs_init):
    return pl.pallas_call(
        _dispatch_kernel,
        grid_spec=pltpu.PrefetchScalarGridSpec(
            num_scalar_prefetch=2, grid=(T_ALL // DISP_TM,),
            in_specs=[pl.BlockSpec((DISP_TM * TOK_SUB, LANES), lambda i, *_: (i, 0)),
                      pl.BlockSpec(memory_space=pl.ANY)],
            out_specs=pl.BlockSpec(memory_space=pl.ANY),
            scratch_shapes=[pltpu.SemaphoreType.DMA((2,))]),
        out_shape=jax.ShapeDtypeStruct((P_MAX * TOK_SUB, LANES), F32),
        input_output_aliases={3: 0},
        compiler_params=_cparams(("arbitrary",)),
        name="moe_dispatch",
    )(pos1, pos2, h, xs_init)


EXP_TF = 512
EXP_NF = D_FF_EXPERT // EXP_TF


def _expert_kernel(te_ref, nv_ref, rows_ref, xs_ref, wg_ref, wu_ref, wd_ref, o_ref, xb_s, acc_s):
    m = pl.program_id(0)
    f = pl.program_id(1)
    nf = pl.num_programs(1)
    n_valid = nv_ref[0]
    quarter = TME // 4
    nq = (rows_ref[m] + quarter - 1) // quarter

    @pl.when(jnp.logical_and(m == 0, f == 0))
    def _():
        acc_s[...] = jnp.zeros_like(acc_s)

    def compute(mrows):
        @pl.when(f == 0)
        def _():
            for c in range(TOK_SUB):
                xb_s[0:mrows, c * LANES:(c + 1) * LANES] = _load_token_tiles(xs_ref, 0, mrows, c).astype(BF16)

        xb = xb_s[0:mrows, :]
        gg = _dot(xb, wg_ref[...].astype(BF16))
        uu = _dot(xb, wu_ref[...].astype(BF16))
        a = (gg * jax.nn.sigmoid(gg) * uu).astype(BF16)
        contrib = _dot(a, wd_ref[...].astype(BF16))

        acc_s[0:mrows, :] = jnp.where(f == 0, 0.0, acc_s[0:mrows, :]) + contrib

        @pl.when(f == nf - 1)
        def _():
            _store_token_tiles(o_ref, 0, acc_s[0:mrows, :])
            if mrows < TME:
                o_ref[mrows * TOK_SUB:TME * TOK_SUB, :] = jnp.zeros(((TME - mrows) * TOK_SUB, LANES), F32)

    for q in range(1, 5):
        pl.when(jnp.logical_and(m < n_valid, nq == q))(functools.partial(compute, q * quarter))

    @pl.when(jnp.logical_and(m >= n_valid, f == 0))
    def _():
        o_ref[...] = jnp.zeros_like(o_ref)


def _expert_ffn(tile_expert, n_valid, tile_rows, xs, w_gu, w_down, li):
    def feff(m, f, nv):
        return jnp.where(m < nv[0], f, EXP_NF - 1)

    def meff(m, nv):
        return jnp.minimum(m, nv[0] - 1)
    return pl.pallas_call(
        _expert_kernel,
        grid_spec=pltpu.PrefetchScalarGridSpec(
            num_scalar_prefetch=3, grid=(N_EXPERT_TILES, EXP_NF),
            in_specs=[pl.BlockSpec((TME * TOK_SUB, LANES), lambda m, f, te, nv, tr: (meff(m, nv), 0)),
                      pl.BlockSpec((None, None, D, EXP_TF),
                                   lambda m, f, te, nv, tr: (li, te[m], 0, feff(m, f, nv))),
                      pl.BlockSpec((None, None, D, EXP_TF),
                                   lambda m, f, te, nv, tr: (li, te[m], 0, EXP_NF + feff(m, f, nv))),
                      pl.BlockSpec((None, None, EXP_TF, D),
                                   lambda m, f, te, nv, tr: (li, te[m], feff(m, f, nv), 0))],
            out_specs=pl.BlockSpec((TME * TOK_SUB, LANES), lambda m, f, te, nv, tr: (m, 0)),
            scratch_shapes=[pltpu.VMEM((TME, D), BF16), pltpu.VMEM((TME, D), F32)]),
        out_shape=jax.ShapeDtypeStruct((P_MAX * TOK_SUB, LANES), F32),
        compiler_params=_cparams(("arbitrary", "arbitrary")),
        name="moe_expert_swiglu",
    )(tile_expert, n_valid, tile_rows, xs, w_gu, w_gu, w_down)


COMB_TM = 512


def _combine_kernel(p1_ref, p2_ref, x_ref, gates_ref, mod_ref, gf_ref, y_hbm, *rest, final):
    i = pl.program_id(0)
    if final:
        oc_ref, ol_ref, a_s, b_s, sem = rest
    else:
        o_ref, a_s, b_s, sem = rest

    def tile_copy(p, r, dst, s):
        return pltpu.make_async_copy(y_hbm.at[pl.ds(pl.multiple_of(p * TOK_SUB, TOK_SUB), TOK_SUB), :],
                                     dst.at[pl.ds(pl.multiple_of(r * TOK_SUB, TOK_SUB), TOK_SUB), :], s)

    def body(g, carry):
        for k in range(DMA_UNROLL):
            r = g * DMA_UNROLL + k
            t = i * COMB_TM + r
            tile_copy(p1_ref[t], r, a_s, sem.at[0]).start(priority=0)
            tile_copy(p2_ref[t], r, b_s, sem.at[1]).start(priority=1)
        return carry
    lax.fori_loop(0, COMB_TM // DMA_UNROLL, body, 0)
    pltpu.make_async_copy(y_hbm.at[pl.ds(0, COMB_TM * TOK_SUB), :], a_s, sem.at[0]).wait()
    pltpu.make_async_copy(y_hbm.at[pl.ds(0, COMB_TM * TOK_SUB), :], b_s, sem.at[1]).wait()

    def finish(out_ref):
        for c in range(COMB_TM // ROW_CHUNK):
            r0 = c * ROW_CHUNK
            w1 = gates_ref[r0:r0 + ROW_CHUNK, 0:1]
            w2 = gates_ref[r0:r0 + ROW_CHUNK, 1:2]
            for j in range(TOK_SUB):
                cols = slice(j * LANES, (j + 1) * LANES)
                y = (w1 * _load_token_tiles(a_s, r0, ROW_CHUNK, j) + w2 * _load_token_tiles(b_s, r0, ROW_CHUNK, j))
                gate = mod_ref[:, 5 * D + j * LANES:5 * D + (j + 1) * LANES]
                out_ref[r0:r0 + ROW_CHUNK, cols] = x_ref[r0:r0 + ROW_CHUNK, cols] + gate * y
            if final:
                xn = out_ref[r0:r0 + ROW_CHUNK, :]
                ms = jnp.mean(xn * xn, axis=-1, keepdims=True)
                out_ref[r0:r0 + ROW_CHUNK, :] = xn * lax.rsqrt(ms + EPS) * gf_ref[...]

    if final:
        is_ctx = i < T_CTX // COMB_TM

        @pl.when(i == 0)
        def _():
            ol_ref[...] = jnp.zeros_like(ol_ref)

        pl.when(is_ctx)(lambda: finish(oc_ref))
        pl.when(jnp.logical_not(is_ctx))(lambda: finish(ol_ref))
    else:
        finish(o_ref)


def _combine(pos1, pos2, x, gates, mod4, layer, g_final, y, final):
    tok = pl.BlockSpec((COMB_TM, D), lambda i, *_: (i, 0))
    if final:
        out_specs = list(_ctx_lat_specs(COMB_TM))
        out_shape = [jax.ShapeDtypeStruct((T_CTX, D), F32), jax.ShapeDtypeStruct((T_LAT, D), F32)]
    else:
        out_specs = tok
        out_shape = jax.ShapeDtypeStruct((T_ALL, D), F32)
    return pl.pallas_call(
        functools.partial(_combine_kernel, final=final),
        grid_spec=pltpu.PrefetchScalarGridSpec(
            num_scalar_prefetch=2, grid=(T_ALL // COMB_TM,),
            in_specs=[tok, pl.BlockSpec((COMB_TM, LANES), lambda i, *_: (i, 0)),
                      _mod_spec(layer, COMB_TM), _row_spec(),
                      pl.BlockSpec(memory_space=pl.ANY)],
            out_specs=out_specs,
            scratch_shapes=[pltpu.VMEM((COMB_TM * TOK_SUB, LANES), F32), pltpu.VMEM((COMB_TM * TOK_SUB, LANES), F32),
                            pltpu.SemaphoreType.DMA((2,))]),
        out_shape=out_shape,
        compiler_params=_cparams(("arbitrary",)),
        name="moe_combine_final" if final else "moe_combine",
    )(pos1, pos2, x, gates, mod4, g_final.reshape(1, D), y)


def _moe_layer(x, mod4, layer, g_ffn, w_router, w_gu, w_down, li, g_final, final, xs_init):
    h, info, gates = _router(x, mod4, layer, g_ffn, w_router)
    pos1, pos2, tile_expert, n_valid, tile_rows = _route_plan(info)
    xs = _dispatch(pos1, pos2, h, xs_init)
    ys = _expert_ffn(tile_expert, n_valid, tile_rows, xs, w_gu, w_down, li)
    return _combine(pos1, pos2, x, gates, mod4, layer, g_final, ys, final), xs


def kernel(x_prompt, x_sample, cache_k, cache_v, state_ssm_re, state_ssm_im, c, c_ctx, w_ada, b_ada, g_mix, g_ffn, g_final, conv_w_in, conv_w, conv_w_out, s5_lambda_re, s5_lambda_im, s5_b_re, s5_b_im, s5_c_re, s5_c_im, s5_log_dt, s5_d, s5_w_glu, attn_w_q, attn_w_kv, attn_w_o, attn_sink, fnet_w_out, ffn_w_gu, ffn_w_down, moe_w_router, moe_w_gu, moe_w_down):
    cond16 =jnp.concatenate([jnp.broadcast_to(c_ctx[None, :], (8, D)), c], axis=0)
    mod3 = _mod_all(cond16, w_ada, b_ada)
    mod4 = mod3.reshape(DEPTH, 16, 1, N_MOD * D)

    x = _conv_mixer(x_prompt.reshape(T_CTX, D), x_sample.reshape(T_LAT, D), mod4, 0, g_mix[0],
                    conv_w_in[0].astype(BF16), conv_w[0], conv_w_out[0].astype(BF16))
    x = _dense_ffn(x, mod4, 0, g_ffn[0], ffn_w_gu[0].astype(BF16), ffn_w_down[0].astype(BF16))

    lbr, lbi, bbr, bbi = _s5_prep(s5_lambda_re[0], s5_lambda_im[0], s5_log_dt[0], s5_b_re[0], s5_b_im[0])
    w_b, w_c, lam_s = _s5_weights(lbr, lbi, bbr, bbi, s5_c_re[0], s5_c_im[0])
    w_glu = s5_w_glu[0].astype(BF16)
    xc = x[0:T_CTX].reshape(BATCH, SEQ, D).transpose(1, 0, 2).reshape(T_CTX, D)
    xl = x[T_CTX:].reshape(DEC_BATCH, DEC_SEQ, D).transpose(1, 0, 2).reshape(T_LAT, D)
    h0_ctx = jnp.zeros((2, S5_JT, BATCH, 2 * S5_HALF), F32)
    h0_lat = _s5_state_in(state_ssm_re[:, 0], state_ssm_im[:, 0])
    yc, fin_c = _s5_scan(xc, BATCH, mod3, 1, 0, g_mix[1], w_b, w_c, lam_s, h0_ctx)
    yl, _ = _s5_scan(xl, DEC_BATCH, mod3, 1, 1, g_mix[1], w_b, w_c, lam_s, h0_lat)
    xc = _s5_glu(xc, yc, mod3, 1, 0, g_mix[1], s5_d[0], w_glu)
    xl = _s5_glu(xl, yl, mod3, 1, 1, g_mix[1], s5_d[0], w_glu)
    x = jnp.concatenate([xc.reshape(SEQ, BATCH, D).transpose(1, 0, 2).reshape(T_CTX, D),
                         xl.reshape(DEC_SEQ, DEC_BATCH, D).transpose(1, 0, 2).reshape(T_LAT, D)], axis=0)
    new_re, new_im = _s5_state_out(fin_c)
    xs_zero = jnp.zeros((P_MAX * TOK_SUB, LANES), F32)
    x, xs_buf = _moe_layer(x, mod4, 1, g_ffn[1], moe_w_router[0], moe_w_gu, moe_w_down, 0, g_final, False, xs_zero)

    wq = attn_w_q[0]
    wk = _expand_kv_cols(attn_w_kv[0][:, 0:N_KV_HEADS * HEAD_DIM])
    wv = _expand_kv_cols(attn_w_kv[0][:, N_KV_HEADS * HEAD_DIM:])
    wq_b, wk_b, wv_b = wq.astype(BF16), wk.astype(BF16), wv.astype(BF16)
    cos_t, sin_t = _rope_tables()
    q_c, k_c, v_c, kv_c = _qkv_ctx(x, mod4, 2, g_mix[2], wq_b, wk_b, wv_b, attn_w_kv[0].astype(BF16))
    q_l, k_l, v_l = _qkv_lat(x, mod4, 2, g_mix[2], wq_b, _rot_half_cols(wq).astype(BF16), wk_b,
                             _rot_half_cols(wk).astype(BF16), wv_b, cos_t, sin_t)
    sink = attn_sink[0]
    o_c = _attn_ctx(sink, q_c, k_c, v_c)
    expand = lambda a: jnp.broadcast_to(
        a.reshape(DEC_BATCH * PAST_LEN, N_KV_HEADS, 1, HEAD_DIM),
        (DEC_BATCH * PAST_LEN, N_KV_HEADS, Q_PER_KV, HEAD_DIM)).reshape(DEC_BATCH * PAST_LEN, D).astype(BF16)
    o_l = _attn_lat(sink, q_l, k_l, v_l, expand(cache_k[:, 0]), expand(cache_v[:, 0]))
    x = _resproj(x, o_c, o_l, mod4, 2, attn_w_o[0].astype(BF16))
    kvw = N_KV_HEADS * HEAD_DIM
    new_k = kv_c[:, 0:kvw].reshape(BATCH, 1, SEQ, N_KV_HEADS, HEAD_DIM)
    new_v = kv_c[:, kvw:].reshape(BATCH, 1, SEQ, N_KV_HEADS, HEAD_DIM)
    x = _dense_ffn(x, mod4, 2, g_ffn[2], ffn_w_gu[1].astype(BF16), ffn_w_down[1].astype(BF16))

    cs, fl = _fnet_tables()
    x = _fnet_mixer(x, mod4, 3, g_mix[3], cs, fl, fnet_w_out[0].astype(BF16))
    (y_ctx, y_lat), _ = _moe_layer(x, mod4, 3, g_ffn[3], moe_w_router[1], moe_w_gu, moe_w_down, 1, g_final, True,
                                   xs_buf)

    y_prompt = y_ctx.reshape(BATCH, SEQ, D)
    y_sample = y_lat.reshape(DEC_BATCH, DEC_SEQ, D)
    return (y_prompt, y_sample, new_k, new_v, new_re[:, None], new_im[:, None])
```

```python
import functools
import math

import numpy as np
import jax
import jax.numpy as jnp
from jax import lax
from jax.experimental import pallas as pl
from jax.experimental.pallas import tpu as pltpu

F32 = jnp.float32
BF16 = jnp.bfloat16
I32 = jnp.int32

D = 1024
BATCH = 16
SEQ = 256
DEPTH = 4
DEC_BATCH = 8
DEC_SEQ = 1024
PAST_LEN = 256
GRID_W = 64
EPS = 1e-6
N_MOD = 6
S5_GROUP = 16
S5_GROUPS = D // S5_GROUP
S5_STATE = 64
HEAD_DIM = 64
N_HEADS = D // HEAD_DIM
N_KV_HEADS = 4
Q_PER_KV = N_HEADS // N_KV_HEADS
WINDOW = 128
ROPE_THETA = 10000.0
ROPE_FREQS = HEAD_DIM // 4
FNET_GROUPS = 4
D_FF = 2816
N_EXPERTS = 8
TOP_K = 2
D_FF_EXPERT = 3584
NEG_INF = -1e30

T_CTX = BATCH * SEQ
T_LAT = DEC_BATCH * DEC_SEQ
T_ALL = T_CTX + T_LAT

VMEM_LIMIT_V7X = 56 * 1024 * 1024
LANES = 128

TM = 1024
ROW_CHUNK = 256
MM_CHUNK = 512
TME = 1024
N_EXPERT_TILES = (T_ALL * TOP_K) // TME + N_EXPERTS
P_MAX = N_EXPERT_TILES * TME


def _cparams(sem):
    return pltpu.CompilerParams(dimension_semantics=sem, vmem_limit_bytes=VMEM_LIMIT_V7X)


def _dot(a, b):
    return jnp.dot(a, b, preferred_element_type=F32)


def _dot_nt(a, b):
    return lax.dot_general(a, b, (((1,), (1,)), ((), ())), preferred_element_type=F32)


def _split_bf16(a):
    hi = a.astype(BF16)
    lo = (a - hi.astype(F32)).astype(BF16)
    return hi, lo


def _tile_rows(a, reps, axis=0):
    assert axis == 0
    return jnp.concatenate([a] * reps, axis=0)


def _norm_mod(x, g, sc, sh):
    ms = jnp.mean(x * x, axis=-1, keepdims=True)
    y = x * lax.rsqrt(ms + EPS) * g
    return y * (1.0 + sc) + sh


def _mod_row(i, tm):
    nct = T_CTX // tm
    lpb = DEC_SEQ // tm
    return jnp.where(i < nct, 0, 8 + (i - nct) // lpb)


def _mod_spec(layer, tm):
    return pl.BlockSpec((None, None, 1, N_MOD * D), lambda i, *_: (layer, _mod_row(i, tm), 0, 0))


def _row_spec():
    return pl.BlockSpec((1, D), lambda *_: (0, 0))


def _full_spec(shape):
    nd = len(shape)
    return pl.BlockSpec(shape, lambda *_: (0,) * nd, pipeline_mode=pl.Buffered(1))


def _mod_kernel(c_ref, w_ref, b_ref, o_ref):
    c = c_ref[...]
    s = (c * jax.nn.sigmoid(c)).astype(BF16)
    o_ref[...] = _dot(s, w_ref[...].astype(BF16)) + b_ref[...]


def _mod_all(cond16, w_ada, b_ada):
    tn = 1024
    return pl.pallas_call(
        _mod_kernel,
        grid=(DEPTH, N_MOD * D // tn),
        in_specs=[pl.BlockSpec((16, D), lambda l, n: (0, 0)),
                  pl.BlockSpec((None, D, tn), lambda l, n: (l, 0, n)),
                  pl.BlockSpec((None, 1, tn), lambda l, n: (l, 0, n))],
        out_specs=pl.BlockSpec((None, 16, tn), lambda l, n: (l, 0, n)),
        out_shape=jax.ShapeDtypeStruct((DEPTH, 16, N_MOD * D), F32),
        compiler_params=_cparams(("arbitrary", "arbitrary")),
        name="adaln_mod",
    )(cond16, w_ada, b_ada.reshape(DEPTH, 1, N_MOD * D))


def _conv_kernel(xc_ref, xl_ref, mod_ref, g_ref, win_ref, cw_ref, wout_ref, o_ref, gb_s, u_s, z_s):
    i = pl.program_id(0)
    is_ctx = i < (T_CTX // TM)

    def x_rows(r0, n):
        return jnp.where(is_ctx, xc_ref[r0:r0 + n, :], xl_ref[r0:r0 + n, :])

    sh = mod_ref[:, 0:D]
    sc = mod_ref[:, D:2 * D]
    gate = mod_ref[:, 2 * D:3 * D]
    g = g_ref[...]
    zero8 = jnp.zeros((8, D), F32)
    u_s[0:8, :] = zero8
    u_s[8 + TM:16 + TM, :] = zero8
    for c in range(TM // MM_CHUNK):
        r0 = c * MM_CHUNK
        h = _norm_mod(x_rows(r0, MM_CHUNK), g, sc, sh).astype(BF16)
        proj = _dot(h, win_ref[...])
        gb_s[r0:r0 + MM_CHUNK, :] = proj[:, 0:D]
        u_s[8 + r0:8 + r0 + MM_CHUNK, :] = proj[:, D:2 * D] * proj[:, 2 * D:3 * D]
    row = lax.broadcasted_iota(I32, (ROW_CHUNK, 1), 0)
    first = jnp.logical_and(is_ctx, row == 0)
    last = jnp.logical_and(is_ctx, row == ROW_CHUNK - 1)
    for c in range(TM // ROW_CHUNK):
        r0 = c * ROW_CHUNK
        up = jnp.where(first, 0.0, u_s[7 + r0:7 + r0 + ROW_CHUNK, :])
        mid = u_s[8 + r0:8 + r0 + ROW_CHUNK, :]
        dn = jnp.where(last, 0.0, u_s[9 + r0:9 + r0 + ROW_CHUNK, :])
        conv = up * cw_ref[0:1, :] + mid * cw_ref[1:2, :] + dn * cw_ref[2:3, :]
        z_s[r0:r0 + ROW_CHUNK, :] = (gb_s[r0:r0 + ROW_CHUNK, :] * conv).astype(BF16)
    for c in range(TM // MM_CHUNK):
        r0 = c * MM_CHUNK
        y = _dot(z_s[r0:r0 + MM_CHUNK, :], wout_ref[...])
        o_ref[r0:r0 + MM_CHUNK, :] = x_rows(r0, MM_CHUNK) + gate * y


def _conv_mixer(x_ctx, x_lat, mod4, layer, g_mix, w_in, conv_w, w_out):
    assert SEQ == ROW_CHUNK and DEC_SEQ == TM
    ctx, lat = _ctx_lat_specs(TM)
    return pl.pallas_call(
        _conv_kernel,
        grid=(T_ALL // TM,),
        in_specs=[ctx, lat, _mod_spec(layer, TM), _row_spec(),
                  _full_spec((D, 3 * D)), _full_spec((3, D)), _full_spec((D, D))],
        out_specs=pl.BlockSpec((TM, D), lambda i: (i, 0)),
        out_shape=jax.ShapeDtypeStruct((T_ALL, D), F32),
        scratch_shapes=[pltpu.VMEM((TM, D), F32), pltpu.VMEM((TM + 16, D), F32), pltpu.VMEM((TM, D), BF16)],
        compiler_params=_cparams(("arbitrary",)),
        name="conv_mixer",
    )(x_ctx, x_lat, mod4, g_mix.reshape(1, D), w_in, conv_w, w_out)


FFN_TM = 512
MXU_WIDTH_V7X = 256
FFN_SPLITS = ((0, 6 * MXU_WIDTH_V7X), (6 * MXU_WIDTH_V7X, D_FF))
assert D_FF % MXU_WIDTH_V7X == 0


def _ffn_kernel(x_ref, mod_ref, g_ref, wgu_ref, wd_ref, o_ref):
    sh = mod_ref[:, 3 * D:4 * D]
    sc = mod_ref[:, 4 * D:5 * D]
    gate = mod_ref[:, 5 * D:6 * D]
    x = x_ref[...]
    h = _norm_mod(x, g_ref[...], sc, sh).astype(BF16)
    acc = None
    for lo, hi in FFN_SPLITS:
        gg = _dot(h, wgu_ref[:, lo:hi])
        uu = _dot(h, wgu_ref[:, D_FF + lo:D_FF + hi])
        a = (gg * jax.nn.sigmoid(gg) * uu).astype(BF16)
        contrib = _dot(a, wd_ref[lo:hi, :])
        acc = contrib if acc is None else acc + contrib
    o_ref[...] = x + gate * acc


def _dense_ffn(x, mod4, layer, g_ffn, w_gu, w_down):
    tok = pl.BlockSpec((FFN_TM, D), lambda i: (i, 0))
    return pl.pallas_call(
        _ffn_kernel,
        grid=(T_ALL // FFN_TM,),
        in_specs=[tok, _mod_spec(layer, FFN_TM), _row_spec(),
                  _full_spec((D, 2 * D_FF)), _full_spec((D_FF, D))],
        out_specs=tok,
        out_shape=jax.ShapeDtypeStruct((T_ALL, D), F32),
        compiler_params=_cparams(("arbitrary",)),
        name="dense_swiglu",
    )(x, mod4, g_ffn.reshape(1, D), w_gu, w_down)


def _s5_prep_kernel(lr_ref, li_ref, ldt_ref, br_ref, bi_ref, lbr_ref, lbi_ref, bbr_ref, bbi_ref):
    lr = lr_ref[...]
    li = li_ref[...]
    dt = jnp.exp(ldt_ref[...])
    mag = jnp.exp(lr * dt)
    ar = mag * jnp.cos(li * dt)
    ai = mag * jnp.sin(li * dt)
    nr = ar - 1.0
    den = lr * lr + li * li
    fr = (nr * lr + ai * li) / den
    fi = (ai * lr - nr * li) / den
    br = br_ref[...]
    bi = bi_ref[...]
    lbr_ref[...] = ar
    lbi_ref[...] = ai
    bbr_ref[...] = fr * br - fi * bi
    bbi_ref[...] = fr * bi + fi * br


def _s5_prep(lam_re, lam_im, log_dt, b_re, b_im):
    rows = 2 * S5_GROUPS
    cols = S5_STATE * S5_GROUP
    exp = lambda a: jnp.repeat(a.reshape(rows, S5_STATE), S5_GROUP, axis=1)
    ldt = jnp.broadcast_to(log_dt.reshape(rows, 1), (rows, cols))
    outs = pl.pallas_call(
        _s5_prep_kernel,
        out_shape=[jax.ShapeDtypeStruct((rows, cols), F32)] * 4,
        name="s5_discretize",
    )(exp(lam_re), exp(lam_im), ldt, b_re.reshape(rows, cols), b_im.reshape(rows, cols))
    lbr, lbi, bbr, bbi = outs
    shp = (2, S5_GROUPS, S5_STATE, S5_GROUP)
    return lbr.reshape(shp)[..., 0], lbi.reshape(shp)[..., 0], bbr.reshape(shp), bbi.reshape(shp)


S5_JT = 8
S5_GPT = LANES // S5_GROUP
S5_HALF = S5_GPT * S5_STATE
S5_ROWS = 512


def _s5_scan_kernel(x_ref, mod_ref, g_ref, wb_ref, wc_ref, lam_ref, h0_ref,
                    y_ref, fin_ref, bu_s, st_s, *, nb, jgroup):
    d = pl.program_id(0)
    c = pl.program_id(1)
    lc = S5_ROWS // nb

    @pl.when(c == 0)
    def _():
        st_s[...] = h0_ref[...]

    rep = S5_ROWS // 8
    sh = _tile_rows(mod_ref[:, 0:D], rep, axis=0)
    sc = _tile_rows(mod_ref[:, D:2 * D], rep, axis=0)
    u = _norm_mod(x_ref[...], g_ref[...], sc, sh).astype(BF16)
    for j in range(S5_JT):
        bu_s[j] = _dot(u[:, j * LANES:(j + 1) * LANES], wb_ref[j])

    for j0 in range(0, S5_JT, jgroup):
        js = list(range(j0, j0 + jgroup))
        lam = [(jnp.broadcast_to(lam_ref[j][:, 0:S5_HALF], (nb, S5_HALF)),
                jnp.broadcast_to(lam_ref[j][:, S5_HALF:], (nb, S5_HALF))) for j in js]

        def body(t, carry):
            l = jnp.where(d == 0, t, lc - 1 - t)
            r0 = pl.multiple_of(l * nb, nb)
            out = []
            for k, j in enumerate(js):
                sr, si = carry[k]
                ar, ai = lam[k]
                bu = bu_s[j, pl.ds(r0, nb), :]
                hr = ar * sr - ai * si + bu[:, 0:S5_HALF]
                hi = ar * si + ai * sr + bu[:, S5_HALF:]
                bu_s[j, pl.ds(r0, nb), 0:S5_HALF] = hr
                bu_s[j, pl.ds(r0, nb), S5_HALF:] = hi
                out.append((hr, hi))
            return tuple(out)

        init = tuple((st_s[j][:, 0:S5_HALF], st_s[j][:, S5_HALF:]) for j in js)
        fin = lax.fori_loop(0, lc, body, init)
        for k, j in enumerate(js):
            st_s[j, :, 0:S5_HALF] = fin[k][0]
            st_s[j, :, S5_HALF:] = fin[k][1]

    for j in range(S5_JT):
        y_ref[:, j * LANES:(j + 1) * LANES] = _dot(bu_s[j].astype(BF16), wc_ref[j])
    fin_ref[...] = st_s[...]


def _s5_scan(xt, nb, mod3, layer, path, g_mix, w_b, w_c, lam_s, h0):
    rows = xt.shape[0]
    nc = rows // S5_ROWS
    chunk = lambda d, c: c + d * (nc - 1 - 2 * c)
    kern = functools.partial(_s5_scan_kernel, nb=nb, jgroup=2 if nb == 8 else 1)
    return pl.pallas_call(
        kern,
        grid=(2, nc),
        in_specs=[pl.BlockSpec((S5_ROWS, D), lambda d, c: (chunk(d, c), 0)),
                  pl.BlockSpec((None, 8, N_MOD * D), lambda d, c: (layer, path, 0)),
                  _row_spec(),
                  pl.BlockSpec((None, S5_JT, LANES, 2 * S5_HALF), lambda d, c: (d, 0, 0, 0)),
                  pl.BlockSpec((None, S5_JT, 2 * S5_HALF, LANES), lambda d, c: (d, 0, 0, 0)),
                  pl.BlockSpec((None, S5_JT, 1, 2 * S5_HALF), lambda d, c: (d, 0, 0, 0)),
                  pl.BlockSpec((None, S5_JT, nb, 2 * S5_HALF), lambda d, c: (d, 0, 0, 0))],
        out_specs=[pl.BlockSpec((None, S5_ROWS, D), lambda d, c: (d, chunk(d, c), 0)),
                   pl.BlockSpec((None, S5_JT, nb, 2 * S5_HALF), lambda d, c: (d, 0, 0, 0))],
        out_shape=[jax.ShapeDtypeStruct((2, rows, D), F32),
                   jax.ShapeDtypeStruct((2, S5_JT, nb, 2 * S5_HALF), F32)],
        scratch_shapes=[pltpu.VMEM((S5_JT, S5_ROWS, 2 * S5_HALF), F32),
                        pltpu.VMEM((S5_JT, nb, 2 * S5_HALF), F32)],
        compiler_params=_cparams(("arbitrary", "arbitrary")),
        name="s5_scan_b%d" % nb,
    )(xt, mod3, g_mix.reshape(1, D), w_b, w_c, lam_s, h0)


S5_GLU_ROWS = 512


def _s5_glu_kernel(x_ref, yf_ref, yb_ref, mod_ref, g_ref, dsk_ref, w_ref, o_ref):
    rep = MM_CHUNK // 8
    sh = _tile_rows(mod_ref[:, 0:D], rep, axis=0)
    sc = _tile_rows(mod_ref[:, D:2 * D], rep, axis=0)
    gate = _tile_rows(mod_ref[:, 2 * D:3 * D], rep, axis=0)
    for c in range(S5_GLU_ROWS // MM_CHUNK):
        r0 = c * MM_CHUNK
        x = x_ref[r0:r0 + MM_CHUNK, :]
        u = _norm_mod(x, g_ref[...], sc, sh)
        y = u * dsk_ref[...] + yf_ref[r0:r0 + MM_CHUNK, :] + yb_ref[r0:r0 + MM_CHUNK, :]
        z = jax.nn.gelu(y).astype(BF16)
        ag = _dot(z, w_ref[...])
        out = ag[:, 0:D] * jax.nn.sigmoid(ag[:, D:2 * D])
        o_ref[r0:r0 + MM_CHUNK, :] = x + gate * out


def _s5_glu(xt, y2, mod3, layer, path, g_mix, d_skip, w_glu):
    rows = xt.shape[0]
    return pl.pallas_call(
        _s5_glu_kernel,
        grid=(rows // S5_GLU_ROWS,),
        in_specs=[pl.BlockSpec((S5_GLU_ROWS, D), lambda i: (i, 0)),
                  pl.BlockSpec((None, S5_GLU_ROWS, D), lambda i: (0, i, 0)),
                  pl.BlockSpec((None, S5_GLU_ROWS, D), lambda i: (1, i, 0)),
                  pl.BlockSpec((None, 8, N_MOD * D), lambda i: (layer, path, 0)),
                  _row_spec(), _row_spec(), _full_spec((D, 2 * D))],
        out_specs=pl.BlockSpec((S5_GLU_ROWS, D), lambda i: (i, 0)),
        out_shape=jax.ShapeDtypeStruct((rows, D), F32),
        compiler_params=_cparams(("arbitrary",)),
        name="s5_glu",
    )(xt, y2, y2, mod3, g_mix.reshape(1, D), d_skip.reshape(1, D), w_glu)


def _s5_weights(lbr, lbi, bbr, bbi, c_re, c_im):
    eye = jnp.eye(S5_GPT, dtype=F32)
    bb = jnp.stack([bbr, bbi]).reshape(2, 2, S5_JT, S5_GPT, S5_STATE, S5_GROUP)
    w_bu = jnp.einsum('rdjgps,gh->djgsrhp', bb, eye).reshape(2, S5_JT, LANES, 2 * S5_HALF)
    w_b = w_bu.astype(BF16)
    cc =jnp.stack([c_re, -c_im]).reshape(2, 2, S5_JT, S5_GPT, S5_GROUP, S5_STATE)
    w_c = jnp.einsum('rdjgsp,gh->djrgphs', cc, eye).reshape(2, S5_JT, 2 * S5_HALF, LANES).astype(BF16)
    lam_s = jnp.concatenate([lbr.reshape(2, S5_JT, S5_HALF), lbi.reshape(2, S5_JT, S5_HALF)], axis=-1)
    return w_b, w_c, lam_s.reshape(2, S5_JT, 1, 2 * S5_HALF)


def _s5_state_in(st_re, st_im):
    def lay(a):
        b = a.shape[0]
        return a.transpose(1, 0, 2, 3).reshape(2, b, S5_JT, S5_HALF).transpose(0, 2, 1, 3)
    return jnp.concatenate([lay(st_re), lay(st_im)], axis=-1)


def _s5_state_out(fin):
    def lay(a):
        b = a.shape[2]
        return a.transpose(2, 0, 1, 3).reshape(b, 2, S5_GROUPS, S5_STATE)
    return lay(fin[..., 0:S5_HALF]), lay(fin[..., S5_HALF:])


QKV_TM = 512


def _qkv_ctx_kernel(x_ref, mod_ref, g_ref, wq_ref, wk_ref, wv_ref, wkv_ref, q_ref, k_ref, v_ref, kv_ref):
    sh = mod_ref[:, 0:D]
    sc = mod_ref[:, D:2 * D]
    for c in range(QKV_TM // MM_CHUNK):
        r0 = c * MM_CHUNK
        h = _norm_mod(x_ref[r0:r0 + MM_CHUNK, :], g_ref[...], sc, sh).astype(BF16)
        q_ref[r0:r0 + MM_CHUNK, :] = _dot(h, wq_ref[...]).astype(BF16)
        k_ref[r0:r0 + MM_CHUNK, :] = _dot(h, wk_ref[...]).astype(BF16)
        v_ref[r0:r0 + MM_CHUNK, :] = _dot(h, wv_ref[...]).astype(BF16)
        kv_ref[r0:r0 + MM_CHUNK, :] = _dot(h, wkv_ref[...])


def _qkv_ctx(x, mod4, layer, g_mix, wq, wk, wv, wkv):
    bf = jax.ShapeDtypeStruct((T_CTX, D), BF16)
    tok = pl.BlockSpec((QKV_TM, D), lambda i: (i, 0))
    kvw = wkv.shape[1]
    return pl.pallas_call(
        _qkv_ctx_kernel,
        grid=(T_CTX // QKV_TM,),
        in_specs=[tok, _mod_spec(layer, QKV_TM), _row_spec(),
                  _full_spec((D, D)), _full_spec((D, D)), _full_spec((D, D)), _full_spec((D, kvw))],
        out_specs=[tok, tok, tok, pl.BlockSpec((QKV_TM, kvw), lambda i: (i, 0))],
        out_shape=[bf, bf, bf, jax.ShapeDtypeStruct((T_CTX, kvw), F32)],
        compiler_params=_cparams(("arbitrary",)),
        name="qkv_ctx",
    )(x, mod4, g_mix.reshape(1, D), wq, wk, wv, wkv)


def _qkv_lat_kernel(x_ref, mod_ref, g_ref, wq_ref, wqr_ref, wk_ref, wkr_ref, wv_ref, cos_ref, sin_ref,
                    q_ref, k_ref, v_ref):
    sh = mod_ref[:, 0:D]
    sc = mod_ref[:, D:2 * D]
    for c in range(QKV_TM // MM_CHUNK):
        r0 = c * MM_CHUNK
        h = _norm_mod(x_ref[r0:r0 + MM_CHUNK, :], g_ref[...], sc, sh).astype(BF16)
        cos = cos_ref[r0:r0 + MM_CHUNK, :]
        sin = sin_ref[r0:r0 + MM_CHUNK, :]
        q_ref[r0:r0 + MM_CHUNK, :] = (_dot(h, wq_ref[...]) * cos + _dot(h, wqr_ref[...]) * sin).astype(BF16)
        k_ref[r0:r0 + MM_CHUNK, :] = (_dot(h, wk_ref[...]) * cos + _dot(h, wkr_ref[...]) * sin).astype(BF16)
        v_ref[r0:r0 + MM_CHUNK, :] = _dot(h, wv_ref[...]).astype(BF16)


def _qkv_lat(x, mod4, layer, g_mix, wq, wq_rot, wk, wk_rot, wv, cos_t, sin_t):
    nct = T_CTX // QKV_TM
    lpb = DEC_SEQ // QKV_TM
    bf = jax.ShapeDtypeStruct((T_LAT, D), BF16)
    tok_out = pl.BlockSpec((QKV_TM, D), lambda i: (i, 0))
    rope = pl.BlockSpec((QKV_TM, D), lambda i: (i % lpb, 0))
    return pl.pallas_call(
        _qkv_lat_kernel,
        grid=(T_LAT // QKV_TM,),
        in_specs=[pl.BlockSpec((QKV_TM, D), lambda i: (i + nct, 0)),
                  pl.BlockSpec((None, None, 1, N_MOD * D), lambda i: (layer, 8 + i // lpb, 0, 0)),
                  _row_spec()] + [_full_spec((D, D))] * 5 + [rope, rope],
        out_specs=[tok_out, tok_out, tok_out],
        out_shape=[bf, bf, bf],
        compiler_params=_cparams(("arbitrary",)),
        name="qkv_lat",
    )(x, mod4, g_mix.reshape(1, D), wq, wq_rot, wk, wk_rot, wv, cos_t, sin_t)


KVW = Q_PER_KV * HEAD_DIM


def _head_masks(rows):
    lane = lax.broadcasted_iota(I32, (rows, KVW), 1)
    return [jnp.logical_and(lane >= g * HEAD_DIM, lane < (g + 1) * HEAD_DIM) for g in range(Q_PER_KV)]


def _attn_ctx_kernel(sink_ref, q_ref, k_ref, v_ref, o_ref):
    scale = HEAD_DIM ** -0.5
    masks = _head_masks(SEQ)
    for kv in range(N_KV_HEADS):
        c0 = kv * KVW
        q = q_ref[:, c0:c0 + KVW]
        k = k_ref[:, c0:c0 + KVW]
        v = v_ref[:, c0:c0 + KVW]
        acc = jnp.zeros((SEQ, KVW), F32)
        for g in range(Q_PER_KV):
            sink = sink_ref[kv * Q_PER_KV + g]
            qg = jnp.where(masks[g], q, jnp.zeros_like(q))
            s = _dot_nt(qg, k) * scale
            m = jnp.maximum(jnp.max(s, axis=-1, keepdims=True), sink)
            e = jnp.exp(s - m)
            den = jnp.sum(e, axis=-1, keepdims=True) + jnp.exp(sink - m)
            og = _dot(e.astype(BF16), v) / den
            acc = jnp.where(masks[g], og, acc)
        o_ref[:, c0:c0 + KVW] = acc.astype(BF16)


def _attn_ctx(sink, q, k, v):
    tok = pl.BlockSpec((SEQ, D), lambda b, *_: (b, 0))
    return pl.pallas_call(
        _attn_ctx_kernel,
        grid_spec=pltpu.PrefetchScalarGridSpec(
            num_scalar_prefetch=1, grid=(BATCH,),
            in_specs=[tok, tok, tok], out_specs=tok),
        out_shape=jax.ShapeDtypeStruct((T_CTX, D), BF16),
        compiler_params=_cparams(("arbitrary",)),
        name="attn_ctx",
    )(sink, q, k, v)


ATT_TQ = 128
ATT_SPAN = ATT_TQ + 2 * WINDOW
assert math.log2(HEAD_DIM ** -0.5).is_integer()


def _attn_lat_kernel(sink_ref, q_ref, k_ref, v_ref, ck_ref, cv_ref, o_ref):
    qb = pl.program_id(1)
    scale = HEAD_DIM ** -0.5
    w0 = pl.multiple_of(jnp.clip(qb * ATT_TQ - WINDOW, 0, DEC_SEQ - ATT_SPAN), ATT_TQ)
    rows = Q_PER_KV * ATT_TQ
    ridx = lax.broadcasted_iota(I32, (rows, ATT_SPAN), 0)
    qpos = qb * ATT_TQ + (ridx & (ATT_TQ - 1))
    kpos = w0 + lax.broadcasted_iota(I32, (rows, ATT_SPAN), 1)
    valid = jnp.abs(qpos - kpos) <= WINDOW
    rcol = lax.broadcasted_iota(I32, (rows, 1), 0)
    masks = _head_masks(ATT_TQ)
    for kv in range(N_KV_HEADS):
        c0 = kv * KVW
        q = q_ref[:, c0:c0 + KVW] * scale
        qs = jnp.concatenate([jnp.where(masks[g], q, jnp.zeros_like(q)) for g in range(Q_PER_KV)], axis=0)
        sink = jnp.zeros((rows, 1), F32)
        for g in range(Q_PER_KV):
            sink = jnp.where(rcol >= g * ATT_TQ, sink_ref[kv * Q_PER_KV + g], sink)
        s_ctx = _dot_nt(qs, ck_ref[:, c0:c0 + KVW])
        s_win = _dot_nt(qs, k_ref[pl.ds(w0, ATT_SPAN), c0:c0 + KVW])
        s_win = jnp.where(valid, s_win, NEG_INF)
        m = jnp.maximum(jnp.maximum(jnp.max(s_ctx, axis=-1, keepdims=True),
                                    jnp.max(s_win, axis=-1, keepdims=True)), sink)
        e_ctx = jnp.exp(s_ctx - m)
        e_win = jnp.exp(s_win - m)
        den = (jnp.exp(sink - m) + jnp.sum(e_ctx, axis=-1, keepdims=True)
               + jnp.sum(e_win, axis=-1, keepdims=True))
        o = (_dot(e_ctx.astype(BF16), cv_ref[:, c0:c0 + KVW])
             + _dot(e_win.astype(BF16), v_ref[pl.ds(w0, ATT_SPAN), c0:c0 + KVW])) / den
        acc = jnp.zeros((ATT_TQ, KVW), F32)
        for g in range(Q_PER_KV):
            acc = jnp.where(masks[g], o[g * ATT_TQ:(g + 1) * ATT_TQ, :], acc)
        o_ref[:, c0:c0 + KVW] = acc.astype(BF16)


def _attn_lat(sink, q, k, v, ck, cv):
    nqb = DEC_SEQ // ATT_TQ
    qspec = pl.BlockSpec((ATT_TQ, D), lambda b, i, *_: (b * nqb + i, 0))
    seq = pl.BlockSpec((DEC_SEQ, D), lambda b, i, *_: (b, 0))
    ctx = pl.BlockSpec((PAST_LEN, D), lambda b, i, *_: (b, 0))
    return pl.pallas_call(
        _attn_lat_kernel,
        grid_spec=pltpu.PrefetchScalarGridSpec(
            num_scalar_prefetch=1, grid=(DEC_BATCH, nqb),
            in_specs=[qspec, seq, seq, ctx, ctx], out_specs=qspec),
        out_shape=jax.ShapeDtypeStruct((T_LAT, D), BF16),
        compiler_params=_cparams(("arbitrary", "arbitrary")),
        name="attn_lat",
    )(sink, q, k, v, ck, cv)


def _ctx_lat_specs(tm):
    nct = T_CTX // tm
    ctx = pl.BlockSpec((tm, D), lambda i, *_: (jnp.minimum(i, nct - 1), 0))
    lat = pl.BlockSpec((tm, D), lambda i, *_: (jnp.maximum(i - nct, 0), 0))
    return ctx, lat


def _resproj_kernel(x_ref, ac_ref, al_ref, mod_ref, w_ref, o_ref):
    gate = mod_ref[:, 2 * D:3 * D]
    is_ctx = pl.program_id(0) < T_CTX // TM
    a = jnp.where(is_ctx, ac_ref[...], al_ref[...])
    o_ref[...] = x_ref[...] + gate * _dot(a, w_ref[...])


def _resproj(x, a_ctx, a_lat, mod4, layer, w):
    tok = pl.BlockSpec((TM, D), lambda i: (i, 0))
    ctx, lat = _ctx_lat_specs(TM)
    return pl.pallas_call(
        _resproj_kernel,
        grid=(T_ALL // TM,),
        in_specs=[tok, ctx, lat, _mod_spec(layer, TM), _full_spec((D, D))],
        out_specs=tok,
        out_shape=jax.ShapeDtypeStruct((T_ALL, D), F32),
        compiler_params=_cparams(("arbitrary",)),
        name="attn_out_proj",
    )(x, a_ctx, a_lat, mod4, w)


def _rope_tables():
    rows = DEC_SEQ // GRID_W
    row = jnp.repeat(jnp.arange(rows), GRID_W).astype(F32)
    col = jnp.tile(jnp.arange(GRID_W), rows).astype(F32)
    inv = ROPE_THETA ** (-jnp.arange(ROPE_FREQS, dtype=F32) / ROPE_FREQS)
    ang = jnp.concatenate([row[:, None] * inv, col[:, None] * inv], axis=-1)
    cos = jnp.cos(ang)
    sin = jnp.sin(ang)
    cos_h = jnp.concatenate([cos, cos], axis=-1)
    sin_h = jnp.concatenate([sin, sin], axis=-1)
    return jnp.tile(cos_h, (1, N_HEADS)), jnp.tile(sin_h, (1, N_HEADS))


def _rot_half_cols(w):
    k = w.shape[0]
    w4 = w.reshape(k, -1, 2, HEAD_DIM // 2)
    return jnp.stack([-w4[:, :, 1], w4[:, :, 0]], axis=2).reshape(k, -1)


def _expand_kv_cols(w):
    k = w.shape[0]
    w3 = w.reshape(k, N_KV_HEADS, 1, HEAD_DIM)
    return jnp.broadcast_to(w3, (k, N_KV_HEADS, Q_PER_KV, HEAD_DIM)).reshape(k, N_HEADS * HEAD_DIM)


FG = D // FNET_GROUPS


def _fnet_kernel(x_ref, mod_ref, g_ref, cs_ref, fl_ref, w_ref, o_ref, h_s, ab_s):
    sh = mod_ref[:, 0:D]
    sc = mod_ref[:, D:2 * D]
    gate = mod_ref[:, 2 * D:3 * D]
    for c in range(TM // ROW_CHUNK):
        r0 = c * ROW_CHUNK
        h_s[r0:r0 + ROW_CHUNK, :] = _norm_mod(x_ref[r0:r0 + ROW_CHUNK, :], g_ref[...], sc, sh).astype(BF16)
    for g in range(FNET_GROUPS):
        ab = _dot(h_s[:, g * FG:(g + 1) * FG], cs_ref[...])
        ab_s[0:TM, g * FG:(g + 1) * FG] = ab[:, 0:FG].astype(BF16)
        ab_s[TM:2 * TM, g * FG:(g + 1) * FG] = ab[:, FG:2 * FG].astype(BF16)
    for c in range(TM // MM_CHUNK):
        r0 = c * MM_CHUNK
        f = _dot(fl_ref[r0:r0 + MM_CHUNK, :], ab_s[...])
        y = _dot(f.astype(BF16), w_ref[...])
        o_ref[r0:r0 + MM_CHUNK, :] = x_ref[r0:r0 + MM_CHUNK, :] + gate * y


def _fnet_mixer(x, mod4, layer, g_mix, cs, fl, w_out):
    nct = T_CTX // TM
    tok = pl.BlockSpec((TM, D), lambda i: (i, 0))
    return pl.pallas_call(
        _fnet_kernel,
        grid=(T_ALL // TM,),
        in_specs=[tok, _mod_spec(layer, TM), _row_spec(), _full_spec((FG, 2 * FG)),
                  pl.BlockSpec((None, TM, 2 * TM), lambda i: (jnp.where(i < nct, 0, 1), 0, 0)),
                  _full_spec((D, D))],
        out_specs=tok,
        out_shape=jax.ShapeDtypeStruct((T_ALL, D), F32),
        scratch_shapes=[pltpu.VMEM((TM, D), BF16), pltpu.VMEM((2 * TM, D), BF16)],
        compiler_params=_cparams(("arbitrary",)),
        name="fnet_mixer",
    )(x, mod4, g_mix.reshape(1, D), cs, fl, w_out)


def _dft_cos_sin(n):
    k = np.arange(n)
    ang = 2.0 * np.pi * ((k[:, None] * k[None, :]) % n) / n
    return np.cos(ang), np.sin(ang)


def _fnet_tables():
    cc, sc = _dft_cos_sin(FG)
    cs = np.concatenate([cc, sc], axis=1) / math.sqrt(FG)
    mats = []
    for seq in (SEQ, DEC_SEQ):
        cl, sl = _dft_cos_sin(seq)
        reps = TM // seq
        eye = np.eye(reps)
        mats.append(np.concatenate([np.kron(eye, cl), -np.kron(eye, sl)], axis=1) / math.sqrt(seq))
    return jnp.asarray(cs, F32).astype(BF16), jnp.asarray(np.stack(mats), F32).astype(BF16)


ROUTER_TM = 512
TOK_SUB = D // LANES


def _store_token_tiles(ref, r0, val):
    rows = val.shape[0]
    for c in range(TOK_SUB):
        ref[pl.ds(r0 * TOK_SUB + c, rows, stride=TOK_SUB), :] = val[:, c * LANES:(c + 1) * LANES]


def _load_token_tiles(ref, r0, rows, c, lead=None):
    idx = pl.ds(r0 * TOK_SUB + c, rows, stride=TOK_SUB)
    return ref[idx, :] if lead is None else ref[lead, idx, :]


def _x_pair(x):
    return x if isinstance(x, tuple) else (x, x)


def _x_pair_specs(x, tm):
    nct = T_CTX // tm
    if isinstance(x, tuple):
        return list(_ctx_lat_specs(tm))
    return [pl.BlockSpec((tm, D), lambda i, *_: (jnp.minimum(i, nct - 1), 0)),
            pl.BlockSpec((tm, D), lambda i, *_: (jnp.maximum(i, nct), 0))]


def _router_kernel(xc_ref, xl_ref, mod_ref, g_ref, wr_ref, h_ref, info_ref, gates_ref):
    sh = mod_ref[:, 3 * D:4 * D]
    sc = mod_ref[:, 4 * D:5 * D]
    w_hi = wr_ref[0]
    w_lo = wr_ref[1]
    is_ctx = pl.program_id(0) < T_CTX // ROUTER_TM
    for c in range(ROUTER_TM // ROW_CHUNK):
        r0 = c * ROW_CHUNK
        x = jnp.where(is_ctx, xc_ref[r0:r0 + ROW_CHUNK, :], xl_ref[r0:r0 + ROW_CHUNK, :])
        h = _norm_mod(x, g_ref[...], sc, sh)
        _store_token_tiles(h_ref, r0, h)
        h_hi, h_lo = _split_bf16(h)
        logits = _dot(h_hi, w_hi) + _dot(h_lo, w_hi) + _dot(h_hi, w_lo)
        lane = lax.broadcasted_iota(I32, logits.shape, 1)
        lg = jnp.where(lane < N_EXPERTS, logits, -jnp.inf)
        m1 = jnp.max(lg, axis=-1, keepdims=True)
        i1 = jnp.min(jnp.where(lg == m1, lane, LANES), axis=-1, keepdims=True)
        lg2 = jnp.where(lane == i1, -jnp.inf, lg)
        m2 = jnp.max(lg2, axis=-1, keepdims=True)
        i2 = jnp.min(jnp.where(lg2 == m2, lane, LANES), axis=-1, keepdims=True)
        e2 = jnp.exp(m2 - m1)
        den = 1.0 + e2
        gates_ref[r0:r0 + ROW_CHUNK, :] = jnp.where(lane == 0, 1.0 / den, jnp.where(lane == 1, e2 / den, 0.0))
        info_ref[r0:r0 + ROW_CHUNK, :] = jnp.where(lane == 0, i1, jnp.where(lane == 1, i2, 0)).astype(I32)


def _router(x, mod4, layer, g_ffn, w_router):
    wr = jnp.zeros((D, LANES), F32).at[:, 0:N_EXPERTS].set(w_router)
    wr_hi = wr.astype(BF16)
    wr_lo = (wr - wr_hi.astype(F32)).astype(BF16)
    nar = pl.BlockSpec((ROUTER_TM, LANES), lambda i: (i, 0))
    return pl.pallas_call(
        _router_kernel,
        grid=(T_ALL // ROUTER_TM,),
        in_specs=_x_pair_specs(x, ROUTER_TM) + [_mod_spec(layer, ROUTER_TM), _row_spec(),
                                                 _full_spec((2, D, LANES))],
        out_specs=[pl.BlockSpec((ROUTER_TM * TOK_SUB, LANES), lambda i: (i, 0)), nar, nar],
        out_shape=[jax.ShapeDtypeStruct((T_ALL * TOK_SUB, LANES), F32),
                   jax.ShapeDtypeStruct((T_ALL, LANES), I32),
                   jax.ShapeDtypeStruct((T_ALL, LANES), F32)],
        compiler_params=_cparams(("arbitrary",)),
        name="moe_router",
    )(*_x_pair(x), mod4, g_ffn.reshape(1, D), jnp.stack([wr_hi, wr_lo]))


def _route_plan(info):
    eid = jnp.arange(N_EXPERTS, dtype=I32)[None, :]
    first = info[:, 0:1] == eid
    second = info[:, 1:2] == eid
    onehot = jnp.logical_or(first, second).astype(I32)
    csum = jnp.cumsum(onehot, axis=0)
    rank = csum - onehot
    counts = csum[-1]
    padded = ((counts + TME - 1) // TME) * TME
    gend = jnp.cumsum(padded)
    gstart = gend - padded
    pos_te = gstart[None, :] + rank
    pos1 = jnp.sum(jnp.where(first, pos_te, 0), axis=1).astype(I32)
    pos2 = jnp.sum(jnp.where(second, pos_te, 0), axis=1).astype(I32)
    n_valid = gend[-1] // TME
    tile_start = jnp.arange(N_EXPERT_TILES, dtype=I32) * TME
    eff_start = jnp.minimum(tile_start, jnp.maximum(n_valid - 1, 0) * TME)
    tile_expert = jnp.minimum(jnp.sum((eff_start[:, None] >= gend[None, :]).astype(I32), axis=1),
                              N_EXPERTS - 1).astype(I32)
    valid_end = (gstart + counts)[tile_expert]
    tile_rows = jnp.where(tile_start < gend[-1], jnp.clip(valid_end - tile_start, 0, TME), 0).astype(I32)
    return pos1, pos2, tile_expert, n_valid.astype(I32).reshape(1), tile_rows


DISP_TM = 512
DMA_UNROLL = 8


def _dispatch_kernel(p1_ref, p2_ref, h_ref, xs_init, xs_hbm, sem):
    del xs_init
    i = pl.program_id(0)

    def body(g, carry):
        for k in range(DMA_UNROLL):
            r = g * DMA_UNROLL + k
            t = i * DISP_TM + r
            tile = h_ref.at[pl.ds(pl.multiple_of(r * TOK_SUB, TOK_SUB), TOK_SUB), :]
            for j, p_ref in enumerate((p1_ref, p2_ref)):
                dst = xs_hbm.at[pl.ds(pl.multiple_of(p_ref[t] * TOK_SUB, TOK_SUB), TOK_SUB), :]
                pltpu.make_async_copy(tile, dst, sem.at[j]).start(priority=j)
        return carry
    lax.fori_loop(0, DISP_TM // DMA_UNROLL, body, 0)
    for j in range(2):
        pltpu.make_async_copy(h_ref, xs_hbm.at[pl.ds(0, DISP_TM * TOK_SUB), :], sem.at[j]).wait()


def _dispatch(pos1, pos2, h, xs_init):
    return pl.pallas_call(
        _dispatch_kernel,
        grid_spec=pltpu.PrefetchScalarGridSpec(
            num_scalar_prefetch=2, grid=(T_ALL // DISP_TM,),
            in_specs=[pl.BlockSpec((DISP_TM * TOK_SUB, LANES), lambda i, *_: (i, 0)),
                      pl.BlockSpec(memory_space=pl.ANY)],
            out_specs=pl.BlockSpec(memory_space=pl.ANY),
            scratch_shapes=[pltpu.SemaphoreType.DMA((2,))]),
        out_shape=jax.ShapeDtypeStruct((P_MAX * TOK_SUB, LANES), F32),
        input_output_aliases={3: 0},
        compiler_params=_cparams(("arbitrary",)),
        name="moe_dispatch",
    )(pos1, pos2, h, xs_init)


EXP_TF = 512
EXP_NF = D_FF_EXPERT // EXP_TF


def _expert_kernel(te_ref, nv_ref, rows_ref, xs_ref, wg_ref, wu_ref, wd_ref, o_ref, xb_s, acc_s):
    m = pl.program_id(0)
    f = pl.program_id(1)
    nf = pl.num_programs(1)
    n_valid = nv_ref[0]
    quarter = TME // 4
    nq = (rows_ref[m] + quarter - 1) // quarter

    @pl.when(jnp.logical_and(m == 0, f == 0))
    def _():
        acc_s[...] = jnp.zeros_like(acc_s)

    def compute(mrows):
        @pl.when(f == 0)
        def _():
            for c in range(TOK_SUB):
                xb_s[0:mrows, c * LANES:(c + 1) * LANES] = _load_token_tiles(xs_ref, 0, mrows, c).astype(BF16)

        xb = xb_s[0:mrows, :]
        gg = _dot(xb, wg_ref[...].astype(BF16))
        uu = _dot(xb, wu_ref[...].astype(BF16))
        a = (gg * jax.nn.sigmoid(gg) * uu).astype(BF16)
        contrib = _dot(a, wd_ref[...].astype(BF16))

        acc_s[0:mrows, :] = jnp.where(f == 0, 0.0, acc_s[0:mrows, :]) + contrib

        @pl.when(f == nf - 1)
        def _():
            _store_token_tiles(o_ref, 0, acc_s[0:mrows, :])
            if mrows < TME:
                o_ref[mrows * TOK_SUB:TME * TOK_SUB, :] = jnp.zeros(((TME - mrows) * TOK_SUB, LANES), F32)

    for q in range(1, 5):
        pl.when(jnp.logical_and(m < n_valid, nq == q))(functools.partial(compute, q * quarter))

    @pl.when(jnp.logical_and(m >= n_valid, f == 0))
    def _():
        o_ref[...] = jnp.zeros_like(o_ref)


def _expert_ffn(tile_expert, n_valid, tile_rows, xs, w_gu, w_down, li):
    def feff(m, f, nv):
        return jnp.where(m < nv[0], f, EXP_NF - 1)

    def meff(m, nv):
        return jnp.minimum(m, nv[0] - 1)
    return pl.pallas_call(
        _expert_kernel,
        grid_spec=pltpu.PrefetchScalarGridSpec(
            num_scalar_prefetch=3, grid=(N_EXPERT_TILES, EXP_NF),
            in_specs=[pl.BlockSpec((TME * TOK_SUB, LANES), lambda m, f, te, nv, tr: (meff(m, nv), 0)),
                      pl.BlockSpec((None, None, D, EXP_TF),
                                   lambda m, f, te, nv, tr: (li, te[m], 0, feff(m, f, nv))),
                      pl.BlockSpec((None, None, D, EXP_TF),
                                   lambda m, f, te, nv, tr: (li, te[m], 0, EXP_NF + feff(m, f, nv))),
                      pl.BlockSpec((None, None, EXP_TF, D),
                                   lambda m, f, te, nv, tr: (li, te[m], feff(m, f, nv), 0))],
            out_specs=pl.BlockSpec((TME * TOK_SUB, LANES), lambda m, f, te, nv, tr: (m, 0)),
            scratch_shapes=[pltpu.VMEM((TME, D), BF16), pltpu.VMEM((TME, D), F32)]),
        out_shape=jax.ShapeDtypeStruct((P_MAX * TOK_SUB, LANES), F32),
        compiler_params=_cparams(("arbitrary", "arbitrary")),
        name="moe_expert_swiglu",
    )(tile_expert, n_valid, tile_rows, xs, w_gu, w_gu, w_down)


COMB_TM = 512


def _combine_kernel(p1_ref, p2_ref, xc_ref, xl_ref, gates_ref, mod_ref, gf_ref, y_hbm, *rest, final):
    i = pl.program_id(0)
    x_is_ctx = i < T_CTX // COMB_TM
    if final:
        oc_ref, ol_ref, a_s, b_s, sem = rest
    else:
        o_ref, a_s, b_s, sem = rest

    def tile_copy(p, r, dst, s):
        return pltpu.make_async_copy(y_hbm.at[pl.ds(pl.multiple_of(p * TOK_SUB, TOK_SUB), TOK_SUB), :],
                                     dst.at[pl.ds(pl.multiple_of(r * TOK_SUB, TOK_SUB), TOK_SUB), :], s)

    def body(g, carry):
        for k in range(DMA_UNROLL):
            r = g * DMA_UNROLL + k
            t = i * COMB_TM + r
            tile_copy(p1_ref[t], r, a_s, sem.at[0]).start(priority=0)
            tile_copy(p2_ref[t], r, b_s, sem.at[1]).start(priority=1)
        return carry
    lax.fori_loop(0, COMB_TM // DMA_UNROLL, body, 0)
    pltpu.make_async_copy(y_hbm.at[pl.ds(0, COMB_TM * TOK_SUB), :], a_s, sem.at[0]).wait()
    pltpu.make_async_copy(y_hbm.at[pl.ds(0, COMB_TM * TOK_SUB), :], b_s, sem.at[1]).wait()

    def finish(out_ref):
        for c in range(COMB_TM // ROW_CHUNK):
            r0 = c * ROW_CHUNK
            w1 = gates_ref[r0:r0 + ROW_CHUNK, 0:1]
            w2 = gates_ref[r0:r0 + ROW_CHUNK, 1:2]
            for j in range(TOK_SUB):
                cols = slice(j * LANES, (j + 1) * LANES)
                y = (w1 * _load_token_tiles(a_s, r0, ROW_CHUNK, j) + w2 * _load_token_tiles(b_s, r0, ROW_CHUNK, j))
                gate = mod_ref[:, 5 * D + j * LANES:5 * D + (j + 1) * LANES]
                x = jnp.where(x_is_ctx, xc_ref[r0:r0 + ROW_CHUNK, cols], xl_ref[r0:r0 + ROW_CHUNK, cols])
                out_ref[r0:r0 + ROW_CHUNK, cols] = x + gate * y
            if final:
                xn = out_ref[r0:r0 + ROW_CHUNK, :]
                ms = jnp.mean(xn * xn, axis=-1, keepdims=True)
                out_ref[r0:r0 + ROW_CHUNK, :] = xn * lax.rsqrt(ms + EPS) * gf_ref[...]

    if final:
        is_ctx = i < T_CTX // COMB_TM

        @pl.when(i == 0)
        def _():
            ol_ref[...] = jnp.zeros_like(ol_ref)

        pl.when(is_ctx)(lambda: finish(oc_ref))
        pl.when(jnp.logical_not(is_ctx))(lambda: finish(ol_ref))
    else:
        finish(o_ref)


def _combine(pos1, pos2, x, gates, mod4, layer, g_final, y, final):
    tok = pl.BlockSpec((COMB_TM, D), lambda i, *_: (i, 0))
    if final:
        out_specs = list(_ctx_lat_specs(COMB_TM))
        out_shape = [jax.ShapeDtypeStruct((T_CTX, D), F32), jax.ShapeDtypeStruct((T_LAT, D), F32)]
    else:
        out_specs = tok
        out_shape = jax.ShapeDtypeStruct((T_ALL, D), F32)
    return pl.pallas_call(
        functools.partial(_combine_kernel, final=final),
        grid_spec=pltpu.PrefetchScalarGridSpec(
            num_scalar_prefetch=2, grid=(T_ALL // COMB_TM,),
            in_specs=_x_pair_specs(x, COMB_TM) + [pl.BlockSpec((COMB_TM, LANES), lambda i, *_: (i, 0)),
                                                  _mod_spec(layer, COMB_TM), _row_spec(),
                                                  pl.BlockSpec(memory_space=pl.ANY)],
            out_specs=out_specs,
            scratch_shapes=[pltpu.VMEM((COMB_TM * TOK_SUB, LANES), F32), pltpu.VMEM((COMB_TM * TOK_SUB, LANES), F32),
                            pltpu.SemaphoreType.DMA((2,))]),
        out_shape=out_shape,
        compiler_params=_cparams(("arbitrary",)),
        name="moe_combine_final" if final else "moe_combine",
    )(pos1, pos2, *_x_pair(x), gates, mod4, g_final.reshape(1, D), y)


def _moe_layer(x, mod4, layer, g_ffn, w_router, w_gu, w_down, li, g_final, final, xs_init):
    h, info, gates = _router(x, mod4, layer, g_ffn, w_router)
    pos1, pos2, tile_expert, n_valid, tile_rows = _route_plan(info)
    xs = _dispatch(pos1, pos2, h, xs_init)
    ys = _expert_ffn(tile_expert, n_valid, tile_rows, xs, w_gu, w_down, li)
    return _combine(pos1, pos2, x, gates, mod4, layer, g_final, ys, final), xs


def kernel(x_prompt, x_sample, cache_k, cache_v, state_ssm_re, state_ssm_im, c, c_ctx, w_ada, b_ada, g_mix, g_ffn, g_final, conv_w_in, conv_w, conv_w_out, s5_lambda_re, s5_lambda_im, s5_b_re, s5_b_im, s5_c_re, s5_c_im, s5_log_dt, s5_d, s5_w_glu, attn_w_q, attn_w_kv, attn_w_o, attn_sink, fnet_w_out, ffn_w_gu, ffn_w_down, moe_w_router, moe_w_gu, moe_w_down):
    cond16 =jnp.concatenate([jnp.broadcast_to(c_ctx[None, :], (8, D)), c], axis=0)
    mod3 = _mod_all(cond16, w_ada, b_ada)
    mod4 = mod3.reshape(DEPTH, 16, 1, N_MOD * D)

    x = _conv_mixer(x_prompt.reshape(T_CTX, D), x_sample.reshape(T_LAT, D), mod4, 0, g_mix[0],
                    conv_w_in[0].astype(BF16), conv_w[0], conv_w_out[0].astype(BF16))
    x = _dense_ffn(x, mod4, 0, g_ffn[0], ffn_w_gu[0].astype(BF16), ffn_w_down[0].astype(BF16))

    lbr, lbi, bbr, bbi = _s5_prep(s5_lambda_re[0], s5_lambda_im[0], s5_log_dt[0], s5_b_re[0], s5_b_im[0])
    w_b, w_c, lam_s = _s5_weights(lbr, lbi, bbr, bbi, s5_c_re[0], s5_c_im[0])
    w_glu = s5_w_glu[0].astype(BF16)
    xc = x[0:T_CTX].reshape(BATCH, SEQ, D).transpose(1, 0, 2).reshape(T_CTX, D)
    xl = x[T_CTX:].reshape(DEC_BATCH, DEC_SEQ, D).transpose(1, 0, 2).reshape(T_LAT, D)
    h0_ctx = jnp.zeros((2, S5_JT, BATCH, 2 * S5_HALF), F32)
    h0_lat = _s5_state_in(state_ssm_re[:, 0], state_ssm_im[:, 0])
    yc, fin_c = _s5_scan(xc, BATCH, mod3, 1, 0, g_mix[1], w_b, w_c, lam_s, h0_ctx)
    yl, _ = _s5_scan(xl, DEC_BATCH, mod3, 1, 1, g_mix[1], w_b, w_c, lam_s, h0_lat)
    xc = _s5_glu(xc, yc, mod3, 1, 0, g_mix[1], s5_d[0], w_glu)
    xl = _s5_glu(xl, yl, mod3, 1, 1, g_mix[1], s5_d[0], w_glu)
    x_pair = (xc.reshape(SEQ, BATCH, D).transpose(1, 0, 2).reshape(T_CTX, D),
              xl.reshape(DEC_SEQ, DEC_BATCH, D).transpose(1, 0, 2).reshape(T_LAT, D))
    new_re, new_im = _s5_state_out(fin_c)
    xs_zero = jnp.zeros((P_MAX * TOK_SUB, LANES), F32)
    x, xs_buf = _moe_layer(x_pair, mod4, 1, g_ffn[1], moe_w_router[0], moe_w_gu, moe_w_down, 0, g_final, False,
                           xs_zero)

    wq = attn_w_q[0]
    wk = _expand_kv_cols(attn_w_kv[0][:, 0:N_KV_HEADS * HEAD_DIM])
    wv = _expand_kv_cols(attn_w_kv[0][:, N_KV_HEADS * HEAD_DIM:])
    wq_b, wk_b, wv_b = wq.astype(BF16), wk.astype(BF16), wv.astype(BF16)
    cos_t, sin_t = _rope_tables()
    q_c, k_c, v_c, kv_c = _qkv_ctx(x, mod4, 2, g_mix[2], wq_b, wk_b, wv_b, attn_w_kv[0].astype(BF16))
    q_l, k_l, v_l = _qkv_lat(x, mod4, 2, g_mix[2], wq_b, _rot_half_cols(wq).astype(BF16), wk_b,
                             _rot_half_cols(wk).astype(BF16), wv_b, cos_t, sin_t)
    sink = attn_sink[0]
    o_c = _attn_ctx(sink, q_c, k_c, v_c)
    expand = lambda a: jnp.broadcast_to(
        a.reshape(DEC_BATCH * PAST_LEN, N_KV_HEADS, 1, HEAD_DIM),
        (DEC_BATCH * PAST_LEN, N_KV_HEADS, Q_PER_KV, HEAD_DIM)).reshape(DEC_BATCH * PAST_LEN, D).astype(BF16)
    o_l = _attn_lat(sink, q_l, k_l, v_l, expand(cache_k[:, 0]), expand(cache_v[:, 0]))
    x = _resproj(x, o_c, o_l, mod4, 2, attn_w_o[0].astype(BF16))
    kvw = N_KV_HEADS * HEAD_DIM
    new_k = kv_c[:, 0:kvw].reshape(BATCH, 1, SEQ, N_KV_HEADS, HEAD_DIM)
    new_v = kv_c[:, kvw:].reshape(BATCH, 1, SEQ, N_KV_HEADS, HEAD_DIM)
    x = _dense_ffn(x, mod4, 2, g_ffn[2], ffn_w_gu[1].astype(BF16), ffn_w_down[1].astype(BF16))

    cs, fl = _fnet_tables()
    x = _fnet_mixer(x, mod4, 3, g_mix[3], cs, fl, fnet_w_out[0].astype(BF16))
    (y_ctx, y_lat), _ = _moe_layer(x, mod4, 3, g_ffn[3], moe_w_router[1], moe_w_gu, moe_w_down, 1, g_final, True,
                                   xs_buf)

    y_prompt = y_ctx.reshape(BATCH, SEQ, D)
    y_sample = y_lat.reshape(DEC_BATCH, DEC_SEQ, D)
    return (y_prompt, y_sample, new_k, new_v, new_re[:, None], new_im[:, None])
```

```python
import functools
import math

import numpy as np
import jax
import jax.numpy as jnp
from jax import lax
from jax.experimental import pallas as pl
from jax.experimental.pallas import tpu as pltpu

F32 = jnp.float32
BF16 = jnp.bfloat16
I32 = jnp.int32

D = 1024
BATCH = 16
SEQ = 256
DEPTH = 4
DEC_BATCH = 8
DEC_SEQ = 1024
PAST_LEN = 256
GRID_W = 64
EPS = 1e-6
N_MOD = 6
S5_GROUP = 16
S5_GROUPS = D // S5_GROUP
S5_STATE = 64
HEAD_DIM = 64
N_HEADS = D // HEAD_DIM
N_KV_HEADS = 4
Q_PER_KV = N_HEADS // N_KV_HEADS
WINDOW = 128
ROPE_THETA = 10000.0
ROPE_FREQS = HEAD_DIM // 4
FNET_GROUPS = 4
D_FF = 2816
N_EXPERTS = 8
TOP_K = 2
D_FF_EXPERT = 3584
NEG_INF = -1e30

T_CTX = BATCH * SEQ
T_LAT = DEC_BATCH * DEC_SEQ
T_ALL = T_CTX + T_LAT

VMEM_LIMIT_V7X = 56 * 1024 * 1024
LANES = 128

TM = 1024
ROW_CHUNK = 256
MM_CHUNK = 512
TME = 1024
N_EXPERT_TILES = (T_ALL * TOP_K) // TME + N_EXPERTS
P_MAX = N_EXPERT_TILES * TME


def _cparams(sem):
    return pltpu.CompilerParams(dimension_semantics=sem, vmem_limit_bytes=VMEM_LIMIT_V7X)


def _dot(a, b):
    return jnp.dot(a, b, preferred_element_type=F32)


def _dot_nt(a, b):
    return lax.dot_general(a, b, (((1,), (1,)), ((), ())), preferred_element_type=F32)


def _split_bf16(a):
    hi = a.astype(BF16)
    lo = (a - hi.astype(F32)).astype(BF16)
    return hi, lo


def _tile_rows(a, reps, axis=0):
    assert axis == 0
    return jnp.concatenate([a] * reps, axis=0)


def _norm_mod(x, g, sc, sh):
    ms = jnp.mean(x * x, axis=-1, keepdims=True)
    y = x * lax.rsqrt(ms + EPS) * g
    return y * (1.0 + sc) + sh


def _mod_row(i, tm):
    nct = T_CTX // tm
    lpb = DEC_SEQ // tm
    return jnp.where(i < nct, 0, 8 + (i - nct) // lpb)


def _mod_spec(layer, tm):
    return pl.BlockSpec((None, None, 1, N_MOD * D), lambda i, *_: (layer, _mod_row(i, tm), 0, 0))


def _row_spec():
    return pl.BlockSpec((1, D), lambda *_: (0, 0))


def _full_spec(shape):
    nd = len(shape)
    return pl.BlockSpec(shape, lambda *_: (0,) * nd, pipeline_mode=pl.Buffered(1))


def _mod_kernel(c_ref, w_ref, b_ref, o_ref):
    c = c_ref[...]
    s = (c * jax.nn.sigmoid(c)).astype(BF16)
    o_ref[...] = _dot(s, w_ref[...].astype(BF16)) + b_ref[...]


def _mod_all(cond16, w_ada, b_ada):
    tn = 1024
    return pl.pallas_call(
        _mod_kernel,
        grid=(DEPTH, N_MOD * D // tn),
        in_specs=[pl.BlockSpec((16, D), lambda l, n: (0, 0)),
                  pl.BlockSpec((None, D, tn), lambda l, n: (l, 0, n)),
                  pl.BlockSpec((None, 1, tn), lambda l, n: (l, 0, n))],
        out_specs=pl.BlockSpec((None, 16, tn), lambda l, n: (l, 0, n)),
        out_shape=jax.ShapeDtypeStruct((DEPTH, 16, N_MOD * D), F32),
        compiler_params=_cparams(("arbitrary", "arbitrary")),
        name="adaln_mod",
    )(cond16, w_ada, b_ada.reshape(DEPTH, 1, N_MOD * D))


def _conv_kernel(xc_ref, xl_ref, mod_ref, g_ref, win_ref, cw_ref, wout_ref, o_ref, gb_s, u_s, z_s):
    i = pl.program_id(0)
    is_ctx = i < (T_CTX // TM)

    def x_rows(r0, n):
        return jnp.where(is_ctx, xc_ref[r0:r0 + n, :], xl_ref[r0:r0 + n, :])

    sh = mod_ref[:, 0:D]
    sc = mod_ref[:, D:2 * D]
    gate = mod_ref[:, 2 * D:3 * D]
    g = g_ref[...]
    zero8 = jnp.zeros((8, D), F32)
    u_s[0:8, :] = zero8
    u_s[8 + TM:16 + TM, :] = zero8
    for c in range(TM // MM_CHUNK):
        r0 = c * MM_CHUNK
        h = _norm_mod(x_rows(r0, MM_CHUNK), g, sc, sh).astype(BF16)
        proj = _dot(h, win_ref[...])
        gb_s[r0:r0 + MM_CHUNK, :] = proj[:, 0:D]
        u_s[8 + r0:8 + r0 + MM_CHUNK, :] = proj[:, D:2 * D] * proj[:, 2 * D:3 * D]
    row = lax.broadcasted_iota(I32, (ROW_CHUNK, 1), 0)
    first = jnp.logical_and(is_ctx, row == 0)
    last = jnp.logical_and(is_ctx, row == ROW_CHUNK - 1)
    for c in range(TM // ROW_CHUNK):
        r0 = c * ROW_CHUNK
        up = jnp.where(first, 0.0, u_s[7 + r0:7 + r0 + ROW_CHUNK, :])
        mid = u_s[8 + r0:8 + r0 + ROW_CHUNK, :]
        dn = jnp.where(last, 0.0, u_s[9 + r0:9 + r0 + ROW_CHUNK, :])
        conv = up * cw_ref[0:1, :] + mid * cw_ref[1:2, :] + dn * cw_ref[2:3, :]
        z_s[r0:r0 + ROW_CHUNK, :] = (gb_s[r0:r0 + ROW_CHUNK, :] * conv).astype(BF16)
    for c in range(TM // MM_CHUNK):
        r0 = c * MM_CHUNK
        y = _dot(z_s[r0:r0 + MM_CHUNK, :], wout_ref[...])
        o_ref[r0:r0 + MM_CHUNK, :] = x_rows(r0, MM_CHUNK) + gate * y


def _conv_mixer(x_ctx, x_lat, mod4, layer, g_mix, w_in, conv_w, w_out):
    assert SEQ == ROW_CHUNK and DEC_SEQ == TM
    ctx, lat = _ctx_lat_specs(TM)
    return pl.pallas_call(
        _conv_kernel,
        grid=(T_ALL // TM,),
        in_specs=[ctx, lat, _mod_spec(layer, TM), _row_spec(),
                  _full_spec((D, 3 * D)), _full_spec((3, D)), _full_spec((D, D))],
        out_specs=pl.BlockSpec((TM, D), lambda i: (i, 0)),
        out_shape=jax.ShapeDtypeStruct((T_ALL, D), F32),
        scratch_shapes=[pltpu.VMEM((TM, D), F32), pltpu.VMEM((TM + 16, D), F32), pltpu.VMEM((TM, D), BF16)],
        compiler_params=_cparams(("arbitrary",)),
        name="conv_mixer",
    )(x_ctx, x_lat, mod4, g_mix.reshape(1, D), w_in, conv_w, w_out)


FFN_TM = 512
MXU_WIDTH_V7X = 256
FFN_SPLITS = ((0, 6 * MXU_WIDTH_V7X), (6 * MXU_WIDTH_V7X, D_FF))
assert D_FF % MXU_WIDTH_V7X == 0


def _ffn_kernel(x_ref, mod_ref, g_ref, wgu_ref, wd_ref, o_ref):
    sh = mod_ref[:, 3 * D:4 * D]
    sc = mod_ref[:, 4 * D:5 * D]
    gate = mod_ref[:, 5 * D:6 * D]
    x = x_ref[...]
    h = _norm_mod(x, g_ref[...], sc, sh).astype(BF16)
    acc = None
    for lo, hi in FFN_SPLITS:
        gg = _dot(h, wgu_ref[:, lo:hi])
        uu = _dot(h, wgu_ref[:, D_FF + lo:D_FF + hi])
        a = (gg * jax.nn.sigmoid(gg) * uu).astype(BF16)
        contrib = _dot(a, wd_ref[lo:hi, :])
        acc = contrib if acc is None else acc + contrib
    o_ref[...] = x + gate * acc


def _dense_ffn(x, mod4, layer, g_ffn, w_gu, w_down):
    tok = pl.BlockSpec((FFN_TM, D), lambda i: (i, 0))
    return pl.pallas_call(
        _ffn_kernel,
        grid=(T_ALL // FFN_TM,),
        in_specs=[tok, _mod_spec(layer, FFN_TM), _row_spec(),
                  _full_spec((D, 2 * D_FF)), _full_spec((D_FF, D))],
        out_specs=tok,
        out_shape=jax.ShapeDtypeStruct((T_ALL, D), F32),
        compiler_params=_cparams(("arbitrary",)),
        name="dense_swiglu",
    )(x, mod4, g_ffn.reshape(1, D), w_gu, w_down)


def _s5_prep_kernel(lr_ref, li_ref, ldt_ref, br_ref, bi_ref, lbr_ref, lbi_ref, bbr_ref, bbi_ref):
    lr = lr_ref[...]
    li = li_ref[...]
    dt = jnp.exp(ldt_ref[...])
    mag = jnp.exp(lr * dt)
    ar = mag * jnp.cos(li * dt)
    ai = mag * jnp.sin(li * dt)
    nr = ar - 1.0
    den = lr * lr + li * li
    fr = (nr * lr + ai * li) / den
    fi = (ai * lr - nr * li) / den
    br = br_ref[...]
    bi = bi_ref[...]
    lbr_ref[...] = ar
    lbi_ref[...] = ai
    bbr_ref[...] = fr * br - fi * bi
    bbi_ref[...] = fr * bi + fi * br


def _s5_prep(lam_re, lam_im, log_dt, b_re, b_im):
    rows = 2 * S5_GROUPS
    cols = S5_STATE * S5_GROUP
    exp = lambda a: jnp.repeat(a.reshape(rows, S5_STATE), S5_GROUP, axis=1)
    ldt = jnp.broadcast_to(log_dt.reshape(rows, 1), (rows, cols))
    outs = pl.pallas_call(
        _s5_prep_kernel,
        out_shape=[jax.ShapeDtypeStruct((rows, cols), F32)] * 4,
        name="s5_discretize",
    )(exp(lam_re), exp(lam_im), ldt, b_re.reshape(rows, cols), b_im.reshape(rows, cols))
    lbr, lbi, bbr, bbi = outs
    shp = (2, S5_GROUPS, S5_STATE, S5_GROUP)
    return lbr.reshape(shp)[..., 0], lbi.reshape(shp)[..., 0], bbr.reshape(shp), bbi.reshape(shp)


S5_JT = 8
S5_GPT = LANES // S5_GROUP
S5_HALF = S5_GPT * S5_STATE
S5_ROWS = 512


def _s5_scan_kernel(x_ref, mod_ref, g_ref, wb_ref, wc_ref, lam_ref, h0_ref,
                    y_ref, fin_ref, bu_s, st_s, *, nb, jgroup):
    d = pl.program_id(0)
    c = pl.program_id(1)
    lc = S5_ROWS // nb

    @pl.when(c == 0)
    def _():
        st_s[...] = h0_ref[...]

    rep = S5_ROWS // 8
    sh = _tile_rows(mod_ref[:, 0:D], rep, axis=0)
    sc = _tile_rows(mod_ref[:, D:2 * D], rep, axis=0)
    u = _norm_mod(x_ref[...], g_ref[...], sc, sh).astype(BF16)
    for j in range(S5_JT):
        bu_s[j] = _dot(u[:, j * LANES:(j + 1) * LANES], wb_ref[j])

    for j0 in range(0, S5_JT, jgroup):
        js = list(range(j0, j0 + jgroup))
        lam = [(jnp.broadcast_to(lam_ref[j][:, 0:S5_HALF], (nb, S5_HALF)),
                jnp.broadcast_to(lam_ref[j][:, S5_HALF:], (nb, S5_HALF))) for j in js]

        def body(t, carry):
            l = jnp.where(d == 0, t, lc - 1 - t)
            r0 = pl.multiple_of(l * nb, nb)
            out = []
            for k, j in enumerate(js):
                sr, si = carry[k]
                ar, ai = lam[k]
                bu = bu_s[j, pl.ds(r0, nb), :]
                hr = ar * sr - ai * si + bu[:, 0:S5_HALF]
                hi = ar * si + ai * sr + bu[:, S5_HALF:]
                bu_s[j, pl.ds(r0, nb), 0:S5_HALF] = hr
                bu_s[j, pl.ds(r0, nb), S5_HALF:] = hi
                out.append((hr, hi))
            return tuple(out)

        init = tuple((st_s[j][:, 0:S5_HALF], st_s[j][:, S5_HALF:]) for j in js)
        fin = lax.fori_loop(0, lc, body, init)
        for k, j in enumerate(js):
            st_s[j, :, 0:S5_HALF] = fin[k][0]
            st_s[j, :, S5_HALF:] = fin[k][1]

    for j in range(S5_JT):
        y_ref[:, j * LANES:(j + 1) * LANES] = _dot(bu_s[j].astype(BF16), wc_ref[j])
    fin_ref[...] = st_s[...]


def _s5_scan(xt, nb, mod3, layer, path, g_mix, w_b, w_c, lam_s, h0):
    rows = xt.shape[0]
    nc = rows // S5_ROWS
    chunk = lambda d, c: c + d * (nc - 1 - 2 * c)
    kern = functools.partial(_s5_scan_kernel, nb=nb, jgroup=2 if nb == 8 else 1)
    return pl.pallas_call(
        kern,
        grid=(2, nc),
        in_specs=[pl.BlockSpec((S5_ROWS, D), lambda d, c: (chunk(d, c), 0)),
                  pl.BlockSpec((None, 8, N_MOD * D), lambda d, c: (layer, path, 0)),
                  _row_spec(),
                  pl.BlockSpec((None, S5_JT, LANES, 2 * S5_HALF), lambda d, c: (d, 0, 0, 0)),
                  pl.BlockSpec((None, S5_JT, 2 * S5_HALF, LANES), lambda d, c: (d, 0, 0, 0)),
                  pl.BlockSpec((None, S5_JT, 1, 2 * S5_HALF), lambda d, c: (d, 0, 0, 0)),
                  pl.BlockSpec((None, S5_JT, nb, 2 * S5_HALF), lambda d, c: (d, 0, 0, 0))],
        out_specs=[pl.BlockSpec((None, S5_ROWS, D), lambda d, c: (d, chunk(d, c), 0)),
                   pl.BlockSpec((None, S5_JT, nb, 2 * S5_HALF), lambda d, c: (d, 0, 0, 0))],
        out_shape=[jax.ShapeDtypeStruct((2, rows, D), F32),
                   jax.ShapeDtypeStruct((2, S5_JT, nb, 2 * S5_HALF), F32)],
        scratch_shapes=[pltpu.VMEM((S5_JT, S5_ROWS, 2 * S5_HALF), F32),
                        pltpu.VMEM((S5_JT, nb, 2 * S5_HALF), F32)],
        compiler_params=_cparams(("arbitrary", "arbitrary")),
        name="s5_scan_b%d" % nb,
    )(xt, mod3, g_mix.reshape(1, D), w_b, w_c, lam_s, h0)


S5_GLU_ROWS = 512


def _s5_glu_kernel(x_ref, yf_ref, yb_ref, mod_ref, g_ref, dsk_ref, w_ref, o_ref):
    rep = MM_CHUNK // 8
    sh = _tile_rows(mod_ref[:, 0:D], rep, axis=0)
    sc = _tile_rows(mod_ref[:, D:2 * D], rep, axis=0)
    gate = _tile_rows(mod_ref[:, 2 * D:3 * D], rep, axis=0)
    for c in range(S5_GLU_ROWS // MM_CHUNK):
        r0 = c * MM_CHUNK
        x = x_ref[r0:r0 + MM_CHUNK, :]
        u = _norm_mod(x, g_ref[...], sc, sh)
        y = u * dsk_ref[...] + yf_ref[r0:r0 + MM_CHUNK, :] + yb_ref[r0:r0 + MM_CHUNK, :]
        z = jax.nn.gelu(y).astype(BF16)
        ag = _dot(z, w_ref[...])
        out = ag[:, 0:D] * jax.nn.sigmoid(ag[:, D:2 * D])
        o_ref[r0:r0 + MM_CHUNK, :] = x + gate * out


def _s5_glu(xt, y2, mod3, layer, path, g_mix, d_skip, w_glu):
    rows = xt.shape[0]
    return pl.pallas_call(
        _s5_glu_kernel,
        grid=(rows // S5_GLU_ROWS,),
        in_specs=[pl.BlockSpec((S5_GLU_ROWS, D), lambda i: (i, 0)),
                  pl.BlockSpec((None, S5_GLU_ROWS, D), lambda i: (0, i, 0)),
                  pl.BlockSpec((None, S5_GLU_ROWS, D), lambda i: (1, i, 0)),
                  pl.BlockSpec((None, 8, N_MOD * D), lambda i: (layer, path, 0)),
                  _row_spec(), _row_spec(), _full_spec((D, 2 * D))],
        out_specs=pl.BlockSpec((S5_GLU_ROWS, D), lambda i: (i, 0)),
        out_shape=jax.ShapeDtypeStruct((rows, D), F32),
        compiler_params=_cparams(("arbitrary",)),
        name="s5_glu",
    )(xt, y2, y2, mod3, g_mix.reshape(1, D), d_skip.reshape(1, D), w_glu)


def _s5_weights(lbr, lbi, bbr, bbi, c_re, c_im):
    eye = jnp.eye(S5_GPT, dtype=F32)
    bb = jnp.stack([bbr, bbi]).reshape(2, 2, S5_JT, S5_GPT, S5_STATE, S5_GROUP)
    w_bu = jnp.einsum('rdjgps,gh->djgsrhp', bb, eye).reshape(2, S5_JT, LANES, 2 * S5_HALF)
    w_b = w_bu.astype(BF16)
    cc =jnp.stack([c_re, -c_im]).reshape(2, 2, S5_JT, S5_GPT, S5_GROUP, S5_STATE)
    w_c = jnp.einsum('rdjgsp,gh->djrgphs', cc, eye).reshape(2, S5_JT, 2 * S5_HALF, LANES).astype(BF16)
    lam_s = jnp.concatenate([lbr.reshape(2, S5_JT, S5_HALF), lbi.reshape(2, S5_JT, S5_HALF)], axis=-1)
    return w_b, w_c, lam_s.reshape(2, S5_JT, 1, 2 * S5_HALF)


def _s5_state_in(st_re, st_im):
    def lay(a):
        b = a.shape[0]
        return a.transpose(1, 0, 2, 3).reshape(2, b, S5_JT, S5_HALF).transpose(0, 2, 1, 3)
    return jnp.concatenate([lay(st_re), lay(st_im)], axis=-1)


def _s5_state_out(fin):
    def lay(a):
        b = a.shape[2]
        return a.transpose(2, 0, 1, 3).reshape(b, 2, S5_GROUPS, S5_STATE)
    return lay(fin[..., 0:S5_HALF]), lay(fin[..., S5_HALF:])


QKV_TM = 512


def _qkv_ctx_kernel(x_ref, mod_ref, g_ref, wq_ref, wk_ref, wv_ref, wkv_ref, q_ref, k_ref, v_ref, kv_ref):
    sh = mod_ref[:, 0:D]
    sc = mod_ref[:, D:2 * D]
    for c in range(QKV_TM // MM_CHUNK):
        r0 = c * MM_CHUNK
        h = _norm_mod(x_ref[r0:r0 + MM_CHUNK, :], g_ref[...], sc, sh).astype(BF16)
        q_ref[r0:r0 + MM_CHUNK, :] = _dot(h, wq_ref[...]).astype(BF16)
        k_ref[r0:r0 + MM_CHUNK, :] = _dot(h, wk_ref[...]).astype(BF16)
        v_ref[r0:r0 + MM_CHUNK, :] = _dot(h, wv_ref[...]).astype(BF16)
        kv_ref[r0:r0 + MM_CHUNK, :] = _dot(h, wkv_ref[...])


def _qkv_ctx(x, mod4, layer, g_mix, wq, wk, wv, wkv):
    bf = jax.ShapeDtypeStruct((T_CTX, D), BF16)
    tok = pl.BlockSpec((QKV_TM, D), lambda i: (i, 0))
    kvw = wkv.shape[1]
    return pl.pallas_call(
        _qkv_ctx_kernel,
        grid=(T_CTX // QKV_TM,),
        in_specs=[tok, _mod_spec(layer, QKV_TM), _row_spec(),
                  _full_spec((D, D)), _full_spec((D, D)), _full_spec((D, D)), _full_spec((D, kvw))],
        out_specs=[tok, tok, tok, pl.BlockSpec((QKV_TM, kvw), lambda i: (i, 0))],
        out_shape=[bf, bf, bf, jax.ShapeDtypeStruct((T_CTX, kvw), F32)],
        compiler_params=_cparams(("arbitrary",)),
        name="qkv_ctx",
    )(x, mod4, g_mix.reshape(1, D), wq, wk, wv, wkv)


def _qkv_lat_kernel(x_ref, mod_ref, g_ref, wq_ref, wqr_ref, wk_ref, wkr_ref, wv_ref, cos_ref, sin_ref,
                    q_ref, k_ref, v_ref):
    sh = mod_ref[:, 0:D]
    sc = mod_ref[:, D:2 * D]
    for c in range(QKV_TM // MM_CHUNK):
        r0 = c * MM_CHUNK
        h = _norm_mod(x_ref[r0:r0 + MM_CHUNK, :], g_ref[...], sc, sh).astype(BF16)
        cos = cos_ref[r0:r0 + MM_CHUNK, :]
        sin = sin_ref[r0:r0 + MM_CHUNK, :]
        q_ref[r0:r0 + MM_CHUNK, :] = (_dot(h, wq_ref[...]) * cos + _dot(h, wqr_ref[...]) * sin).astype(BF16)
        k_ref[r0:r0 + MM_CHUNK, :] = (_dot(h, wk_ref[...]) * cos + _dot(h, wkr_ref[...]) * sin).astype(BF16)
        v_ref[r0:r0 + MM_CHUNK, :] = _dot(h, wv_ref[...]).astype(BF16)


def _qkv_lat(x, mod4, layer, g_mix, wq, wq_rot, wk, wk_rot, wv, cos_t, sin_t):
    nct = T_CTX // QKV_TM
    lpb = DEC_SEQ // QKV_TM
    bf = jax.ShapeDtypeStruct((T_LAT, D), BF16)
    tok_out = pl.BlockSpec((QKV_TM, D), lambda i: (i, 0))
    rope = pl.BlockSpec((QKV_TM, D), lambda i: (i % lpb, 0))
    return pl.pallas_call(
        _qkv_lat_kernel,
        grid=(T_LAT // QKV_TM,),
        in_specs=[pl.BlockSpec((QKV_TM, D), lambda i: (i + nct, 0)),
                  pl.BlockSpec((None, None, 1, N_MOD * D), lambda i: (layer, 8 + i // lpb, 0, 0)),
                  _row_spec()] + [_full_spec((D, D))] * 5 + [rope, rope],
        out_specs=[tok_out, tok_out, tok_out],
        out_shape=[bf, bf, bf],
        compiler_params=_cparams(("arbitrary",)),
        name="qkv_lat",
    )(x, mod4, g_mix.reshape(1, D), wq, wq_rot, wk, wk_rot, wv, cos_t, sin_t)


KVW = Q_PER_KV * HEAD_DIM


def _head_masks(rows):
    lane = lax.broadcasted_iota(I32, (rows, KVW), 1)
    return [jnp.logical_and(lane >= g * HEAD_DIM, lane < (g + 1) * HEAD_DIM) for g in range(Q_PER_KV)]


def _attn_ctx_kernel(sink_ref, q_ref, k_ref, v_ref, o_ref):
    scale = HEAD_DIM ** -0.5
    masks = _head_masks(SEQ)
    for kv in range(N_KV_HEADS):
        c0 = kv * KVW
        q = q_ref[:, c0:c0 + KVW]
        k = k_ref[:, c0:c0 + KVW]
        v = v_ref[:, c0:c0 + KVW]
        acc = jnp.zeros((SEQ, KVW), F32)
        for g in range(Q_PER_KV):
            sink = sink_ref[kv * Q_PER_KV + g]
            qg = jnp.where(masks[g], q, jnp.zeros_like(q))
            s = _dot_nt(qg, k) * scale
            m = jnp.maximum(jnp.max(s, axis=-1, keepdims=True), sink)
            e = jnp.exp(s - m)
            den = jnp.sum(e, axis=-1, keepdims=True) + jnp.exp(sink - m)
            og = _dot(e.astype(BF16), v) / den
            acc = jnp.where(masks[g], og, acc)
        o_ref[:, c0:c0 + KVW] = acc.astype(BF16)


def _attn_ctx(sink, q, k, v):
    tok = pl.BlockSpec((SEQ, D), lambda b, *_: (b, 0))
    return pl.pallas_call(
        _attn_ctx_kernel,
        grid_spec=pltpu.PrefetchScalarGridSpec(
            num_scalar_prefetch=1, grid=(BATCH,),
            in_specs=[tok, tok, tok], out_specs=tok),
        out_shape=jax.ShapeDtypeStruct((T_CTX, D), BF16),
        compiler_params=_cparams(("arbitrary",)),
        name="attn_ctx",
    )(sink, q, k, v)


ATT_TQ = 128
ATT_SPAN = ATT_TQ + 2 * WINDOW
assert math.log2(HEAD_DIM ** -0.5).is_integer()


def _attn_lat_kernel(sink_ref, q_ref, k_ref, v_ref, ck_ref, cv_ref, o_ref):
    qb = pl.program_id(1)
    scale = HEAD_DIM ** -0.5
    w0 = pl.multiple_of(jnp.clip(qb * ATT_TQ - WINDOW, 0, DEC_SEQ - ATT_SPAN), ATT_TQ)
    rows = Q_PER_KV * ATT_TQ
    ridx = lax.broadcasted_iota(I32, (rows, ATT_SPAN), 0)
    qpos = qb * ATT_TQ + (ridx & (ATT_TQ - 1))
    kpos = w0 + lax.broadcasted_iota(I32, (rows, ATT_SPAN), 1)
    valid = jnp.abs(qpos - kpos) <= WINDOW
    rcol = lax.broadcasted_iota(I32, (rows, 1), 0)
    masks = _head_masks(ATT_TQ)
    for kv in range(N_KV_HEADS):
        c0 = kv * KVW
        q = q_ref[:, c0:c0 + KVW] * scale
        qs = jnp.concatenate([jnp.where(masks[g], q, jnp.zeros_like(q)) for g in range(Q_PER_KV)], axis=0)
        sink = jnp.zeros((rows, 1), F32)
        for g in range(Q_PER_KV):
            sink = jnp.where(rcol >= g * ATT_TQ, sink_ref[kv * Q_PER_KV + g], sink)
        s_ctx = _dot_nt(qs, ck_ref[:, c0:c0 + KVW])
        s_win = _dot_nt(qs, k_ref[pl.ds(w0, ATT_SPAN), c0:c0 + KVW])
        s_win = jnp.where(valid, s_win, NEG_INF)
        m = jnp.maximum(jnp.maximum(jnp.max(s_ctx, axis=-1, keepdims=True),
                                    jnp.max(s_win, axis=-1, keepdims=True)), sink)
        e_ctx = jnp.exp(s_ctx - m)
        e_win = jnp.exp(s_win - m)
        den = (jnp.exp(sink - m) + jnp.sum(e_ctx, axis=-1, keepdims=True)
               + jnp.sum(e_win, axis=-1, keepdims=True))
        o = (_dot(e_ctx.astype(BF16), cv_ref[:, c0:c0 + KVW])
             + _dot(e_win.astype(BF16), v_ref[pl.ds(w0, ATT_SPAN), c0:c0 + KVW])) / den
        acc = jnp.zeros((ATT_TQ, KVW), F32)
        for g in range(Q_PER_KV):
            acc = jnp.where(masks[g], o[g * ATT_TQ:(g + 1) * ATT_TQ, :], acc)
        o_ref[:, c0:c0 + KVW] = acc.astype(BF16)


def _attn_lat(sink, q, k, v, ck, cv):
    nqb = DEC_SEQ // ATT_TQ
    qspec = pl.BlockSpec((ATT_TQ, D), lambda b, i, *_: (b * nqb + i, 0))
    seq = pl.BlockSpec((DEC_SEQ, D), lambda b, i, *_: (b, 0))
    ctx = pl.BlockSpec((PAST_LEN, D), lambda b, i, *_: (b, 0))
    return pl.pallas_call(
        _attn_lat_kernel,
        grid_spec=pltpu.PrefetchScalarGridSpec(
            num_scalar_prefetch=1, grid=(DEC_BATCH, nqb),
            in_specs=[qspec, seq, seq, ctx, ctx], out_specs=qspec),
        out_shape=jax.ShapeDtypeStruct((T_LAT, D), BF16),
        compiler_params=_cparams(("arbitrary", "arbitrary")),
        name="attn_lat",
    )(sink, q, k, v, ck, cv)


def _ctx_lat_specs(tm):
    nct = T_CTX // tm
    ctx = pl.BlockSpec((tm, D), lambda i, *_: (jnp.minimum(i, nct - 1), 0))
    lat = pl.BlockSpec((tm, D), lambda i, *_: (jnp.maximum(i - nct, 0), 0))
    return ctx, lat


def _resproj_kernel(x_ref, ac_ref, al_ref, mod_ref, w_ref, o_ref):
    gate = mod_ref[:, 2 * D:3 * D]
    is_ctx = pl.program_id(0) < T_CTX // TM
    a = jnp.where(is_ctx, ac_ref[...], al_ref[...])
    o_ref[...] = x_ref[...] + gate * _dot(a, w_ref[...])


def _resproj(x, a_ctx, a_lat, mod4, layer, w):
    tok = pl.BlockSpec((TM, D), lambda i: (i, 0))
    ctx, lat = _ctx_lat_specs(TM)
    return pl.pallas_call(
        _resproj_kernel,
        grid=(T_ALL // TM,),
        in_specs=[tok, ctx, lat, _mod_spec(layer, TM), _full_spec((D, D))],
        out_specs=tok,
        out_shape=jax.ShapeDtypeStruct((T_ALL, D), F32),
        compiler_params=_cparams(("arbitrary",)),
        name="attn_out_proj",
    )(x, a_ctx, a_lat, mod4, w)


def _rope_tables():
    rows = DEC_SEQ // GRID_W
    row = jnp.repeat(jnp.arange(rows), GRID_W).astype(F32)
    col = jnp.tile(jnp.arange(GRID_W), rows).astype(F32)
    inv = ROPE_THETA ** (-jnp.arange(ROPE_FREQS, dtype=F32) / ROPE_FREQS)
    ang = jnp.concatenate([row[:, None] * inv, col[:, None] * inv], axis=-1)
    cos = jnp.cos(ang)
    sin = jnp.sin(ang)
    cos_h = jnp.concatenate([cos, cos], axis=-1)
    sin_h = jnp.concatenate([sin, sin], axis=-1)
    return jnp.tile(cos_h, (1, N_HEADS)), jnp.tile(sin_h, (1, N_HEADS))


def _rot_half_cols(w):
    k = w.shape[0]
    w4 = w.reshape(k, -1, 2, HEAD_DIM // 2)
    return jnp.stack([-w4[:, :, 1], w4[:, :, 0]], axis=2).reshape(k, -1)


def _expand_kv_cols(w):
    k = w.shape[0]
    w3 = w.reshape(k, N_KV_HEADS, 1, HEAD_DIM)
    return jnp.broadcast_to(w3, (k, N_KV_HEADS, Q_PER_KV, HEAD_DIM)).reshape(k, N_HEADS * HEAD_DIM)


FG = D // FNET_GROUPS


def _fnet_kernel(x_ref, mod_ref, g_ref, cs_ref, fl_ref, w_ref, o_ref, h_s, ab_s):
    sh = mod_ref[:, 0:D]
    sc = mod_ref[:, D:2 * D]
    gate = mod_ref[:, 2 * D:3 * D]
    for c in range(TM // ROW_CHUNK):
        r0 = c * ROW_CHUNK
        h_s[r0:r0 + ROW_CHUNK, :] = _norm_mod(x_ref[r0:r0 + ROW_CHUNK, :], g_ref[...], sc, sh).astype(BF16)
    for g in range(FNET_GROUPS):
        ab = _dot(h_s[:, g * FG:(g + 1) * FG], cs_ref[...])
        ab_s[0:TM, g * FG:(g + 1) * FG] = ab[:, 0:FG].astype(BF16)
        ab_s[TM:2 * TM, g * FG:(g + 1) * FG] = ab[:, FG:2 * FG].astype(BF16)
    for c in range(TM // MM_CHUNK):
        r0 = c * MM_CHUNK
        f = _dot(fl_ref[r0:r0 + MM_CHUNK, :], ab_s[...])
        y = _dot(f.astype(BF16), w_ref[...])
        o_ref[r0:r0 + MM_CHUNK, :] = x_ref[r0:r0 + MM_CHUNK, :] + gate * y


def _fnet_mixer(x, mod4, layer, g_mix, cs, fl, w_out):
    nct = T_CTX // TM
    tok = pl.BlockSpec((TM, D), lambda i: (i, 0))
    return pl.pallas_call(
        _fnet_kernel,
        grid=(T_ALL // TM,),
        in_specs=[tok, _mod_spec(layer, TM), _row_spec(), _full_spec((FG, 2 * FG)),
                  pl.BlockSpec((None, TM, 2 * TM), lambda i: (jnp.where(i < nct, 0, 1), 0, 0)),
                  _full_spec((D, D))],
        out_specs=tok,
        out_shape=jax.ShapeDtypeStruct((T_ALL, D), F32),
        scratch_shapes=[pltpu.VMEM((TM, D), BF16), pltpu.VMEM((2 * TM, D), BF16)],
        compiler_params=_cparams(("arbitrary",)),
        name="fnet_mixer",
    )(x, mod4, g_mix.reshape(1, D), cs, fl, w_out)


def _dft_cos_sin(n):
    k = np.arange(n)
    ang = 2.0 * np.pi * ((k[:, None] * k[None, :]) % n) / n
    return np.cos(ang), np.sin(ang)


def _fnet_tables():
    cc, sc = _dft_cos_sin(FG)
    cs = np.concatenate([cc, sc], axis=1) / math.sqrt(FG)
    mats = []
    for seq in (SEQ, DEC_SEQ):
        cl, sl = _dft_cos_sin(seq)
        reps = TM // seq
        eye = np.eye(reps)
        mats.append(np.concatenate([np.kron(eye, cl), -np.kron(eye, sl)], axis=1) / math.sqrt(seq))
    return jnp.asarray(cs, F32).astype(BF16), jnp.asarray(np.stack(mats), F32).astype(BF16)


ROUTER_TM = 512
TOK_SUB = D // LANES


def _store_token_tiles(ref, r0, val):
    rows = val.shape[0]
    for c in range(TOK_SUB):
        ref[pl.ds(r0 * TOK_SUB + c, rows, stride=TOK_SUB), :] = val[:, c * LANES:(c + 1) * LANES]


def _load_token_tiles(ref, r0, rows, c, lead=None):
    idx = pl.ds(r0 * TOK_SUB + c, rows, stride=TOK_SUB)
    return ref[idx, :] if lead is None else ref[lead, idx, :]


def _x_pair(x):
    return x if isinstance(x, tuple) else (x, x)


def _x_pair_specs(x, tm):
    nct = T_CTX // tm
    if isinstance(x, tuple):
        return list(_ctx_lat_specs(tm))
    return [pl.BlockSpec((tm, D), lambda i, *_: (jnp.minimum(i, nct - 1), 0)),
            pl.BlockSpec((tm, D), lambda i, *_: (jnp.maximum(i, nct), 0))]


def _router_kernel(xc_ref, xl_ref, mod_ref, g_ref, wr_ref, h_ref, info_ref, gates_ref):
    sh = mod_ref[:, 3 * D:4 * D]
    sc = mod_ref[:, 4 * D:5 * D]
    w_hi = wr_ref[0]
    w_lo = wr_ref[1]
    is_ctx = pl.program_id(0) < T_CTX // ROUTER_TM
    for c in range(ROUTER_TM // ROW_CHUNK):
        r0 = c * ROW_CHUNK
        x = jnp.where(is_ctx, xc_ref[r0:r0 + ROW_CHUNK, :], xl_ref[r0:r0 + ROW_CHUNK, :])
        h = _norm_mod(x, g_ref[...], sc, sh)
        _store_token_tiles(h_ref, r0, h)
        h_hi, h_lo = _split_bf16(h)
        logits = _dot(h_hi, w_hi) + _dot(h_lo, w_hi) + _dot(h_hi, w_lo)
        lane = lax.broadcasted_iota(I32, logits.shape, 1)
        lg = jnp.where(lane < N_EXPERTS, logits, -jnp.inf)
        m1 = jnp.max(lg, axis=-1, keepdims=True)
        i1 = jnp.min(jnp.where(lg == m1, lane, LANES), axis=-1, keepdims=True)
        lg2 = jnp.where(lane == i1, -jnp.inf, lg)
        m2 = jnp.max(lg2, axis=-1, keepdims=True)
        i2 = jnp.min(jnp.where(lg2 == m2, lane, LANES), axis=-1, keepdims=True)
        e2 = jnp.exp(m2 - m1)
        den = 1.0 + e2
        gates_ref[r0:r0 + ROW_CHUNK, :] = jnp.where(lane == 0, 1.0 / den, jnp.where(lane == 1, e2 / den, 0.0))
        info_ref[r0:r0 + ROW_CHUNK, :] = jnp.where(lane == 0, i1, jnp.where(lane == 1, i2, 0)).astype(I32)


def _router(x, mod4, layer, g_ffn, w_router):
    wr = jnp.zeros((D, LANES), F32).at[:, 0:N_EXPERTS].set(w_router)
    wr_hi = wr.astype(BF16)
    wr_lo = (wr - wr_hi.astype(F32)).astype(BF16)
    nar = pl.BlockSpec((ROUTER_TM, LANES), lambda i: (i, 0))
    return pl.pallas_call(
        _router_kernel,
        grid=(T_ALL // ROUTER_TM,),
        in_specs=_x_pair_specs(x, ROUTER_TM) + [_mod_spec(layer, ROUTER_TM), _row_spec(),
                                                 _full_spec((2, D, LANES))],
        out_specs=[pl.BlockSpec((ROUTER_TM * TOK_SUB, LANES), lambda i: (i, 0)), nar, nar],
        out_shape=[jax.ShapeDtypeStruct((T_ALL * TOK_SUB, LANES), F32),
                   jax.ShapeDtypeStruct((T_ALL, LANES), I32),
                   jax.ShapeDtypeStruct((T_ALL, LANES), F32)],
        compiler_params=_cparams(("arbitrary",)),
        name="moe_router",
    )(*_x_pair(x), mod4, g_ffn.reshape(1, D), jnp.stack([wr_hi, wr_lo]))


def _route_plan(info):
    eid = jnp.arange(N_EXPERTS, dtype=I32)[None, :]
    first = info[:, 0:1] == eid
    second = info[:, 1:2] == eid
    onehot = jnp.logical_or(first, second).astype(I32)
    csum = jnp.cumsum(onehot, axis=0)
    rank = csum - onehot
    counts = csum[-1]
    padded = ((counts + TME - 1) // TME) * TME
    gend = jnp.cumsum(padded)
    gstart = gend - padded
    pos_te = gstart[None, :] + rank
    pos1 = jnp.sum(jnp.where(first, pos_te, 0), axis=1).astype(I32)
    pos2 = jnp.sum(jnp.where(second, pos_te, 0), axis=1).astype(I32)
    n_valid = gend[-1] // TME
    tile_start = jnp.arange(N_EXPERT_TILES, dtype=I32) * TME
    eff_start = jnp.minimum(tile_start, jnp.maximum(n_valid - 1, 0) * TME)
    tile_expert = jnp.minimum(jnp.sum((eff_start[:, None] >= gend[None, :]).astype(I32), axis=1),
                              N_EXPERTS - 1).astype(I32)
    valid_end = (gstart + counts)[tile_expert]
    tile_rows = jnp.where(tile_start < gend[-1], jnp.clip(valid_end - tile_start, 0, TME), 0).astype(I32)
    return pos1, pos2, tile_expert, n_valid.astype(I32).reshape(1), tile_rows


DISP_TM = 512
DMA_UNROLL = 8


def _dispatch_kernel(p1_ref, p2_ref, h_ref, xs_init, xs_hbm, sem):
    del xs_init
    i = pl.program_id(0)

    def body(g, carry):
        for k in range(DMA_UNROLL):
            r = g * DMA_UNROLL + k
            t = i * DISP_TM + r
            tile = h_ref.at[pl.ds(pl.multiple_of(r * TOK_SUB, TOK_SUB), TOK_SUB), :]
            for j, p_ref in enumerate((p1_ref, p2_ref)):
                dst = xs_hbm.at[pl.ds(pl.multiple_of(p_ref[t] * TOK_SUB, TOK_SUB), TOK_SUB), :]
                pltpu.make_async_copy(tile, dst, sem.at[j]).start(priority=j)
        return carry
    lax.fori_loop(0, DISP_TM // DMA_UNROLL, body, 0)
    for j in range(2):
        pltpu.make_async_copy(h_ref, xs_hbm.at[pl.ds(0, DISP_TM * TOK_SUB), :], sem.at[j]).wait()


def _dispatch(pos1, pos2, h, xs_init):
    return pl.pallas_call(
        _dispatch_kernel,
        grid_spec=pltpu.PrefetchScalarGridSpec(
            num_scalar_prefetch=2, grid=(T_ALL // DISP_TM,),
            in_specs=[pl.BlockSpec((DISP_TM * TOK_SUB, LANES), lambda i, *_: (i, 0)),
                      pl.BlockSpec(memory_space=pl.ANY)],
            out_specs=pl.BlockSpec(memory_space=pl.ANY),
            scratch_shapes=[pltpu.SemaphoreType.DMA((2,))]),
        out_shape=jax.ShapeDtypeStruct((P_MAX * TOK_SUB, LANES), F32),
        input_output_aliases={3: 0},
        compiler_params=_cparams(("arbitrary",)),
        name="moe_dispatch",
    )(pos1, pos2, h, xs_init)


EXP_TF = 512
EXP_NF = D_FF_EXPERT // EXP_TF


def _expert_kernel(te_ref, nv_ref, rows_ref, xs_ref, wg_ref, wu_ref, wd_ref, o_ref, xb_s, acc_s):
    m = pl.program_id(0)
    f = pl.program_id(1)
    nf = pl.num_programs(1)
    n_valid = nv_ref[0]
    quarter = TME // 4
    nq = (rows_ref[m] + quarter - 1) // quarter

    @pl.when(jnp.logical_and(m == 0, f == 0))
    def _():
        acc_s[...] = jnp.zeros_like(acc_s)

    def compute(mrows):
        @pl.when(f == 0)
        def _():
            for c in range(TOK_SUB):
                xb_s[0:mrows, c * LANES:(c + 1) * LANES] = _load_token_tiles(xs_ref, 0, mrows, c).astype(BF16)

        xb = xb_s[0:mrows, :]
        gg = _dot(xb, wg_ref[...].astype(BF16))
        uu = _dot(xb, wu_ref[...].astype(BF16))
        a = (gg * jax.nn.sigmoid(gg) * uu).astype(BF16)
        contrib = _dot(a, wd_ref[...].astype(BF16))

        acc_s[0:mrows, :] = jnp.where(f == 0, 0.0, acc_s[0:mrows, :]) + contrib

        @pl.when(f == nf - 1)
        def _():
            _store_token_tiles(o_ref, 0, acc_s[0:mrows, :])
            if mrows < TME:
                o_ref[mrows * TOK_SUB:TME * TOK_SUB, :] = jnp.zeros(((TME - mrows) * TOK_SUB, LANES), F32)

    for q in range(1, 5):
        pl.when(jnp.logical_and(m < n_valid, nq == q))(functools.partial(compute, q * quarter))

    @pl.when(jnp.logical_and(m >= n_valid, f == 0))
    def _():
        o_ref[...] = jnp.zeros_like(o_ref)


def _expert_ffn(tile_expert, n_valid, tile_rows, xs, w_gu, w_down, li):
    def feff(m, f, nv):
        return jnp.where(m < nv[0], f, EXP_NF - 1)

    def meff(m, nv):
        return jnp.minimum(m, nv[0] - 1)
    return pl.pallas_call(
        _expert_kernel,
        grid_spec=pltpu.PrefetchScalarGridSpec(
            num_scalar_prefetch=3, grid=(N_EXPERT_TILES, EXP_NF),
            in_specs=[pl.BlockSpec((TME * TOK_SUB, LANES), lambda m, f, te, nv, tr: (meff(m, nv), 0)),
                      pl.BlockSpec((None, None, D, EXP_TF),
                                   lambda m, f, te, nv, tr: (li, te[m], 0, feff(m, f, nv))),
                      pl.BlockSpec((None, None, D, EXP_TF),
                                   lambda m, f, te, nv, tr: (li, te[m], 0, EXP_NF + feff(m, f, nv))),
                      pl.BlockSpec((None, None, EXP_TF, D),
                                   lambda m, f, te, nv, tr: (li, te[m], feff(m, f, nv), 0))],
            out_specs=pl.BlockSpec((TME * TOK_SUB, LANES), lambda m, f, te, nv, tr: (m, 0)),
            scratch_shapes=[pltpu.VMEM((TME, D), BF16), pltpu.VMEM((TME, D), F32)]),
        out_shape=jax.ShapeDtypeStruct((P_MAX * TOK_SUB, LANES), F32),
        compiler_params=_cparams(("arbitrary", "arbitrary")),
        name="moe_expert_swiglu",
    )(tile_expert, n_valid, tile_rows, xs, w_gu, w_gu, w_down)


COMB_TM = 512


def _combine_kernel(p1_ref, p2_ref, xc_ref, xl_ref, gates_ref, mod_ref, gf_ref, y_hbm, *rest, final):
    i = pl.program_id(0)
    x_is_ctx = i < T_CTX // COMB_TM
    if final:
        oc_ref, ol_ref, a_s, b_s, sem = rest
    else:
        o_ref, a_s, b_s, sem = rest

    def tile_copy(p, r, dst, s):
        return pltpu.make_async_copy(y_hbm.at[pl.ds(pl.multiple_of(p * TOK_SUB, TOK_SUB), TOK_SUB), :],
                                     dst.at[pl.ds(pl.multiple_of(r * TOK_SUB, TOK_SUB), TOK_SUB), :], s)

    slot = i % 2

    def fetch(tile, dst_slot):
        def body(g, carry):
            for k in range(DMA_UNROLL):
                r = g * DMA_UNROLL + k
                t = tile * COMB_TM + r
                tile_copy(p1_ref[t], r, a_s.at[dst_slot], sem.at[dst_slot, 0]).start(priority=0)
                tile_copy(p2_ref[t], r, b_s.at[dst_slot], sem.at[dst_slot, 1]).start(priority=1)
            return carry
        lax.fori_loop(0, COMB_TM // DMA_UNROLL, body, 0)

    @pl.when(i == 0)
    def _():
        fetch(0, 0)

    @pl.when(i + 1 < pl.num_programs(0))
    def _():
        fetch(i + 1, 1 - slot)

    pltpu.make_async_copy(y_hbm.at[pl.ds(0, COMB_TM * TOK_SUB), :], a_s.at[slot], sem.at[slot, 0]).wait()
    pltpu.make_async_copy(y_hbm.at[pl.ds(0, COMB_TM * TOK_SUB), :], b_s.at[slot], sem.at[slot, 1]).wait()

    def finish(out_ref):
        for c in range(COMB_TM // ROW_CHUNK):
            r0 = c * ROW_CHUNK
            w1 = gates_ref[r0:r0 + ROW_CHUNK, 0:1]
            w2 = gates_ref[r0:r0 + ROW_CHUNK, 1:2]
            for j in range(TOK_SUB):
                cols = slice(j * LANES, (j + 1) * LANES)
                y = (w1 * _load_token_tiles(a_s, r0, ROW_CHUNK, j, lead=slot)
                     + w2 * _load_token_tiles(b_s, r0, ROW_CHUNK, j, lead=slot))
                gate = mod_ref[:, 5 * D + j * LANES:5 * D + (j + 1) * LANES]
                x = jnp.where(x_is_ctx, xc_ref[r0:r0 + ROW_CHUNK, cols], xl_ref[r0:r0 + ROW_CHUNK, cols])
                out_ref[r0:r0 + ROW_CHUNK, cols] = x + gate * y
            if final:
                xn = out_ref[r0:r0 + ROW_CHUNK, :]
                ms = jnp.mean(xn * xn, axis=-1, keepdims=True)
                out_ref[r0:r0 + ROW_CHUNK, :] = xn * lax.rsqrt(ms + EPS) * gf_ref[...]

    if final:
        is_ctx = i < T_CTX // COMB_TM

        @pl.when(i == 0)
        def _():
            ol_ref[...] = jnp.zeros_like(ol_ref)

        pl.when(is_ctx)(lambda: finish(oc_ref))
        pl.when(jnp.logical_not(is_ctx))(lambda: finish(ol_ref))
    else:
        finish(o_ref)


def _combine(pos1, pos2, x, gates, mod4, layer, g_final, y, final):
    tok = pl.BlockSpec((COMB_TM, D), lambda i, *_: (i, 0))
    if final:
        out_specs = list(_ctx_lat_specs(COMB_TM))
        out_shape = [jax.ShapeDtypeStruct((T_CTX, D), F32), jax.ShapeDtypeStruct((T_LAT, D), F32)]
    else:
        out_specs = tok
        out_shape = jax.ShapeDtypeStruct((T_ALL, D), F32)
    return pl.pallas_call(
        functools.partial(_combine_kernel, final=final),
        grid_spec=pltpu.PrefetchScalarGridSpec(
            num_scalar_prefetch=2, grid=(T_ALL // COMB_TM,),
            in_specs=_x_pair_specs(x, COMB_TM) + [pl.BlockSpec((COMB_TM, LANES), lambda i, *_: (i, 0)),
                                                  _mod_spec(layer, COMB_TM), _row_spec(),
                                                  pl.BlockSpec(memory_space=pl.ANY)],
            out_specs=out_specs,
            scratch_shapes=[pltpu.VMEM((2, COMB_TM * TOK_SUB, LANES), F32),
                            pltpu.VMEM((2, COMB_TM * TOK_SUB, LANES), F32),
                            pltpu.SemaphoreType.DMA((2, 2))]),
        out_shape=out_shape,
        compiler_params=_cparams(("arbitrary",)),
        name="moe_combine_final" if final else "moe_combine",
    )(pos1, pos2, *_x_pair(x), gates, mod4, g_final.reshape(1, D), y)


def _moe_layer(x, mod4, layer, g_ffn, w_router, w_gu, w_down, li, g_final, final, xs_init):
    h, info, gates = _router(x, mod4, layer, g_ffn, w_router)
    pos1, pos2, tile_expert, n_valid, tile_rows = _route_plan(info)
    xs = _dispatch(pos1, pos2, h, xs_init)
    ys = _expert_ffn(tile_expert, n_valid, tile_rows, xs, w_gu, w_down, li)
    return _combine(pos1, pos2, x, gates, mod4, layer, g_final, ys, final), xs


def kernel(x_prompt, x_sample, cache_k, cache_v, state_ssm_re, state_ssm_im, c, c_ctx, w_ada, b_ada, g_mix, g_ffn, g_final, conv_w_in, conv_w, conv_w_out, s5_lambda_re, s5_lambda_im, s5_b_re, s5_b_im, s5_c_re, s5_c_im, s5_log_dt, s5_d, s5_w_glu, attn_w_q, attn_w_kv, attn_w_o, attn_sink, fnet_w_out, ffn_w_gu, ffn_w_down, moe_w_router, moe_w_gu, moe_w_down):
    cond16 =jnp.concatenate([jnp.broadcast_to(c_ctx[None, :], (8, D)), c], axis=0)
    mod3 = _mod_all(cond16, w_ada, b_ada)
    mod4 = mod3.reshape(DEPTH, 16, 1, N_MOD * D)

    x = _conv_mixer(x_prompt.reshape(T_CTX, D), x_sample.reshape(T_LAT, D), mod4, 0, g_mix[0],
                    conv_w_in[0].astype(BF16), conv_w[0], conv_w_out[0].astype(BF16))
    x = _dense_ffn(x, mod4, 0, g_ffn[0], ffn_w_gu[0].astype(BF16), ffn_w_down[0].astype(BF16))

    lbr, lbi, bbr, bbi = _s5_prep(s5_lambda_re[0], s5_lambda_im[0], s5_log_dt[0], s5_b_re[0], s5_b_im[0])
    w_b, w_c, lam_s = _s5_weights(lbr, lbi, bbr, bbi, s5_c_re[0], s5_c_im[0])
    w_glu = s5_w_glu[0].astype(BF16)
    xc = x[0:T_CTX].reshape(BATCH, SEQ, D).transpose(1, 0, 2).reshape(T_CTX, D)
    xl = x[T_CTX:].reshape(DEC_BATCH, DEC_SEQ, D).transpose(1, 0, 2).reshape(T_LAT, D)
    h0_ctx = jnp.zeros((2, S5_JT, BATCH, 2 * S5_HALF), F32)
    h0_lat = _s5_state_in(state_ssm_re[:, 0], state_ssm_im[:, 0])
    yc, fin_c = _s5_scan(xc, BATCH, mod3, 1, 0, g_mix[1], w_b, w_c, lam_s, h0_ctx)
    yl, _ = _s5_scan(xl, DEC_BATCH, mod3, 1, 1, g_mix[1], w_b, w_c, lam_s, h0_lat)
    xc = _s5_glu(xc, yc, mod3, 1, 0, g_mix[1], s5_d[0], w_glu)
    xl = _s5_glu(xl, yl, mod3, 1, 1, g_mix[1], s5_d[0], w_glu)
    x_pair = (xc.reshape(SEQ, BATCH, D).transpose(1, 0, 2).reshape(T_CTX, D),
              xl.reshape(DEC_SEQ, DEC_BATCH, D).transpose(1, 0, 2).reshape(T_LAT, D))
    new_re, new_im = _s5_state_out(fin_c)
    xs_zero = jnp.zeros((P_MAX * TOK_SUB, LANES), F32)
    x, xs_buf = _moe_layer(x_pair, mod4, 1, g_ffn[1], moe_w_router[0], moe_w_gu, moe_w_down, 0, g_final, False,
                           xs_zero)

    wq = attn_w_q[0]
    wk = _expand_kv_cols(attn_w_kv[0][:, 0:N_KV_HEADS * HEAD_DIM])
    wv = _expand_kv_cols(attn_w_kv[0][:, N_KV_HEADS * HEAD_DIM:])
    wq_b, wk_b, wv_b = wq.astype(BF16), wk.astype(BF16), wv.astype(BF16)
    cos_t, sin_t = _rope_tables()
    q_c, k_c, v_c, kv_c = _qkv_ctx(x, mod4, 2, g_mix[2], wq_b, wk_b, wv_b, attn_w_kv[0].astype(BF16))
    q_l, k_l, v_l = _qkv_lat(x, mod4, 2, g_mix[2], wq_b, _rot_half_cols(wq).astype(BF16), wk_b,
                             _rot_half_cols(wk).astype(BF16), wv_b, cos_t, sin_t)
    sink = attn_sink[0]
    o_c = _attn_ctx(sink, q_c, k_c, v_c)
    expand = lambda a: jnp.broadcast_to(
        a.reshape(DEC_BATCH * PAST_LEN, N_KV_HEADS, 1, HEAD_DIM),
        (DEC_BATCH * PAST_LEN, N_KV_HEADS, Q_PER_KV, HEAD_DIM)).reshape(DEC_BATCH * PAST_LEN, D).astype(BF16)
    o_l = _attn_lat(sink, q_l, k_l, v_l, expand(cache_k[:, 0]), expand(cache_v[:, 0]))
    x = _resproj(x, o_c, o_l, mod4, 2, attn_w_o[0].astype(BF16))
    kvw = N_KV_HEADS * HEAD_DIM
    new_k = kv_c[:, 0:kvw].reshape(BATCH, 1, SEQ, N_KV_HEADS, HEAD_DIM)
    new_v = kv_c[:, kvw:].reshape(BATCH, 1, SEQ, N_KV_HEADS, HEAD_DIM)
    x = _dense_ffn(x, mod4, 2, g_ffn[2], ffn_w_gu[1].astype(BF16), ffn_w_down[1].astype(BF16))

    cs, fl = _fnet_tables()
    x = _fnet_mixer(x, mod4, 3, g_mix[3], cs, fl, fnet_w_out[0].astype(BF16))
    (y_ctx, y_lat), _ = _moe_layer(x, mod4, 3, g_ffn[3], moe_w_router[1], moe_w_gu, moe_w_down, 1, g_final, True,
                                   xs_buf)

    y_prompt = y_ctx.reshape(BATCH, SEQ, D)
    y_sample = y_lat.reshape(DEC_BATCH, DEC_SEQ, D)
    return (y_prompt, y_sample, new_k, new_v, new_re[:, None], new_im[:, None])
```
